```python
import jax
import jax.numpy as jnp
from jax import lax
import numpy as np

D_MODEL = 2048
BATCH = 4
SEQ = 2048
DEPTH = 2

CTX_LEN = 256
GRID_W = 64
N_MIXERS = 2
HEAD_SIZE = 64
N_HEADS = D_MODEL // HEAD_SIZE
DECAY_LORA = 96
ICLR_LORA = 96
GATE_LORA = 256
POOL_WINDOWS = (2, 4, 8, 16)
N_POOL_GROUPS = len(POOL_WINDOWS)
POOL_GROUP = D_MODEL // N_POOL_GROUPS
D_FF = 5632
N_EXPERTS = 8
TOP_K = 2
D_FF_EXPERT = 7168
N_RWKV_LAYERS = (DEPTH + N_MIXERS - 1) // N_MIXERS
N_POOL_LAYERS = DEPTH // N_MIXERS
N_DENSE_LAYERS = (DEPTH + 1) // 2
N_MOE_LAYERS = DEPTH // 2
DEEPNORM_ALPHA = (2.0 * DEPTH) ** 0.25
DEEPNORM_BETA = (8.0 * DEPTH) ** -0.25
LN_EPS = 1e-5
GN_EPS = 64e-5
POS_BASE = 10000.0

kernel_name = 'hybrid_rwkv7_pool_moe_flow_trunk'


def layer_norm(x, w, b):
    xf = x.astype(jnp.float32)
    mu = jnp.mean(xf, axis=-1, keepdims=True)
    var = jnp.mean(jnp.square(xf - mu), axis=-1, keepdims=True)
    return ((xf - mu) * lax.rsqrt(var + LN_EPS) * w + b).astype(x.dtype)


def post_norm(x, y, w, b):
    return layer_norm(DEEPNORM_ALPHA * x + y, w, b)


def adaln(cond, w_mod, b_mod):
    m = jax.nn.silu(cond) @ w_mod + b_mod
    return jnp.split(m, 6, axis=-1)


def grid_position_embedding(rows, d):
    quarter = d // 4
    omega = 1.0 / (POS_BASE ** (jnp.arange(quarter, dtype=jnp.float32) / quarter))
    row = jnp.repeat(jnp.arange(rows, dtype=jnp.float32), GRID_W)
    col = jnp.tile(jnp.arange(GRID_W, dtype=jnp.float32), rows)
    ar = row[:, None] * omega[None, :]
    ac = col[:, None] * omega[None, :]
    return jnp.concatenate([jnp.sin(ar), jnp.cos(ar), jnp.sin(ac), jnp.cos(ac)], axis=-1)


def centred_shift(h):
    hp = jnp.pad(h, ((0, 0), (1, 1), (0, 0)))
    return 0.5 * (hp[:, :-2] + hp[:, 2:]) - h


def to_heads(t):
    return t.reshape(t.shape[0], t.shape[1], N_HEADS, HEAD_SIZE).astype(jnp.float32)


def wkv7_scan(decay, k, v, kk, b, s0, r, reverse):
    emit = r is not None
    xs = (decay, k, v, kk, b) + ((r,) if emit else ())
    xs = tuple(jnp.moveaxis(t, 1, 0) for t in xs)

    def step(S, inp):
        w_t, k_t, v_t, kk_t, b_t = inp[:5]
        sa = jnp.einsum('bhvk,bhk->bhv', S, kk_t)
        S = S * w_t[:, :, None, :] - sa[..., None] * b_t[:, :, None, :] + v_t[..., None] * k_t[:, :, None, :]
        y = jnp.einsum('bhvk,bhk->bhv', S, inp[5]) if emit else None
        return S, y

    s_final, ys = lax.scan(step, s0, xs, reverse=reverse)
    return s_final, (jnp.moveaxis(ys, 0, 1) if emit else None)


def rwkv7_mixer(h_lat, h_ctx, mu, w_r, w_k, w_v, w_o, dw0, dw1, dw2, ia0, ia1, ia2,
                g1, g2, k_k, k_a, r_k, gn_w, gn_b, ctx_out):
    B = h_lat.shape[0]
    states = [jnp.zeros((B, N_HEADS, HEAD_SIZE, HEAD_SIZE), jnp.float32) for _ in range(2)]
    outs = []
    for h, emit in ((h_ctx, ctx_out), (h_lat, True)):
        T = h.shape[1]
        xx = centred_shift(h)
        mix = lambda n: h + xx * mu[n]
        xw, xk, xv, xa = mix(1), mix(2), mix(3), mix(4)
        k = xk @ w_k
        v = to_heads(xv @ w_v)
        r = to_heads(mix(0) @ w_r) if emit else None
        kk = to_heads(k * k_k)
        kk = kk / jnp.maximum(jnp.linalg.norm(kk, axis=-1, keepdims=True), 1e-12)
        y_sum, bonus = 0.0, 0.0
        for d in range(2):
            logw = -jax.nn.softplus(-(dw0[d] + jnp.tanh(xw @ dw1[d]) @ dw2[d])) - 0.5
            decay = to_heads(jnp.exp(-jnp.exp(logw.astype(jnp.float32))))
            a = jax.nn.sigmoid(ia0[d] + (xa @ ia1[d]) @ ia2[d])
            k_d = to_heads(k * (1.0 + (a - 1.0) * k_a))
            states[d], y = wkv7_scan(decay, k_d, v, kk, kk * to_heads(a), states[d], r, reverse=(d == 1))
            if emit:
                y_sum = y_sum + y
                bonus = bonus + jnp.sum(r * k_d * r_k, axis=-1, keepdims=True) * v
        if emit:
            mean = jnp.mean(y_sum, axis=-1, keepdims=True)
            var = jnp.mean(jnp.square(y_sum - mean), axis=-1, keepdims=True)
            o = ((y_sum - mean) * lax.rsqrt(var + GN_EPS)).reshape(B, T, D_MODEL) * gn_w + gn_b
            o = (o + bonus.reshape(B, T, D_MODEL)).astype(h.dtype)
            gate = jax.nn.sigmoid(mix(5) @ g1) @ g2
            outs.append((o * gate) @ w_o)
        else:
            outs.append(None)
    return outs[1], outs[0]


def multiscale_pool(h, w_pool, scale):
    B, T, D = h.shape
    hf = h.astype(jnp.float32)
    cs = jnp.pad(jnp.cumsum(hf, axis=1), ((0, 0), (1, 0), (0, 0)))
    t = jnp.arange(T)
    groups = []
    for g, win in enumerate(POOL_WINDOWS):
        lo = jnp.clip(t - win // 2, 0, T)
        hi = jnp.clip(t + win // 2, 0, T)
        sl = slice(g * POOL_GROUP, (g + 1) * POOL_GROUP)
        window_sum = cs[:, hi, sl] - cs[:, lo, sl]
        count = (hi - lo).astype(jnp.float32)[None, :, None]
        groups.append(window_sum / count - hf[:, :, sl])
    pooled = jnp.stack(groups, axis=2).astype(h.dtype)
    y = jnp.einsum('btgp,gpq->btgq', pooled, w_pool).reshape(B, T, D)
    return y * scale


def swiglu(h, w1, w3, w2):
    return (jax.nn.silu(h @ w1) * (h @ w3)) @ w2


def moe_swiglu(h, router, w1, w3, w2):
    probs = jax.nn.softmax((h @ router).astype(jnp.float32), axis=-1)
    top_p, top_i = lax.top_k(probs, TOP_K)
    top_p = top_p / jnp.sum(top_p, axis=-1, keepdims=True)
    combine = jnp.sum(jax.nn.one_hot(top_i, N_EXPERTS, dtype=jnp.float32) * top_p[..., None], axis=-2).astype(h.dtype)
    y = jnp.zeros_like(h)
    for e in range(N_EXPERTS):
        y = y + combine[..., e:e + 1] * swiglu(h, w1[e], w3[e], w2[e])
    return y


def setup_inputs(seed: int = 0) -> dict:
    key = jax.random.key(seed)
    keys = iter(jax.random.split(key, 48))

    def normal(shape, scale):
        return scale * jax.random.normal(next(keys), shape, jnp.float32)

    def uniform(shape, lo, hi):
        return jax.random.uniform(next(keys), shape, jnp.float32, lo, hi)

    D = D_MODEL
    s = D ** -0.5
    NR, NP, ND, NM = N_RWKV_LAYERS, N_POOL_LAYERS, N_DENSE_LAYERS, N_MOE_LAYERS
    return {
        'x': normal((BATCH, SEQ, D), 1.0),
        'c': normal((BATCH, D), 1.0),
        'ctx': normal((BATCH, CTX_LEN, D), 1.0),
        'c_ctx': normal((D,), 1.0),
        'w_mod': normal((DEPTH, D, 6 * D), 0.5 * s),
        'b_mod': normal((DEPTH, 6 * D), 0.02),
        'ln_w': 1.0 + normal((DEPTH, 2, D), 0.02),
        'ln_b': normal((DEPTH, 2, D), 0.02),
        'rwkv_mu': uniform((NR, 6, D), 0.0, 1.0),
        'rwkv_w_r': normal((NR, D, D), s),
        'rwkv_w_k': normal((NR, D, D), 0.5 * s),
        'rwkv_w_v': normal((NR, D, D), s),
        'rwkv_w_o': normal((NR, D, D), DEEPNORM_BETA * s),
        'rwkv_decay_w0': uniform((NR, 2, D), -6.0, -1.0),
        'rwkv_decay_w1': normal((NR, 2, D, DECAY_LORA), s),
        'rwkv_decay_w2': normal((NR, 2, DECAY_LORA, D), 0.1 * DECAY_LORA ** -0.5),
        'rwkv_iclr_a0': normal((NR, 2, D), 0.1),
        'rwkv_iclr_a1': normal((NR, 2, D, ICLR_LORA), s),
        'rwkv_iclr_a2': normal((NR, 2, ICLR_LORA, D), 0.5 * ICLR_LORA ** -0.5),
        'rwkv_gate_g1': normal((NR, D, GATE_LORA), s),
        'rwkv_gate_g2': normal((NR, GATE_LORA, D), GATE_LORA ** -0.5),
        'rwkv_k_k': 0.85 + normal((NR, D), 0.02),
        'rwkv_k_a': 1.0 + normal((NR, D), 0.02),
        'rwkv_r_k': -0.04 + normal((NR, N_HEADS, HEAD_SIZE), 0.01),
        'rwkv_gn_w': 1.0 + normal((NR, D), 0.02),
        'rwkv_gn_b': normal((NR, D), 0.02),
        'pool_w': normal((NP, N_POOL_GROUPS, POOL_GROUP, POOL_GROUP), DEEPNORM_BETA * POOL_GROUP ** -0.5),
        'pool_scale': 1.0 + normal((NP, D), 0.02),
        'ffn_w1': normal((ND, D, D_FF), s),
        'ffn_w3': normal((ND, D, D_FF), s),
        'ffn_w2': normal((ND, D_FF, D), DEEPNORM_BETA * D_FF ** -0.5),
        'moe_router': normal((NM, D, N_EXPERTS), s),
        'moe_w1': normal((NM, N_EXPERTS, D, D_FF_EXPERT), s),
        'moe_w3': normal((NM, N_EXPERTS, D, D_FF_EXPERT), s),
        'moe_w2': normal((NM, N_EXPERTS, D_FF_EXPERT, D), DEEPNORM_BETA * D_FF_EXPERT ** -0.5),
    }


def reference(x, c, ctx, c_ctx, w_mod, b_mod, ln_w, ln_b,
              rwkv_mu, rwkv_w_r, rwkv_w_k, rwkv_w_v, rwkv_w_o,
              rwkv_decay_w0, rwkv_decay_w1, rwkv_decay_w2,
              rwkv_iclr_a0, rwkv_iclr_a1, rwkv_iclr_a2,
              rwkv_gate_g1, rwkv_gate_g2, rwkv_k_k, rwkv_k_a, rwkv_r_k,
              rwkv_gn_w, rwkv_gn_b,
              pool_w, pool_scale,
              ffn_w1, ffn_w3, ffn_w2,
              moe_router, moe_w1, moe_w3, moe_w2):
    B, T, D = x.shape
    rows = T // GRID_W
    x = x + grid_position_embedding(rows, D).astype(x.dtype)[None]
    xc = ctx
    for i in range(DEPTH):
        mixer = i % N_MIXERS
        j = i // N_MIXERS
        f = i // 2
        ctx_later = any(l % N_MIXERS == 0 for l in range(i + 1, DEPTH))
        if i % 2 == 0:
            ffn = lambda t: swiglu(t, ffn_w1[f], ffn_w3[f], ffn_w2[f])
        else:
            ffn = lambda t: moe_swiglu(t, moe_router[f], moe_w1[f], moe_w3[f], moe_w2[f])

        sh1, sc1, g1, sh2, sc2, g2 = [t[:, None, :] for t in adaln(c, w_mod[i], b_mod[i])]
        h = x * (1.0 + sc1) + sh1
        if mixer == 0 or ctx_later:
            csh1, csc1, cg1, csh2, csc2, cg2 = adaln(c_ctx, w_mod[i], b_mod[i])
            hc = xc * (1.0 + csc1) + csh1
        if mixer == 0:
            y, yc = rwkv7_mixer(h, hc, rwkv_mu[j], rwkv_w_r[j], rwkv_w_k[j], rwkv_w_v[j], rwkv_w_o[j],
                                rwkv_decay_w0[j], rwkv_decay_w1[j], rwkv_decay_w2[j],
                                rwkv_iclr_a0[j], rwkv_iclr_a1[j], rwkv_iclr_a2[j],
                                rwkv_gate_g1[j], rwkv_gate_g2[j], rwkv_k_k[j], rwkv_k_a[j], rwkv_r_k[j],
                                rwkv_gn_w[j], rwkv_gn_b[j], ctx_out=ctx_later)
        else:
            y = multiscale_pool(h, pool_w[j], pool_scale[j])
            yc = multiscale_pool(hc, pool_w[j], pool_scale[j]) if ctx_later else None
        x = post_norm(x, g1 * y, ln_w[i, 0], ln_b[i, 0])
        h = x * (1.0 + sc2) + sh2
        x = post_norm(x, g2 * ffn(h), ln_w[i, 1], ln_b[i, 1])
        if ctx_later:
            xc = post_norm(xc, cg1 * yc, ln_w[i, 0], ln_b[i, 0])
            hc = xc * (1.0 + csc2) + csh2
            xc = post_norm(xc, cg2 * ffn(hc), ln_w[i, 1], ln_b[i, 1])
    return x
```

```python
import functools
import math

import numpy as np
import jax
import jax.numpy as jnp
from jax import lax
from jax.experimental import pallas as pl
from jax.experimental.pallas import tpu as pltpu

F32 = jnp.float32
BF16 = jnp.bfloat16

HEAD = 64
GROUP = 4 * HEAD
CHUNK = 64
LN_EPS = 1e-5
GN_EPS = 64e-5
POS_BASE = 10000.0
POOL_WINDOWS = (2, 4, 8, 16)
N_EXPERTS = 8
MOE_CHUNK = 1024
MOE_SUB = 256
VMEM_LIMIT_BYTES = 56 * 1024 * 1024


def _params(semantics):
    return pltpu.CompilerParams(dimension_semantics=semantics, vmem_limit_bytes=VMEM_LIMIT_BYTES)


def _tile(n, preferred):
    t = min(preferred, n)
    while n % t:
        t //= 2
    return t


def _dot(a, b):
    return jnp.dot(a, b, preferred_element_type=F32)


def _dot_nt(a, b):
    return lax.dot_general(a, b, (((1,), (1,)), ((), ())), preferred_element_type=F32)


def _dot_tn(a, b):
    return lax.dot_general(a, b, (((0,), (0,)), ((), ())), preferred_element_type=F32)


def _split2(x):
    hi = x.astype(BF16)
    lo = (x - hi.astype(F32)).astype(BF16)
    return hi, lo


def _split3(x):
    hi = x.astype(BF16)
    r1 = x - hi.astype(F32)
    mid = r1.astype(BF16)
    lo = (r1 - mid.astype(F32)).astype(BF16)
    return hi, mid, lo


def _layer_norm(z, w, b):
    mu = jnp.mean(z, axis=-1, keepdims=True)
    zc = z - mu
    var = jnp.mean(zc * zc, axis=-1, keepdims=True)
    return zc * lax.rsqrt(var + LN_EPS) * w + b


def _adaln_kernel(c_ref, w_ref, b_ref, o_ref):
    c = c_ref[...]
    a = (c * jax.nn.sigmoid(c)).astype(BF16)
    o_ref[0] = _dot(a, w_ref[0].astype(BF16)) + b_ref[0]


def _adaln(cond, w_mod, b_mod, tn=1024):
    depth, d, n = w_mod.shape
    tn = _tile(n, tn)
    rows = cond.shape[0]
    return pl.pallas_call(
        _adaln_kernel,
        out_shape=jax.ShapeDtypeStruct((depth, rows, n), F32),
        grid=(depth, n // tn),
        in_specs=[pl.BlockSpec((rows, d), lambda l, j: (0, 0)),
                  pl.BlockSpec((1, d, tn), lambda l, j: (l, 0, j)),
                  pl.BlockSpec((1, 1, tn), lambda l, j: (l, 0, j))],
        out_specs=pl.BlockSpec((1, rows, tn), lambda l, j: (l, 0, j)),
        compiler_params=_params(("parallel", "parallel")),
        name="adaln",
    )(cond, w_mod, b_mod.reshape(depth, 1, n))


def _mm_kernel(a_ref, w_ref, b_ref, o_ref, *, act):
    acc = _dot(a_ref[...], w_ref[...].astype(BF16)) + b_ref[...]
    if act == "tanh":
        acc = jnp.tanh(acc)
    elif act == "sigmoid":
        acc = jax.nn.sigmoid(acc)
    o_ref[...] = acc.astype(o_ref.dtype)


def _mm(a, w, bias=None, *, act=None, out_dtype=F32, tm=1024, tn=512):
    m, k = a.shape
    n = w.shape[1]
    tm = _tile(m, tm)
    tn = _tile(n, tn)
    if bias is None:
        bias = jnp.zeros((1, n), F32)
    return pl.pallas_call(
        functools.partial(_mm_kernel, act=act),
        out_shape=jax.ShapeDtypeStruct((m, n), out_dtype),
        grid=(m // tm, n // tn),
        in_specs=[pl.BlockSpec((tm, k), lambda i, j: (i, 0)),
                  pl.BlockSpec((k, tn), lambda i, j: (0, j)),
                  pl.BlockSpec((1, tn), lambda i, j: (0, j))],
        out_specs=pl.BlockSpec((tm, tn), lambda i, j: (i, j)),
        compiler_params=_params(("parallel", "arbitrary")),
        name="matmul",
    )(a, w, bias.reshape(1, n))


def _mix_kernel(ctx_ref, x_ref, xp_ref, xn_ref, pos_ref, pp_ref, pn_ref,
                sc_ref, sh_ref, csc_ref, csh_ref, mu_ref,
                o0, o1, o2, o3, o4, o5, *, n_lat_tiles):
    s = pl.program_id(1)
    is_ctx = s == 0
    tm = x_ref.shape[1]
    scale = jnp.where(is_ctx, csc_ref[0], sc_ref[0]) + 1.0
    shift = jnp.where(is_ctx, csh_ref[0], sh_ref[0])
    src = jnp.where(is_ctx, ctx_ref[0], x_ref[0] + pos_ref[...])
    h = src * scale + shift
    has_prev = s > 1
    has_next = jnp.logical_and(s >= 1, s < n_lat_tiles)
    h_prev = jnp.where(has_prev, (xp_ref[0] + pp_ref[...]) * scale + shift, 0.0)[7:8]
    h_next = jnp.where(has_next, (xn_ref[0] + pn_ref[...]) * scale + shift, 0.0)[0:1]
    row = lax.broadcasted_iota(jnp.int32, h.shape, 0)
    h_m1 = jnp.where(row == 0, h_prev, pltpu.roll(h, 1, axis=0))
    h_p1 = jnp.where(row == tm - 1, h_next, pltpu.roll(h, tm - 1, axis=0))
    xx = 0.5 * (h_m1 + h_p1) - h
    for n, o_ref in enumerate((o0, o1, o2, o3, o4, o5)):
        o_ref[0] = (h + xx * mu_ref[n:n + 1]).astype(o_ref.dtype)


def _rwkv_mix(ctx, x, pos, sc, sh, csc, csh, mu):
    b, t, d = x.shape
    tm = ctx.shape[1]
    assert t % tm == 0 and tm % 8 == 0
    n_lat = t // tm
    r8 = tm // 8
    lat = lambda bi, s: (bi, jnp.maximum(s - 1, 0), 0)
    prev8 = lambda bi, s: (bi, jnp.maximum((s - 1) * r8 - 1, 0), 0)
    next8 = lambda bi, s: (bi, jnp.minimum(jnp.maximum(s, 1) * r8, t // 8 - 1), 0)
    vec = pl.BlockSpec((1, 1, d), lambda bi, s: (bi, 0, 0))
    cvec = pl.BlockSpec((1, 1, d), lambda bi, s: (0, 0, 0))
    out_sds = jax.ShapeDtypeStruct((b, tm + t, d), BF16)
    return pl.pallas_call(
        functools.partial(_mix_kernel, n_lat_tiles=n_lat),
        out_shape=[out_sds] * 6,
        grid=(b, n_lat + 1),
        in_specs=[pl.BlockSpec((1, tm, d), lambda bi, s: (bi, 0, 0)),
                  pl.BlockSpec((1, tm, d), lat),
                  pl.BlockSpec((1, 8, d), prev8),
                  pl.BlockSpec((1, 8, d), next8),
                  pl.BlockSpec((tm, d), lambda bi, s: (jnp.maximum(s - 1, 0), 0)),
                  pl.BlockSpec((8, d), lambda bi, s: (jnp.maximum((s - 1) * r8 - 1, 0), 0)),
                  pl.BlockSpec((8, d), lambda bi, s: (jnp.minimum(jnp.maximum(s, 1) * r8, t // 8 - 1), 0)),
                  vec, vec, cvec, cvec,
                  pl.BlockSpec((6, d), lambda bi, s: (0, 0))],
        out_specs=[pl.BlockSpec((1, tm, d), lambda bi, s: (bi, s, 0))] * 6,
        compiler_params=_params(("parallel", "parallel")),
        name="rwkv_mix",
    )(ctx, x, x, x, pos, pos, pos, sc, sh, csc, csh, mu)


def _block_diag(x, bmask):
    xb = x.astype(BF16)
    return jnp.concatenate([xb, xb, xb, xb], axis=0) * bmask


def _fold_heads(full, bmask_f32):
    m = full * bmask_f32
    return m[0:HEAD] + m[HEAD:2 * HEAD] + m[2 * HEAD:3 * HEAD] + m[3 * HEAD:4 * HEAD]


SUB = 16


def _block_diag16(x, bmask16):
    xb = x.astype(BF16)
    return jnp.concatenate([xb] * (GROUP // SUB), axis=0) * bmask16


def _dot3_bd16(a, b, bmask16):
    rows = a.shape[0]
    a_hi, a_lo = _split2(a)
    b_hi, b_lo = _split2(b)
    main = _dot(jnp.concatenate([a_hi, a_lo], axis=0), _block_diag16(b_hi, bmask16))
    return main[:rows] + main[rows:] + _dot(a_hi, _block_diag16(b_lo, bmask16))


def _unit_triangular_inverse(l_mat, bmask, bmask16, eye16, diag16, off_a, off_b):
    nq = CHUNK // SUB
    l16 = l_mat[0:SUB] * diag16[0:SUB]
    for q in range(1, nq):
        l16 = l16 + l_mat[q * SUB:(q + 1) * SUB] * diag16[q * SUB:(q + 1) * SUB]
    t16 = eye16 + l16
    l_pow = _dot3_bd16(l16, l16, bmask16)
    for _ in range(2):
        both = _dot3_bd16(jnp.concatenate([t16, l_pow], axis=0), l_pow, bmask16)
        t16 = t16 + both[:SUB]
        l_pow = both[SUB:]
    t16 = t16 + _dot3_bd16(t16, l_pow, bmask16)
    d = jnp.concatenate([t16] * nq, axis=0) * diag16
    for off in (off_a, off_b):
        x = _dot(d.astype(BF16), _block_diag(l_mat * off, bmask))
        d = d + _dot(x.astype(BF16), _block_diag(d, bmask))
    return d


def _scan_prep_kernel(r_ref, k_ref, v_ref, zw_ref, za_ref, kk_ref, ka_ref,
                      bmask_ref, bmask16_ref, tri_ref, dm_ref, cm_ref, eye16_ref,
                      p_ref, rh_ref, q_ref, yl_ref):
    bmask = bmask_ref[...]
    bmask_f = bmask.astype(F32)
    eye = cm_ref[0]
    r = r_ref[0]
    k = k_ref[0]
    v = v_ref[0]
    lw = (-math.exp(-0.5)) * jax.nn.sigmoid(zw_ref[0])
    a = jax.nn.sigmoid(za_ref[0])
    kkr = k * kk_ref[...]
    sq_hi, sq_lo = _split2(kkr * kkr)
    ssq = _dot(sq_hi, bmask) + _dot(sq_lo, bmask)
    kk = kkr * lax.rsqrt(jnp.maximum(ssq, 1e-24))
    kd = k * (1.0 + (a - 1.0) * ka_ref[...])
    bb = kk * a
    tri = tri_ref[0]
    lw_hi, lw_lo = _split2(lw)
    g = _dot(tri, lw_hi) + _dot(tri, lw_lo)
    g_end = jnp.sum(lw, axis=0, keepdims=True)
    e_pos = jnp.exp(g)
    e_neg = jnp.exp(-g)
    e_end = jnp.exp(g_end - g)
    a_t = -kk * jnp.exp(g - lw)
    r_t = r * e_pos
    ar = jnp.concatenate([a_t, r_t], axis=0).astype(BF16)
    mb = _dot_nt(ar, _block_diag(bb * e_neg, bmask))
    mk = _dot_nt(ar, _block_diag(kd * e_neg, bmask))
    m_strict = dm_ref[0, 0]
    m_incl = dm_ref[0, 1]
    l_mat = mb[:CHUNK] * m_strict
    m_rb = (mb[CHUNK:] * m_incl).astype(BF16)
    m_ak = mk[:CHUNK] * m_strict
    m_rk = mk[CHUNK:] * m_incl
    t_mat = _unit_triangular_inverse(l_mat, bmask, bmask16_ref[...], eye16_ref[...],
                                     cm_ref[1], cm_ref[2], cm_ref[3])
    mv =_dot(jnp.concatenate([m_ak, m_rk], axis=0).astype(BF16), _block_diag(v, bmask))
    t_b = t_mat.astype(BF16)
    a_h = _dot(t_b, _block_diag(a_t, bmask))
    u_0 = _dot(t_b, _block_diag(mv[:CHUNK], bmask))
    a_hb = a_h.astype(BF16)
    rh_ref[0, 0, 0] = (r_t + _dot(m_rb, _block_diag(a_hb, bmask))).astype(rh_ref.dtype)
    yl_ref[0, 0, 0] = _dot(m_rb, _block_diag(u_0, bmask)) + mv[CHUNK:]
    b_h = (bb * e_end).astype(BF16)
    k_h = (kd * e_end).astype(BF16)
    p_full = _dot_tn(b_h, a_hb)
    q_full = _dot_tn(jnp.concatenate([b_h, k_h], axis=0),
                     jnp.concatenate([u_0, v], axis=0).astype(BF16))
    p_ref[0, 0, 0] = (_fold_heads(p_full, bmask_f) + eye * jnp.exp(g_end)).astype(p_ref.dtype)
    q_ref[0, 0, 0] = _fold_heads(q_full, bmask_f)


def _scan_consts():
    lane = np.arange(GROUP)
    bmask = (lane[:, None] // HEAD == lane[None, :] // HEAD).astype(np.float32)
    i = np.arange(CHUNK)[:, None]
    j = np.arange(CHUNK)[None, :]
    jl = (lane % HEAD)[None, :]
    tri = np.stack([(j <= i), (j >= i)]).astype(np.float32)
    dir_masks = np.stack([np.stack([(jl < i), (jl <= i)]),
                          np.stack([(jl > i), (jl >= i)])]).astype(np.float32)
    same16 = (jl // SUB == i // SUB)
    same32 = (jl // (2 * SUB) == i // (2 * SUB))
    common = np.stack([(jl == i), same16, same32 & ~same16, ~same32]).astype(np.float32)
    bmask16 = (lane[:, None] // SUB == lane[None, :] // SUB).astype(np.float32)
    eye16 = ((lane % SUB)[None, :] == np.arange(SUB)[:, None]).astype(np.float32)
    return (jnp.asarray(bmask, BF16), jnp.asarray(bmask16, BF16), jnp.asarray(tri, BF16),
            jnp.asarray(dir_masks, F32), jnp.asarray(common, F32), jnp.asarray(eye16, F32))


def _scan_prep(r, k, v, zw, za, k_k, k_a):
    b, l, d = r.shape
    nc = l // CHUNK
    ng = d // GROUP
    bmask, bmask16, tri, dir_masks, common, eye16 = _scan_consts()
    tok = pl.BlockSpec((1, CHUNK, GROUP), lambda di, bi, c, g: (bi, c, g))
    tok2 = pl.BlockSpec((1, CHUNK, GROUP), lambda di, bi, c, g: (bi, c, di * ng + g))
    par = pl.BlockSpec((1, GROUP), lambda di, bi, c, g: (0, g))
    out_spec = pl.BlockSpec((1, 1, 1, CHUNK, GROUP), lambda di, bi, c, g: (di, bi, c, 0, g))
    sds = lambda dt: jax.ShapeDtypeStruct((2, b, nc, CHUNK, d), dt)
    return pl.pallas_call(
        _scan_prep_kernel,
        out_shape=[sds(BF16), sds(BF16), sds(F32), sds(F32)],
        grid=(2, b, nc, ng),
        in_specs=[tok, tok, tok, tok2, tok2, par, par,
                  pl.BlockSpec((GROUP, GROUP), lambda di, bi, c, g: (0, 0)),
                  pl.BlockSpec((GROUP, GROUP), lambda di, bi, c, g: (0, 0)),
                  pl.BlockSpec((1, CHUNK, CHUNK), lambda di, bi, c, g: (di, 0, 0)),
                  pl.BlockSpec((1, 2, CHUNK, GROUP), lambda di, bi, c, g: (di, 0, 0, 0)),
                  pl.BlockSpec((4, CHUNK, GROUP), lambda di, bi, c, g: (0, 0, 0)),
                  pl.BlockSpec((SUB, GROUP), lambda di, bi, c, g: (0, 0))],
        out_specs=[out_spec] * 4,
        compiler_params=_params(("parallel",) * 4),
        name="scan_prep",
    )(r, k, v, zw, za, k_k.reshape(1, d), k_a.reshape(1, d), bmask, bmask16, tri, dir_masks, common, eye16)


def _scan_seq_kernel(p_ref, rh_ref, q_ref, yl_ref, bmask_ref, y_ref, s_ref):
    @pl.when(pl.program_id(2) == 0)
    def _():
        s_ref[...] = jnp.zeros_like(s_ref)

    bmask = bmask_ref[...]
    ng = s_ref.shape[1] // GROUP
    for g in range(ng):
        cols = slice(g * GROUP, (g + 1) * GROUP)
        s_bd = _block_diag(s_ref[:, cols], bmask)
        lhs = jnp.concatenate([p_ref[0, 0, 0, :, cols], rh_ref[0, 0, 0, :, cols]], axis=0)
        out = _dot(lhs, s_bd)
        s_ref[:, cols] = out[:CHUNK] + q_ref[0, 0, 0, :, cols]
        y_ref[0, 0, :, cols] = out[CHUNK:] + yl_ref[0, 0, 0, :, cols]


def _scan_seq(p, rh, q, yl, n_ctx_chunks):
    _, b, nc, _, d = p.shape
    bmask = _scan_consts()[0]

    def chunk_of(di, s):
        back = jnp.where(s < n_ctx_chunks, n_ctx_chunks - 1 - s, nc - 1 + n_ctx_chunks - s)
        return jnp.where(di == 0, s, back)

    spec = pl.BlockSpec((1, 1, 1, CHUNK, d), lambda di, bi, s: (di, bi, chunk_of(di, s), 0, 0))
    return pl.pallas_call(
        _scan_seq_kernel,
        out_shape=jax.ShapeDtypeStruct((2, b, nc * CHUNK, d), F32),
        grid=(2, b, nc),
        in_specs=[spec, spec, spec, spec, pl.BlockSpec((GROUP, GROUP), lambda di, bi, s: (0, 0))],
        out_specs=pl.BlockSpec((1, 1, CHUNK, d), lambda di, bi, s: (di, bi, chunk_of(di, s), 0)),
        scratch_shapes=[pltpu.VMEM((CHUNK, d), F32)],
        compiler_params=_params(("parallel", "parallel", "arbitrary")),
        name="scan_seq",
    )(p, rh, q, yl, bmask)


def _head_sum(x, ones_bd):
    hi, lo = _split2(x)
    return _dot(hi, ones_bd) + _dot(lo, ones_bd)


def _rwkv_post_kernel(y_ref, r_ref, k_ref, v_ref, za0_ref, za1_ref, gate_ref,
                      ka_ref, rk_ref, gnw_ref, gnb_ref, ones_ref, o_ref):
    ones_bd = ones_ref[...]
    y = y_ref[0, 0] + y_ref[1, 0]
    mean = _head_sum(y, ones_bd) * (1.0 / HEAD)
    yc = y - mean
    var = _head_sum(yc * yc, ones_bd) * (1.0 / HEAD)
    o = yc * lax.rsqrt(var + GN_EPS) * gnw_ref[...] + gnb_ref[...]
    a_sum = jax.nn.sigmoid(za0_ref[0]) + jax.nn.sigmoid(za1_ref[0])
    r = r_ref[0]
    k_sum = k_ref[0] * (2.0 + (a_sum - 2.0) * ka_ref[...])
    bonus = _head_sum(r * k_sum * rk_ref[...], ones_bd) * v_ref[0]
    o_ref[0] = ((o + bonus) * gate_ref[0]).astype(o_ref.dtype)


def _rwkv_post(y, r, k, v, za, gate, k_a, r_k, gn_w, gn_b, n_ctx, tm=256):
    _, b, l, d = y.shape
    t = l - n_ctx
    off = n_ctx // tm
    nl = d // GROUP
    ones_bd = _scan_consts()[0]
    tok = pl.BlockSpec((1, tm, GROUP), lambda bi, s, g: (bi, s + off, g))
    par = pl.BlockSpec((1, GROUP), lambda bi, s, g: (0, g))
    return pl.pallas_call(
        _rwkv_post_kernel,
        out_shape=jax.ShapeDtypeStruct((b, t, d), BF16),
        grid=(b, t // tm, nl),
        in_specs=[pl.BlockSpec((2, 1, tm, GROUP), lambda bi, s, g: (0, bi, s + off, g)),
                  tok, tok, tok,
                  pl.BlockSpec((1, tm, GROUP), lambda bi, s, g: (bi, s + off, g)),
                  pl.BlockSpec((1, tm, GROUP), lambda bi, s, g: (bi, s + off, nl + g)),
                  tok, par, par, par, par,
                  pl.BlockSpec((GROUP, GROUP), lambda bi, s, g: (0, 0))],
        out_specs=pl.BlockSpec((1, tm, GROUP), lambda bi, s, g: (bi, s, g)),
        compiler_params=_params(("parallel",) * 3),
        name="rwkv_post",
    )(y, r, k, v, za, za, gate, k_a.reshape(1, d), r_k.reshape(1, d), gn_w.reshape(1, d),
      gn_b.reshape(1, d), ones_bd)


def _proj_norm_kernel(a_ref, w_ref, x_ref, pos_ref, g_ref, lnw_ref, lnb_ref, o_ref, acc_ref, *, alpha):
    j = pl.program_id(1)
    nj = acc_ref.shape[0]
    tn = acc_ref.shape[2]
    acc_ref[j] = _dot(a_ref[...], w_ref[...].astype(BF16))

    @pl.when(j == nj - 1)
    def _():
        gate = g_ref[0]
        for jj in range(nj):
            cols = slice(jj * tn, (jj + 1) * tn)
            o_ref[:, cols] = alpha * (x_ref[:, cols] + pos_ref[:, cols]) + gate[:, cols] * acc_ref[jj]
        o_ref[...] = _layer_norm(o_ref[...], lnw_ref[...], lnb_ref[...])


def _proj_norm(a, w, x, pos, gate, ln_w, ln_b, alpha, tm=512, tn=512):
    m, k = a.shape
    d = w.shape[1]
    t = pos.shape[0]
    tm = _tile(t, tm)
    tn = _tile(d, tn)
    tpb = t // tm
    return pl.pallas_call(
        functools.partial(_proj_norm_kernel, alpha=alpha),
        out_shape=jax.ShapeDtypeStruct((m, d), F32),
        grid=(m // tm, d // tn),
        in_specs=[pl.BlockSpec((tm, k), lambda i, j: (i, 0)),
                  pl.BlockSpec((k, tn), lambda i, j: (0, j)),
                  pl.BlockSpec((tm, d), lambda i, j: (i, 0)),
                  pl.BlockSpec((tm, d), lambda i, j: (i % tpb, 0)),
                  pl.BlockSpec((1, 1, d), lambda i, j: (i // tpb, 0, 0)),
                  pl.BlockSpec((1, d), lambda i, j: (0, 0)),
                  pl.BlockSpec((1, d), lambda i, j: (0, 0))],
        out_specs=pl.BlockSpec((tm, d), lambda i, j: (i, 0)),
        scratch_shapes=[pltpu.VMEM((d // tn, tm, tn), F32)],
        compiler_params=_params(("parallel", "arbitrary")),
        name="proj_norm",
    )(a, w, x, pos, gate, ln_w.reshape(1, d), ln_b.reshape(1, d))


def _ffn_kernel(x_ref, sc_ref, sh_ref, g_ref, w1_ref, w3_ref, w2_ref, lnw_ref, lnb_ref,
                o_ref, h_ref, *, alpha):
    j = pl.program_id(1)

    @pl.when(j == 0)
    def _():
        h_ref[...] = (x_ref[...] * (1.0 + sc_ref[0]) + sh_ref[0]).astype(BF16)
        o_ref[...] = jnp.zeros_like(o_ref)

    h = h_ref[...]
    a1 = _dot(h, w1_ref[...].astype(BF16))
    a3 = _dot(h, w3_ref[...].astype(BF16))
    u = (a1 * jax.nn.sigmoid(a1) * a3).astype(BF16)
    o_ref[...] += _dot(u, w2_ref[...].astype(BF16))

    @pl.when(j == pl.num_programs(1) - 1)
    def _():
        z = alpha * x_ref[...] + g_ref[0] * o_ref[...]
        o_ref[...] = _layer_norm(z, lnw_ref[...], lnb_ref[...])


def _ffn(x, sc, sh, gate, w1, w3, w2, ln_w, ln_b, alpha, rows_per_batch, tm=512, tf=256):
    m, d = x.shape
    f = w1.shape[1]
    tm = _tile(rows_per_batch, tm)
    tf = _tile(f, tf)
    tpb = rows_per_batch // tm
    vec = pl.BlockSpec((1, 1, d), lambda i, j: (i // tpb, 0, 0))
    row = pl.BlockSpec((1, d), lambda i, j: (0, 0))
    return pl.pallas_call(
        functools.partial(_ffn_kernel, alpha=alpha),
        out_shape=jax.ShapeDtypeStruct((m, d), F32),
        grid=(m // tm, f // tf),
        in_specs=[pl.BlockSpec((tm, d), lambda i, j: (i, 0)), vec, vec, vec,
                  pl.BlockSpec((d, tf), lambda i, j: (0, j)),
                  pl.BlockSpec((d, tf), lambda i, j: (0, j)),
                  pl.BlockSpec((tf, d), lambda i, j: (j, 0)),
                  row, row],
        out_specs=pl.BlockSpec((tm, d), lambda i, j: (i, 0)),
        scratch_shapes=[pltpu.VMEM((tm, d), BF16)],
        compiler_params=_params(("parallel", "arbitrary")),
        name="ffn",
    )(x, sc, sh, gate, w1, w3, w2, ln_w.reshape(1, d), ln_b.reshape(1, d))


def _shift_down(x, s, row):
    return jnp.where(row >= s, pltpu.roll(x, s, axis=0), 0.0)


def _shift_up(x, s, row):
    t = x.shape[0]
    return jnp.where(row < t - s, pltpu.roll(x, t - s, axis=0), 0.0)


def _pool_kernel(x_ref, sc_ref, sh_ref, w_ref, scale_ref, o_ref):
    g = pl.program_id(0)
    t = x_ref.shape[1]
    h = x_ref[0] * (1.0 + sc_ref[0]) + sh_ref[0]
    row = lax.broadcasted_iota(jnp.int32, h.shape, 0)
    w = w_ref[0].astype(BF16)
    for gi, win in enumerate(POOL_WINDOWS):
        @pl.when(g == gi)
        def _(win=win):
            half = win // 2
            back = h
            fwd = h
            m = 1
            while m < half:
                back = back + _shift_down(back, m, row)
                fwd = fwd + _shift_up(fwd, m, row)
                m *= 2
            total = _shift_down(back, 1, row) + fwd
            count = (jnp.minimum(row + half, t) - jnp.maximum(row - half, 0)).astype(F32)
            pooled = (total / count - h).astype(BF16)
            o_ref[0] = _dot(pooled, w) * scale_ref[...]


def _pool(x, sc, sh, w_pool, scale):
    b, t, d = x.shape
    ng, p, _ = w_pool.shape
    vec = pl.BlockSpec((1, 1, p), lambda g, bi: (bi, 0, g))
    return pl.pallas_call(
        _pool_kernel,
        out_shape=jax.ShapeDtypeStruct((b, t, d), F32),
        grid=(ng, b),
        in_specs=[pl.BlockSpec((1, t, p), lambda g, bi: (bi, 0, g)), vec, vec,
                  pl.BlockSpec((1, p, p), lambda g, bi: (g, 0, 0)),
                  pl.BlockSpec((1, p), lambda g, bi: (0, g))],
        out_specs=pl.BlockSpec((1, t, p), lambda g, bi: (bi, 0, g)),
        compiler_params=_params(("parallel", "parallel")),
        name="pool",
    )(x, sc, sh, w_pool, scale.reshape(1, d))


def _route_kernel(x_ref, y_ref, g_ref, lnw_ref, lnb_ref, sc_ref, sh_ref, rt_ref,
                  xo_ref, h_ref, route_ref, *, alpha):
    x = _layer_norm(alpha * x_ref[...] + g_ref[0] * y_ref[...], lnw_ref[...], lnb_ref[...])
    xo_ref[...] = x
    h = x * (1.0 + sc_ref[0]) + sh_ref[0]
    h_ref[...] = h
    r1, r2, r3 = _split3(rt_ref[...])
    h1, h2, h3 = _split3(h)
    logits = (_dot_nt(r1, h1) + (_dot_nt(r1, h2) + _dot_nt(r2, h1))
              + (_dot_nt(r1, h3) + _dot_nt(r2, h2) + _dot_nt(r3, h1)))
    mx = jnp.max(logits, axis=0, keepdims=True)
    e = jnp.exp(logits - mx)
    p = e / jnp.sum(e, axis=0, keepdims=True)
    idx = lax.broadcasted_iota(jnp.int32, p.shape, 0)
    p1 = jnp.max(p, axis=0, keepdims=True)
    i1 = jnp.min(jnp.where(p == p1, idx, N_EXPERTS), axis=0, keepdims=True)
    rest = jnp.where(idx == i1, -1.0, p)
    p2 = jnp.max(rest, axis=0, keepdims=True)
    i2 = jnp.min(jnp.where(rest == p2, idx, N_EXPERTS), axis=0, keepdims=True)
    den = p1 + p2
    out = jnp.where(idx == 0, i1.astype(F32), 0.0)
    out = jnp.where(idx == 1, i2.astype(F32), out)
    out = jnp.where(idx == 2, p1 / den, out)
    out = jnp.where(idx == 3, p2 / den, out)
    route_ref[...] = out


def _route(x, y, gate, ln_w, ln_b, sc, sh, router, alpha, rows_per_batch, tm=256):
    m, d = x.shape
    tpb = rows_per_batch // tm
    vec = pl.BlockSpec((1, 1, d), lambda i: (i // tpb, 0, 0))
    row = pl.BlockSpec((1, d), lambda i: (0, 0))
    tok = pl.BlockSpec((tm, d), lambda i: (i, 0))
    return pl.pallas_call(
        functools.partial(_route_kernel, alpha=alpha),
        out_shape=[jax.ShapeDtypeStruct((m, d), F32), jax.ShapeDtypeStruct((m, d), F32),
                   jax.ShapeDtypeStruct((N_EXPERTS, m), F32)],
        grid=(m // tm,),
        in_specs=[tok, tok, vec, row, row, vec, vec,
                  pl.BlockSpec((N_EXPERTS, d), lambda i: (0, 0))],
        out_specs=[tok, tok, pl.BlockSpec((N_EXPERTS, tm), lambda i: (0, i))],
        compiler_params=_params(("parallel",)),
        name="route",
    )(x, y, gate, ln_w.reshape(1, d), ln_b.reshape(1, d), sc, sh, router.T)


def _row_copy(src_hbm, dst_vmem, sem, src_row, dst_row):
    return pltpu.make_async_copy(src_hbm.at[pl.ds(src_row, 1)], dst_vmem.at[pl.ds(dst_row, 1)], sem)


def _gather_kernel(idx_ref, valid_ref, src_ref, o_ref, buf_ref, sem):
    i = pl.program_id(0)
    rows = buf_ref.shape[0]

    @pl.when(valid_ref[i] > 0)
    def _():
        def start(r, carry):
            _row_copy(src_ref, buf_ref, sem, idx_ref[i * rows + r], r).start()
            return carry

        def wait(r, carry):
            _row_copy(src_ref, buf_ref, sem, 0, r).wait()
            return carry

        lax.fori_loop(0, rows, start, 0)
        lax.fori_loop(0, rows, wait, 0)
        o_ref[...] = buf_ref[...].astype(o_ref.dtype)

    @pl.when(valid_ref[i] == 0)
    def _():
        o_ref[...] = jnp.zeros_like(o_ref)


def _gather_rows(src, idx, tile_valid, rows=MOE_SUB):
    d = src.shape[1]
    n = idx.shape[0]
    return pl.pallas_call(
        _gather_kernel,
        out_shape=jax.ShapeDtypeStruct((n, d), BF16),
        grid_spec=pltpu.PrefetchScalarGridSpec(
            num_scalar_prefetch=2,
            grid=(n // rows,),
            in_specs=[pl.BlockSpec(memory_space=pl.ANY)],
            out_specs=pl.BlockSpec((rows, d), lambda i, idx_r, val_r: (i, 0)),
            scratch_shapes=[pltpu.VMEM((rows, d), src.dtype), pltpu.SemaphoreType.DMA]),
        compiler_params=_params(("arbitrary",)),
        name="gather_rows",
    )(idx, tile_valid, src)


def _moe_kernel(ce_ref, nv_ref, h_ref, w1_ref, w3_ref, w2_ref, o_ref, w1b, w3b, w2b):
    c = pl.program_id(0)
    j = pl.program_id(1)
    nv = nv_ref[c]

    @pl.when(j == 0)
    def _():
        o_ref[...] = jnp.zeros_like(o_ref)

    @pl.when(nv > 0)
    def _():
        w1b[...] = w1_ref[0].astype(BF16)
        w3b[...] = w3_ref[0].astype(BF16)
        w2b[...] = w2_ref[0].astype(BF16)
        for s in range(h_ref.shape[0] // MOE_SUB):
            @pl.when(s * MOE_SUB < nv)
            def _(s=s):
                rows = slice(s * MOE_SUB, (s + 1) * MOE_SUB)
                h = h_ref[rows, :]
                a1 = _dot(h, w1b[...])
                a3 = _dot(h, w3b[...])
                u = (a1 * jax.nn.sigmoid(a1) * a3).astype(BF16)
                o_ref[rows, :] += _dot(u, w2b[...])


def _moe_ffn(hs, chunk_expert, chunk_valid, w1, w3, w2, tf=256):
    n, d = hs.shape
    f = w1.shape[2]
    tf = _tile(f, tf)
    nj = f // tf
    nchunk = n // MOE_CHUNK

    def jeff(c, j, nv):
        return jnp.where(nv[c] > 0, j, nj - 1)

    return pl.pallas_call(
        _moe_kernel,
        out_shape=jax.ShapeDtypeStruct((n, d), F32),
        grid_spec=pltpu.PrefetchScalarGridSpec(
            num_scalar_prefetch=2,
            grid=(nchunk, nj),
            in_specs=[pl.BlockSpec((MOE_CHUNK, d), lambda c, j, ce, nv: (c, 0)),
                      pl.BlockSpec((1, d, tf), lambda c, j, ce, nv: (ce[c], 0, jeff(c, j, nv))),
                      pl.BlockSpec((1, d, tf), lambda c, j, ce, nv: (ce[c], 0, jeff(c, j, nv))),
                      pl.BlockSpec((1, tf, d), lambda c, j, ce, nv: (ce[c], jeff(c, j, nv), 0))],
            out_specs=pl.BlockSpec((MOE_CHUNK, d), lambda c, j, ce, nv: (c, 0)),
            scratch_shapes=[pltpu.VMEM((d, tf), BF16), pltpu.VMEM((d, tf), BF16),
                            pltpu.VMEM((tf, d), BF16)]),
        compiler_params=_params(("arbitrary", "arbitrary")),
        name="moe_ffn",
    )(chunk_expert, chunk_valid, hs, w1, w3, w2)


def _combine_kernel(p0_ref, p1_ref, ys_ref, x_ref, gates_ref, g_ref, lnw_ref, lnb_ref,
                    o_ref, b0_ref, b1_ref, sem, *, alpha):
    i = pl.program_id(0)
    rows = b0_ref.shape[0]

    def start(r, carry):
        _row_copy(ys_ref, b0_ref, sem, p0_ref[i * rows + r], r).start()
        _row_copy(ys_ref, b1_ref, sem, p1_ref[i * rows + r], r).start()
        return carry

    def wait(r, carry):
        _row_copy(ys_ref, b0_ref, sem, 0, r).wait()
        _row_copy(ys_ref, b1_ref, sem, 0, r).wait()
        return carry

    lax.fori_loop(0, rows, start, 0)
    lax.fori_loop(0, rows, wait, 0)
    gates = gates_ref[...]
    y = gates[:, 0:1] * b0_ref[...] + gates[:, 1:2] * b1_ref[...]
    z = alpha * x_ref[...] + g_ref[0] * y
    o_ref[...] = _layer_norm(z, lnw_ref[...], lnb_ref[...])


def _combine(ys, pos0, pos1, gates, x, gate_vec, ln_w, ln_b, alpha, rows_per_batch, rows=256):
    m, d = x.shape
    tpb = rows_per_batch // rows
    tok = lambda i, a, b: (i, 0)
    return pl.pallas_call(
        functools.partial(_combine_kernel, alpha=alpha),
        out_shape=jax.ShapeDtypeStruct((m, d), F32),
        grid_spec=pltpu.PrefetchScalarGridSpec(
            num_scalar_prefetch=2,
            grid=(m // rows,),
            in_specs=[pl.BlockSpec(memory_space=pl.ANY),
                      pl.BlockSpec((rows, d), tok),
                      pl.BlockSpec((rows, 2), tok),
                      pl.BlockSpec((1, 1, d), lambda i, a, b: (i // tpb, 0, 0)),
                      pl.BlockSpec((1, d), lambda i, a, b: (0, 0)),
                      pl.BlockSpec((1, d), lambda i, a, b: (0, 0))],
            out_specs=pl.BlockSpec((rows, d), tok),
            scratch_shapes=[pltpu.VMEM((rows, d), F32), pltpu.VMEM((rows, d), F32),
                            pltpu.SemaphoreType.DMA]),
        compiler_params=_params(("arbitrary",)),
        name="moe_combine",
    )(pos0, pos1, ys, x, gates, gate_vec, ln_w.reshape(1, d), ln_b.reshape(1, d))


def _routing_tables(route, n_rows_padded):
    n = route.shape[1]
    experts = jnp.concatenate([route[0], route[1]]).astype(jnp.int32)
    onehot = (experts[:, None] == jnp.arange(N_EXPERTS, dtype=jnp.int32)[None, :]).astype(jnp.int32)
    csum = jnp.cumsum(onehot, axis=0)
    rank = jnp.sum((csum - 1) * onehot, axis=1)
    counts = csum[-1]
    chunks_e = (counts + MOE_CHUNK - 1) // MOE_CHUNK
    chunk_end = jnp.cumsum(chunks_e)
    chunk_start = chunk_end - chunks_e
    dest = (chunk_start * MOE_CHUNK)[experts] + rank
    token = jnp.arange(2 * n, dtype=jnp.int32) % n
    src_idx = jnp.zeros((n_rows_padded,), jnp.int32).at[dest].set(token)
    nchunk = n_rows_padded // MOE_CHUNK
    cid = jnp.arange(nchunk, dtype=jnp.int32)
    used = cid < chunk_end[-1]
    last_used = jnp.maximum(chunk_end[-1] - 1, 0)
    ce = jnp.searchsorted(chunk_end, jnp.minimum(cid, last_used), side="right").astype(jnp.int32)
    ce = jnp.minimum(ce, N_EXPERTS - 1)
    nvalid = jnp.clip(counts[ce] - (cid - chunk_start[ce]) * MOE_CHUNK, 0, MOE_CHUNK)
    nvalid = jnp.where(used, nvalid, 0).astype(jnp.int32)
    sub = jnp.arange(n_rows_padded // MOE_SUB, dtype=jnp.int32)
    tile_valid = ((sub % (MOE_CHUNK // MOE_SUB)) * MOE_SUB < nvalid[sub // (MOE_CHUNK // MOE_SUB)])
    return src_idx, tile_valid.astype(jnp.int32), ce, nvalid, dest[:n], dest[n:]


def _position_embedding(rows, width, d):
    quarter = d // 4
    omega = 1.0 / (POS_BASE ** (jnp.arange(quarter, dtype=F32) / quarter))
    row = jnp.repeat(jnp.arange(rows, dtype=F32), width)
    col = jnp.tile(jnp.arange(width, dtype=F32), rows)
    ar = row[:, None] * omega[None, :]
    ac = col[:, None] * omega[None, :]
    return jnp.concatenate([jnp.sin(ar), jnp.cos(ar), jnp.sin(ac), jnp.cos(ac)], axis=-1)


def _block_diag2(w):
    z = jnp.zeros_like(w[0])
    return jnp.concatenate([jnp.concatenate([w[0], z], axis=1), jnp.concatenate([z, w[1]], axis=1)], axis=0)


def kernel(x, c, ctx, c_ctx, w_mod, b_mod, ln_w, ln_b, rwkv_mu, rwkv_w_r, rwkv_w_k, rwkv_w_v, rwkv_w_o, rwkv_decay_w0, rwkv_decay_w1, rwkv_decay_w2, rwkv_iclr_a0, rwkv_iclr_a1, rwkv_iclr_a2, rwkv_gate_g1, rwkv_gate_g2, rwkv_k_k, rwkv_k_a, rwkv_r_k, rwkv_gn_w, rwkv_gn_b, pool_w, pool_scale, ffn_w1, ffn_w3, ffn_w2, moe_router, moe_w1, moe_w3, moe_w2):
    b, t, d = x.shape
    n_ctx = ctx.shape[1]
    depth = w_mod.shape[0]
    assert depth == 2 and rwkv_mu.shape[0] == 1 and pool_w.shape[0] == 1
    alpha = (2.0 * depth) ** 0.25
    grid_w = 64
    l = n_ctx + t
    n = b * t

    cond = jnp.zeros((8, d), F32).at[:b].set(c).at[b].set(c_ctx)
    mod = _adaln(cond, w_mod, b_mod).reshape(depth, 8, 6, d)
    lat = lambda layer, which: mod[layer, :b, which].reshape(b, 1, d)
    cvec = lambda layer, which: mod[layer, b, which].reshape(1, 1, d)

    pos = _position_embedding(t // grid_w, grid_w, d)

    xr, xw, xk, xv, xa, xg = _rwkv_mix(ctx, x, pos, lat(0, 1), lat(0, 0), cvec(0, 1), cvec(0, 0), rwkv_mu[0])
    flat = lambda a: a.reshape(b * l, a.shape[-1])
    r = _mm(flat(xr), rwkv_w_r[0])
    k = _mm(flat(xk), rwkv_w_k[0])
    v = _mm(flat(xv), rwkv_w_v[0])
    dw1 = jnp.concatenate([rwkv_decay_w1[0, 0], rwkv_decay_w1[0, 1]], axis=1)
    ia1 = jnp.concatenate([rwkv_iclr_a1[0, 0], rwkv_iclr_a1[0, 1]], axis=1)
    lora_w = _mm(flat(xw), dw1, act="tanh", out_dtype=BF16)
    zw = _mm(lora_w, _block_diag2(rwkv_decay_w2[0]), rwkv_decay_w0[0].reshape(1, 2 * d))
    lora_a = _mm(flat(xa), ia1, out_dtype=BF16)
    za = _mm(lora_a, _block_diag2(rwkv_iclr_a2[0]), rwkv_iclr_a0[0].reshape(1, 2 * d))
    lora_g = _mm(flat(xg), rwkv_gate_g1[0], act="sigmoid", out_dtype=BF16)
    gate = _mm(lora_g, rwkv_gate_g2[0])
    seq = lambda a: a.reshape(b, l, a.shape[-1])
    p_c, rh_c, q_c, yl_c = _scan_prep(seq(r), seq(k), seq(v), seq(zw), seq(za), rwkv_k_k[0], rwkv_k_a[0])
    y_scan = _scan_seq(p_c, rh_c, q_c, yl_c, n_ctx // CHUNK)
    og = _rwkv_post(y_scan, seq(r), seq(k), seq(v), seq(za), seq(gate), rwkv_k_a[0],
                    rwkv_r_k[0].reshape(d), rwkv_gn_w[0], rwkv_gn_b[0], n_ctx)
    x1 = _proj_norm(og.reshape(n, d), rwkv_w_o[0], x.reshape(n, d), pos, lat(0, 2),
                    ln_w[0, 0], ln_b[0, 0], alpha)
    x2 = _ffn(x1, lat(0, 4), lat(0, 3), lat(0, 5), ffn_w1[0], ffn_w3[0], ffn_w2[0],
              ln_w[0, 1], ln_b[0, 1], alpha, t)

    y_pool = _pool(x2.reshape(b, t, d), lat(1, 1), lat(1, 0), pool_w[0], pool_scale[0])
    x3, h3, route = _route(x2, y_pool.reshape(n, d), lat(1, 2), ln_w[1, 0], ln_b[1, 0],
                           lat(1, 4), lat(1, 3), moe_router[0], alpha, t)
    n_pad = ((2 * n + N_EXPERTS * (MOE_CHUNK - 1)) // MOE_CHUNK) * MOE_CHUNK
    src_idx, tile_valid, chunk_expert, chunk_valid, pos0, pos1 = _routing_tables(route, n_pad)
    hs = _gather_rows(h3, src_idx, tile_valid)
    ys = _moe_ffn(hs, chunk_expert, chunk_valid, moe_w1[0], moe_w3[0], moe_w2[0])
    gates = jnp.stack([route[2], route[3]], axis=1)
    out = _combine(ys, pos0, pos1, gates, x3, lat(1, 5), ln_w[1, 1], ln_b[1, 1], alpha, t)
    return out.reshape(b, t, d)
```

```python
import functools
import math

import numpy as np
import jax
import jax.numpy as jnp
from jax import lax
from jax.experimental import pallas as pl
from jax.experimental.pallas import tpu as pltpu

F32 = jnp.float32
BF16 = jnp.bfloat16

HEAD = 64
GROUP = 4 * HEAD
CHUNK = 64
LN_EPS = 1e-5
GN_EPS = 64e-5
POS_BASE = 10000.0
POOL_WINDOWS = (2, 4, 8, 16)
N_EXPERTS = 8
MOE_CHUNK = 1024
MOE_SUB = 256
VMEM_LIMIT_BYTES = 56 * 1024 * 1024


def _params(semantics):
    return pltpu.CompilerParams(dimension_semantics=semantics, vmem_limit_bytes=VMEM_LIMIT_BYTES)


def _tile(n, preferred):
    t = min(preferred, n)
    while n % t:
        t //= 2
    return t


def _dot(a, b):
    return jnp.dot(a, b, preferred_element_type=F32)


def _dot_nt(a, b):
    return lax.dot_general(a, b, (((1,), (1,)), ((), ())), preferred_element_type=F32)


def _dot_tn(a, b):
    return lax.dot_general(a, b, (((0,), (0,)), ((), ())), preferred_element_type=F32)


def _split2(x):
    hi = x.astype(BF16)
    lo = (x - hi.astype(F32)).astype(BF16)
    return hi, lo


def _split3(x):
    hi = x.astype(BF16)
    r1 = x - hi.astype(F32)
    mid = r1.astype(BF16)
    lo = (r1 - mid.astype(F32)).astype(BF16)
    return hi, mid, lo


def _layer_norm(z, w, b):
    mu = jnp.mean(z, axis=-1, keepdims=True)
    zc = z - mu
    var = jnp.mean(zc * zc, axis=-1, keepdims=True)
    return zc * lax.rsqrt(var + LN_EPS) * w + b


def _adaln_kernel(c_ref, w_ref, b_ref, o_ref):
    c = c_ref[...]
    a = (c * jax.nn.sigmoid(c)).astype(BF16)
    o_ref[0] = _dot(a, w_ref[0].astype(BF16)) + b_ref[0]


def _adaln(cond, w_mod, b_mod, tn=1024):
    depth, d, n = w_mod.shape
    tn = _tile(n, tn)
    rows = cond.shape[0]
    return pl.pallas_call(
        _adaln_kernel,
        out_shape=jax.ShapeDtypeStruct((depth, rows, n), F32),
        grid=(depth, n // tn),
        in_specs=[pl.BlockSpec((rows, d), lambda l, j: (0, 0)),
                  pl.BlockSpec((1, d, tn), lambda l, j: (l, 0, j)),
                  pl.BlockSpec((1, 1, tn), lambda l, j: (l, 0, j))],
        out_specs=pl.BlockSpec((1, rows, tn), lambda l, j: (l, 0, j)),
        compiler_params=_params(("parallel", "parallel")),
        name="adaln",
    )(cond, w_mod, b_mod.reshape(depth, 1, n))


def _mm_kernel(a_ref, w_ref, b_ref, o_ref, *, act):
    acc = _dot(a_ref[...], w_ref[...].astype(BF16)) + b_ref[...]
    if act == "tanh":
        acc = jnp.tanh(acc)
    elif act == "sigmoid":
        acc = jax.nn.sigmoid(acc)
    o_ref[...] = acc.astype(o_ref.dtype)


def _mm(a, w, bias=None, *, act=None, out_dtype=F32, tm=1024, tn=512):
    m, k = a.shape
    n = w.shape[1]
    tm = _tile(m, tm)
    tn = _tile(n, tn)
    if bias is None:
        bias = jnp.zeros((1, n), F32)
    return pl.pallas_call(
        functools.partial(_mm_kernel, act=act),
        out_shape=jax.ShapeDtypeStruct((m, n), out_dtype),
        grid=(m // tm, n // tn),
        in_specs=[pl.BlockSpec((tm, k), lambda i, j: (i, 0)),
                  pl.BlockSpec((k, tn), lambda i, j: (0, j)),
                  pl.BlockSpec((1, tn), lambda i, j: (0, j))],
        out_specs=pl.BlockSpec((tm, tn), lambda i, j: (i, j)),
        compiler_params=_params(("parallel", "arbitrary")),
        name="matmul",
    )(a, w, bias.reshape(1, n))


def _mix_kernel(ctx_ref, x_ref, xp_ref, xn_ref, pos_ref, pp_ref, pn_ref,
                sc_ref, sh_ref, csc_ref, csh_ref, mu_ref,
                o0, o1, o2, o3, o4, o5, *, n_lat_tiles):
    s = pl.program_id(1)
    is_ctx = s == 0
    tm = x_ref.shape[1]
    scale = jnp.where(is_ctx, csc_ref[0], sc_ref[0]) + 1.0
    shift = jnp.where(is_ctx, csh_ref[0], sh_ref[0])
    src = jnp.where(is_ctx, ctx_ref[0], x_ref[0] + pos_ref[...])
    h = src * scale + shift
    has_prev = s > 1
    has_next = jnp.logical_and(s >= 1, s < n_lat_tiles)
    h_prev = jnp.where(has_prev, (xp_ref[0] + pp_ref[...]) * scale + shift, 0.0)[7:8]
    h_next = jnp.where(has_next, (xn_ref[0] + pn_ref[...]) * scale + shift, 0.0)[0:1]
    row = lax.broadcasted_iota(jnp.int32, h.shape, 0)
    h_m1 = jnp.where(row == 0, h_prev, pltpu.roll(h, 1, axis=0))
    h_p1 = jnp.where(row == tm - 1, h_next, pltpu.roll(h, tm - 1, axis=0))
    xx = 0.5 * (h_m1 + h_p1) - h
    for n, o_ref in enumerate((o0, o1, o2, o3, o4, o5)):
        o_ref[0] = (h + xx * mu_ref[n:n + 1]).astype(o_ref.dtype)


def _rwkv_mix(ctx, x, pos, sc, sh, csc, csh, mu):
    b, t, d = x.shape
    tm = ctx.shape[1]
    assert t % tm == 0 and tm % 8 == 0
    n_lat = t // tm
    r8 = tm // 8
    lat = lambda bi, s: (bi, jnp.maximum(s - 1, 0), 0)
    prev8 = lambda bi, s: (bi, jnp.maximum((s - 1) * r8 - 1, 0), 0)
    next8 = lambda bi, s: (bi, jnp.minimum(jnp.maximum(s, 1) * r8, t // 8 - 1), 0)
    vec = pl.BlockSpec((1, 1, d), lambda bi, s: (bi, 0, 0))
    cvec = pl.BlockSpec((1, 1, d), lambda bi, s: (0, 0, 0))
    out_sds = jax.ShapeDtypeStruct((b, tm + t, d), BF16)
    return pl.pallas_call(
        functools.partial(_mix_kernel, n_lat_tiles=n_lat),
        out_shape=[out_sds] * 6,
        grid=(b, n_lat + 1),
        in_specs=[pl.BlockSpec((1, tm, d), lambda bi, s: (bi, 0, 0)),
                  pl.BlockSpec((1, tm, d), lat),
                  pl.BlockSpec((1, 8, d), prev8),
                  pl.BlockSpec((1, 8, d), next8),
                  pl.BlockSpec((tm, d), lambda bi, s: (jnp.maximum(s - 1, 0), 0)),
                  pl.BlockSpec((8, d), lambda bi, s: (jnp.maximum((s - 1) * r8 - 1, 0), 0)),
                  pl.BlockSpec((8, d), lambda bi, s: (jnp.minimum(jnp.maximum(s, 1) * r8, t // 8 - 1), 0)),
                  vec, vec, cvec, cvec,
                  pl.BlockSpec((6, d), lambda bi, s: (0, 0))],
        out_specs=[pl.BlockSpec((1, tm, d), lambda bi, s: (bi, s, 0))] * 6,
        compiler_params=_params(("parallel", "parallel")),
        name="rwkv_mix",
    )(ctx, x, x, x, pos, pos, pos, sc, sh, csc, csh, mu)


def _block_diag(x, bmask):
    xb = x.astype(BF16)
    return jnp.concatenate([xb, xb, xb, xb], axis=0) * bmask


def _fold_heads(full, bmask_f32):
    m = full * bmask_f32
    return m[0:HEAD] + m[HEAD:2 * HEAD] + m[2 * HEAD:3 * HEAD] + m[3 * HEAD:4 * HEAD]


SUB = 16


def _block_diag16(x, bmask16):
    xb = x.astype(BF16)
    return jnp.concatenate([xb] * (GROUP // SUB), axis=0) * bmask16


def _dot3_bd16(a, b, bmask16):
    rows = a.shape[0]
    a_hi, a_lo = _split2(a)
    b_hi, b_lo = _split2(b)
    main = _dot(jnp.concatenate([a_hi, a_lo], axis=0), _block_diag16(b_hi, bmask16))
    return main[:rows] + main[rows:] + _dot(a_hi, _block_diag16(b_lo, bmask16))


def _unit_triangular_inverse(l_mats, bmask, bmask16, eye16, diag16, off_a, off_b):
    nq = CHUNK // SUB
    l16 = [sum(l[q * SUB:(q + 1) * SUB] * diag16[q * SUB:(q + 1) * SUB] for q in range(nq)) for l in l_mats]
    t16 = [eye16 + x for x in l16]
    l_pow = [_dot3_bd16(x, x, bmask16) for x in l16]
    for _ in range(2):
        both = [_dot3_bd16(jnp.concatenate([t, lp], axis=0), lp, bmask16) for t, lp in zip(t16, l_pow)]
        t16 = [t + bo[:SUB] for t, bo in zip(t16, both)]
        l_pow = [bo[SUB:] for bo in both]
    t16 = [t + _dot3_bd16(t, lp, bmask16) for t, lp in zip(t16, l_pow)]
    d = [jnp.concatenate([t] * nq, axis=0) * diag16 for t in t16]
    for off in (off_a, off_b):
        x = [_dot(di.astype(BF16), _block_diag(l * off, bmask)) for di, l in zip(d, l_mats)]
        d = [di + _dot(xi.astype(BF16), _block_diag(di, bmask)) for di, xi in zip(d, x)]
    return d


def _scan_prep_kernel(r_ref, k_ref, v_ref, zw_ref, za_ref, kk_ref, ka_ref,
                      bmask_ref, bmask16_ref, tri_ref, dm_ref, cm_ref, eye16_ref,
                      p_ref, rh_ref, q_ref, yl_ref):
    ng = r_ref.shape[2] // GROUP
    bmask = bmask_ref[...]
    bmask_f = bmask.astype(F32)
    eye = cm_ref[0]
    m_strict = dm_ref[0, 0]
    m_incl = dm_ref[0, 1]
    groups = lambda x: [x[:, g * GROUP:(g + 1) * GROUP] for g in range(ng)]
    bd = lambda x: _block_diag(x, bmask)
    stack = lambda x, y: jnp.concatenate([x, y], axis=0)

    r = r_ref[0]
    k = k_ref[0]
    v = v_ref[0]
    lw = (-math.exp(-0.5)) * jax.nn.sigmoid(zw_ref[0])
    a = jax.nn.sigmoid(za_ref[0])
    kkr = k * kk_ref[...]
    sq = jnp.concatenate(groups(kkr * kkr), axis=0)
    sq_hi, sq_lo = _split2(sq)
    ssq = _dot(sq_hi, bmask) + _dot(sq_lo, bmask)
    ssq = jnp.concatenate([ssq[g * CHUNK:(g + 1) * CHUNK] for g in range(ng)], axis=1)
    kk = kkr * lax.rsqrt(jnp.maximum(ssq, 1e-24))
    kd = k * (1.0 + (a - 1.0) * ka_ref[...])
    bb = kk * a
    tri = tri_ref[0]
    lw_hi, lw_lo = _split2(lw)
    g_cum = _dot(tri, lw_hi) + _dot(tri, lw_lo)
    g_end = jnp.sum(lw, axis=0, keepdims=True)
    e_neg = jnp.exp(-g_cum)
    e_end = jnp.exp(g_end - g_cum)
    a_t = groups(-kk * jnp.exp(g_cum - lw))
    r_t = groups(r * jnp.exp(g_cum))
    b_t = groups(bb * e_neg)
    k_t = groups(kd * e_neg)
    b_h = groups((bb * e_end).astype(BF16))
    k_h = groups((kd * e_end).astype(BF16))
    v_g = groups(v)
    decay_end = groups(jnp.exp(g_end))

    ar = [stack(x, y).astype(BF16) for x, y in zip(a_t, r_t)]
    mb = [_dot_nt(x, bd(y)) for x, y in zip(ar, b_t)]
    mk = [_dot_nt(x, bd(y)) for x, y in zip(ar, k_t)]
    l_mat = [x[:CHUNK] * m_strict for x in mb]
    m_rb = [(x[CHUNK:] * m_incl).astype(BF16) for x in mb]
    m_k = [stack(x[:CHUNK] * m_strict, x[CHUNK:] * m_incl).astype(BF16) for x in mk]
    mv = [_dot(x, bd(y)) for x, y in zip(m_k, v_g)]
    t_mat = _unit_triangular_inverse(l_mat, bmask, bmask16_ref[...], eye16_ref[...],
                                     cm_ref[1], cm_ref[2], cm_ref[3])
    t_b = [x.astype(BF16) for x in t_mat]
    a_h = [_dot(x, bd(y)).astype(BF16) for x, y in zip(t_b, a_t)]
    u_0 = [_dot(x, bd(y[:CHUNK])) for x, y in zip(t_b, mv)]
    rh = [x + _dot(m, bd(y)) for x, m, y in zip(r_t, m_rb, a_h)]
    yl = [_dot(m, bd(u)) + y[CHUNK:] for m, u, y in zip(m_rb, u_0, mv)]
    p_full = [_dot_tn(x, y) for x, y in zip(b_h, a_h)]
    q_full = [_dot_tn(stack(x, y), stack(u, w).astype(BF16)) for x, y, u, w in zip(b_h, k_h, u_0, v_g)]
    for g in range(ng):
        cols = slice(g * GROUP, (g + 1) * GROUP)
        p_ref[0, 0, 0, :, cols] = (_fold_heads(p_full[g], bmask_f) + eye * decay_end[g]).astype(p_ref.dtype)
        rh_ref[0, 0, 0, :, cols] = rh[g].astype(rh_ref.dtype)
        q_ref[0, 0, 0, :, cols] = _fold_heads(q_full[g], bmask_f)
        yl_ref[0, 0, 0, :, cols] = yl[g]


def _scan_consts():
    lane = np.arange(GROUP)
    bmask = (lane[:, None] // HEAD == lane[None, :] // HEAD).astype(np.float32)
    i = np.arange(CHUNK)[:, None]
    j = np.arange(CHUNK)[None, :]
    jl = (lane % HEAD)[None, :]
    tri = np.stack([(j <= i), (j >= i)]).astype(np.float32)
    dir_masks = np.stack([np.stack([(jl < i), (jl <= i)]),
                          np.stack([(jl > i), (jl >= i)])]).astype(np.float32)
    same16 = (jl // SUB == i // SUB)
    same32 = (jl // (2 * SUB) == i // (2 * SUB))
    common = np.stack([(jl == i), same16, same32 & ~same16, ~same32]).astype(np.float32)
    bmask16 = (lane[:, None] // SUB == lane[None, :] // SUB).astype(np.float32)
    eye16 = ((lane % SUB)[None, :] == np.arange(SUB)[:, None]).astype(np.float32)
    return (jnp.asarray(bmask, BF16), jnp.asarray(bmask16, BF16), jnp.asarray(tri, BF16),
            jnp.asarray(dir_masks, F32), jnp.asarray(common, F32), jnp.asarray(eye16, F32))


def _scan_prep(r, k, v, zw, za, k_k, k_a):
    b, l, d = r.shape
    nc = l // CHUNK
    ng = d // GROUP
    bmask, bmask16, tri, dir_masks, common, eye16 = _scan_consts()
    tok = pl.BlockSpec((1, CHUNK, d), lambda di, bi, c: (bi, c, 0))
    tok2 = pl.BlockSpec((1, CHUNK, d), lambda di, bi, c: (bi, c, di))
    par = pl.BlockSpec((1, d), lambda di, bi, c: (0, 0))
    out_spec = pl.BlockSpec((1, 1, 1, CHUNK, d), lambda di, bi, c: (di, bi, c, 0, 0))
    sds = lambda dt: jax.ShapeDtypeStruct((2, b, nc, CHUNK, d), dt)
    return pl.pallas_call(
        _scan_prep_kernel,
        out_shape=[sds(BF16), sds(BF16), sds(F32), sds(F32)],
        grid=(2, b, nc),
        in_specs=[tok, tok, tok, tok2, tok2, par, par,
                  pl.BlockSpec((GROUP, GROUP), lambda di, bi, c: (0, 0)),
                  pl.BlockSpec((GROUP, GROUP), lambda di, bi, c: (0, 0)),
                  pl.BlockSpec((1, CHUNK, CHUNK), lambda di, bi, c: (di, 0, 0)),
                  pl.BlockSpec((1, 2, CHUNK, GROUP), lambda di, bi, c: (di, 0, 0, 0)),
                  pl.BlockSpec((4, CHUNK, GROUP), lambda di, bi, c: (0, 0, 0)),
                  pl.BlockSpec((SUB, GROUP), lambda di, bi, c: (0, 0))],
        out_specs=[out_spec] * 4,
        compiler_params=_params(("parallel",) * 3),
        name="scan_prep",
    )(r, k, v, zw, za, k_k.reshape(1, d), k_a.reshape(1, d), bmask, bmask16, tri, dir_masks, common, eye16)


def _scan_seq_kernel(p_ref, rh_ref, q_ref, yl_ref, bmask_ref, y_ref, s_ref):
    @pl.when(pl.program_id(2) == 0)
    def _():
        s_ref[...] = jnp.zeros_like(s_ref)

    bmask = bmask_ref[...]
    ng = s_ref.shape[1] // GROUP
    for g in range(ng):
        cols = slice(g * GROUP, (g + 1) * GROUP)
        s_bd = _block_diag(s_ref[:, cols], bmask)
        lhs = jnp.concatenate([p_ref[0, 0, 0, :, cols], rh_ref[0, 0, 0, :, cols]], axis=0)
        out = _dot(lhs, s_bd)
        s_ref[:, cols] = out[:CHUNK] + q_ref[0, 0, 0, :, cols]
        y_ref[0, 0, :, cols] = out[CHUNK:] + yl_ref[0, 0, 0, :, cols]


def _scan_seq(p, rh, q, yl, n_ctx_chunks):
    _, b, nc, _, d = p.shape
    bmask = _scan_consts()[0]

    def chunk_of(di, s):
        back = jnp.where(s < n_ctx_chunks, n_ctx_chunks - 1 - s, nc - 1 + n_ctx_chunks - s)
        return jnp.where(di == 0, s, back)

    spec = pl.BlockSpec((1, 1, 1, CHUNK, d), lambda di, bi, s: (di, bi, chunk_of(di, s), 0, 0))
    return pl.pallas_call(
        _scan_seq_kernel,
        out_shape=jax.ShapeDtypeStruct((2, b, nc * CHUNK, d), F32),
        grid=(2, b, nc),
        in_specs=[spec, spec, spec, spec, pl.BlockSpec((GROUP, GROUP), lambda di, bi, s: (0, 0))],
        out_specs=pl.BlockSpec((1, 1, CHUNK, d), lambda di, bi, s: (di, bi, chunk_of(di, s), 0)),
        scratch_shapes=[pltpu.VMEM((CHUNK, d), F32)],
        compiler_params=_params(("parallel", "parallel", "arbitrary")),
        name="scan_seq",
    )(p, rh, q, yl, bmask)


def _head_sum(x, ones_bd):
    hi, lo = _split2(x)
    return _dot(hi, ones_bd) + _dot(lo, ones_bd)


def _rwkv_post_kernel(y_ref, r_ref, k_ref, v_ref, za0_ref, za1_ref, gate_ref,
                      ka_ref, rk_ref, gnw_ref, gnb_ref, ones_ref, o_ref):
    ones_bd = ones_ref[...]
    y = y_ref[0, 0] + y_ref[1, 0]
    mean = _head_sum(y, ones_bd) * (1.0 / HEAD)
    yc = y - mean
    var = _head_sum(yc * yc, ones_bd) * (1.0 / HEAD)
    o = yc * lax.rsqrt(var + GN_EPS) * gnw_ref[...] + gnb_ref[...]
    a_sum = jax.nn.sigmoid(za0_ref[0]) + jax.nn.sigmoid(za1_ref[0])
    r = r_ref[0]
    k_sum = k_ref[0] * (2.0 + (a_sum - 2.0) * ka_ref[...])
    bonus = _head_sum(r * k_sum * rk_ref[...], ones_bd) * v_ref[0]
    o_ref[0] = ((o + bonus) * gate_ref[0]).astype(o_ref.dtype)


def _rwkv_post(y, r, k, v, za, gate, k_a, r_k, gn_w, gn_b, n_ctx, tm=256):
    _, b, l, d = y.shape
    t = l - n_ctx
    off = n_ctx // tm
    nl = d // GROUP
    ones_bd = _scan_consts()[0]
    tok = pl.BlockSpec((1, tm, GROUP), lambda bi, s, g: (bi, s + off, g))
    par = pl.BlockSpec((1, GROUP), lambda bi, s, g: (0, g))
    return pl.pallas_call(
        _rwkv_post_kernel,
        out_shape=jax.ShapeDtypeStruct((b, t, d), BF16),
        grid=(b, t // tm, nl),
        in_specs=[pl.BlockSpec((2, 1, tm, GROUP), lambda bi, s, g: (0, bi, s + off, g)),
                  tok, tok, tok,
                  pl.BlockSpec((1, tm, GROUP), lambda bi, s, g: (bi, s + off, g)),
                  pl.BlockSpec((1, tm, GROUP), lambda bi, s, g: (bi, s + off, nl + g)),
                  tok, par, par, par, par,
                  pl.BlockSpec((GROUP, GROUP), lambda bi, s, g: (0, 0))],
        out_specs=pl.BlockSpec((1, tm, GROUP), lambda bi, s, g: (bi, s, g)),
        compiler_params=_params(("parallel",) * 3),
        name="rwkv_post",
    )(y, r, k, v, za, za, gate, k_a.reshape(1, d), r_k.reshape(1, d), gn_w.reshape(1, d),
      gn_b.reshape(1, d), ones_bd)


def _proj_norm_kernel(a_ref, w_ref, x_ref, pos_ref, g_ref, lnw_ref, lnb_ref, o_ref, acc_ref, *, alpha):
    j = pl.program_id(1)
    nj = acc_ref.shape[0]
    tn = acc_ref.shape[2]
    acc_ref[j] = _dot(a_ref[...], w_ref[...].astype(BF16))

    @pl.when(j == nj - 1)
    def _():
        gate = g_ref[0]
        for jj in range(nj):
            cols = slice(jj * tn, (jj + 1) * tn)
            o_ref[:, cols] = alpha * (x_ref[:, cols] + pos_ref[:, cols]) + gate[:, cols] * acc_ref[jj]
        o_ref[...] = _layer_norm(o_ref[...], lnw_ref[...], lnb_ref[...])


def _proj_norm(a, w, x, pos, gate, ln_w, ln_b, alpha, tm=512, tn=512):
    m, k = a.shape
    d = w.shape[1]
    t = pos.shape[0]
    tm = _tile(t, tm)
    tn = _tile(d, tn)
    tpb = t // tm
    return pl.pallas_call(
        functools.partial(_proj_norm_kernel, alpha=alpha),
        out_shape=jax.ShapeDtypeStruct((m, d), F32),
        grid=(m // tm, d // tn),
        in_specs=[pl.BlockSpec((tm, k), lambda i, j: (i, 0)),
                  pl.BlockSpec((k, tn), lambda i, j: (0, j)),
                  pl.BlockSpec((tm, d), lambda i, j: (i, 0)),
                  pl.BlockSpec((tm, d), lambda i, j: (i % tpb, 0)),
                  pl.BlockSpec((1, 1, d), lambda i, j: (i // tpb, 0, 0)),
                  pl.BlockSpec((1, d), lambda i, j: (0, 0)),
                  pl.BlockSpec((1, d), lambda i, j: (0, 0))],
        out_specs=pl.BlockSpec((tm, d), lambda i, j: (i, 0)),
        scratch_shapes=[pltpu.VMEM((d // tn, tm, tn), F32)],
        compiler_params=_params(("parallel", "arbitrary")),
        name="proj_norm",
    )(a, w, x, pos, gate, ln_w.reshape(1, d), ln_b.reshape(1, d))


def _ffn_kernel(x_ref, sc_ref, sh_ref, g_ref, w1_ref, w3_ref, w2_ref, lnw_ref, lnb_ref,
                o_ref, h_ref, *, alpha):
    j = pl.program_id(1)

    @pl.when(j == 0)
    def _():
        h_ref[...] = (x_ref[...] * (1.0 + sc_ref[0]) + sh_ref[0]).astype(BF16)
        o_ref[...] = jnp.zeros_like(o_ref)

    h = h_ref[...]
    a1 = _dot(h, w1_ref[...].astype(BF16))
    a3 = _dot(h, w3_ref[...].astype(BF16))
    u = (a1 * jax.nn.sigmoid(a1) * a3).astype(BF16)
    o_ref[...] += _dot(u, w2_ref[...].astype(BF16))

    @pl.when(j == pl.num_programs(1) - 1)
    def _():
        z = alpha * x_ref[...] + g_ref[0] * o_ref[...]
        o_ref[...] = _layer_norm(z, lnw_ref[...], lnb_ref[...])


def _ffn(x, sc, sh, gate, w1, w3, w2, ln_w, ln_b, alpha, rows_per_batch, tm=512, tf=256):
    m, d = x.shape
    f = w1.shape[1]
    tm = _tile(rows_per_batch, tm)
    tf = _tile(f, tf)
    tpb = rows_per_batch // tm
    vec = pl.BlockSpec((1, 1, d), lambda i, j: (i // tpb, 0, 0))
    row = pl.BlockSpec((1, d), lambda i, j: (0, 0))
    return pl.pallas_call(
        functools.partial(_ffn_kernel, alpha=alpha),
        out_shape=jax.ShapeDtypeStruct((m, d), F32),
        grid=(m // tm, f // tf),
        in_specs=[pl.BlockSpec((tm, d), lambda i, j: (i, 0)), vec, vec, vec,
                  pl.BlockSpec((d, tf), lambda i, j: (0, j)),
                  pl.BlockSpec((d, tf), lambda i, j: (0, j)),
                  pl.BlockSpec((tf, d), lambda i, j: (j, 0)),
                  row, row],
        out_specs=pl.BlockSpec((tm, d), lambda i, j: (i, 0)),
        scratch_shapes=[pltpu.VMEM((tm, d), BF16)],
        compiler_params=_params(("parallel", "arbitrary")),
        name="ffn",
    )(x, sc, sh, gate, w1, w3, w2, ln_w.reshape(1, d), ln_b.reshape(1, d))


def _shift_down(x, s, row):
    return jnp.where(row >= s, pltpu.roll(x, s, axis=0), 0.0)


def _shift_up(x, s, row):
    t = x.shape[0]
    return jnp.where(row < t - s, pltpu.roll(x, t - s, axis=0), 0.0)


def _pool_kernel(x_ref, sc_ref, sh_ref, w_ref, scale_ref, o_ref):
    g = pl.program_id(0)
    t = x_ref.shape[1]
    h = x_ref[0] * (1.0 + sc_ref[0]) + sh_ref[0]
    row = lax.broadcasted_iota(jnp.int32, h.shape, 0)
    w = w_ref[0].astype(BF16)
    for gi, win in enumerate(POOL_WINDOWS):
        @pl.when(g == gi)
        def _(win=win):
            half = win // 2
            back = h
            fwd = h
            m = 1
            while m < half:
                back = back + _shift_down(back, m, row)
                fwd = fwd + _shift_up(fwd, m, row)
                m *= 2
            total = _shift_down(back, 1, row) + fwd
            count = (jnp.minimum(row + half, t) - jnp.maximum(row - half, 0)).astype(F32)
            pooled = (total / count - h).astype(BF16)
            o_ref[0] = _dot(pooled, w) * scale_ref[...]


def _pool(x, sc, sh, w_pool, scale):
    b, t, d = x.shape
    ng, p, _ = w_pool.shape
    vec = pl.BlockSpec((1, 1, p), lambda g, bi: (bi, 0, g))
    return pl.pallas_call(
        _pool_kernel,
        out_shape=jax.ShapeDtypeStruct((b, t, d), F32),
        grid=(ng, b),
        in_specs=[pl.BlockSpec((1, t, p), lambda g, bi: (bi, 0, g)), vec, vec,
                  pl.BlockSpec((1, p, p), lambda g, bi: (g, 0, 0)),
                  pl.BlockSpec((1, p), lambda g, bi: (0, g))],
        out_specs=pl.BlockSpec((1, t, p), lambda g, bi: (bi, 0, g)),
        compiler_params=_params(("parallel", "parallel")),
        name="pool",
    )(x, sc, sh, w_pool, scale.reshape(1, d))


def _route_kernel(x_ref, y_ref, g_ref, lnw_ref, lnb_ref, sc_ref, sh_ref, rt_ref,
                  xo_ref, h_ref, route_ref, *, alpha):
    x = _layer_norm(alpha * x_ref[...] + g_ref[0] * y_ref[...], lnw_ref[...], lnb_ref[...])
    xo_ref[...] = x
    h = x * (1.0 + sc_ref[0]) + sh_ref[0]
    h_ref[...] = h
    r1, r2, r3 = _split3(rt_ref[...])
    h1, h2, h3 = _split3(h)
    logits = (_dot_nt(r1, h1) + (_dot_nt(r1, h2) + _dot_nt(r2, h1))
              + (_dot_nt(r1, h3) + _dot_nt(r2, h2) + _dot_nt(r3, h1)))
    mx = jnp.max(logits, axis=0, keepdims=True)
    e = jnp.exp(logits - mx)
    p = e / jnp.sum(e, axis=0, keepdims=True)
    idx = lax.broadcasted_iota(jnp.int32, p.shape, 0)
    p1 = jnp.max(p, axis=0, keepdims=True)
    i1 = jnp.min(jnp.where(p == p1, idx, N_EXPERTS), axis=0, keepdims=True)
    rest = jnp.where(idx == i1, -1.0, p)
    p2 = jnp.max(rest, axis=0, keepdims=True)
    i2 = jnp.min(jnp.where(rest == p2, idx, N_EXPERTS), axis=0, keepdims=True)
    den = p1 + p2
    out = jnp.where(idx == 0, i1.astype(F32), 0.0)
    out = jnp.where(idx == 1, i2.astype(F32), out)
    out = jnp.where(idx == 2, p1 / den, out)
    out = jnp.where(idx == 3, p2 / den, out)
    route_ref[...] = out


def _route(x, y, gate, ln_w, ln_b, sc, sh, router, alpha, rows_per_batch, tm=256):
    m, d = x.shape
    tpb = rows_per_batch // tm
    vec = pl.BlockSpec((1, 1, d), lambda i: (i // tpb, 0, 0))
    row = pl.BlockSpec((1, d), lambda i: (0, 0))
    tok = pl.BlockSpec((tm, d), lambda i: (i, 0))
    return pl.pallas_call(
        functools.partial(_route_kernel, alpha=alpha),
        out_shape=[jax.ShapeDtypeStruct((m, d), F32), jax.ShapeDtypeStruct((m, d), F32),
                   jax.ShapeDtypeStruct((N_EXPERTS, m), F32)],
        grid=(m // tm,),
        in_specs=[tok, tok, vec, row, row, vec, vec,
                  pl.BlockSpec((N_EXPERTS, d), lambda i: (0, 0))],
        out_specs=[tok, tok, pl.BlockSpec((N_EXPERTS, tm), lambda i: (0, i))],
        compiler_params=_params(("parallel",)),
        name="route",
    )(x, y, gate, ln_w.reshape(1, d), ln_b.reshape(1, d), sc, sh, router.T)


def _row_copy(src_hbm, dst_vmem, sem, src_row, dst_row):
    return pltpu.make_async_copy(src_hbm.at[pl.ds(src_row, 1)], dst_vmem.at[pl.ds(dst_row, 1)], sem)


def _gather_kernel(idx_ref, valid_ref, src_ref, o_ref, buf_ref, sem):
    i = pl.program_id(0)
    rows = buf_ref.shape[0]

    @pl.when(valid_ref[i] > 0)
    def _():
        def start(r, carry):
            _row_copy(src_ref, buf_ref, sem, idx_ref[i * rows + r], r).start()
            return carry

        def wait(r, carry):
            _row_copy(src_ref, buf_ref, sem, 0, r).wait()
            return carry

        lax.fori_loop(0, rows, start, 0)
        lax.fori_loop(0, rows, wait, 0)
        o_ref[...] = buf_ref[...].astype(o_ref.dtype)

    @pl.when(valid_ref[i] == 0)
    def _():
        o_ref[...] = jnp.zeros_like(o_ref)


def _gather_rows(src, idx, tile_valid, rows=MOE_SUB):
    d = src.shape[1]
    n = idx.shape[0]
    return pl.pallas_call(
        _gather_kernel,
        out_shape=jax.ShapeDtypeStruct((n, d), BF16),
        grid_spec=pltpu.PrefetchScalarGridSpec(
            num_scalar_prefetch=2,
            grid=(n // rows,),
            in_specs=[pl.BlockSpec(memory_space=pl.ANY)],
            out_specs=pl.BlockSpec((rows, d), lambda i, idx_r, val_r: (i, 0)),
            scratch_shapes=[pltpu.VMEM((rows, d), src.dtype), pltpu.SemaphoreType.DMA]),
        compiler_params=_params(("arbitrary",)),
        name="gather_rows",
    )(idx, tile_valid, src)


def _moe_kernel(ce_ref, nv_ref, h_ref, w1_ref, w3_ref, w2_ref, o_ref, w1b, w3b, w2b):
    c = pl.program_id(0)
    j = pl.program_id(1)
    nv = nv_ref[c]

    @pl.when(j == 0)
    def _():
        o_ref[...] = jnp.zeros_like(o_ref)

    @pl.when(nv > 0)
    def _():
        w1b[...] = w1_ref[0].astype(BF16)
        w3b[...] = w3_ref[0].astype(BF16)
        w2b[...] = w2_ref[0].astype(BF16)
        for s in range(h_ref.shape[0] // MOE_SUB):
            @pl.when(s * MOE_SUB < nv)
            def _(s=s):
                rows = slice(s * MOE_SUB, (s + 1) * MOE_SUB)
                h = h_ref[rows, :]
                a1 = _dot(h, w1b[...])
                a3 = _dot(h, w3b[...])
                u = (a1 * jax.nn.sigmoid(a1) * a3).astype(BF16)
                o_ref[rows, :] += _dot(u, w2b[...])


def _moe_ffn(hs, chunk_expert, chunk_valid, w1, w3, w2, tf=256):
    n, d = hs.shape
    f = w1.shape[2]
    tf = _tile(f, tf)
    nj = f // tf
    nchunk = n // MOE_CHUNK

    def jeff(c, j, nv):
        return jnp.where(nv[c] > 0, j, nj - 1)

    return pl.pallas_call(
        _moe_kernel,
        out_shape=jax.ShapeDtypeStruct((n, d), F32),
        grid_spec=pltpu.PrefetchScalarGridSpec(
            num_scalar_prefetch=2,
            grid=(nchunk, nj),
            in_specs=[pl.BlockSpec((MOE_CHUNK, d), lambda c, j, ce, nv: (c, 0)),
                      pl.BlockSpec((1, d, tf), lambda c, j, ce, nv: (ce[c], 0, jeff(c, j, nv))),
                      pl.BlockSpec((1, d, tf), lambda c, j, ce, nv: (ce[c], 0, jeff(c, j, nv))),
                      pl.BlockSpec((1, tf, d), lambda c, j, ce, nv: (ce[c], jeff(c, j, nv), 0))],
            out_specs=pl.BlockSpec((MOE_CHUNK, d), lambda c, j, ce, nv: (c, 0)),
            scratch_shapes=[pltpu.VMEM((d, tf), BF16), pltpu.VMEM((d, tf), BF16),
                            pltpu.VMEM((tf, d), BF16)]),
        compiler_params=_params(("arbitrary", "arbitrary")),
        name="moe_ffn",
    )(chunk_expert, chunk_valid, hs, w1, w3, w2)


def _combine_kernel(p0_ref, p1_ref, ys_ref, x_ref, gates_ref, g_ref, lnw_ref, lnb_ref,
                    o_ref, b0_ref, b1_ref, sem, *, alpha):
    i = pl.program_id(0)
    rows = b0_ref.shape[0]

    def start(r, carry):
        _row_copy(ys_ref, b0_ref, sem, p0_ref[i * rows + r], r).start()
        _row_copy(ys_ref, b1_ref, sem, p1_ref[i * rows + r], r).start()
        return carry

    def wait(r, carry):
        _row_copy(ys_ref, b0_ref, sem, 0, r).wait()
        _row_copy(ys_ref, b1_ref, sem, 0, r).wait()
        return carry

    lax.fori_loop(0, rows, start, 0)
    lax.fori_loop(0, rows, wait, 0)
    gates = gates_ref[...]
    y = gates[:, 0:1] * b0_ref[...] + gates[:, 1:2] * b1_ref[...]
    z = alpha * x_ref[...] + g_ref[0] * y
    o_ref[...] = _layer_norm(z, lnw_ref[...], lnb_ref[...])


def _combine(ys, pos0, pos1, gates, x, gate_vec, ln_w, ln_b, alpha, rows_per_batch, rows=256):
    m, d = x.shape
    tpb = rows_per_batch // rows
    tok = lambda i, a, b: (i, 0)
    return pl.pallas_call(
        functools.partial(_combine_kernel, alpha=alpha),
        out_shape=jax.ShapeDtypeStruct((m, d), F32),
        grid_spec=pltpu.PrefetchScalarGridSpec(
            num_scalar_prefetch=2,
            grid=(m // rows,),
            in_specs=[pl.BlockSpec(memory_space=pl.ANY),
                      pl.BlockSpec((rows, d), tok),
                      pl.BlockSpec((rows, 2), tok),
                      pl.BlockSpec((1, 1, d), lambda i, a, b: (i // tpb, 0, 0)),
                      pl.BlockSpec((1, d), lambda i, a, b: (0, 0)),
                      pl.BlockSpec((1, d), lambda i, a, b: (0, 0))],
            out_specs=pl.BlockSpec((rows, d), tok),
            scratch_shapes=[pltpu.VMEM((rows, d), F32), pltpu.VMEM((rows, d), F32),
                            pltpu.SemaphoreType.DMA]),
        compiler_params=_params(("arbitrary",)),
        name="moe_combine",
    )(pos0, pos1, ys, x, gates, gate_vec, ln_w.reshape(1, d), ln_b.reshape(1, d))


def _routing_tables(route, n_rows_padded):
    n = route.shape[1]
    experts = jnp.concatenate([route[0], route[1]]).astype(jnp.int32)
    onehot = (experts[:, None] == jnp.arange(N_EXPERTS, dtype=jnp.int32)[None, :]).astype(jnp.int32)
    csum = jnp.cumsum(onehot, axis=0)
    rank = jnp.sum((csum - 1) * onehot, axis=1)
    counts = csum[-1]
    chunks_e = (counts + MOE_CHUNK - 1) // MOE_CHUNK
    chunk_end = jnp.cumsum(chunks_e)
    chunk_start = chunk_end - chunks_e
    dest = (chunk_start * MOE_CHUNK)[experts] + rank
    token = jnp.arange(2 * n, dtype=jnp.int32) % n
    src_idx = jnp.zeros((n_rows_padded,), jnp.int32).at[dest].set(token)
    nchunk = n_rows_padded // MOE_CHUNK
    cid = jnp.arange(nchunk, dtype=jnp.int32)
    used = cid < chunk_end[-1]
    last_used = jnp.maximum(chunk_end[-1] - 1, 0)
    ce = jnp.sum((jnp.minimum(cid, last_used)[:, None] >= chunk_end[None, :]).astype(jnp.int32), axis=1)
    ce = jnp.minimum(ce, N_EXPERTS - 1)
    nvalid = jnp.clip(counts[ce] - (cid - chunk_start[ce]) * MOE_CHUNK, 0, MOE_CHUNK)
    nvalid = jnp.where(used, nvalid, 0).astype(jnp.int32)
    sub = jnp.arange(n_rows_padded // MOE_SUB, dtype=jnp.int32)
    tile_valid = ((sub % (MOE_CHUNK // MOE_SUB)) * MOE_SUB < nvalid[sub // (MOE_CHUNK // MOE_SUB)])
    return src_idx, tile_valid.astype(jnp.int32), ce, nvalid, dest[:n], dest[n:]


def _position_embedding(rows, width, d):
    quarter = d // 4
    omega = 1.0 / (POS_BASE ** (jnp.arange(quarter, dtype=F32) / quarter))
    row = jnp.repeat(jnp.arange(rows, dtype=F32), width)
    col = jnp.tile(jnp.arange(width, dtype=F32), rows)
    ar = row[:, None] * omega[None, :]
    ac = col[:, None] * omega[None, :]
    return jnp.concatenate([jnp.sin(ar), jnp.cos(ar), jnp.sin(ac), jnp.cos(ac)], axis=-1)


def _block_diag2(w):
    z = jnp.zeros_like(w[0])
    return jnp.concatenate([jnp.concatenate([w[0], z], axis=1), jnp.concatenate([z, w[1]], axis=1)], axis=0)


def kernel(x, c, ctx, c_ctx, w_mod, b_mod, ln_w, ln_b, rwkv_mu, rwkv_w_r, rwkv_w_k, rwkv_w_v, rwkv_w_o, rwkv_decay_w0, rwkv_decay_w1, rwkv_decay_w2, rwkv_iclr_a0, rwkv_iclr_a1, rwkv_iclr_a2, rwkv_gate_g1, rwkv_gate_g2, rwkv_k_k, rwkv_k_a, rwkv_r_k, rwkv_gn_w, rwkv_gn_b, pool_w, pool_scale, ffn_w1, ffn_w3, ffn_w2, moe_router, moe_w1, moe_w3, moe_w2):
    b, t, d = x.shape
    n_ctx = ctx.shape[1]
    depth = w_mod.shape[0]
    assert depth == 2 and rwkv_mu.shape[0] == 1 and pool_w.shape[0] == 1
    alpha = (2.0 * depth) ** 0.25
    grid_w = 64
    l = n_ctx + t
    n = b * t

    cond = jnp.zeros((8, d), F32).at[:b].set(c).at[b].set(c_ctx)
    mod = _adaln(cond, w_mod, b_mod).reshape(depth, 8, 6, d)
    lat = lambda layer, which: mod[layer, :b, which].reshape(b, 1, d)
    cvec = lambda layer, which: mod[layer, b, which].reshape(1, 1, d)

    pos = _position_embedding(t // grid_w, grid_w, d)

    xr, xw, xk, xv, xa, xg = _rwkv_mix(ctx, x, pos, lat(0, 1), lat(0, 0), cvec(0, 1), cvec(0, 0), rwkv_mu[0])
    flat = lambda a: a.reshape(b * l, a.shape[-1])
    r = _mm(flat(xr), rwkv_w_r[0])
    k = _mm(flat(xk), rwkv_w_k[0])
    v = _mm(flat(xv), rwkv_w_v[0])
    dw1 = jnp.concatenate([rwkv_decay_w1[0, 0], rwkv_decay_w1[0, 1]], axis=1)
    ia1 = jnp.concatenate([rwkv_iclr_a1[0, 0], rwkv_iclr_a1[0, 1]], axis=1)
    lora_w = _mm(flat(xw), dw1, act="tanh", out_dtype=BF16)
    zw = _mm(lora_w, _block_diag2(rwkv_decay_w2[0]), rwkv_decay_w0[0].reshape(1, 2 * d))
    lora_a = _mm(flat(xa), ia1, out_dtype=BF16)
    za = _mm(lora_a, _block_diag2(rwkv_iclr_a2[0]), rwkv_iclr_a0[0].reshape(1, 2 * d))
    lora_g = _mm(flat(xg), rwkv_gate_g1[0], act="sigmoid", out_dtype=BF16)
    gate = _mm(lora_g, rwkv_gate_g2[0])
    seq = lambda a: a.reshape(b, l, a.shape[-1])
    p_c, rh_c, q_c, yl_c = _scan_prep(seq(r), seq(k), seq(v), seq(zw), seq(za), rwkv_k_k[0], rwkv_k_a[0])
    y_scan = _scan_seq(p_c, rh_c, q_c, yl_c, n_ctx // CHUNK)
    og = _rwkv_post(y_scan, seq(r), seq(k), seq(v), seq(za), seq(gate), rwkv_k_a[0],
                    rwkv_r_k[0].reshape(d), rwkv_gn_w[0], rwkv_gn_b[0], n_ctx)
    x1 = _proj_norm(og.reshape(n, d), rwkv_w_o[0], x.reshape(n, d), pos, lat(0, 2),
                    ln_w[0, 0], ln_b[0, 0], alpha)
    x2 = _ffn(x1, lat(0, 4), lat(0, 3), lat(0, 5), ffn_w1[0], ffn_w3[0], ffn_w2[0],
              ln_w[0, 1], ln_b[0, 1], alpha, t)

    y_pool = _pool(x2.reshape(b, t, d), lat(1, 1), lat(1, 0), pool_w[0], pool_scale[0])
    x3, h3, route = _route(x2, y_pool.reshape(n, d), lat(1, 2), ln_w[1, 0], ln_b[1, 0],
                           lat(1, 4), lat(1, 3), moe_router[0], alpha, t)
    n_pad = ((2 * n + N_EXPERTS * (MOE_CHUNK - 1)) // MOE_CHUNK) * MOE_CHUNK
    src_idx, tile_valid, chunk_expert, chunk_valid, pos0, pos1 = _routing_tables(route, n_pad)
    hs = _gather_rows(h3, src_idx, tile_valid)
    ys = _moe_ffn(hs, chunk_expert, chunk_valid, moe_w1[0], moe_w3[0], moe_w2[0])
    gates = jnp.stack([route[2], route[3]], axis=1)
    out = _combine(ys, pos0, pos1, gates, x3, lat(1, 5), ln_w[1, 1], ln_b[1, 1], alpha, t)
    return out.reshape(b, t, d)
```

```python
import functools
import math

import numpy as np
import jax
import jax.numpy as jnp
from jax import lax
from jax.experimental import pallas as pl
from jax.experimental.pallas import tpu as pltpu

F32 = jnp.float32
BF16 = jnp.bfloat16

HEAD = 64
GROUP = 4 * HEAD
CHUNK = 64
LN_EPS = 1e-5
GN_EPS = 64e-5
POS_BASE = 10000.0
POOL_WINDOWS = (2, 4, 8, 16)
N_EXPERTS = 8
MOE_CHUNK = 1024
MOE_SUB = 256
VMEM_LIMIT_BYTES = 56 * 1024 * 1024


def _params(semantics):
    return pltpu.CompilerParams(dimension_semantics=semantics, vmem_limit_bytes=VMEM_LIMIT_BYTES)


def _tile(n, preferred):
    t = min(preferred, n)
    while n % t:
        t //= 2
    return t


def _dot(a, b):
    return jnp.dot(a, b, preferred_element_type=F32)


def _dot_nt(a, b):
    return lax.dot_general(a, b, (((1,), (1,)), ((), ())), preferred_element_type=F32)


def _dot_tn(a, b):
    return lax.dot_general(a, b, (((0,), (0,)), ((), ())), preferred_element_type=F32)


def _split2(x):
    hi = x.astype(BF16)
    lo = (x - hi.astype(F32)).astype(BF16)
    return hi, lo


def _split3(x):
    hi = x.astype(BF16)
    r1 = x - hi.astype(F32)
    mid = r1.astype(BF16)
    lo = (r1 - mid.astype(F32)).astype(BF16)
    return hi, mid, lo


def _layer_norm(z, w, b):
    mu = jnp.mean(z, axis=-1, keepdims=True)
    zc = z - mu
    var = jnp.mean(zc * zc, axis=-1, keepdims=True)
    return zc * lax.rsqrt(var + LN_EPS) * w + b


def _adaln_kernel(c_ref, w_ref, b_ref, o_ref):
    c = c_ref[...]
    a = (c * jax.nn.sigmoid(c)).astype(BF16)
    o_ref[0] = _dot(a, w_ref[0].astype(BF16)) + b_ref[0]


def _adaln(cond, w_mod, b_mod, tn=1024):
    depth, d, n = w_mod.shape
    tn = _tile(n, tn)
    rows = cond.shape[0]
    return pl.pallas_call(
        _adaln_kernel,
        out_shape=jax.ShapeDtypeStruct((depth, rows, n), F32),
        grid=(depth, n // tn),
        in_specs=[pl.BlockSpec((rows, d), lambda l, j: (0, 0)),
                  pl.BlockSpec((1, d, tn), lambda l, j: (l, 0, j)),
                  pl.BlockSpec((1, 1, tn), lambda l, j: (l, 0, j))],
        out_specs=pl.BlockSpec((1, rows, tn), lambda l, j: (l, 0, j)),
        compiler_params=_params(("parallel", "parallel")),
        name="adaln",
    )(cond, w_mod, b_mod.reshape(depth, 1, n))


def _mm_kernel(a_ref, w_ref, b_ref, o_ref, *, act):
    acc = _dot(a_ref[...], w_ref[...].astype(BF16)) + b_ref[...]
    if act == "tanh":
        acc = jnp.tanh(acc)
    elif act == "sigmoid":
        acc = jax.nn.sigmoid(acc)
    o_ref[...] = acc.astype(o_ref.dtype)


def _mm(a, w, bias=None, *, act=None, out_dtype=F32, tm=2304, tn=512):
    m, k = a.shape
    n = w.shape[1]
    tm = _tile(m, tm)
    tn = _tile(n, tn)
    if bias is None:
        bias = jnp.zeros((1, n), F32)
    return pl.pallas_call(
        functools.partial(_mm_kernel, act=act),
        out_shape=jax.ShapeDtypeStruct((m, n), out_dtype),
        grid=(m // tm, n // tn),
        in_specs=[pl.BlockSpec((tm, k), lambda i, j: (i, 0)),
                  pl.BlockSpec((k, tn), lambda i, j: (0, j)),
                  pl.BlockSpec((1, tn), lambda i, j: (0, j))],
        out_specs=pl.BlockSpec((tm, tn), lambda i, j: (i, j)),
        compiler_params=_params(("parallel", "arbitrary")),
        name="matmul",
    )(a, w, bias.reshape(1, n))


def _mix_kernel(ctx_ref, x_ref, xp_ref, xn_ref, pos_ref, pp_ref, pn_ref,
                sc_ref, sh_ref, csc_ref, csh_ref, mu_ref,
                o0, o1, o2, o3, o4, o5, *, n_lat_tiles):
    s = pl.program_id(1)
    is_ctx = s == 0
    tm = x_ref.shape[1]
    scale = jnp.where(is_ctx, csc_ref[0], sc_ref[0]) + 1.0
    shift = jnp.where(is_ctx, csh_ref[0], sh_ref[0])
    src = jnp.where(is_ctx, ctx_ref[0], x_ref[0] + pos_ref[...])
    h = src * scale + shift
    has_prev = s > 1
    has_next = jnp.logical_and(s >= 1, s < n_lat_tiles)
    h_prev = jnp.where(has_prev, (xp_ref[0] + pp_ref[...]) * scale + shift, 0.0)[7:8]
    h_next = jnp.where(has_next, (xn_ref[0] + pn_ref[...]) * scale + shift, 0.0)[0:1]
    row = lax.broadcasted_iota(jnp.int32, h.shape, 0)
    h_m1 = jnp.where(row == 0, h_prev, pltpu.roll(h, 1, axis=0))
    h_p1 = jnp.where(row == tm - 1, h_next, pltpu.roll(h, tm - 1, axis=0))
    xx = 0.5 * (h_m1 + h_p1) - h
    for n, o_ref in enumerate((o0, o1, o2, o3, o4, o5)):
        o_ref[0] = (h + xx * mu_ref[n:n + 1]).astype(o_ref.dtype)


def _rwkv_mix(ctx, x, pos, sc, sh, csc, csh, mu):
    b, t, d = x.shape
    tm = ctx.shape[1]
    assert t % tm == 0 and tm % 8 == 0
    n_lat = t // tm
    r8 = tm // 8
    lat = lambda bi, s: (bi, jnp.maximum(s - 1, 0), 0)
    prev8 = lambda bi, s: (bi, jnp.maximum((s - 1) * r8 - 1, 0), 0)
    next8 = lambda bi, s: (bi, jnp.minimum(jnp.maximum(s, 1) * r8, t // 8 - 1), 0)
    vec = pl.BlockSpec((1, 1, d), lambda bi, s: (bi, 0, 0))
    cvec = pl.BlockSpec((1, 1, d), lambda bi, s: (0, 0, 0))
    out_sds = jax.ShapeDtypeStruct((b, tm + t, d), BF16)
    return pl.pallas_call(
        functools.partial(_mix_kernel, n_lat_tiles=n_lat),
        out_shape=[out_sds] * 6,
        grid=(b, n_lat + 1),
        in_specs=[pl.BlockSpec((1, tm, d), lambda bi, s: (bi, 0, 0)),
                  pl.BlockSpec((1, tm, d), lat),
                  pl.BlockSpec((1, 8, d), prev8),
                  pl.BlockSpec((1, 8, d), next8),
                  pl.BlockSpec((tm, d), lambda bi, s: (jnp.maximum(s - 1, 0), 0)),
                  pl.BlockSpec((8, d), lambda bi, s: (jnp.maximum((s - 1) * r8 - 1, 0), 0)),
                  pl.BlockSpec((8, d), lambda bi, s: (jnp.minimum(jnp.maximum(s, 1) * r8, t // 8 - 1), 0)),
                  vec, vec, cvec, cvec,
                  pl.BlockSpec((6, d), lambda bi, s: (0, 0))],
        out_specs=[pl.BlockSpec((1, tm, d), lambda bi, s: (bi, s, 0))] * 6,
        compiler_params=_params(("parallel", "parallel")),
        name="rwkv_mix",
    )(ctx, x, x, x, pos, pos, pos, sc, sh, csc, csh, mu)


def _block_diag(x, bmask):
    xb = x.astype(BF16)
    return jnp.concatenate([xb, xb, xb, xb], axis=0) * bmask


def _fold_heads(full, bmask_f32):
    m = full * bmask_f32
    return m[0:HEAD] + m[HEAD:2 * HEAD] + m[2 * HEAD:3 * HEAD] + m[3 * HEAD:4 * HEAD]


SUB = 16


def _block_diag16(x, bmask16):
    xb = x.astype(BF16)
    return jnp.concatenate([xb] * (GROUP // SUB), axis=0) * bmask16


def _dot3_bd16(a, b, bmask16):
    rows = a.shape[0]
    a_hi, a_lo = _split2(a)
    b_hi, b_lo = _split2(b)
    main = _dot(jnp.concatenate([a_hi, a_lo], axis=0), _block_diag16(b_hi, bmask16))
    return main[:rows] + main[rows:] + _dot(a_hi, _block_diag16(b_lo, bmask16))


def _unit_triangular_inverse(l_mats, bmask, bmask16, eye16, diag16, off_a, off_b):
    nq = CHUNK // SUB
    l16 = [sum(l[q * SUB:(q + 1) * SUB] * diag16[q * SUB:(q + 1) * SUB] for q in range(nq)) for l in l_mats]
    t16 = [eye16 + x for x in l16]
    l_pow = [_dot3_bd16(x, x, bmask16) for x in l16]
    for _ in range(2):
        both = [_dot3_bd16(jnp.concatenate([t, lp], axis=0), lp, bmask16) for t, lp in zip(t16, l_pow)]
        t16 = [t + bo[:SUB] for t, bo in zip(t16, both)]
        l_pow = [bo[SUB:] for bo in both]
    t16 = [t + _dot3_bd16(t, lp, bmask16) for t, lp in zip(t16, l_pow)]
    d = [jnp.concatenate([t] * nq, axis=0) * diag16 for t in t16]
    for off in (off_a, off_b):
        x = [_dot(di.astype(BF16), _block_diag(l * off, bmask)) for di, l in zip(d, l_mats)]
        d = [di + _dot(xi.astype(BF16), _block_diag(di, bmask)) for di, xi in zip(d, x)]
    return d


def _scan_kernel(r_ref, k_ref, v_ref, zw_ref, za_ref, kk_ref, ka_ref,
                 bmask_ref, bmask16_ref, tri_ref, dm_ref, cm_ref, eye16_ref,
                 y_ref, s_ref, p_ref, rh_ref, q_ref, yl_ref):
    ng = r_ref.shape[2] // GROUP
    bmask = bmask_ref[...]

    @pl.when(pl.program_id(2) == 0)
    def _():
        s_ref[...] = jnp.zeros_like(s_ref)
        p_ref[...] = jnp.zeros_like(p_ref)
        rh_ref[...] = jnp.zeros_like(rh_ref)
        q_ref[...] = jnp.zeros_like(q_ref)
        yl_ref[...] = jnp.zeros_like(yl_ref)

    for g in range(ng):
        cols = slice(g * GROUP, (g + 1) * GROUP)
        s_bd = _block_diag(s_ref[:, cols], bmask)
        out = _dot(jnp.concatenate([p_ref[:, cols], rh_ref[:, cols]], axis=0), s_bd)
        s_ref[:, cols] = out[:CHUNK] + q_ref[:, cols]
        y_ref[0, 0, :, cols] = out[CHUNK:] + yl_ref[:, cols]

    bmask_f = bmask.astype(F32)
    eye = cm_ref[0]
    m_strict = dm_ref[0, 0]
    m_incl = dm_ref[0, 1]
    groups = lambda x: [x[:, g * GROUP:(g + 1) * GROUP] for g in range(ng)]
    bd = lambda x: _block_diag(x, bmask)
    stack = lambda x, y: jnp.concatenate([x, y], axis=0)

    r = r_ref[0].astype(F32)
    k = k_ref[0].astype(F32)
    v = v_ref[0].astype(F32)
    lw = (-math.exp(-0.5)) * jax.nn.sigmoid(zw_ref[0])
    a = jax.nn.sigmoid(za_ref[0])
    kkr = k * kk_ref[...]
    sq = jnp.concatenate(groups(kkr * kkr), axis=0)
    sq_hi, sq_lo = _split2(sq)
    ssq = _dot(sq_hi, bmask) + _dot(sq_lo, bmask)
    ssq = jnp.concatenate([ssq[g * CHUNK:(g + 1) * CHUNK] for g in range(ng)], axis=1)
    kk = kkr * lax.rsqrt(jnp.maximum(ssq, 1e-24))
    kd = k * (1.0 + (a - 1.0) * ka_ref[...])
    bb = kk * a
    tri = tri_ref[0]
    lw_hi, lw_lo = _split2(lw)
    g_cum = _dot(tri, lw_hi) + _dot(tri, lw_lo)
    g_end = jnp.sum(lw, axis=0, keepdims=True)
    e_neg = jnp.exp(-g_cum)
    e_end = jnp.exp(g_end - g_cum)
    a_t = groups(-kk * jnp.exp(g_cum - lw))
    r_t = groups(r * jnp.exp(g_cum))
    b_t = groups(bb * e_neg)
    k_t = groups(kd * e_neg)
    b_h = groups((bb * e_end).astype(BF16))
    k_h = groups((kd * e_end).astype(BF16))
    v_g = groups(v)
    decay_end = groups(jnp.exp(g_end))

    ar = [stack(x, y).astype(BF16) for x, y in zip(a_t, r_t)]
    mb = [_dot_nt(x, bd(y)) for x, y in zip(ar, b_t)]
    mk = [_dot_nt(x, bd(y)) for x, y in zip(ar, k_t)]
    l_mat = [x[:CHUNK] * m_strict for x in mb]
    m_rb = [(x[CHUNK:] * m_incl).astype(BF16) for x in mb]
    m_k = [stack(x[:CHUNK] * m_strict, x[CHUNK:] * m_incl).astype(BF16) for x in mk]
    mv = [_dot(x, bd(y)) for x, y in zip(m_k, v_g)]
    t_mat = _unit_triangular_inverse(l_mat, bmask, bmask16_ref[...], eye16_ref[...],
                                     cm_ref[1], cm_ref[2], cm_ref[3])
    t_b = [x.astype(BF16) for x in t_mat]
    a_h = [_dot(x, bd(y)).astype(BF16) for x, y in zip(t_b, a_t)]
    u_0 = [_dot(x, bd(y[:CHUNK])) for x, y in zip(t_b, mv)]
    rh = [x + _dot(m, bd(y)) for x, m, y in zip(r_t, m_rb, a_h)]
    yl = [_dot(m, bd(u)) + y[CHUNK:] for m, u, y in zip(m_rb, u_0, mv)]
    p_full = [_dot_tn(x, y) for x, y in zip(b_h, a_h)]
    q_full = [_dot_tn(stack(x, y), stack(u, w).astype(BF16)) for x, y, u, w in zip(b_h, k_h, u_0, v_g)]
    for g in range(ng):
        cols = slice(g * GROUP, (g + 1) * GROUP)
        p_ref[:, cols] = (_fold_heads(p_full[g], bmask_f) + eye * decay_end[g]).astype(p_ref.dtype)
        rh_ref[:, cols] = rh[g].astype(rh_ref.dtype)
        q_ref[:, cols] = _fold_heads(q_full[g], bmask_f)
        yl_ref[:, cols] = yl[g]


def _scan_consts():
    lane = np.arange(GROUP)
    bmask = (lane[:, None] // HEAD == lane[None, :] // HEAD).astype(np.float32)
    i = np.arange(CHUNK)[:, None]
    j = np.arange(CHUNK)[None, :]
    jl = (lane % HEAD)[None, :]
    tri = np.stack([(j <= i), (j >= i)]).astype(np.float32)
    dir_masks = np.stack([np.stack([(jl < i), (jl <= i)]),
                          np.stack([(jl > i), (jl >= i)])]).astype(np.float32)
    same16 = (jl // SUB == i // SUB)
    same32 = (jl // (2 * SUB) == i // (2 * SUB))
    common = np.stack([(jl == i), same16, same32 & ~same16, ~same32]).astype(np.float32)
    bmask16 = (lane[:, None] // SUB == lane[None, :] // SUB).astype(np.float32)
    eye16 = ((lane % SUB)[None, :] == np.arange(SUB)[:, None]).astype(np.float32)
    return (jnp.asarray(bmask, BF16), jnp.asarray(bmask16, BF16), jnp.asarray(tri, BF16),
            jnp.asarray(dir_masks, F32), jnp.asarray(common, F32), jnp.asarray(eye16, F32))


def _scan(r, k, v, zw, za, k_k, k_a, n_ctx_chunks):
    b, l, d = r.shape
    nc = l // CHUNK
    bmask, bmask16, tri, dir_masks, common, eye16 = _scan_consts()

    def chunk_of(di, s):
        back = jnp.where(s < n_ctx_chunks, n_ctx_chunks - 1 - s, nc - 1 + n_ctx_chunks - s)
        return jnp.where(di == 0, s, back)

    fold_chunk = lambda di, s: chunk_of(di, jnp.minimum(s, nc - 1))
    apply_chunk = lambda di, s: chunk_of(di, jnp.maximum(s - 1, 0))
    tok = pl.BlockSpec((1, CHUNK, d), lambda di, bi, s: (bi, fold_chunk(di, s), 0))
    tok2 = pl.BlockSpec((1, CHUNK, d), lambda di, bi, s: (bi, fold_chunk(di, s), di))
    par = pl.BlockSpec((1, d), lambda di, bi, s: (0, 0))
    return pl.pallas_call(
        _scan_kernel,
        out_shape=jax.ShapeDtypeStruct((2, b, l, d), F32),
        grid=(2, b, nc + 1),
        in_specs=[tok, tok, tok, tok2, tok2, par, par,
                  pl.BlockSpec((GROUP, GROUP), lambda di, bi, s: (0, 0)),
                  pl.BlockSpec((GROUP, GROUP), lambda di, bi, s: (0, 0)),
                  pl.BlockSpec((1, CHUNK, CHUNK), lambda di, bi, s: (di, 0, 0)),
                  pl.BlockSpec((1, 2, CHUNK, GROUP), lambda di, bi, s: (di, 0, 0, 0)),
                  pl.BlockSpec((4, CHUNK, GROUP), lambda di, bi, s: (0, 0, 0)),
                  pl.BlockSpec((SUB, GROUP), lambda di, bi, s: (0, 0))],
        out_specs=pl.BlockSpec((1, 1, CHUNK, d), lambda di, bi, s: (di, bi, apply_chunk(di, s), 0)),
        scratch_shapes=[pltpu.VMEM((CHUNK, d), F32), pltpu.VMEM((CHUNK, d), BF16),
                        pltpu.VMEM((CHUNK, d), BF16), pltpu.VMEM((CHUNK, d), F32),
                        pltpu.VMEM((CHUNK, d), F32)],
        compiler_params=_params(("parallel", "parallel", "arbitrary")),
        name="scan",
    )(r, k, v, zw, za, k_k.reshape(1, d), k_a.reshape(1, d), bmask, bmask16, tri, dir_masks, common, eye16)


def _head_sum(x, ones_bd):
    hi, lo = _split2(x)
    return _dot(hi, ones_bd) + _dot(lo, ones_bd)


def _rwkv_post_kernel(y_ref, r_ref, k_ref, v_ref, za0_ref, za1_ref, gate_ref,
                      ka_ref, rk_ref, gnw_ref, gnb_ref, ones_ref, o_ref):
    ones_bd = ones_ref[...]
    tm, d = o_ref.shape[1], o_ref.shape[2]
    ng = d // GROUP
    to_rows = lambda x: jnp.concatenate([x[:, g * GROUP:(g + 1) * GROUP] for g in range(ng)], axis=0)
    to_cols = lambda x: jnp.concatenate([x[g * tm:(g + 1) * tm] for g in range(ng)], axis=1)
    head_mean = lambda x: to_cols(_head_sum(to_rows(x), ones_bd)) * (1.0 / HEAD)
    y = y_ref[0, 0] + y_ref[1, 0]
    yc = y - head_mean(y)
    var = head_mean(yc * yc)
    o = yc * lax.rsqrt(var + GN_EPS) * gnw_ref[...] + gnb_ref[...]
    a_sum = jax.nn.sigmoid(za0_ref[0]) + jax.nn.sigmoid(za1_ref[0])
    r = r_ref[0].astype(F32)
    k_sum = k_ref[0].astype(F32) * (2.0 + (a_sum - 2.0) * ka_ref[...])
    bonus = head_mean(r * k_sum * rk_ref[...]) * float(HEAD) * v_ref[0].astype(F32)
    o_ref[0] = ((o + bonus) * gate_ref[0].astype(F32)).astype(o_ref.dtype)


def _rwkv_post(y, r, k, v, za, gate, k_a, r_k, gn_w, gn_b, n_ctx, tm=256):
    _, b, l, d = y.shape
    t = l - n_ctx
    off = n_ctx // tm
    ones_bd = _scan_consts()[0]
    tok = pl.BlockSpec((1, tm, d), lambda bi, s: (bi, s + off, 0))
    par = pl.BlockSpec((1, d), lambda bi, s: (0, 0))
    return pl.pallas_call(
        _rwkv_post_kernel,
        out_shape=jax.ShapeDtypeStruct((b, t, d), BF16),
        grid=(b, t // tm),
        in_specs=[pl.BlockSpec((2, 1, tm, d), lambda bi, s: (0, bi, s + off, 0)),
                  tok, tok, tok,
                  pl.BlockSpec((1, tm, d), lambda bi, s: (bi, s + off, 0)),
                  pl.BlockSpec((1, tm, d), lambda bi, s: (bi, s + off, 1)),
                  tok, par, par, par, par,
                  pl.BlockSpec((GROUP, GROUP), lambda bi, s: (0, 0))],
        out_specs=pl.BlockSpec((1, tm, d), lambda bi, s: (bi, s, 0)),
        compiler_params=_params(("parallel", "parallel")),
        name="rwkv_post",
    )(y, r, k, v, za, za, gate, k_a.reshape(1, d), r_k.reshape(1, d), gn_w.reshape(1, d),
      gn_b.reshape(1, d), ones_bd)


def _proj_norm_kernel(a_ref, w_ref, x_ref, pos_ref, g_ref, lnw_ref, lnb_ref, sc_ref, sh_ref,
                      o_ref, h_ref, acc_ref, *, alpha):
    j = pl.program_id(1)
    nj = acc_ref.shape[0]
    tn = acc_ref.shape[2]
    acc_ref[j] = _dot(a_ref[...], w_ref[...].astype(BF16))

    @pl.when(j == nj - 1)
    def _():
        gate = g_ref[0]
        for jj in range(nj):
            cols = slice(jj * tn, (jj + 1) * tn)
            o_ref[:, cols] = alpha * (x_ref[:, cols] + pos_ref[:, cols]) + gate[:, cols] * acc_ref[jj]
        x_new = _layer_norm(o_ref[...], lnw_ref[...], lnb_ref[...])
        o_ref[...] = x_new
        h_ref[...] = (x_new * (1.0 + sc_ref[0]) + sh_ref[0]).astype(h_ref.dtype)


def _proj_norm(a, w, x, pos, gate, ln_w, ln_b, sc, sh, alpha, tm=512, tn=512):
    m, k = a.shape
    d = w.shape[1]
    t = pos.shape[0]
    tm = _tile(t, tm)
    tn = _tile(d, tn)
    tpb = t // tm
    vec = pl.BlockSpec((1, 1, d), lambda i, j: (i // tpb, 0, 0))
    return pl.pallas_call(
        functools.partial(_proj_norm_kernel, alpha=alpha),
        out_shape=[jax.ShapeDtypeStruct((m, d), F32), jax.ShapeDtypeStruct((m, d), BF16)],
        grid=(m // tm, d // tn),
        in_specs=[pl.BlockSpec((tm, k), lambda i, j: (i, 0)),
                  pl.BlockSpec((k, tn), lambda i, j: (0, j)),
                  pl.BlockSpec((tm, d), lambda i, j: (i, 0)),
                  pl.BlockSpec((tm, d), lambda i, j: (i % tpb, 0)),
                  vec,
                  pl.BlockSpec((1, d), lambda i, j: (0, 0)),
                  pl.BlockSpec((1, d), lambda i, j: (0, 0)),
                  vec, vec],
        out_specs=[pl.BlockSpec((tm, d), lambda i, j: (i, 0))] * 2,
        scratch_shapes=[pltpu.VMEM((d // tn, tm, tn), F32)],
        compiler_params=_params(("parallel", "arbitrary")),
        name="proj_norm",
    )(a, w, x, pos, gate, ln_w.reshape(1, d), ln_b.reshape(1, d), sc, sh)


def _swiglu_halves(h, w1_ref, w3_ref, w2_ref, lead):
    tf = w1_ref.shape[-1]
    halves = [slice(0, tf // 2), slice(tf // 2, tf)]
    a1 = [_dot(h, w1_ref[lead + (slice(None), c)].astype(BF16)) for c in halves]
    a3 = [_dot(h, w3_ref[lead + (slice(None), c)].astype(BF16)) for c in halves]
    u = [(x * jax.nn.sigmoid(x) * y).astype(BF16) for x, y in zip(a1, a3)]
    y = [_dot(x, w2_ref[lead + (c, slice(None))].astype(BF16)) for x, c in zip(u, halves)]
    return y[0] + y[1]


def _ffn_kernel(h_ref, w1_ref, w3_ref, w2_ref, o_ref, acc_ref):
    j = pl.program_id(1)

    @pl.when(j == 0)
    def _():
        acc_ref[...] = jnp.zeros_like(acc_ref)

    acc_ref[...] += _swiglu_halves(h_ref[...], w1_ref, w3_ref, w2_ref, ())

    @pl.when(j == pl.num_programs(1) - 1)
    def _():
        o_ref[...] = acc_ref[...].astype(o_ref.dtype)


def _ffn(h, w1, w3, w2, tm=1024, tf=512):
    m, d = h.shape
    f = w1.shape[1]
    tm = _tile(m, tm)
    tf = _tile(f, tf)
    return pl.pallas_call(
        _ffn_kernel,
        out_shape=jax.ShapeDtypeStruct((m, d), BF16),
        grid=(m // tm, f // tf),
        in_specs=[pl.BlockSpec((tm, d), lambda i, j: (i, 0)),
                  pl.BlockSpec((d, tf), lambda i, j: (0, j)),
                  pl.BlockSpec((d, tf), lambda i, j: (0, j)),
                  pl.BlockSpec((tf, d), lambda i, j: (j, 0))],
        out_specs=pl.BlockSpec((tm, d), lambda i, j: (i, 0)),
        scratch_shapes=[pltpu.VMEM((tm, d), F32)],
        compiler_params=_params(("parallel", "arbitrary")),
        name="ffn",
    )(h, w1, w3, w2)


def _residual_norm_kernel(x_ref, y_ref, g_ref, lnw_ref, lnb_ref, o_ref, *, alpha):
    z = alpha * x_ref[...] + g_ref[0] * y_ref[...].astype(F32)
    o_ref[...] = _layer_norm(z, lnw_ref[...], lnb_ref[...])


def _residual_norm(x, y, gate, ln_w, ln_b, alpha, rows_per_batch, tm=512):
    m, d = x.shape
    tm = _tile(rows_per_batch, tm)
    tpb = rows_per_batch // tm
    tok = pl.BlockSpec((tm, d), lambda i: (i, 0))
    row = pl.BlockSpec((1, d), lambda i: (0, 0))
    return pl.pallas_call(
        functools.partial(_residual_norm_kernel, alpha=alpha),
        out_shape=jax.ShapeDtypeStruct((m, d), F32),
        grid=(m // tm,),
        in_specs=[tok, tok, pl.BlockSpec((1, 1, d), lambda i: (i // tpb, 0, 0)), row, row],
        out_specs=tok,
        compiler_params=_params(("parallel",)),
        name="residual_norm",
    )(x, y, gate, ln_w.reshape(1, d), ln_b.reshape(1, d))


def _shift_down(x, s, row):
    return jnp.where(row >= s, pltpu.roll(x, s, axis=0), 0.0)


def _shift_up(x, s, row):
    t = x.shape[0]
    return jnp.where(row < t - s, pltpu.roll(x, t - s, axis=0), 0.0)


def _pool_kernel(x_ref, sc_ref, sh_ref, w_ref, scale_ref, o_ref):
    g = pl.program_id(0)
    t = x_ref.shape[1]
    h = x_ref[0] * (1.0 + sc_ref[0]) + sh_ref[0]
    row = lax.broadcasted_iota(jnp.int32, h.shape, 0)
    w = w_ref[0].astype(BF16)
    for gi, win in enumerate(POOL_WINDOWS):
        @pl.when(g == gi)
        def _(win=win):
            half = win // 2
            back = h
            fwd = h
            m = 1
            while m < half:
                back = back + _shift_down(back, m, row)
                fwd = fwd + _shift_up(fwd, m, row)
                m *= 2
            total = _shift_down(back, 1, row) + fwd
            count = (jnp.minimum(row + half, t) - jnp.maximum(row - half, 0)).astype(F32)
            pooled = (total / count - h).astype(BF16)
            o_ref[0] = _dot(pooled, w) * scale_ref[...]


def _pool(x, sc, sh, w_pool, scale):
    b, t, d = x.shape
    ng, p, _ = w_pool.shape
    vec = pl.BlockSpec((1, 1, p), lambda g, bi: (bi, 0, g))
    return pl.pallas_call(
        _pool_kernel,
        out_shape=jax.ShapeDtypeStruct((b, t, d), F32),
        grid=(ng, b),
        in_specs=[pl.BlockSpec((1, t, p), lambda g, bi: (bi, 0, g)), vec, vec,
                  pl.BlockSpec((1, p, p), lambda g, bi: (g, 0, 0)),
                  pl.BlockSpec((1, p), lambda g, bi: (0, g))],
        out_specs=pl.BlockSpec((1, t, p), lambda g, bi: (bi, 0, g)),
        compiler_params=_params(("parallel", "parallel")),
        name="pool",
    )(x, sc, sh, w_pool, scale.reshape(1, d))


def _route_kernel(x_ref, y_ref, g_ref, lnw_ref, lnb_ref, sc_ref, sh_ref, rt_ref,
                  xo_ref, h_ref, route_ref, *, alpha):
    x = _layer_norm(alpha * x_ref[...] + g_ref[0] * y_ref[...], lnw_ref[...], lnb_ref[...])
    xo_ref[...] = x
    h = x * (1.0 + sc_ref[0]) + sh_ref[0]
    h_ref[...] = h
    r1, r2, r3 = _split3(rt_ref[...])
    h1, h2, h3 = _split3(h)
    logits = (_dot_nt(r1, h1) + (_dot_nt(r1, h2) + _dot_nt(r2, h1))
              + (_dot_nt(r1, h3) + _dot_nt(r2, h2) + _dot_nt(r3, h1)))
    mx = jnp.max(logits, axis=0, keepdims=True)
    e = jnp.exp(logits - mx)
    p = e / jnp.sum(e, axis=0, keepdims=True)
    idx = lax.broadcasted_iota(jnp.int32, p.shape, 0)
    p1 = jnp.max(p, axis=0, keepdims=True)
    i1 = jnp.min(jnp.where(p == p1, idx, N_EXPERTS), axis=0, keepdims=True)
    rest = jnp.where(idx == i1, -1.0, p)
    p2 = jnp.max(rest, axis=0, keepdims=True)
    i2 = jnp.min(jnp.where(rest == p2, idx, N_EXPERTS), axis=0, keepdims=True)
    den = p1 + p2
    out = jnp.where(idx == 0, i1.astype(F32), 0.0)
    out = jnp.where(idx == 1, i2.astype(F32), out)
    out = jnp.where(idx == 2, p1 / den, out)
    out = jnp.where(idx == 3, p2 / den, out)
    route_ref[...] = out


def _route(x, y, gate, ln_w, ln_b, sc, sh, router, alpha, rows_per_batch, tm=256):
    m, d = x.shape
    tpb = rows_per_batch // tm
    vec = pl.BlockSpec((1, 1, d), lambda i: (i // tpb, 0, 0))
    row = pl.BlockSpec((1, d), lambda i: (0, 0))
    tok = pl.BlockSpec((tm, d), lambda i: (i, 0))
    return pl.pallas_call(
        functools.partial(_route_kernel, alpha=alpha),
        out_shape=[jax.ShapeDtypeStruct((m, d), F32), jax.ShapeDtypeStruct((m, d), F32),
                   jax.ShapeDtypeStruct((N_EXPERTS, m), F32)],
        grid=(m // tm,),
        in_specs=[tok, tok, vec, row, row, vec, vec,
                  pl.BlockSpec((N_EXPERTS, d), lambda i: (0, 0))],
        out_specs=[tok, tok, pl.BlockSpec((N_EXPERTS, tm), lambda i: (0, i))],
        compiler_params=_params(("parallel",)),
        name="route",
    )(x, y, gate, ln_w.reshape(1, d), ln_b.reshape(1, d), sc, sh, router.T)


def _row_copy(src_hbm, dst_vmem, sem, src_row, dst_row):
    return pltpu.make_async_copy(src_hbm.at[pl.ds(src_row, 1)], dst_vmem.at[pl.ds(dst_row, 1)], sem)


def _gather_kernel(idx_ref, valid_ref, src_ref, o_ref, buf_ref, sem):
    i = pl.program_id(0)
    n = pl.num_programs(0)
    rows = buf_ref.shape[1]
    slot = i % 2

    def request(tile, to_slot):
        def body(r, carry):
            _row_copy(src_ref, buf_ref.at[to_slot], sem.at[to_slot], idx_ref[tile * rows + r], r).start()
            return carry
        lax.fori_loop(0, rows, body, 0, unroll=8)

    @pl.when(jnp.logical_and(i == 0, valid_ref[0] > 0))
    def _():
        request(0, 0)

    nxt = jnp.minimum(i + 1, n - 1)

    @pl.when(jnp.logical_and(i + 1 < n, valid_ref[nxt] > 0))
    def _():
        request(nxt, 1 - slot)

    @pl.when(valid_ref[i] > 0)
    def _():
        def wait(r, carry):
            _row_copy(src_ref, buf_ref.at[slot], sem.at[slot], 0, r).wait()
            return carry
        lax.fori_loop(0, rows, wait, 0, unroll=8)
        o_ref[...] = buf_ref[slot].astype(o_ref.dtype)

    @pl.when(valid_ref[i] == 0)
    def _():
        o_ref[...] = jnp.zeros_like(o_ref)


def _gather_rows(src, idx, tile_valid, rows=MOE_SUB):
    d = src.shape[1]
    n = idx.shape[0]
    return pl.pallas_call(
        _gather_kernel,
        out_shape=jax.ShapeDtypeStruct((n, d), BF16),
        grid_spec=pltpu.PrefetchScalarGridSpec(
            num_scalar_prefetch=2,
            grid=(n // rows,),
            in_specs=[pl.BlockSpec(memory_space=pl.ANY)],
            out_specs=pl.BlockSpec((rows, d), lambda i, idx_r, val_r: (i, 0)),
            scratch_shapes=[pltpu.VMEM((2, rows, d), src.dtype), pltpu.SemaphoreType.DMA((2,))]),
        compiler_params=_params(("arbitrary",)),
        name="gather_rows",
    )(idx, tile_valid, src)


def _moe_kernel(ce_ref, nv_ref, h_ref, w1_ref, w3_ref, w2_ref, o_ref):
    c = pl.program_id(0)
    j = pl.program_id(1)
    nv = nv_ref[c]

    @pl.when(j == 0)
    def _():
        o_ref[...] = jnp.zeros_like(o_ref)

    def swiglu_rows(rows):
        o_ref[rows, :] += _swiglu_halves(h_ref[rows, :], w1_ref, w3_ref, w2_ref, (0,))

    full = h_ref.shape[0]

    @pl.when(nv == full)
    def _():
        swiglu_rows(slice(0, full))

    for s in range(full // MOE_SUB):
        @pl.when(jnp.logical_and(s * MOE_SUB < nv, nv < full))
        def _(s=s):
            swiglu_rows(slice(s * MOE_SUB, (s + 1) * MOE_SUB))


def _moe_ffn(hs, chunk_expert, chunk_valid, w1, w3, w2, tf=512):
    n, d = hs.shape
    f = w1.shape[2]
    tf = _tile(f, tf)
    nj = f // tf
    nchunk = n // MOE_CHUNK

    def jeff(c, j, nv):
        return jnp.where(nv[c] > 0, j, nj - 1)

    return pl.pallas_call(
        _moe_kernel,
        out_shape=jax.ShapeDtypeStruct((n, d), F32),
        grid_spec=pltpu.PrefetchScalarGridSpec(
            num_scalar_prefetch=2,
            grid=(nchunk, nj),
            in_specs=[pl.BlockSpec((MOE_CHUNK, d), lambda c, j, ce, nv: (c, 0)),
                      pl.BlockSpec((1, d, tf), lambda c, j, ce, nv: (ce[c], 0, jeff(c, j, nv))),
                      pl.BlockSpec((1, d, tf), lambda c, j, ce, nv: (ce[c], 0, jeff(c, j, nv))),
                      pl.BlockSpec((1, tf, d), lambda c, j, ce, nv: (ce[c], jeff(c, j, nv), 0))],
            out_specs=pl.BlockSpec((MOE_CHUNK, d), lambda c, j, ce, nv: (c, 0))),
        compiler_params=_params(("arbitrary", "arbitrary")),
        name="moe_ffn",
    )(chunk_expert, chunk_valid, hs, w1, w3, w2)


def _combine_kernel(p0_ref, p1_ref, ys_ref, x_ref, gates_ref, g_ref, lnw_ref, lnb_ref,
                    o_ref, b0_ref, b1_ref, sem, *, alpha):
    i = pl.program_id(0)
    n = pl.num_programs(0)
    rows = b0_ref.shape[1]
    slot = i % 2

    def request(tile, to_slot):
        def body(r, carry):
            _row_copy(ys_ref, b0_ref.at[to_slot], sem.at[to_slot], p0_ref[tile * rows + r], r).start()
            _row_copy(ys_ref, b1_ref.at[to_slot], sem.at[to_slot], p1_ref[tile * rows + r], r).start()
            return carry
        lax.fori_loop(0, rows, body, 0, unroll=4)

    @pl.when(i == 0)
    def _():
        request(0, 0)

    @pl.when(i + 1 < n)
    def _():
        request(jnp.minimum(i + 1, n - 1), 1 - slot)

    def wait(r, carry):
        _row_copy(ys_ref, b0_ref.at[slot], sem.at[slot], 0, r).wait()
        _row_copy(ys_ref, b1_ref.at[slot], sem.at[slot], 0, r).wait()
        return carry

    lax.fori_loop(0, rows, wait, 0, unroll=4)
    gates = gates_ref[...]
    y = gates[:, 0:1] * b0_ref[slot] + gates[:, 1:2] * b1_ref[slot]
    z = alpha * x_ref[...] + g_ref[0] * y
    o_ref[...] = _layer_norm(z, lnw_ref[...], lnb_ref[...])


def _combine(ys, pos0, pos1, gates, x, gate_vec, ln_w, ln_b, alpha, rows_per_batch, rows=256):
    m, d = x.shape
    tpb = rows_per_batch // rows
    tok = lambda i, a, b: (i, 0)
    return pl.pallas_call(
        functools.partial(_combine_kernel, alpha=alpha),
        out_shape=jax.ShapeDtypeStruct((m, d), F32),
        grid_spec=pltpu.PrefetchScalarGridSpec(
            num_scalar_prefetch=2,
            grid=(m // rows,),
            in_specs=[pl.BlockSpec(memory_space=pl.ANY),
                      pl.BlockSpec((rows, d), tok),
                      pl.BlockSpec((rows, 2), tok),
                      pl.BlockSpec((1, 1, d), lambda i, a, b: (i // tpb, 0, 0)),
                      pl.BlockSpec((1, d), lambda i, a, b: (0, 0)),
                      pl.BlockSpec((1, d), lambda i, a, b: (0, 0))],
            out_specs=pl.BlockSpec((rows, d), tok),
            scratch_shapes=[pltpu.VMEM((2, rows, d), F32), pltpu.VMEM((2, rows, d), F32),
                            pltpu.SemaphoreType.DMA((2,))]),
        compiler_params=_params(("arbitrary",)),
        name="moe_combine",
    )(pos0, pos1, ys, x, gates, gate_vec, ln_w.reshape(1, d), ln_b.reshape(1, d))


def _routing_tables(route, n_rows_padded):
    n = route.shape[1]
    experts = jnp.concatenate([route[0], route[1]]).astype(jnp.int32)
    onehot = (experts[:, None] == jnp.arange(N_EXPERTS, dtype=jnp.int32)[None, :]).astype(jnp.int32)
    csum = jnp.cumsum(onehot, axis=0)
    rank = jnp.sum((csum - 1) * onehot, axis=1)
    counts = csum[-1]
    chunks_e = (counts + MOE_CHUNK - 1) // MOE_CHUNK
    chunk_end = jnp.cumsum(chunks_e)
    chunk_start = chunk_end - chunks_e
    dest = (chunk_start * MOE_CHUNK)[experts] + rank
    token = jnp.arange(2 * n, dtype=jnp.int32) % n
    src_idx = jnp.zeros((n_rows_padded,), jnp.int32).at[dest].set(token)
    nchunk = n_rows_padded // MOE_CHUNK
    cid = jnp.arange(nchunk, dtype=jnp.int32)
    used = cid < chunk_end[-1]
    last_used = jnp.maximum(chunk_end[-1] - 1, 0)
    ce = jnp.sum((jnp.minimum(cid, last_used)[:, None] >= chunk_end[None, :]).astype(jnp.int32), axis=1)
    ce = jnp.minimum(ce, N_EXPERTS - 1)
    nvalid = jnp.clip(counts[ce] - (cid - chunk_start[ce]) * MOE_CHUNK, 0, MOE_CHUNK)
    nvalid = jnp.where(used, nvalid, 0).astype(jnp.int32)
    sub = jnp.arange(n_rows_padded // MOE_SUB, dtype=jnp.int32)
    tile_valid = ((sub % (MOE_CHUNK // MOE_SUB)) * MOE_SUB < nvalid[sub // (MOE_CHUNK // MOE_SUB)])
    return src_idx, tile_valid.astype(jnp.int32), ce, nvalid, dest[:n], dest[n:]


def _position_embedding(rows, width, d):
    quarter = d // 4
    omega = 1.0 / (POS_BASE ** (jnp.arange(quarter, dtype=F32) / quarter))
    row = jnp.repeat(jnp.arange(rows, dtype=F32), width)
    col = jnp.tile(jnp.arange(width, dtype=F32), rows)
    ar = row[:, None] * omega[None, :]
    ac = col[:, None] * omega[None, :]
    return jnp.concatenate([jnp.sin(ar), jnp.cos(ar), jnp.sin(ac), jnp.cos(ac)], axis=-1)


def _block_diag2(w):
    z = jnp.zeros_like(w[0])
    return jnp.concatenate([jnp.concatenate([w[0], z], axis=1), jnp.concatenate([z, w[1]], axis=1)], axis=0)


def kernel(x, c, ctx, c_ctx, w_mod, b_mod, ln_w, ln_b, rwkv_mu, rwkv_w_r, rwkv_w_k, rwkv_w_v, rwkv_w_o, rwkv_decay_w0, rwkv_decay_w1, rwkv_decay_w2, rwkv_iclr_a0, rwkv_iclr_a1, rwkv_iclr_a2, rwkv_gate_g1, rwkv_gate_g2, rwkv_k_k, rwkv_k_a, rwkv_r_k, rwkv_gn_w, rwkv_gn_b, pool_w, pool_scale, ffn_w1, ffn_w3, ffn_w2, moe_router, moe_w1, moe_w3, moe_w2):
    b, t, d = x.shape
    n_ctx = ctx.shape[1]
    depth = w_mod.shape[0]
    assert depth == 2 and rwkv_mu.shape[0] == 1 and pool_w.shape[0] == 1
    alpha = (2.0 * depth) ** 0.25
    grid_w = 64
    l = n_ctx + t
    n = b * t

    cond = jnp.zeros((8, d), F32).at[:b].set(c).at[b].set(c_ctx)
    mod = _adaln(cond, w_mod, b_mod).reshape(depth, 8, 6, d)
    lat = lambda layer, which: mod[layer, :b, which].reshape(b, 1, d)
    cvec = lambda layer, which: mod[layer, b, which].reshape(1, 1, d)

    pos = _position_embedding(t // grid_w, grid_w, d)

    xr, xw, xk, xv, xa, xg = _rwkv_mix(ctx, x, pos, lat(0, 1), lat(0, 0), cvec(0, 1), cvec(0, 0), rwkv_mu[0])
    flat = lambda a: a.reshape(b * l, a.shape[-1])
    r = _mm(flat(xr), rwkv_w_r[0], out_dtype=BF16)
    k = _mm(flat(xk), rwkv_w_k[0], out_dtype=BF16)
    v = _mm(flat(xv), rwkv_w_v[0], out_dtype=BF16)
    dw1 = jnp.concatenate([rwkv_decay_w1[0, 0], rwkv_decay_w1[0, 1]], axis=1)
    ia1 = jnp.concatenate([rwkv_iclr_a1[0, 0], rwkv_iclr_a1[0, 1]], axis=1)
    lora_w = _mm(flat(xw), dw1, act="tanh", out_dtype=BF16)
    zw = _mm(lora_w, _block_diag2(rwkv_decay_w2[0]), rwkv_decay_w0[0].reshape(1, 2 * d))
    lora_a = _mm(flat(xa), ia1, out_dtype=BF16)
    za = _mm(lora_a, _block_diag2(rwkv_iclr_a2[0]), rwkv_iclr_a0[0].reshape(1, 2 * d))
    lora_g = _mm(flat(xg), rwkv_gate_g1[0], act="sigmoid", out_dtype=BF16)
    gate = _mm(lora_g, rwkv_gate_g2[0], out_dtype=BF16)
    seq = lambda a: a.reshape(b, l, a.shape[-1])
    y_scan = _scan(seq(r), seq(k), seq(v), seq(zw), seq(za), rwkv_k_k[0], rwkv_k_a[0], n_ctx // CHUNK)
    og = _rwkv_post(y_scan, seq(r), seq(k), seq(v), seq(za), seq(gate), rwkv_k_a[0],
                    rwkv_r_k[0].reshape(d), rwkv_gn_w[0], rwkv_gn_b[0], n_ctx)
    x1, h1 = _proj_norm(og.reshape(n, d), rwkv_w_o[0], x.reshape(n, d), pos, lat(0, 2),
                        ln_w[0, 0], ln_b[0, 0], lat(0, 4), lat(0, 3), alpha)
    y_ffn = _ffn(h1, ffn_w1[0], ffn_w3[0], ffn_w2[0])
    x2 = _residual_norm(x1, y_ffn, lat(0, 5), ln_w[0, 1], ln_b[0, 1], alpha, t)

    y_pool = _pool(x2.reshape(b, t, d), lat(1, 1), lat(1, 0), pool_w[0], pool_scale[0])
    x3, h3, route = _route(x2, y_pool.reshape(n, d), lat(1, 2), ln_w[1, 0], ln_b[1, 0],
                           lat(1, 4), lat(1, 3), moe_router[0], alpha, t)
    n_pad = ((2 * n + N_EXPERTS * (MOE_CHUNK - 1)) // MOE_CHUNK) * MOE_CHUNK
    src_idx, tile_valid, chunk_expert, chunk_valid, pos0, pos1 = _routing_tables(route, n_pad)
    hs = _gather_rows(h3, src_idx, tile_valid)
    ys = _moe_ffn(hs, chunk_expert, chunk_valid, moe_w1[0], moe_w3[0], moe_w2[0])
    gates = jnp.stack([route[2], route[3]], axis=1)
    out = _combine(ys, pos0, pos1, gates, x3, lat(1, 5), ln_w[1, 1], ln_b[1, 1], alpha, t)
    return out.reshape(b, t, d)
```

```python
import functools
import math

import numpy as np
import jax
import jax.numpy as jnp
from jax import lax
from jax.experimental import pallas as pl
from jax.experimental.pallas import tpu as pltpu

F32 = jnp.float32
BF16 = jnp.bfloat16

HEAD = 64
GROUP = 4 * HEAD
CHUNK = 64
LN_EPS = 1e-5
GN_EPS = 64e-5
POS_BASE = 10000.0
POOL_WINDOWS = (2, 4, 8, 16)
N_EXPERTS = 8
MOE_CHUNK = 1280
MOE_SUB = 256
VMEM_LIMIT_BYTES = 56 * 1024 * 1024
MOE_VMEM_LIMIT_BYTES = 60 * 1024 * 1024


def _params(semantics, vmem_limit_bytes=VMEM_LIMIT_BYTES):
    return pltpu.CompilerParams(dimension_semantics=semantics, vmem_limit_bytes=vmem_limit_bytes)


def _tile(n, preferred):
    t = min(preferred, n)
    while n % t:
        t //= 2
    return t


def _dot(a, b):
    return jnp.dot(a, b, preferred_element_type=F32)


def _dot_nt(a, b):
    return lax.dot_general(a, b, (((1,), (1,)), ((), ())), preferred_element_type=F32)


def _dot_tn(a, b):
    return lax.dot_general(a, b, (((0,), (0,)), ((), ())), preferred_element_type=F32)


def _split2(x):
    hi = x.astype(BF16)
    lo = (x - hi.astype(F32)).astype(BF16)
    return hi, lo


def _split3(x):
    hi = x.astype(BF16)
    r1 = x - hi.astype(F32)
    mid = r1.astype(BF16)
    lo = (r1 - mid.astype(F32)).astype(BF16)
    return hi, mid, lo


def _layer_norm(z, w, b):
    mu = jnp.mean(z, axis=-1, keepdims=True)
    zc = z - mu
    var = jnp.mean(zc * zc, axis=-1, keepdims=True)
    return zc * lax.rsqrt(var + LN_EPS) * w + b


def _adaln_kernel(c_ref, w_ref, b_ref, o_ref):
    c = c_ref[...]
    a = (c * jax.nn.sigmoid(c)).astype(BF16)
    o_ref[0] = _dot(a, w_ref[0].astype(BF16)) + b_ref[0]


def _adaln(cond, w_mod, b_mod, tn=1024):
    depth, d, n = w_mod.shape
    tn = _tile(n, tn)
    rows = cond.shape[0]
    return pl.pallas_call(
        _adaln_kernel,
        out_shape=jax.ShapeDtypeStruct((depth, rows, n), F32),
        grid=(depth, n // tn),
        in_specs=[pl.BlockSpec((rows, d), lambda l, j: (0, 0)),
                  pl.BlockSpec((1, d, tn), lambda l, j: (l, 0, j)),
                  pl.BlockSpec((1, 1, tn), lambda l, j: (l, 0, j))],
        out_specs=pl.BlockSpec((1, rows, tn), lambda l, j: (l, 0, j)),
        compiler_params=_params(("parallel", "parallel")),
        name="adaln",
    )(cond, w_mod, b_mod.reshape(depth, 1, n))


def _mm_kernel(a_ref, w_ref, b_ref, o_ref, *, act):
    acc = _dot(a_ref[...], w_ref[...].astype(BF16)) + b_ref[...]
    if act == "tanh":
        acc = jnp.tanh(acc)
    elif act == "sigmoid":
        acc = jax.nn.sigmoid(acc)
    o_ref[...] = acc.astype(o_ref.dtype)


def _mm(a, w, bias=None, *, act=None, out_dtype=F32, tm=2304, tn=512):
    m, k = a.shape
    n = w.shape[1]
    tm = _tile(m, tm)
    tn = _tile(n, tn)
    if bias is None:
        bias = jnp.zeros((1, n), F32)
    return pl.pallas_call(
        functools.partial(_mm_kernel, act=act),
        out_shape=jax.ShapeDtypeStruct((m, n), out_dtype),
        grid=(m // tm, n // tn),
        in_specs=[pl.BlockSpec((tm, k), lambda i, j: (i, 0)),
                  pl.BlockSpec((k, tn), lambda i, j: (0, j)),
                  pl.BlockSpec((1, tn), lambda i, j: (0, j))],
        out_specs=pl.BlockSpec((tm, tn), lambda i, j: (i, j)),
        compiler_params=_params(("parallel", "arbitrary")),
        name="matmul",
    )(a, w, bias.reshape(1, n))


def _mix_kernel(ctx_ref, x_ref, xp_ref, xn_ref, pos_ref, pp_ref, pn_ref,
                sc_ref, sh_ref, csc_ref, csh_ref, mu_ref,
                o0, o1, o2, o3, o4, o5, *, n_lat_tiles):
    s = pl.program_id(1)
    is_ctx = s == 0
    tm = x_ref.shape[1]
    scale = jnp.where(is_ctx, csc_ref[0], sc_ref[0]) + 1.0
    shift = jnp.where(is_ctx, csh_ref[0], sh_ref[0])
    src = jnp.where(is_ctx, ctx_ref[0], x_ref[0] + pos_ref[...])
    h = src * scale + shift
    has_prev = s > 1
    has_next = jnp.logical_and(s >= 1, s < n_lat_tiles)
    h_prev = jnp.where(has_prev, (xp_ref[0] + pp_ref[...]) * scale + shift, 0.0)[7:8]
    h_next = jnp.where(has_next, (xn_ref[0] + pn_ref[...]) * scale + shift, 0.0)[0:1]
    row = lax.broadcasted_iota(jnp.int32, h.shape, 0)
    h_m1 = jnp.where(row == 0, h_prev, pltpu.roll(h, 1, axis=0))
    h_p1 = jnp.where(row == tm - 1, h_next, pltpu.roll(h, tm - 1, axis=0))
    xx = 0.5 * (h_m1 + h_p1) - h
    for n, o_ref in enumerate((o0, o1, o2, o3, o4, o5)):
        o_ref[0] = (h + xx * mu_ref[n:n + 1]).astype(o_ref.dtype)


def _rwkv_mix(ctx, x, pos, sc, sh, csc, csh, mu):
    b, t, d = x.shape
    tm = ctx.shape[1]
    assert t % tm == 0 and tm % 8 == 0
    n_lat = t // tm
    r8 = tm // 8
    lat = lambda bi, s: (bi, jnp.maximum(s - 1, 0), 0)
    prev8 = lambda bi, s: (bi, jnp.maximum((s - 1) * r8 - 1, 0), 0)
    next8 = lambda bi, s: (bi, jnp.minimum(jnp.maximum(s, 1) * r8, t // 8 - 1), 0)
    vec = pl.BlockSpec((1, 1, d), lambda bi, s: (bi, 0, 0))
    cvec = pl.BlockSpec((1, 1, d), lambda bi, s: (0, 0, 0))
    out_sds = jax.ShapeDtypeStruct((b, tm + t, d), BF16)
    return pl.pallas_call(
        functools.partial(_mix_kernel, n_lat_tiles=n_lat),
        out_shape=[out_sds] * 6,
        grid=(b, n_lat + 1),
        in_specs=[pl.BlockSpec((1, tm, d), lambda bi, s: (bi, 0, 0)),
                  pl.BlockSpec((1, tm, d), lat),
                  pl.BlockSpec((1, 8, d), prev8),
                  pl.BlockSpec((1, 8, d), next8),
                  pl.BlockSpec((tm, d), lambda bi, s: (jnp.maximum(s - 1, 0), 0)),
                  pl.BlockSpec((8, d), lambda bi, s: (jnp.maximum((s - 1) * r8 - 1, 0), 0)),
                  pl.BlockSpec((8, d), lambda bi, s: (jnp.minimum(jnp.maximum(s, 1) * r8, t // 8 - 1), 0)),
                  vec, vec, cvec, cvec,
                  pl.BlockSpec((6, d), lambda bi, s: (0, 0))],
        out_specs=[pl.BlockSpec((1, tm, d), lambda bi, s: (bi, s, 0))] * 6,
        compiler_params=_params(("parallel", "parallel")),
        name="rwkv_mix",
    )(ctx, x, x, x, pos, pos, pos, sc, sh, csc, csh, mu)


def _block_diag(x, bmask):
    xb = x.astype(BF16)
    return jnp.concatenate([xb, xb, xb, xb], axis=0) * bmask


def _fold_heads(full, bmask_f32):
    m = full * bmask_f32
    return m[0:HEAD] + m[HEAD:2 * HEAD] + m[2 * HEAD:3 * HEAD] + m[3 * HEAD:4 * HEAD]


SUB = 16


def _block_diag16(x, bmask16):
    xb = x.astype(BF16)
    return jnp.concatenate([xb] * (GROUP // SUB), axis=0) * bmask16


def _dot3_bd16(a, b, bmask16):
    rows = a.shape[0]
    a_hi, a_lo = _split2(a)
    b_hi, b_lo = _split2(b)
    main = _dot(jnp.concatenate([a_hi, a_lo], axis=0), _block_diag16(b_hi, bmask16))
    return main[:rows] + main[rows:] + _dot(a_hi, _block_diag16(b_lo, bmask16))


def _unit_triangular_inverse(l_mats, bmask, bmask16, eye16, diag16, off_a, off_b):
    nq = CHUNK // SUB
    l16 = [sum(l[q * SUB:(q + 1) * SUB] * diag16[q * SUB:(q + 1) * SUB] for q in range(nq)) for l in l_mats]
    t16 = [eye16 + x for x in l16]
    l_pow = [_dot3_bd16(x, x, bmask16) for x in l16]
    for _ in range(2):
        both = [_dot3_bd16(jnp.concatenate([t, lp], axis=0), lp, bmask16) for t, lp in zip(t16, l_pow)]
        t16 = [t + bo[:SUB] for t, bo in zip(t16, both)]
        l_pow = [bo[SUB:] for bo in both]
    t16 = [t + _dot3_bd16(t, lp, bmask16) for t, lp in zip(t16, l_pow)]
    d = [jnp.concatenate([t] * nq, axis=0) * diag16 for t in t16]
    for off in (off_a, off_b):
        x = [_dot(di.astype(BF16), _block_diag(l * off, bmask)) for di, l in zip(d, l_mats)]
        d = [di + _dot(xi.astype(BF16), _block_diag(di, bmask)) for di, xi in zip(d, x)]
    return d


def _scan_kernel(r_ref, k_ref, v_ref, zw_ref, za_ref, kk_ref, ka_ref,
                 bmask_ref, bmask16_ref, tri_ref, dm_ref, cm_ref, eye16_ref,
                 y_ref, s_ref, p_ref, rh_ref, q_ref, yl_ref):
    ng = r_ref.shape[2] // GROUP
    bmask = bmask_ref[...]

    @pl.when(pl.program_id(2) == 0)
    def _():
        s_ref[...] = jnp.zeros_like(s_ref)
        p_ref[...] = jnp.zeros_like(p_ref)
        rh_ref[...] = jnp.zeros_like(rh_ref)
        q_ref[...] = jnp.zeros_like(q_ref)
        yl_ref[...] = jnp.zeros_like(yl_ref)

    for g in range(ng):
        cols = slice(g * GROUP, (g + 1) * GROUP)
        s_bd = _block_diag(s_ref[:, cols], bmask)
        out = _dot(jnp.concatenate([p_ref[:, cols], rh_ref[:, cols]], axis=0), s_bd)
        s_ref[:, cols] = out[:CHUNK] + q_ref[:, cols]
        y_ref[0, 0, :, cols] = (out[CHUNK:] + yl_ref[:, cols]).astype(y_ref.dtype)

    bmask_f = bmask.astype(F32)
    eye = cm_ref[0]
    m_strict = dm_ref[0, 0]
    m_incl = dm_ref[0, 1]
    groups = lambda x: [x[:, g * GROUP:(g + 1) * GROUP] for g in range(ng)]
    bd = lambda x: _block_diag(x, bmask)
    stack = lambda x, y: jnp.concatenate([x, y], axis=0)

    r = r_ref[0].astype(F32)
    k = k_ref[0].astype(F32)
    v = v_ref[0].astype(F32)
    lw = (-math.exp(-0.5)) * jax.nn.sigmoid(zw_ref[0].astype(F32))
    a = jax.nn.sigmoid(za_ref[0].astype(F32))
    kkr = k * kk_ref[...]
    sq = jnp.concatenate(groups(kkr * kkr), axis=0)
    sq_hi, sq_lo = _split2(sq)
    ssq = _dot(sq_hi, bmask) + _dot(sq_lo, bmask)
    ssq = jnp.concatenate([ssq[g * CHUNK:(g + 1) * CHUNK] for g in range(ng)], axis=1)
    kk = kkr * lax.rsqrt(jnp.maximum(ssq, 1e-24))
    kd = k * (1.0 + (a - 1.0) * ka_ref[...])
    bb = kk * a
    tri = tri_ref[0]
    lw_hi, lw_lo = _split2(lw)
    g_cum = _dot(tri, lw_hi) + _dot(tri, lw_lo)
    g_end = jnp.sum(lw, axis=0, keepdims=True)
    e_neg = jnp.exp(-g_cum)
    e_end = jnp.exp(g_end - g_cum)
    a_t = groups(-kk * jnp.exp(g_cum - lw))
    r_t = groups(r * jnp.exp(g_cum))
    b_t = groups(bb * e_neg)
    k_t = groups(kd * e_neg)
    b_h = groups((bb * e_end).astype(BF16))
    k_h = groups((kd * e_end).astype(BF16))
    v_g = groups(v)
    decay_end = groups(jnp.exp(g_end))

    ar = [stack(x, y).astype(BF16) for x, y in zip(a_t, r_t)]
    mb = [_dot_nt(x, bd(y)) for x, y in zip(ar, b_t)]
    mk = [_dot_nt(x, bd(y)) for x, y in zip(ar, k_t)]
    l_mat = [x[:CHUNK] * m_strict for x in mb]
    m_rb = [(x[CHUNK:] * m_incl).astype(BF16) for x in mb]
    m_k = [stack(x[:CHUNK] * m_strict, x[CHUNK:] * m_incl).astype(BF16) for x in mk]
    mv = [_dot(x, bd(y)) for x, y in zip(m_k, v_g)]
    t_mat = _unit_triangular_inverse(l_mat, bmask, bmask16_ref[...], eye16_ref[...],
                                     cm_ref[1], cm_ref[2], cm_ref[3])
    t_b = [x.astype(BF16) for x in t_mat]
    a_h = [_dot(x, bd(y)).astype(BF16) for x, y in zip(t_b, a_t)]
    u_0 = [_dot(x, bd(y[:CHUNK])) for x, y in zip(t_b, mv)]
    rh = [x + _dot(m, bd(y)) for x, m, y in zip(r_t, m_rb, a_h)]
    yl = [_dot(m, bd(u)) + y[CHUNK:] for m, u, y in zip(m_rb, u_0, mv)]
    p_full = [_dot_tn(x, y) for x, y in zip(b_h, a_h)]
    q_full = [_dot_tn(stack(x, y), stack(u, w).astype(BF16)) for x, y, u, w in zip(b_h, k_h, u_0, v_g)]
    for g in range(ng):
        cols = slice(g * GROUP, (g + 1) * GROUP)
        p_ref[:, cols] = (_fold_heads(p_full[g], bmask_f) + eye * decay_end[g]).astype(p_ref.dtype)
        rh_ref[:, cols] = rh[g].astype(rh_ref.dtype)
        q_ref[:, cols] = _fold_heads(q_full[g], bmask_f)
        yl_ref[:, cols] = yl[g]


def _scan_consts():
    lane = np.arange(GROUP)
    bmask = (lane[:, None] // HEAD == lane[None, :] // HEAD).astype(np.float32)
    i = np.arange(CHUNK)[:, None]
    j = np.arange(CHUNK)[None, :]
    jl = (lane % HEAD)[None, :]
    tri = np.stack([(j <= i), (j >= i)]).astype(np.float32)
    dir_masks = np.stack([np.stack([(jl < i), (jl <= i)]),
                          np.stack([(jl > i), (jl >= i)])]).astype(np.float32)
    same16 = (jl // SUB == i // SUB)
    same32 = (jl // (2 * SUB) == i // (2 * SUB))
    common = np.stack([(jl == i), same16, same32 & ~same16, ~same32]).astype(np.float32)
    bmask16 = (lane[:, None] // SUB == lane[None, :] // SUB).astype(np.float32)
    eye16 = ((lane % SUB)[None, :] == np.arange(SUB)[:, None]).astype(np.float32)
    return (jnp.asarray(bmask, BF16), jnp.asarray(bmask16, BF16), jnp.asarray(tri, BF16),
            jnp.asarray(dir_masks, F32), jnp.asarray(common, F32), jnp.asarray(eye16, F32))


def _scan(r, k, v, zw, za, k_k, k_a, n_ctx_chunks):
    b, l, d = r.shape
    nc = l // CHUNK
    bmask, bmask16, tri, dir_masks, common, eye16 = _scan_consts()

    def chunk_of(di, s):
        back = jnp.where(s < n_ctx_chunks, n_ctx_chunks - 1 - s, nc - 1 + n_ctx_chunks - s)
        return jnp.where(di == 0, s, back)

    fold_chunk = lambda di, s: chunk_of(di, jnp.minimum(s, nc - 1))
    apply_chunk = lambda di, s: chunk_of(di, jnp.maximum(s - 1, 0))
    tok = pl.BlockSpec((1, CHUNK, d), lambda di, bi, s: (bi, fold_chunk(di, s), 0))
    tok2 = pl.BlockSpec((1, CHUNK, d), lambda di, bi, s: (bi, fold_chunk(di, s), di))
    par = pl.BlockSpec((1, d), lambda di, bi, s: (0, 0))
    return pl.pallas_call(
        _scan_kernel,
        out_shape=jax.ShapeDtypeStruct((2, b, l, d), BF16),
        grid=(2, b, nc + 1),
        in_specs=[tok, tok, tok, tok2, tok2, par, par,
                  pl.BlockSpec((GROUP, GROUP), lambda di, bi, s: (0, 0)),
                  pl.BlockSpec((GROUP, GROUP), lambda di, bi, s: (0, 0)),
                  pl.BlockSpec((1, CHUNK, CHUNK), lambda di, bi, s: (di, 0, 0)),
                  pl.BlockSpec((1, 2, CHUNK, GROUP), lambda di, bi, s: (di, 0, 0, 0)),
                  pl.BlockSpec((4, CHUNK, GROUP), lambda di, bi, s: (0, 0, 0)),
                  pl.BlockSpec((SUB, GROUP), lambda di, bi, s: (0, 0))],
        out_specs=pl.BlockSpec((1, 1, CHUNK, d), lambda di, bi, s: (di, bi, apply_chunk(di, s), 0)),
        scratch_shapes=[pltpu.VMEM((CHUNK, d), F32), pltpu.VMEM((CHUNK, d), BF16),
                        pltpu.VMEM((CHUNK, d), BF16), pltpu.VMEM((CHUNK, d), F32),
                        pltpu.VMEM((CHUNK, d), F32)],
        compiler_params=_params(("parallel", "parallel", "arbitrary")),
        name="scan",
    )(r, k, v, zw, za, k_k.reshape(1, d), k_a.reshape(1, d), bmask, bmask16, tri, dir_masks, common, eye16)


def _head_sum(x, ones_bd):
    hi, lo = _split2(x)
    return _dot(hi, ones_bd) + _dot(lo, ones_bd)


def _rwkv_post_kernel(y_ref, r_ref, k_ref, v_ref, za0_ref, za1_ref, gate_ref,
                      ka_ref, rk_ref, gnw_ref, gnb_ref, ones_ref, o_ref):
    ones_bd = ones_ref[...]
    tm, d = o_ref.shape[1], o_ref.shape[2]
    ng = d // GROUP
    to_rows = lambda x: jnp.concatenate([x[:, g * GROUP:(g + 1) * GROUP] for g in range(ng)], axis=0)
    to_cols = lambda x: jnp.concatenate([x[g * tm:(g + 1) * tm] for g in range(ng)], axis=1)
    head_mean = lambda x: to_cols(_head_sum(to_rows(x), ones_bd)) * (1.0 / HEAD)
    y = y_ref[0, 0].astype(F32) + y_ref[1, 0].astype(F32)
    yc = y - head_mean(y)
    var = head_mean(yc * yc)
    o = yc * lax.rsqrt(var + GN_EPS) * gnw_ref[...] + gnb_ref[...]
    a_sum = jax.nn.sigmoid(za0_ref[0].astype(F32)) + jax.nn.sigmoid(za1_ref[0].astype(F32))
    r = r_ref[0].astype(F32)
    k_sum = k_ref[0].astype(F32) * (2.0 + (a_sum - 2.0) * ka_ref[...])
    bonus = head_mean(r * k_sum * rk_ref[...]) * float(HEAD) * v_ref[0].astype(F32)
    o_ref[0] = ((o + bonus) * gate_ref[0].astype(F32)).astype(o_ref.dtype)


def _rwkv_post(y, r, k, v, za, gate, k_a, r_k, gn_w, gn_b, n_ctx, tm=256):
    _, b, l, d = y.shape
    t = l - n_ctx
    off = n_ctx // tm
    ones_bd = _scan_consts()[0]
    tok = pl.BlockSpec((1, tm, d), lambda bi, s: (bi, s + off, 0))
    par = pl.BlockSpec((1, d), lambda bi, s: (0, 0))
    return pl.pallas_call(
        _rwkv_post_kernel,
        out_shape=jax.ShapeDtypeStruct((b, t, d), BF16),
        grid=(b, t // tm),
        in_specs=[pl.BlockSpec((2, 1, tm, d), lambda bi, s: (0, bi, s + off, 0)),
                  tok, tok, tok,
                  pl.BlockSpec((1, tm, d), lambda bi, s: (bi, s + off, 0)),
                  pl.BlockSpec((1, tm, d), lambda bi, s: (bi, s + off, 1)),
                  tok, par, par, par, par,
                  pl.BlockSpec((GROUP, GROUP), lambda bi, s: (0, 0))],
        out_specs=pl.BlockSpec((1, tm, d), lambda bi, s: (bi, s, 0)),
        compiler_params=_params(("parallel", "parallel")),
        name="rwkv_post",
    )(y, r, k, v, za, za, gate, k_a.reshape(1, d), r_k.reshape(1, d), gn_w.reshape(1, d),
      gn_b.reshape(1, d), ones_bd)


def _proj_norm_kernel(a_ref, w_ref, x_ref, pos_ref, g_ref, lnw_ref, lnb_ref, sc_ref, sh_ref,
                      o_ref, h_ref, acc_ref, *, alpha):
    j = pl.program_id(1)
    nj = acc_ref.shape[0]
    tn = acc_ref.shape[2]
    acc_ref[j] = _dot(a_ref[...], w_ref[...].astype(BF16))

    @pl.when(j == nj - 1)
    def _():
        gate = g_ref[0]
        for jj in range(nj):
            cols = slice(jj * tn, (jj + 1) * tn)
            o_ref[:, cols] = alpha * (x_ref[:, cols] + pos_ref[:, cols]) + gate[:, cols] * acc_ref[jj]
        x_new = _layer_norm(o_ref[...], lnw_ref[...], lnb_ref[...])
        o_ref[...] = x_new
        h_ref[...] = (x_new * (1.0 + sc_ref[0]) + sh_ref[0]).astype(h_ref.dtype)


def _proj_norm(a, w, x, pos, gate, ln_w, ln_b, sc, sh, alpha, tm=512, tn=512):
    m, k = a.shape
    d = w.shape[1]
    t = pos.shape[0]
    tm = _tile(t, tm)
    tn = _tile(d, tn)
    tpb = t // tm
    vec = pl.BlockSpec((1, 1, d), lambda i, j: (i // tpb, 0, 0))
    return pl.pallas_call(
        functools.partial(_proj_norm_kernel, alpha=alpha),
        out_shape=[jax.ShapeDtypeStruct((m, d), F32), jax.ShapeDtypeStruct((m, d), BF16)],
        grid=(m // tm, d // tn),
        in_specs=[pl.BlockSpec((tm, k), lambda i, j: (i, 0)),
                  pl.BlockSpec((k, tn), lambda i, j: (0, j)),
                  pl.BlockSpec((tm, d), lambda i, j: (i, 0)),
                  pl.BlockSpec((tm, d), lambda i, j: (i % tpb, 0)),
                  vec,
                  pl.BlockSpec((1, d), lambda i, j: (0, 0)),
                  pl.BlockSpec((1, d), lambda i, j: (0, 0)),
                  vec, vec],
        out_specs=[pl.BlockSpec((tm, d), lambda i, j: (i, 0))] * 2,
        scratch_shapes=[pltpu.VMEM((d // tn, tm, tn), F32)],
        compiler_params=_params(("parallel", "arbitrary")),
        name="proj_norm",
    )(a, w, x, pos, gate, ln_w.reshape(1, d), ln_b.reshape(1, d), sc, sh)


def _swiglu_halves(h, w1_ref, w3_ref, w2_ref, lead):
    tf = w1_ref.shape[-1]
    halves = [slice(0, tf // 2), slice(tf // 2, tf)]
    a1 = [_dot(h, w1_ref[lead + (slice(None), c)].astype(BF16)) for c in halves]
    a3 = [_dot(h, w3_ref[lead + (slice(None), c)].astype(BF16)) for c in halves]
    u = [(x * jax.nn.sigmoid(x) * y).astype(BF16) for x, y in zip(a1, a3)]
    y = [_dot(x, w2_ref[lead + (c, slice(None))].astype(BF16)) for x, c in zip(u, halves)]
    return y[0] + y[1]


def _ffn_kernel(h_ref, w1_ref, w3_ref, w2_ref, o_ref, acc_ref):
    j = pl.program_id(1)

    @pl.when(j == 0)
    def _():
        acc_ref[...] = jnp.zeros_like(acc_ref)

    acc_ref[...] += _swiglu_halves(h_ref[...], w1_ref, w3_ref, w2_ref, ())

    @pl.when(j == pl.num_programs(1) - 1)
    def _():
        o_ref[...] = acc_ref[...].astype(o_ref.dtype)


def _ffn(h, w1, w3, w2, tm=1024, tf=512):
    m, d = h.shape
    f = w1.shape[1]
    tm = _tile(m, tm)
    tf = _tile(f, tf)
    return pl.pallas_call(
        _ffn_kernel,
        out_shape=jax.ShapeDtypeStruct((m, d), BF16),
        grid=(m // tm, f // tf),
        in_specs=[pl.BlockSpec((tm, d), lambda i, j: (i, 0)),
                  pl.BlockSpec((d, tf), lambda i, j: (0, j)),
                  pl.BlockSpec((d, tf), lambda i, j: (0, j)),
                  pl.BlockSpec((tf, d), lambda i, j: (j, 0))],
        out_specs=pl.BlockSpec((tm, d), lambda i, j: (i, 0)),
        scratch_shapes=[pltpu.VMEM((tm, d), F32)],
        compiler_params=_params(("parallel", "arbitrary")),
        name="ffn",
    )(h, w1, w3, w2)


def _residual_norm_kernel(x_ref, y_ref, g_ref, lnw_ref, lnb_ref, o_ref, *, alpha):
    z = alpha * x_ref[...] + g_ref[0] * y_ref[...].astype(F32)
    o_ref[...] = _layer_norm(z, lnw_ref[...], lnb_ref[...])


def _residual_norm(x, y, gate, ln_w, ln_b, alpha, rows_per_batch, tm=512):
    m, d = x.shape
    tm = _tile(rows_per_batch, tm)
    tpb = rows_per_batch // tm
    tok = pl.BlockSpec((tm, d), lambda i: (i, 0))
    row = pl.BlockSpec((1, d), lambda i: (0, 0))
    return pl.pallas_call(
        functools.partial(_residual_norm_kernel, alpha=alpha),
        out_shape=jax.ShapeDtypeStruct((m, d), F32),
        grid=(m // tm,),
        in_specs=[tok, tok, pl.BlockSpec((1, 1, d), lambda i: (i // tpb, 0, 0)), row, row],
        out_specs=tok,
        compiler_params=_params(("parallel",)),
        name="residual_norm",
    )(x, y, gate, ln_w.reshape(1, d), ln_b.reshape(1, d))


def _shift_down(x, s, row):
    return jnp.where(row >= s, pltpu.roll(x, s, axis=0), 0.0)


def _shift_up(x, s, row):
    t = x.shape[0]
    return jnp.where(row < t - s, pltpu.roll(x, t - s, axis=0), 0.0)


def _pool_kernel(x_ref, sc_ref, sh_ref, w_ref, scale_ref, o_ref):
    g = pl.program_id(0)
    t = x_ref.shape[1]
    h = x_ref[0] * (1.0 + sc_ref[0]) + sh_ref[0]
    row = lax.broadcasted_iota(jnp.int32, h.shape, 0)
    w = w_ref[0].astype(BF16)
    for gi, win in enumerate(POOL_WINDOWS):
        @pl.when(g == gi)
        def _(win=win):
            half = win // 2
            back = h
            fwd = h
            m = 1
            while m < half:
                back = back + _shift_down(back, m, row)
                fwd = fwd + _shift_up(fwd, m, row)
                m *= 2
            total = _shift_down(back, 1, row) + fwd
            count = (jnp.minimum(row + half, t) - jnp.maximum(row - half, 0)).astype(F32)
            pooled = (total / count - h).astype(BF16)
            o_ref[0] = _dot(pooled, w) * scale_ref[...]


def _pool(x, sc, sh, w_pool, scale):
    b, t, d = x.shape
    ng, p, _ = w_pool.shape
    vec = pl.BlockSpec((1, 1, p), lambda g, bi: (bi, 0, g))
    return pl.pallas_call(
        _pool_kernel,
        out_shape=jax.ShapeDtypeStruct((b, t, d), F32),
        grid=(ng, b),
        in_specs=[pl.BlockSpec((1, t, p), lambda g, bi: (bi, 0, g)), vec, vec,
                  pl.BlockSpec((1, p, p), lambda g, bi: (g, 0, 0)),
                  pl.BlockSpec((1, p), lambda g, bi: (0, g))],
        out_specs=pl.BlockSpec((1, t, p), lambda g, bi: (bi, 0, g)),
        compiler_params=_params(("parallel", "parallel")),
        name="pool",
    )(x, sc, sh, w_pool, scale.reshape(1, d))


def _route_kernel(x_ref, y_ref, g_ref, lnw_ref, lnb_ref, sc_ref, sh_ref, rt_ref,
                  xo_ref, h_ref, route_ref, *, alpha):
    x = _layer_norm(alpha * x_ref[...] + g_ref[0] * y_ref[...], lnw_ref[...], lnb_ref[...])
    xo_ref[...] = x
    h = x * (1.0 + sc_ref[0]) + sh_ref[0]
    h_ref[...] = h
    r1, r2, r3 = _split3(rt_ref[...])
    h1, h2, h3 = _split3(h)
    logits = (_dot_nt(r1, h1) + (_dot_nt(r1, h2) + _dot_nt(r2, h1))
              + (_dot_nt(r1, h3) + _dot_nt(r2, h2) + _dot_nt(r3, h1)))
    mx = jnp.max(logits, axis=0, keepdims=True)
    e = jnp.exp(logits - mx)
    p = e / jnp.sum(e, axis=0, keepdims=True)
    idx = lax.broadcasted_iota(jnp.int32, p.shape, 0)
    p1 = jnp.max(p, axis=0, keepdims=True)
    i1 = jnp.min(jnp.where(p == p1, idx, N_EXPERTS), axis=0, keepdims=True)
    rest = jnp.where(idx == i1, -1.0, p)
    p2 = jnp.max(rest, axis=0, keepdims=True)
    i2 = jnp.min(jnp.where(rest == p2, idx, N_EXPERTS), axis=0, keepdims=True)
    den = p1 + p2
    out = jnp.where(idx == 0, i1.astype(F32), 0.0)
    out = jnp.where(idx == 1, i2.astype(F32), out)
    out = jnp.where(idx == 2, p1 / den, out)
    out = jnp.where(idx == 3, p2 / den, out)
    route_ref[...] = out


def _route(x, y, gate, ln_w, ln_b, sc, sh, router, alpha, rows_per_batch, tm=256):
    m, d = x.shape
    tpb = rows_per_batch // tm
    vec = pl.BlockSpec((1, 1, d), lambda i: (i // tpb, 0, 0))
    row = pl.BlockSpec((1, d), lambda i: (0, 0))
    tok = pl.BlockSpec((tm, d), lambda i: (i, 0))
    return pl.pallas_call(
        functools.partial(_route_kernel, alpha=alpha),
        out_shape=[jax.ShapeDtypeStruct((m, d), F32), jax.ShapeDtypeStruct((m, d), F32),
                   jax.ShapeDtypeStruct((N_EXPERTS, m), F32)],
        grid=(m // tm,),
        in_specs=[tok, tok, vec, row, row, vec, vec,
                  pl.BlockSpec((N_EXPERTS, d), lambda i: (0, 0))],
        out_specs=[tok, tok, pl.BlockSpec((N_EXPERTS, tm), lambda i: (0, i))],
        compiler_params=_params(("parallel",)),
        name="route",
    )(x, y, gate, ln_w.reshape(1, d), ln_b.reshape(1, d), sc, sh, router.T)


def _row_copy(src_hbm, dst_vmem, sem, src_row, dst_row):
    return pltpu.make_async_copy(src_hbm.at[pl.ds(src_row, 1)], dst_vmem.at[pl.ds(dst_row, 1)], sem)


def _moe_kernel(ce_ref, nv_ref, idx_ref, h_ref, w1_ref, w3_ref, w2_ref, ys_ref,
                stage_ref, work_ref, acc_ref, gather_sem, out_sem, *, share):
    c = pl.program_id(0)
    j = pl.program_id(1)
    n_chunks = pl.num_programs(0)
    nj = pl.num_programs(1)
    nv = nv_ref[c]
    full = work_ref.shape[0]
    stage_rows = stage_ref.shape[0]

    def gather(chunk, r):
        return _row_copy(h_ref, stage_ref, gather_sem, idx_ref[chunk * full + r], r)

    def result_copy(chunk):
        return pltpu.make_async_copy(acc_ref, ys_ref.at[pl.ds(chunk * full, full)], out_sem)

    def wait_all_gathers():
        def body(r, carry):
            gather(0, r).wait()
            return carry
        lax.fori_loop(0, stage_rows, body, 0, unroll=8)

    @pl.when(j == 0)
    def _():
        @pl.when(c == 0)
        def _():
            def body(r, carry):
                gather(0, r).start()
                return carry
            lax.fori_loop(0, stage_rows, body, 0, unroll=8)

        @pl.when(c > 0)
        def _():
            result_copy(c - 1).wait()

        wait_all_gathers()
        work_ref[...] = stage_ref[0:full, :].astype(work_ref.dtype)
        acc_ref[...] = jnp.zeros_like(acc_ref)

    def request_next_share():
        for i in range(share):
            gather(c + 1, j * share + i).start()

    def swiglu_rows(rows):
        acc_ref[rows, :] += _swiglu_halves(work_ref[rows, :], w1_ref, w3_ref, w2_ref, (0,))

    n_sub = (nv + (MOE_SUB - 1)) // MOE_SUB
    for k in range(full // MOE_SUB + 1):
        @pl.when(n_sub == k)
        def _(k=k):
            request_next_share()
            if k:
                swiglu_rows(slice(0, k * MOE_SUB))

    @pl.when(j == nj - 1)
    def _():
        result_copy(c).start()

        @pl.when(c == n_chunks - 1)
        def _():
            result_copy(c).wait()
            wait_all_gathers()


def _moe_ffn(h, src_idx, chunk_expert, chunk_valid, w1, w3, w2, n_chunks, tf=512):
    d = h.shape[1]
    f = w1.shape[2]
    tf = _tile(f, tf)
    nj = f // tf
    share = -(-MOE_CHUNK // nj)
    share = -(-share // 8) * 8
    stage_rows = share * nj
    assert src_idx.shape[0] >= n_chunks * MOE_CHUNK + stage_rows

    def jeff(c, j, nv):
        return jnp.where(nv[c] > 0, j, nj - 1)

    return pl.pallas_call(
        functools.partial(_moe_kernel, share=share),
        out_shape=jax.ShapeDtypeStruct((n_chunks * MOE_CHUNK, d), F32),
        grid_spec=pltpu.PrefetchScalarGridSpec(
            num_scalar_prefetch=3,
            grid=(n_chunks, nj),
            in_specs=[pl.BlockSpec(memory_space=pl.ANY),
                      pl.BlockSpec((1, d, tf), lambda c, j, ce, nv, ix: (ce[c], 0, jeff(c, j, nv))),
                      pl.BlockSpec((1, d, tf), lambda c, j, ce, nv, ix: (ce[c], 0, jeff(c, j, nv))),
                      pl.BlockSpec((1, tf, d), lambda c, j, ce, nv, ix: (ce[c], jeff(c, j, nv), 0))],
            out_specs=pl.BlockSpec(memory_space=pl.ANY),
            scratch_shapes=[pltpu.VMEM((stage_rows, d), F32), pltpu.VMEM((MOE_CHUNK, d), BF16),
                            pltpu.VMEM((MOE_CHUNK, d), F32),
                            pltpu.SemaphoreType.DMA, pltpu.SemaphoreType.DMA]),
        compiler_params=_params(("arbitrary", "arbitrary"), MOE_VMEM_LIMIT_BYTES),
        name="moe_ffn",
    )(chunk_expert, chunk_valid, src_idx, h, w1, w3, w2)


def _combine_kernel(p0_ref, p1_ref, ys_ref, x_ref, gates_ref, g_ref, lnw_ref, lnb_ref,
                    o_ref, b0_ref, b1_ref, sem, *, alpha):
    i = pl.program_id(0)
    n = pl.num_programs(0)
    rows = b0_ref.shape[1]
    slot = i % 2

    def request(tile, to_slot):
        def body(r, carry):
            _row_copy(ys_ref, b0_ref.at[to_slot], sem.at[to_slot], p0_ref[tile * rows + r], r).start()
            _row_copy(ys_ref, b1_ref.at[to_slot], sem.at[to_slot], p1_ref[tile * rows + r], r).start()
            return carry
        lax.fori_loop(0, rows, body, 0, unroll=4)

    @pl.when(i == 0)
    def _():
        request(0, 0)

    @pl.when(i + 1 < n)
    def _():
        request(jnp.minimum(i + 1, n - 1), 1 - slot)

    def wait(r, carry):
        _row_copy(ys_ref, b0_ref.at[slot], sem.at[slot], 0, r).wait()
        _row_copy(ys_ref, b1_ref.at[slot], sem.at[slot], 0, r).wait()
        return carry

    lax.fori_loop(0, rows, wait, 0, unroll=4)
    gates = gates_ref[...]
    y = gates[:, 0:1] * b0_ref[slot] + gates[:, 1:2] * b1_ref[slot]
    z = alpha * x_ref[...] + g_ref[0] * y
    o_ref[...] = _layer_norm(z, lnw_ref[...], lnb_ref[...])


def _combine(ys, pos0, pos1, gates, x, gate_vec, ln_w, ln_b, alpha, rows_per_batch, rows=256):
    m, d = x.shape
    tpb = rows_per_batch // rows
    tok = lambda i, a, b: (i, 0)
    return pl.pallas_call(
        functools.partial(_combine_kernel, alpha=alpha),
        out_shape=jax.ShapeDtypeStruct((m, d), F32),
        grid_spec=pltpu.PrefetchScalarGridSpec(
            num_scalar_prefetch=2,
            grid=(m // rows,),
            in_specs=[pl.BlockSpec(memory_space=pl.ANY),
                      pl.BlockSpec((rows, d), tok),
                      pl.BlockSpec((rows, 2), tok),
                      pl.BlockSpec((1, 1, d), lambda i, a, b: (i // tpb, 0, 0)),
                      pl.BlockSpec((1, d), lambda i, a, b: (0, 0)),
                      pl.BlockSpec((1, d), lambda i, a, b: (0, 0))],
            out_specs=pl.BlockSpec((rows, d), tok),
            scratch_shapes=[pltpu.VMEM((2, rows, d), F32), pltpu.VMEM((2, rows, d), F32),
                            pltpu.SemaphoreType.DMA((2,))]),
        compiler_params=_params(("arbitrary",)),
        name="moe_combine",
    )(pos0, pos1, ys, x, gates, gate_vec, ln_w.reshape(1, d), ln_b.reshape(1, d))


def _routing_tables(route, n_chunks, table_len):
    n = route.shape[1]
    experts = jnp.concatenate([route[0], route[1]]).astype(jnp.int32)
    onehot = (experts[:, None] == jnp.arange(N_EXPERTS, dtype=jnp.int32)[None, :]).astype(jnp.int32)
    csum = jnp.cumsum(onehot, axis=0)
    rank = jnp.sum((csum - 1) * onehot, axis=1)
    counts = csum[-1]
    chunks_e = (counts + MOE_CHUNK - 1) // MOE_CHUNK
    chunk_end = jnp.cumsum(chunks_e)
    chunk_start = chunk_end - chunks_e
    dest = (chunk_start * MOE_CHUNK)[experts] + rank
    token = jnp.arange(2 * n, dtype=jnp.int32) % n
    src_idx = jnp.zeros((table_len,), jnp.int32).at[dest].set(token)
    cid = jnp.arange(n_chunks, dtype=jnp.int32)
    used = cid < chunk_end[-1]
    last_used = jnp.maximum(chunk_end[-1] - 1, 0)
    ce = jnp.sum((jnp.minimum(cid, last_used)[:, None] >= chunk_end[None, :]).astype(jnp.int32), axis=1)
    ce = jnp.minimum(ce, N_EXPERTS - 1)
    nvalid = jnp.clip(counts[ce] - (cid - chunk_start[ce]) * MOE_CHUNK, 0, MOE_CHUNK)
    nvalid = jnp.where(used, nvalid, 0).astype(jnp.int32)
    return src_idx, ce, nvalid, dest[:n], dest[n:]


def _position_embedding(rows, width, d):
    quarter = d // 4
    omega = 1.0 / (POS_BASE ** (jnp.arange(quarter, dtype=F32) / quarter))
    ar = jnp.arange(rows, dtype=F32)[:, None] * omega[None, :]
    ac = jnp.arange(width, dtype=F32)[:, None] * omega[None, :]
    row_part = jnp.repeat(jnp.concatenate([jnp.sin(ar), jnp.cos(ar)], axis=-1), width, axis=0)
    col_part = jnp.tile(jnp.concatenate([jnp.sin(ac), jnp.cos(ac)], axis=-1), (rows, 1))
    return jnp.concatenate([row_part, col_part], axis=-1)


def _block_diag2(w):
    z = jnp.zeros_like(w[0])
    return jnp.concatenate([jnp.concatenate([w[0], z], axis=1), jnp.concatenate([z, w[1]], axis=1)], axis=0)


def kernel(x, c, ctx, c_ctx, w_mod, b_mod, ln_w, ln_b, rwkv_mu, rwkv_w_r, rwkv_w_k, rwkv_w_v, rwkv_w_o, rwkv_decay_w0, rwkv_decay_w1, rwkv_decay_w2, rwkv_iclr_a0, rwkv_iclr_a1, rwkv_iclr_a2, rwkv_gate_g1, rwkv_gate_g2, rwkv_k_k, rwkv_k_a, rwkv_r_k, rwkv_gn_w, rwkv_gn_b, pool_w, pool_scale, ffn_w1, ffn_w3, ffn_w2, moe_router, moe_w1, moe_w3, moe_w2):
    b, t, d = x.shape
    n_ctx = ctx.shape[1]
    depth = w_mod.shape[0]
    assert depth == 2 and rwkv_mu.shape[0] == 1 and pool_w.shape[0] == 1
    alpha = (2.0 * depth) ** 0.25
    grid_w = 64
    l = n_ctx + t
    n = b * t

    cond = jnp.zeros((8, d), F32).at[:b].set(c).at[b].set(c_ctx)
    mod = _adaln(cond, w_mod, b_mod).reshape(depth, 8, 6, d)
    lat = lambda layer, which: mod[layer, :b, which].reshape(b, 1, d)
    cvec = lambda layer, which: mod[layer, b, which].reshape(1, 1, d)

    pos = _position_embedding(t // grid_w, grid_w, d)

    xr, xw, xk, xv, xa, xg = _rwkv_mix(ctx, x, pos, lat(0, 1), lat(0, 0), cvec(0, 1), cvec(0, 0), rwkv_mu[0])
    flat = lambda a: a.reshape(b * l, a.shape[-1])
    r = _mm(flat(xr), rwkv_w_r[0], out_dtype=BF16)
    k = _mm(flat(xk), rwkv_w_k[0], out_dtype=BF16)
    v = _mm(flat(xv), rwkv_w_v[0], out_dtype=BF16)
    dw1 = jnp.concatenate([rwkv_decay_w1[0, 0], rwkv_decay_w1[0, 1]], axis=1)
    ia1 = jnp.concatenate([rwkv_iclr_a1[0, 0], rwkv_iclr_a1[0, 1]], axis=1)
    lora_w = _mm(flat(xw), dw1, act="tanh", out_dtype=BF16)
    zw = _mm(lora_w, _block_diag2(rwkv_decay_w2[0]), rwkv_decay_w0[0].reshape(1, 2 * d), out_dtype=BF16)
    lora_a = _mm(flat(xa), ia1, out_dtype=BF16)
    za = _mm(lora_a, _block_diag2(rwkv_iclr_a2[0]), rwkv_iclr_a0[0].reshape(1, 2 * d), out_dtype=BF16)
    lora_g = _mm(flat(xg), rwkv_gate_g1[0], act="sigmoid", out_dtype=BF16)
    gate = _mm(lora_g, rwkv_gate_g2[0], out_dtype=BF16)
    seq = lambda a: a.reshape(b, l, a.shape[-1])
    y_scan = _scan(seq(r), seq(k), seq(v), seq(zw), seq(za), rwkv_k_k[0], rwkv_k_a[0], n_ctx // CHUNK)
    og = _rwkv_post(y_scan, seq(r), seq(k), seq(v), seq(za), seq(gate), rwkv_k_a[0],
                    rwkv_r_k[0].reshape(d), rwkv_gn_w[0], rwkv_gn_b[0], n_ctx)
    x1, h1 = _proj_norm(og.reshape(n, d), rwkv_w_o[0], x.reshape(n, d), pos, lat(0, 2),
                        ln_w[0, 0], ln_b[0, 0], lat(0, 4), lat(0, 3), alpha)
    y_ffn = _ffn(h1, ffn_w1[0], ffn_w3[0], ffn_w2[0])
    x2 = _residual_norm(x1, y_ffn, lat(0, 5), ln_w[0, 1], ln_b[0, 1], alpha, t)

    y_pool = _pool(x2.reshape(b, t, d), lat(1, 1), lat(1, 0), pool_w[0], pool_scale[0])
    x3, h3, route = _route(x2, y_pool.reshape(n, d), lat(1, 2), ln_w[1, 0], ln_b[1, 0],
                           lat(1, 4), lat(1, 3), moe_router[0], alpha, t)
    n_chunks = (2 * n + N_EXPERTS * (MOE_CHUNK - 1)) // MOE_CHUNK
    src_idx, chunk_expert, chunk_valid, pos0, pos1 = _routing_tables(route, n_chunks, (n_chunks + 2) * MOE_CHUNK)
    ys = _moe_ffn(h3, src_idx, chunk_expert, chunk_valid, moe_w1[0], moe_w3[0], moe_w2[0], n_chunks)
    gates = jnp.stack([route[2], route[3]], axis=1)
    out = _combine(ys, pos0, pos1, gates, x3, lat(1, 5), ln_w[1, 1], ln_b[1, 1], alpha, t)
    return out.reshape(b, t, d)
```

```python
import functools
import math

import numpy as np
import jax
import jax.numpy as jnp
from jax import lax
from jax.experimental import pallas as pl
from jax.experimental.pallas import tpu as pltpu

F32 = jnp.float32
BF16 = jnp.bfloat16

HEAD = 64
GROUP = 4 * HEAD
CHUNK = 64
LN_EPS = 1e-5
GN_EPS = 64e-5
POS_BASE = 10000.0
POOL_WINDOWS = (2, 4, 8, 16)
N_EXPERTS = 8
MOE_CHUNK = 1024
MOE_SUB = 256
VMEM_LIMIT_BYTES = 56 * 1024 * 1024
MOE_VMEM_LIMIT_BYTES = 60 * 1024 * 1024


def _params(semantics, vmem_limit_bytes=VMEM_LIMIT_BYTES):
    return pltpu.CompilerParams(dimension_semantics=semantics, vmem_limit_bytes=vmem_limit_bytes)


def _tile(n, preferred):
    t = min(preferred, n)
    while n % t:
        t //= 2
    return t


def _dot(a, b):
    return jnp.dot(a, b, preferred_element_type=F32)


def _dot_nt(a, b):
    return lax.dot_general(a, b, (((1,), (1,)), ((), ())), preferred_element_type=F32)


def _dot_tn(a, b):
    return lax.dot_general(a, b, (((0,), (0,)), ((), ())), preferred_element_type=F32)


def _split2(x):
    hi = x.astype(BF16)
    lo = (x - hi.astype(F32)).astype(BF16)
    return hi, lo


def _split3(x):
    hi = x.astype(BF16)
    r1 = x - hi.astype(F32)
    mid = r1.astype(BF16)
    lo = (r1 - mid.astype(F32)).astype(BF16)
    return hi, mid, lo


def _layer_norm(z, w, b):
    mu = jnp.mean(z, axis=-1, keepdims=True)
    zc = z - mu
    var = jnp.mean(zc * zc, axis=-1, keepdims=True)
    return zc * lax.rsqrt(var + LN_EPS) * w + b


def _adaln_kernel(c_ref, w_ref, b_ref, o_ref):
    c = c_ref[...]
    a = (c * jax.nn.sigmoid(c)).astype(BF16)
    o_ref[0] = _dot(a, w_ref[0].astype(BF16)) + b_ref[0]


def _adaln(cond, w_mod, b_mod, tn=1024):
    depth, d, n = w_mod.shape
    tn = _tile(n, tn)
    rows = cond.shape[0]
    return pl.pallas_call(
        _adaln_kernel,
        out_shape=jax.ShapeDtypeStruct((depth, rows, n), F32),
        grid=(depth, n // tn),
        in_specs=[pl.BlockSpec((rows, d), lambda l, j: (0, 0)),
                  pl.BlockSpec((1, d, tn), lambda l, j: (l, 0, j)),
                  pl.BlockSpec((1, 1, tn), lambda l, j: (l, 0, j))],
        out_specs=pl.BlockSpec((1, rows, tn), lambda l, j: (l, 0, j)),
        compiler_params=_params(("parallel", "parallel")),
        name="adaln",
    )(cond, w_mod, b_mod.reshape(depth, 1, n))


def _mm_kernel(a_ref, w_ref, b_ref, o_ref, *, act):
    acc = _dot(a_ref[...], w_ref[...].astype(BF16)) + b_ref[...]
    if act == "tanh":
        acc = jnp.tanh(acc)
    elif act == "sigmoid":
        acc = jax.nn.sigmoid(acc)
    o_ref[...] = acc.astype(o_ref.dtype)


def _mm(a, w, bias=None, *, act=None, out_dtype=F32, tm=2304, tn=512):
    m, k = a.shape
    n = w.shape[1]
    tm = _tile(m, tm)
    tn = _tile(n, tn)
    if bias is None:
        bias = jnp.zeros((1, n), F32)
    return pl.pallas_call(
        functools.partial(_mm_kernel, act=act),
        out_shape=jax.ShapeDtypeStruct((m, n), out_dtype),
        grid=(m // tm, n // tn),
        in_specs=[pl.BlockSpec((tm, k), lambda i, j: (i, 0)),
                  pl.BlockSpec((k, tn), lambda i, j: (0, j)),
                  pl.BlockSpec((1, tn), lambda i, j: (0, j))],
        out_specs=pl.BlockSpec((tm, tn), lambda i, j: (i, j)),
        compiler_params=_params(("parallel", "arbitrary")),
        name="matmul",
    )(a, w, bias.reshape(1, n))


def _mix_kernel(ctx_ref, x_ref, xp_ref, xn_ref, pos_ref, pp_ref, pn_ref,
                sc_ref, sh_ref, csc_ref, csh_ref, mu_ref,
                o0, o1, o2, o3, o4, o5, *, n_lat_tiles):
    s = pl.program_id(1)
    is_ctx = s == 0
    tm = x_ref.shape[1]
    scale = jnp.where(is_ctx, csc_ref[0], sc_ref[0]) + 1.0
    shift = jnp.where(is_ctx, csh_ref[0], sh_ref[0])
    src = jnp.where(is_ctx, ctx_ref[0], x_ref[0] + pos_ref[...])
    h = src * scale + shift
    has_prev = s > 1
    has_next = jnp.logical_and(s >= 1, s < n_lat_tiles)
    h_prev = jnp.where(has_prev, (xp_ref[0] + pp_ref[...]) * scale + shift, 0.0)[7:8]
    h_next = jnp.where(has_next, (xn_ref[0] + pn_ref[...]) * scale + shift, 0.0)[0:1]
    row = lax.broadcasted_iota(jnp.int32, h.shape, 0)
    h_m1 = jnp.where(row == 0, h_prev, pltpu.roll(h, 1, axis=0))
    h_p1 = jnp.where(row == tm - 1, h_next, pltpu.roll(h, tm - 1, axis=0))
    xx = 0.5 * (h_m1 + h_p1) - h
    for n, o_ref in enumerate((o0, o1, o2, o3, o4, o5)):
        o_ref[0] = (h + xx * mu_ref[n:n + 1]).astype(o_ref.dtype)


def _rwkv_mix(ctx, x, pos, sc, sh, csc, csh, mu):
    b, t, d = x.shape
    tm = ctx.shape[1]
    assert t % tm == 0 and tm % 8 == 0
    n_lat = t // tm
    r8 = tm // 8
    lat = lambda bi, s: (bi, jnp.maximum(s - 1, 0), 0)
    prev8 = lambda bi, s: (bi, jnp.maximum((s - 1) * r8 - 1, 0), 0)
    next8 = lambda bi, s: (bi, jnp.minimum(jnp.maximum(s, 1) * r8, t // 8 - 1), 0)
    vec = pl.BlockSpec((1, 1, d), lambda bi, s: (bi, 0, 0))
    cvec = pl.BlockSpec((1, 1, d), lambda bi, s: (0, 0, 0))
    out_sds = jax.ShapeDtypeStruct((b, tm + t, d), BF16)
    return pl.pallas_call(
        functools.partial(_mix_kernel, n_lat_tiles=n_lat),
        out_shape=[out_sds] * 6,
        grid=(b, n_lat + 1),
        in_specs=[pl.BlockSpec((1, tm, d), lambda bi, s: (bi, 0, 0)),
                  pl.BlockSpec((1, tm, d), lat),
                  pl.BlockSpec((1, 8, d), prev8),
                  pl.BlockSpec((1, 8, d), next8),
                  pl.BlockSpec((tm, d), lambda bi, s: (jnp.maximum(s - 1, 0), 0)),
                  pl.BlockSpec((8, d), lambda bi, s: (jnp.maximum((s - 1) * r8 - 1, 0), 0)),
                  pl.BlockSpec((8, d), lambda bi, s: (jnp.minimum(jnp.maximum(s, 1) * r8, t // 8 - 1), 0)),
                  vec, vec, cvec, cvec,
                  pl.BlockSpec((6, d), lambda bi, s: (0, 0))],
        out_specs=[pl.BlockSpec((1, tm, d), lambda bi, s: (bi, s, 0))] * 6,
        compiler_params=_params(("parallel", "parallel")),
        name="rwkv_mix",
    )(ctx, x, x, x, pos, pos, pos, sc, sh, csc, csh, mu)


def _block_diag(x, bmask):
    xb = x.astype(BF16)
    return jnp.concatenate([xb, xb, xb, xb], axis=0) * bmask


def _fold_heads(full, bmask_f32):
    m = full * bmask_f32
    return m[0:HEAD] + m[HEAD:2 * HEAD] + m[2 * HEAD:3 * HEAD] + m[3 * HEAD:4 * HEAD]


SUB = 16


def _block_diag16(x, bmask16):
    xb = x.astype(BF16)
    return jnp.concatenate([xb] * (GROUP // SUB), axis=0) * bmask16


def _dot3_bd16(a, b, bmask16):
    rows = a.shape[0]
    a_hi, a_lo = _split2(a)
    b_hi, b_lo = _split2(b)
    main = _dot(jnp.concatenate([a_hi, a_lo], axis=0), _block_diag16(b_hi, bmask16))
    return main[:rows] + main[rows:] + _dot(a_hi, _block_diag16(b_lo, bmask16))


def _unit_triangular_inverse(l_mats, bmask, bmask16, eye16, diag16, off_a, off_b):
    nq = CHUNK // SUB
    l16 = [sum(l[q * SUB:(q + 1) * SUB] * diag16[q * SUB:(q + 1) * SUB] for q in range(nq)).astype(BF16)
           for l in l_mats]
    t16 = [eye16 + x.astype(F32) for x in l16]
    l_pow = [_dot(x, _block_diag16(x, bmask16)) for x in l16]
    for _ in range(2):
        both = [_dot3_bd16(jnp.concatenate([t, lp], axis=0), lp, bmask16) for t, lp in zip(t16, l_pow)]
        t16 = [t + bo[:SUB] for t, bo in zip(t16, both)]
        l_pow = [bo[SUB:] for bo in both]
    t16 = [t + _dot3_bd16(t, lp, bmask16) for t, lp in zip(t16, l_pow)]
    d = [jnp.concatenate([t] * nq, axis=0) * diag16 for t in t16]
    for off in (off_a, off_b):
        x = [_dot(di.astype(BF16), _block_diag(l * off, bmask)) for di, l in zip(d, l_mats)]
        d = [di + _dot(xi.astype(BF16), _block_diag(di, bmask)) for di, xi in zip(d, x)]
    return d


PAIR = 2


def _scan_kernel(r_ref, k_ref, v_ref, zw_ref, za_ref, kk_ref, ka_ref,
                 bmask_ref, bmask16_ref, tri_ref, dm_ref, cm_ref, eye16_ref,
                 y_ref, s_ref, p_ref, rh_ref, q_ref, yl_ref):
    ng = r_ref.shape[2] // GROUP
    bmask = bmask_ref[...]
    di = pl.program_id(0)

    @pl.when(pl.program_id(2) == 0)
    def _():
        s_ref[...] = jnp.zeros_like(s_ref)
        p_ref[...] = jnp.zeros_like(p_ref)
        rh_ref[...] = jnp.zeros_like(rh_ref)
        q_ref[...] = jnp.zeros_like(q_ref)
        yl_ref[...] = jnp.zeros_like(yl_ref)

    for step in range(PAIR):
        h = jnp.where(di == 0, step, PAIR - 1 - step)
        row0 = pl.multiple_of(h * CHUNK, CHUNK)
        for g in range(ng):
            cols = slice(g * GROUP, (g + 1) * GROUP)
            s_bd = _block_diag(s_ref[:, cols], bmask)
            out = _dot(jnp.concatenate([p_ref[h, :, cols], rh_ref[h, :, cols]], axis=0), s_bd)
            s_ref[:, cols] = out[:CHUNK] + q_ref[h, :, cols]
            y_ref[0, 0, pl.ds(row0, CHUNK), cols] = (out[CHUNK:] + yl_ref[h, :, cols]).astype(y_ref.dtype)

    bmask_f = bmask.astype(F32)
    eye = cm_ref[0]
    m_strict = dm_ref[0, 0]
    m_incl = dm_ref[0, 1]
    chains = [(h, g) for h in range(PAIR) for g in range(ng)]
    pieces = lambda x: [x[h * CHUNK:(h + 1) * CHUNK, g * GROUP:(g + 1) * GROUP] for h, g in chains]
    halves = lambda f, x: jnp.concatenate([f(x[h * CHUNK:(h + 1) * CHUNK]) for h in range(PAIR)], axis=0)
    bd = lambda x: _block_diag(x, bmask)
    stack = lambda x, y: jnp.concatenate([x, y], axis=0)

    r = r_ref[0].astype(F32)
    k = k_ref[0].astype(F32)
    v = v_ref[0].astype(F32)
    lw = (-math.exp(-0.5)) * jax.nn.sigmoid(zw_ref[0].astype(F32))
    a = jax.nn.sigmoid(za_ref[0].astype(F32))
    kkr = k * kk_ref[...]
    sq = jnp.concatenate(pieces(kkr * kkr), axis=0)
    sq_hi, sq_lo = _split2(sq)
    ssq = _dot(sq_hi, bmask) + _dot(sq_lo, bmask)
    ssq = jnp.concatenate(
        [jnp.concatenate([ssq[(h * ng + g) * CHUNK:(h * ng + g + 1) * CHUNK] for g in range(ng)], axis=1)
         for h in range(PAIR)], axis=0)
    kk = kkr * lax.rsqrt(jnp.maximum(ssq, 1e-24))
    kd = k * (1.0 + (a - 1.0) * ka_ref[...])
    bb = kk * a
    tri = tri_ref[0]

    def cumulative(x):
        x_hi, x_lo = _split2(x)
        return _dot(tri, x_hi) + _dot(tri, x_lo)

    g_cum = halves(cumulative, lw)
    g_end = halves(lambda x: jnp.broadcast_to(jnp.sum(x, axis=0, keepdims=True), x.shape), lw)
    e_neg = jnp.exp(-g_cum)
    e_end = jnp.exp(g_end - g_cum)
    a_t = pieces(-kk * jnp.exp(g_cum - lw))
    r_t = pieces(r * jnp.exp(g_cum))
    b_t = pieces(bb * e_neg)
    k_t = pieces(kd * e_neg)
    b_h = pieces((bb * e_end).astype(BF16))
    k_h = pieces((kd * e_end).astype(BF16))
    v_g = pieces(v)
    decay_end = [x[0:1] for x in pieces(jnp.exp(g_end))]

    ar = [stack(x, y).astype(BF16) for x, y in zip(a_t, r_t)]
    mb = [_dot_nt(x, bd(y)) for x, y in zip(ar, b_t)]
    mk = [_dot_nt(x, bd(y)) for x, y in zip(ar, k_t)]
    l_mat = [x[:CHUNK] * m_strict for x in mb]
    m_rb = [(x[CHUNK:] * m_incl).astype(BF16) for x in mb]
    m_k = [stack(x[:CHUNK] * m_strict, x[CHUNK:] * m_incl).astype(BF16) for x in mk]
    mv = [_dot(x, bd(y)) for x, y in zip(m_k, v_g)]
    t_mat = _unit_triangular_inverse(l_mat, bmask, bmask16_ref[...], eye16_ref[...],
                                     cm_ref[1], cm_ref[2], cm_ref[3])
    t_b = [x.astype(BF16) for x in t_mat]
    a_h = [_dot(x, bd(y)).astype(BF16) for x, y in zip(t_b, a_t)]
    u_0 = [_dot(x, bd(y[:CHUNK])) for x, y in zip(t_b, mv)]
    rh = [x + _dot(m, bd(y)) for x, m, y in zip(r_t, m_rb, a_h)]
    yl = [_dot(m, bd(u)) + y[CHUNK:] for m, u, y in zip(m_rb, u_0, mv)]
    p_full = [_dot_tn(x, y) for x, y in zip(b_h, a_h)]
    q_full = [_dot_tn(stack(x, y), stack(u, w).astype(BF16)) for x, y, u, w in zip(b_h, k_h, u_0, v_g)]
    for i, (h, g) in enumerate(chains):
        cols = slice(g * GROUP, (g + 1) * GROUP)
        p_ref[h, :, cols] = (_fold_heads(p_full[i], bmask_f) + eye * decay_end[i]).astype(p_ref.dtype)
        rh_ref[h, :, cols] = rh[i].astype(rh_ref.dtype)
        q_ref[h, :, cols] = _fold_heads(q_full[i], bmask_f)
        yl_ref[h, :, cols] = yl[i]


def _scan_consts():
    lane = np.arange(GROUP)
    bmask = (lane[:, None] // HEAD == lane[None, :] // HEAD).astype(np.float32)
    i = np.arange(CHUNK)[:, None]
    j = np.arange(CHUNK)[None, :]
    jl = (lane % HEAD)[None, :]
    tri = np.stack([(j <= i), (j >= i)]).astype(np.float32)
    dir_masks = np.stack([np.stack([(jl < i), (jl <= i)]),
                          np.stack([(jl > i), (jl >= i)])]).astype(np.float32)
    same16 = (jl // SUB == i // SUB)
    same32 = (jl // (2 * SUB) == i // (2 * SUB))
    common = np.stack([(jl == i), same16, same32 & ~same16, ~same32]).astype(np.float32)
    bmask16 = (lane[:, None] // SUB == lane[None, :] // SUB).astype(np.float32)
    eye16 = ((lane % SUB)[None, :] == np.arange(SUB)[:, None]).astype(np.float32)
    return (jnp.asarray(bmask, BF16), jnp.asarray(bmask16, BF16), jnp.asarray(tri, BF16),
            jnp.asarray(dir_masks, F32), jnp.asarray(common, F32), jnp.asarray(eye16, F32))


def _scan(r, k, v, zw, za, k_k, k_a, n_ctx_chunks):
    b, l, d = r.shape
    rows = PAIR * CHUNK
    npair = l // rows
    assert l % rows == 0 and n_ctx_chunks % PAIR == 0
    n_ctx = n_ctx_chunks // PAIR
    bmask, bmask16, tri, dir_masks, common, eye16 = _scan_consts()

    def pair_of(di, s):
        back = jnp.where(s < n_ctx, n_ctx - 1 - s, npair - 1 + n_ctx - s)
        return jnp.where(di == 0, s, back)

    fold_pair = lambda di, s: pair_of(di, jnp.minimum(s, npair - 1))
    apply_pair = lambda di, s: pair_of(di, jnp.maximum(s - 1, 0))
    tok = pl.BlockSpec((1, rows, d), lambda di, bi, s: (bi, fold_pair(di, s), 0))
    tok2 = pl.BlockSpec((1, rows, d), lambda di, bi, s: (bi, fold_pair(di, s), di))
    par = pl.BlockSpec((1, d), lambda di, bi, s: (0, 0))
    return pl.pallas_call(
        _scan_kernel,
        out_shape=jax.ShapeDtypeStruct((2, b, l, d), BF16),
        grid=(2, b, npair + 1),
        in_specs=[tok, tok, tok, tok2, tok2, par, par,
                  pl.BlockSpec((GROUP, GROUP), lambda di, bi, s: (0, 0)),
                  pl.BlockSpec((GROUP, GROUP), lambda di, bi, s: (0, 0)),
                  pl.BlockSpec((1, CHUNK, CHUNK), lambda di, bi, s: (di, 0, 0)),
                  pl.BlockSpec((1, 2, CHUNK, GROUP), lambda di, bi, s: (di, 0, 0, 0)),
                  pl.BlockSpec((4, CHUNK, GROUP), lambda di, bi, s: (0, 0, 0)),
                  pl.BlockSpec((SUB, GROUP), lambda di, bi, s: (0, 0))],
        out_specs=pl.BlockSpec((1, 1, rows, d), lambda di, bi, s: (di, bi, apply_pair(di, s), 0)),
        scratch_shapes=[pltpu.VMEM((CHUNK, d), F32), pltpu.VMEM((PAIR, CHUNK, d), BF16),
                        pltpu.VMEM((PAIR, CHUNK, d), BF16), pltpu.VMEM((PAIR, CHUNK, d), F32),
                        pltpu.VMEM((PAIR, CHUNK, d), F32)],
        compiler_params=_params(("parallel", "parallel", "arbitrary")),
        name="scan",
    )(r, k, v, zw, za, k_k.reshape(1, d), k_a.reshape(1, d), bmask, bmask16, tri, dir_masks, common, eye16)


def _head_sum(x, ones_bd):
    hi, lo = _split2(x)
    return _dot(hi, ones_bd) + _dot(lo, ones_bd)


def _rwkv_post_kernel(y_ref, r_ref, k_ref, v_ref, za0_ref, za1_ref, gate_ref,
                      ka_ref, rk_ref, gnw_ref, gnb_ref, ones_ref, o_ref):
    ones_bd = ones_ref[...]
    tm, d = o_ref.shape[1], o_ref.shape[2]
    ng = d // GROUP
    to_rows = lambda x: jnp.concatenate([x[:, g * GROUP:(g + 1) * GROUP] for g in range(ng)], axis=0)
    to_cols = lambda x: jnp.concatenate([x[g * tm:(g + 1) * tm] for g in range(ng)], axis=1)
    head_mean = lambda x: to_cols(_head_sum(to_rows(x), ones_bd)) * (1.0 / HEAD)
    y = y_ref[0, 0].astype(F32) + y_ref[1, 0].astype(F32)
    yc = y - head_mean(y)
    var = head_mean(yc * yc)
    o = yc * lax.rsqrt(var + GN_EPS) * gnw_ref[...] + gnb_ref[...]
    a_sum = jax.nn.sigmoid(za0_ref[0].astype(F32)) + jax.nn.sigmoid(za1_ref[0].astype(F32))
    r = r_ref[0].astype(F32)
    k_sum = k_ref[0].astype(F32) * (2.0 + (a_sum - 2.0) * ka_ref[...])
    bonus = head_mean(r * k_sum * rk_ref[...]) * float(HEAD) * v_ref[0].astype(F32)
    o_ref[0] = ((o + bonus) * gate_ref[0].astype(F32)).astype(o_ref.dtype)


def _rwkv_post(y, r, k, v, za, gate, k_a, r_k, gn_w, gn_b, n_ctx, tm=256):
    _, b, l, d = y.shape
    t = l - n_ctx
    off = n_ctx // tm
    ones_bd = _scan_consts()[0]
    tok = pl.BlockSpec((1, tm, d), lambda bi, s: (bi, s + off, 0))
    par = pl.BlockSpec((1, d), lambda bi, s: (0, 0))
    return pl.pallas_call(
        _rwkv_post_kernel,
        out_shape=jax.ShapeDtypeStruct((b, t, d), BF16),
        grid=(b, t // tm),
        in_specs=[pl.BlockSpec((2, 1, tm, d), lambda bi, s: (0, bi, s + off, 0)),
                  tok, tok, tok,
                  pl.BlockSpec((1, tm, d), lambda bi, s: (bi, s + off, 0)),
                  pl.BlockSpec((1, tm, d), lambda bi, s: (bi, s + off, 1)),
                  tok, par, par, par, par,
                  pl.BlockSpec((GROUP, GROUP), lambda bi, s: (0, 0))],
        out_specs=pl.BlockSpec((1, tm, d), lambda bi, s: (bi, s, 0)),
        compiler_params=_params(("parallel", "parallel")),
        name="rwkv_post",
    )(y, r, k, v, za, za, gate, k_a.reshape(1, d), r_k.reshape(1, d), gn_w.reshape(1, d),
      gn_b.reshape(1, d), ones_bd)


def _proj_norm_kernel(a_ref, w_ref, x_ref, pos_ref, g_ref, lnw_ref, lnb_ref, sc_ref, sh_ref,
                      o_ref, h_ref, acc_ref, *, alpha):
    j = pl.program_id(1)
    nj = acc_ref.shape[0]
    tn = acc_ref.shape[2]
    acc_ref[j] = _dot(a_ref[...], w_ref[...].astype(BF16))

    @pl.when(j == nj - 1)
    def _():
        gate = g_ref[0]
        for jj in range(nj):
            cols = slice(jj * tn, (jj + 1) * tn)
            o_ref[:, cols] = alpha * (x_ref[:, cols] + pos_ref[:, cols]) + gate[:, cols] * acc_ref[jj]
        x_new = _layer_norm(o_ref[...], lnw_ref[...], lnb_ref[...])
        o_ref[...] = x_new
        h_ref[...] = (x_new * (1.0 + sc_ref[0]) + sh_ref[0]).astype(h_ref.dtype)


def _proj_norm(a, w, x, pos, gate, ln_w, ln_b, sc, sh, alpha, tm=512, tn=512):
    m, k = a.shape
    d = w.shape[1]
    t = pos.shape[0]
    tm = _tile(t, tm)
    tn = _tile(d, tn)
    tpb = t // tm
    vec = pl.BlockSpec((1, 1, d), lambda i, j: (i // tpb, 0, 0))
    return pl.pallas_call(
        functools.partial(_proj_norm_kernel, alpha=alpha),
        out_shape=[jax.ShapeDtypeStruct((m, d), F32), jax.ShapeDtypeStruct((m, d), BF16)],
        grid=(m // tm, d // tn),
        in_specs=[pl.BlockSpec((tm, k), lambda i, j: (i, 0)),
                  pl.BlockSpec((k, tn), lambda i, j: (0, j)),
                  pl.BlockSpec((tm, d), lambda i, j: (i, 0)),
                  pl.BlockSpec((tm, d), lambda i, j: (i % tpb, 0)),
                  vec,
                  pl.BlockSpec((1, d), lambda i, j: (0, 0)),
                  pl.BlockSpec((1, d), lambda i, j: (0, 0)),
                  vec, vec],
        out_specs=[pl.BlockSpec((tm, d), lambda i, j: (i, 0))] * 2,
        scratch_shapes=[pltpu.VMEM((d // tn, tm, tn), F32)],
        compiler_params=_params(("parallel", "arbitrary")),
        name="proj_norm",
    )(a, w, x, pos, gate, ln_w.reshape(1, d), ln_b.reshape(1, d), sc, sh)


def _swiglu_halves(h, w1_ref, w3_ref, w2_ref, lead):
    tf = w1_ref.shape[-1]
    halves = [slice(0, tf // 2), slice(tf // 2, tf)]
    a1 = [_dot(h, w1_ref[lead + (slice(None), c)].astype(BF16)) for c in halves]
    a3 = [_dot(h, w3_ref[lead + (slice(None), c)].astype(BF16)) for c in halves]
    u = [(x * jax.nn.sigmoid(x) * y).astype(BF16) for x, y in zip(a1, a3)]
    y = [_dot(x, w2_ref[lead + (c, slice(None))].astype(BF16)) for x, c in zip(u, halves)]
    return y[0] + y[1]


def _ffn_kernel(h_ref, w1_ref, w3_ref, w2_ref, o_ref, acc_ref):
    j = pl.program_id(1)

    @pl.when(j == 0)
    def _():
        acc_ref[...] = jnp.zeros_like(acc_ref)

    acc_ref[...] += _swiglu_halves(h_ref[...], w1_ref, w3_ref, w2_ref, ())

    @pl.when(j == pl.num_programs(1) - 1)
    def _():
        o_ref[...] = acc_ref[...].astype(o_ref.dtype)


def _ffn(h, w1, w3, w2, tm=1024, tf=512):
    m, d = h.shape
    f = w1.shape[1]
    tm = _tile(m, tm)
    tf = _tile(f, tf)
    return pl.pallas_call(
        _ffn_kernel,
        out_shape=jax.ShapeDtypeStruct((m, d), BF16),
        grid=(m // tm, f // tf),
        in_specs=[pl.BlockSpec((tm, d), lambda i, j: (i, 0)),
                  pl.BlockSpec((d, tf), lambda i, j: (0, j)),
                  pl.BlockSpec((d, tf), lambda i, j: (0, j)),
                  pl.BlockSpec((tf, d), lambda i, j: (j, 0))],
        out_specs=pl.BlockSpec((tm, d), lambda i, j: (i, 0)),
        scratch_shapes=[pltpu.VMEM((tm, d), F32)],
        compiler_params=_params(("parallel", "arbitrary")),
        name="ffn",
    )(h, w1, w3, w2)


def _residual_norm_kernel(x_ref, y_ref, g_ref, lnw_ref, lnb_ref, o_ref, *, alpha):
    z = alpha * x_ref[...] + g_ref[0] * y_ref[...].astype(F32)
    o_ref[...] = _layer_norm(z, lnw_ref[...], lnb_ref[...])


def _residual_norm(x, y, gate, ln_w, ln_b, alpha, rows_per_batch, tm=512):
    m, d = x.shape
    tm = _tile(rows_per_batch, tm)
    tpb = rows_per_batch // tm
    tok = pl.BlockSpec((tm, d), lambda i: (i, 0))
    row = pl.BlockSpec((1, d), lambda i: (0, 0))
    return pl.pallas_call(
        functools.partial(_residual_norm_kernel, alpha=alpha),
        out_shape=jax.ShapeDtypeStruct((m, d), F32),
        grid=(m // tm,),
        in_specs=[tok, tok, pl.BlockSpec((1, 1, d), lambda i: (i // tpb, 0, 0)), row, row],
        out_specs=tok,
        compiler_params=_params(("parallel",)),
        name="residual_norm",
    )(x, y, gate, ln_w.reshape(1, d), ln_b.reshape(1, d))


def _shift_down(x, s, row):
    return jnp.where(row >= s, pltpu.roll(x, s, axis=0), 0.0)


def _shift_up(x, s, row):
    t = x.shape[0]
    return jnp.where(row < t - s, pltpu.roll(x, t - s, axis=0), 0.0)


def _pool_kernel(x_ref, sc_ref, sh_ref, w_ref, scale_ref, o_ref):
    g = pl.program_id(0)
    t = x_ref.shape[1]
    h = x_ref[0] * (1.0 + sc_ref[0]) + sh_ref[0]
    row = lax.broadcasted_iota(jnp.int32, h.shape, 0)
    w = w_ref[0].astype(BF16)
    for gi, win in enumerate(POOL_WINDOWS):
        @pl.when(g == gi)
        def _(win=win):
            half = win // 2
            back = h
            fwd = h
            m = 1
            while m < half:
                back = back + _shift_down(back, m, row)
                fwd = fwd + _shift_up(fwd, m, row)
                m *= 2
            total = _shift_down(back, 1, row) + fwd
            count = (jnp.minimum(row + half, t) - jnp.maximum(row - half, 0)).astype(F32)
            pooled = (total / count - h).astype(BF16)
            o_ref[0] = _dot(pooled, w) * scale_ref[...]


def _pool(x, sc, sh, w_pool, scale):
    b, t, d = x.shape
    ng, p, _ = w_pool.shape
    vec = pl.BlockSpec((1, 1, p), lambda g, bi: (bi, 0, g))
    return pl.pallas_call(
        _pool_kernel,
        out_shape=jax.ShapeDtypeStruct((b, t, d), F32),
        grid=(ng, b),
        in_specs=[pl.BlockSpec((1, t, p), lambda g, bi: (bi, 0, g)), vec, vec,
                  pl.BlockSpec((1, p, p), lambda g, bi: (g, 0, 0)),
                  pl.BlockSpec((1, p), lambda g, bi: (0, g))],
        out_specs=pl.BlockSpec((1, t, p), lambda g, bi: (bi, 0, g)),
        compiler_params=_params(("parallel", "parallel")),
        name="pool",
    )(x, sc, sh, w_pool, scale.reshape(1, d))


def _route_kernel(x_ref, y_ref, g_ref, lnw_ref, lnb_ref, sc_ref, sh_ref, rt_ref,
                  xo_ref, h_ref, route_ref, *, alpha):
    x = _layer_norm(alpha * x_ref[...] + g_ref[0] * y_ref[...], lnw_ref[...], lnb_ref[...])
    xo_ref[...] = x
    h = x * (1.0 + sc_ref[0]) + sh_ref[0]
    h_ref[...] = h
    r1, r2, r3 = _split3(rt_ref[...])
    h1, h2, h3 = _split3(h)
    logits = (_dot_nt(r1, h1) + (_dot_nt(r1, h2) + _dot_nt(r2, h1))
              + (_dot_nt(r1, h3) + _dot_nt(r2, h2) + _dot_nt(r3, h1)))
    mx = jnp.max(logits, axis=0, keepdims=True)
    e = jnp.exp(logits - mx)
    p = e / jnp.sum(e, axis=0, keepdims=True)
    idx = lax.broadcasted_iota(jnp.int32, p.shape, 0)
    p1 = jnp.max(p, axis=0, keepdims=True)
    i1 = jnp.min(jnp.where(p == p1, idx, N_EXPERTS), axis=0, keepdims=True)
    rest = jnp.where(idx == i1, -1.0, p)
    p2 = jnp.max(rest, axis=0, keepdims=True)
    i2 = jnp.min(jnp.where(rest == p2, idx, N_EXPERTS), axis=0, keepdims=True)
    den = p1 + p2
    out = jnp.where(idx == 0, i1.astype(F32), 0.0)
    out = jnp.where(idx == 1, i2.astype(F32), out)
    out = jnp.where(idx == 2, p1 / den, out)
    out = jnp.where(idx == 3, p2 / den, out)
    route_ref[...] = out


def _route(x, y, gate, ln_w, ln_b, sc, sh, router, alpha, rows_per_batch, tm=256):
    m, d = x.shape
    tpb = rows_per_batch // tm
    vec = pl.BlockSpec((1, 1, d), lambda i: (i // tpb, 0, 0))
    row = pl.BlockSpec((1, d), lambda i: (0, 0))
    tok = pl.BlockSpec((tm, d), lambda i: (i, 0))
    return pl.pallas_call(
        functools.partial(_route_kernel, alpha=alpha),
        out_shape=[jax.ShapeDtypeStruct((m, d), F32), jax.ShapeDtypeStruct((m, d), F32),
                   jax.ShapeDtypeStruct((N_EXPERTS, m), F32)],
        grid=(m // tm,),
        in_specs=[tok, tok, vec, row, row, vec, vec,
                  pl.BlockSpec((N_EXPERTS, d), lambda i: (0, 0))],
        out_specs=[tok, tok, pl.BlockSpec((N_EXPERTS, tm), lambda i: (0, i))],
        compiler_params=_params(("parallel",)),
        name="route",
    )(x, y, gate, ln_w.reshape(1, d), ln_b.reshape(1, d), sc, sh, router.T)


def _row_copy(src_hbm, dst_vmem, sem, src_row, dst_row):
    return pltpu.make_async_copy(src_hbm.at[pl.ds(src_row, 1)], dst_vmem.at[pl.ds(dst_row, 1)], sem)


def _moe_kernel(ce_ref, nv_ref, idx_ref, h_ref, w1_ref, w3_ref, w2_ref, o_ref,
                stage_ref, work_ref, gather_sem, *, share):
    c = pl.program_id(0)
    j = pl.program_id(1)
    n_chunks = pl.num_programs(0)
    nj = pl.num_programs(1)
    nv = nv_ref[c]
    full = work_ref.shape[0]
    stage_rows = stage_ref.shape[0]

    def gather(chunk, r):
        return _row_copy(h_ref, stage_ref, gather_sem, idx_ref[chunk * full + r], r)

    def wait_all_gathers():
        def body(r, carry):
            gather(0, r).wait()
            return carry
        lax.fori_loop(0, stage_rows, body, 0, unroll=8)

    @pl.when(j == 0)
    def _():
        @pl.when(c == 0)
        def _():
            def body(r, carry):
                gather(0, r).start()
                return carry
            lax.fori_loop(0, stage_rows, body, 0, unroll=8)

        wait_all_gathers()
        work_ref[...] = stage_ref[0:full, :].astype(work_ref.dtype)
        o_ref[...] = jnp.zeros_like(o_ref)

    def request_next_share():
        for i in range(share):
            gather(c + 1, j * share + i).start()

    def swiglu_rows(rows):
        o_ref[rows, :] += _swiglu_halves(work_ref[rows, :], w1_ref, w3_ref, w2_ref, (0,))

    n_sub = (nv + (MOE_SUB - 1)) // MOE_SUB
    for k in range(full // MOE_SUB + 1):
        @pl.when(n_sub == k)
        def _(k=k):
            request_next_share()
            if k:
                swiglu_rows(slice(0, k * MOE_SUB))

    @pl.when(jnp.logical_and(j == nj - 1, c == n_chunks - 1))
    def _():
        wait_all_gathers()


def _moe_ffn(h, src_idx, chunk_expert, chunk_valid, w1, w3, w2, n_chunks, tf=512):
    d = h.shape[1]
    f = w1.shape[2]
    tf = _tile(f, tf)
    nj = f // tf
    share = -(-MOE_CHUNK // nj)
    share = -(-share // 8) * 8
    stage_rows = share * nj
    assert src_idx.shape[0] >= n_chunks * MOE_CHUNK + stage_rows

    def jeff(c, j, nv):
        return jnp.where(nv[c] > 0, j, nj - 1)

    return pl.pallas_call(
        functools.partial(_moe_kernel, share=share),
        out_shape=jax.ShapeDtypeStruct((n_chunks * MOE_CHUNK, d), F32),
        grid_spec=pltpu.PrefetchScalarGridSpec(
            num_scalar_prefetch=3,
            grid=(n_chunks, nj),
            in_specs=[pl.BlockSpec(memory_space=pl.ANY),
                      pl.BlockSpec((1, d, tf), lambda c, j, ce, nv, ix: (ce[c], 0, jeff(c, j, nv))),
                      pl.BlockSpec((1, d, tf), lambda c, j, ce, nv, ix: (ce[c], 0, jeff(c, j, nv))),
                      pl.BlockSpec((1, tf, d), lambda c, j, ce, nv, ix: (ce[c], jeff(c, j, nv), 0))],
            out_specs=pl.BlockSpec((MOE_CHUNK, d), lambda c, j, ce, nv, ix: (c, 0)),
            scratch_shapes=[pltpu.VMEM((stage_rows, d), F32), pltpu.VMEM((MOE_CHUNK, d), BF16),
                            pltpu.SemaphoreType.DMA]),
        compiler_params=_params(("arbitrary", "arbitrary"), MOE_VMEM_LIMIT_BYTES),
        name="moe_ffn",
    )(chunk_expert, chunk_valid, src_idx, h, w1, w3, w2)


def _combine_kernel(p0_ref, p1_ref, ys_ref, x_ref, gates_ref, g_ref, lnw_ref, lnb_ref,
                    o_ref, b0_ref, b1_ref, sem, *, alpha):
    i = pl.program_id(0)
    n = pl.num_programs(0)
    rows = b0_ref.shape[1]
    slot = i % 2

    def request(tile, to_slot):
        def body(r, carry):
            _row_copy(ys_ref, b0_ref.at[to_slot], sem.at[to_slot], p0_ref[tile * rows + r], r).start()
            _row_copy(ys_ref, b1_ref.at[to_slot], sem.at[to_slot], p1_ref[tile * rows + r], r).start()
            return carry
        lax.fori_loop(0, rows, body, 0, unroll=4)

    @pl.when(i == 0)
    def _():
        request(0, 0)

    @pl.when(i + 1 < n)
    def _():
        request(jnp.minimum(i + 1, n - 1), 1 - slot)

    def wait(r, carry):
        _row_copy(ys_ref, b0_ref.at[slot], sem.at[slot], 0, r).wait()
        _row_copy(ys_ref, b1_ref.at[slot], sem.at[slot], 0, r).wait()
        return carry

    lax.fori_loop(0, rows, wait, 0, unroll=4)
    gates = gates_ref[...]
    y = gates[:, 0:1] * b0_ref[slot] + gates[:, 1:2] * b1_ref[slot]
    z = alpha * x_ref[...] + g_ref[0] * y
    o_ref[...] = _layer_norm(z, lnw_ref[...], lnb_ref[...])


def _combine(ys, pos0, pos1, gates, x, gate_vec, ln_w, ln_b, alpha, rows_per_batch, rows=256):
    m, d = x.shape
    tpb = rows_per_batch // rows
    tok = lambda i, a, b: (i, 0)
    return pl.pallas_call(
        functools.partial(_combine_kernel, alpha=alpha),
        out_shape=jax.ShapeDtypeStruct((m, d), F32),
        grid_spec=pltpu.PrefetchScalarGridSpec(
            num_scalar_prefetch=2,
            grid=(m // rows,),
            in_specs=[pl.BlockSpec(memory_space=pl.ANY),
                      pl.BlockSpec((rows, d), tok),
                      pl.BlockSpec((rows, 2), tok),
                      pl.BlockSpec((1, 1, d), lambda i, a, b: (i // tpb, 0, 0)),
                      pl.BlockSpec((1, d), lambda i, a, b: (0, 0)),
                      pl.BlockSpec((1, d), lambda i, a, b: (0, 0))],
            out_specs=pl.BlockSpec((rows, d), tok),
            scratch_shapes=[pltpu.VMEM((2, rows, d), F32), pltpu.VMEM((2, rows, d), F32),
                            pltpu.SemaphoreType.DMA((2,))]),
        compiler_params=_params(("arbitrary",)),
        name="moe_combine",
    )(pos0, pos1, ys, x, gates, gate_vec, ln_w.reshape(1, d), ln_b.reshape(1, d))


def _routing_tables(route, n_chunks, table_len):
    n = route.shape[1]
    experts = jnp.concatenate([route[0], route[1]]).astype(jnp.int32)
    onehot = (experts[:, None] == jnp.arange(N_EXPERTS, dtype=jnp.int32)[None, :]).astype(jnp.int32)
    csum = jnp.cumsum(onehot, axis=0)
    rank = jnp.sum((csum - 1) * onehot, axis=1)
    counts = csum[-1]
    chunks_e = (counts + MOE_CHUNK - 1) // MOE_CHUNK
    chunk_end = jnp.cumsum(chunks_e)
    chunk_start = chunk_end - chunks_e
    dest = (chunk_start * MOE_CHUNK)[experts] + rank
    token = jnp.arange(2 * n, dtype=jnp.int32) % n
    src_idx = jnp.zeros((table_len,), jnp.int32).at[dest].set(token)
    cid = jnp.arange(n_chunks, dtype=jnp.int32)
    used = cid < chunk_end[-1]
    last_used = jnp.maximum(chunk_end[-1] - 1, 0)
    ce = jnp.sum((jnp.minimum(cid, last_used)[:, None] >= chunk_end[None, :]).astype(jnp.int32), axis=1)
    ce = jnp.minimum(ce, N_EXPERTS - 1)
    nvalid = jnp.clip(counts[ce] - (cid - chunk_start[ce]) * MOE_CHUNK, 0, MOE_CHUNK)
    nvalid = jnp.where(used, nvalid, 0).astype(jnp.int32)
    return src_idx, ce, nvalid, dest[:n], dest[n:]


def _position_embedding(rows, width, d):
    quarter = d // 4
    omega = 1.0 / (POS_BASE ** (jnp.arange(quarter, dtype=F32) / quarter))
    ar = jnp.arange(rows, dtype=F32)[:, None] * omega[None, :]
    ac = jnp.arange(width, dtype=F32)[:, None] * omega[None, :]
    row_part = jnp.repeat(jnp.concatenate([jnp.sin(ar), jnp.cos(ar)], axis=-1), width, axis=0)
    col_part = jnp.tile(jnp.concatenate([jnp.sin(ac), jnp.cos(ac)], axis=-1), (rows, 1))
    return jnp.concatenate([row_part, col_part], axis=-1)


def _block_diag2(w):
    z = jnp.zeros_like(w[0])
    return jnp.concatenate([jnp.concatenate([w[0], z], axis=1), jnp.concatenate([z, w[1]], axis=1)], axis=0)


def kernel(x, c, ctx, c_ctx, w_mod, b_mod, ln_w, ln_b, rwkv_mu, rwkv_w_r, rwkv_w_k, rwkv_w_v, rwkv_w_o, rwkv_decay_w0, rwkv_decay_w1, rwkv_decay_w2, rwkv_iclr_a0, rwkv_iclr_a1, rwkv_iclr_a2, rwkv_gate_g1, rwkv_gate_g2, rwkv_k_k, rwkv_k_a, rwkv_r_k, rwkv_gn_w, rwkv_gn_b, pool_w, pool_scale, ffn_w1, ffn_w3, ffn_w2, moe_router, moe_w1, moe_w3, moe_w2):
    b, t, d = x.shape
    n_ctx = ctx.shape[1]
    depth = w_mod.shape[0]
    assert depth == 2 and rwkv_mu.shape[0] == 1 and pool_w.shape[0] == 1
    alpha = (2.0 * depth) ** 0.25
    grid_w = 64
    l = n_ctx + t
    n = b * t

    cond = jnp.zeros((8, d), F32).at[:b].set(c).at[b].set(c_ctx)
    mod = _adaln(cond, w_mod, b_mod).reshape(depth, 8, 6, d)
    lat = lambda layer, which: mod[layer, :b, which].reshape(b, 1, d)
    cvec = lambda layer, which: mod[layer, b, which].reshape(1, 1, d)

    pos = _position_embedding(t // grid_w, grid_w, d)

    xr, xw, xk, xv, xa, xg = _rwkv_mix(ctx, x, pos, lat(0, 1), lat(0, 0), cvec(0, 1), cvec(0, 0), rwkv_mu[0])
    flat = lambda a: a.reshape(b * l, a.shape[-1])
    r = _mm(flat(xr), rwkv_w_r[0], out_dtype=BF16)
    k = _mm(flat(xk), rwkv_w_k[0], out_dtype=BF16)
    v = _mm(flat(xv), rwkv_w_v[0], out_dtype=BF16)
    dw1 = jnp.concatenate([rwkv_decay_w1[0, 0], rwkv_decay_w1[0, 1]], axis=1)
    ia1 = jnp.concatenate([rwkv_iclr_a1[0, 0], rwkv_iclr_a1[0, 1]], axis=1)
    lora_w = _mm(flat(xw), dw1, act="tanh", out_dtype=BF16)
    zw = _mm(lora_w, _block_diag2(rwkv_decay_w2[0]), rwkv_decay_w0[0].reshape(1, 2 * d), out_dtype=BF16)
    lora_a = _mm(flat(xa), ia1, out_dtype=BF16)
    za = _mm(lora_a, _block_diag2(rwkv_iclr_a2[0]), rwkv_iclr_a0[0].reshape(1, 2 * d), out_dtype=BF16)
    lora_g = _mm(flat(xg), rwkv_gate_g1[0], act="sigmoid", out_dtype=BF16)
    gate = _mm(lora_g, rwkv_gate_g2[0], out_dtype=BF16)
    seq = lambda a: a.reshape(b, l, a.shape[-1])
    y_scan = _scan(seq(r), seq(k), seq(v), seq(zw), seq(za), rwkv_k_k[0], rwkv_k_a[0], n_ctx // CHUNK)
    og = _rwkv_post(y_scan, seq(r), seq(k), seq(v), seq(za), seq(gate), rwkv_k_a[0],
                    rwkv_r_k[0].reshape(d), rwkv_gn_w[0], rwkv_gn_b[0], n_ctx)
    x1, h1 = _proj_norm(og.reshape(n, d), rwkv_w_o[0].astype(BF16), x.reshape(n, d), pos, lat(0, 2),
                        ln_w[0, 0], ln_b[0, 0], lat(0, 4), lat(0, 3), alpha)
    y_ffn = _ffn(h1, ffn_w1[0], ffn_w3[0], ffn_w2[0])
    x2 = _residual_norm(x1, y_ffn, lat(0, 5), ln_w[0, 1], ln_b[0, 1], alpha, t)

    y_pool = _pool(x2.reshape(b, t, d), lat(1, 1), lat(1, 0), pool_w[0], pool_scale[0])
    x3, h3, route = _route(x2, y_pool.reshape(n, d), lat(1, 2), ln_w[1, 0], ln_b[1, 0],
                           lat(1, 4), lat(1, 3), moe_router[0], alpha, t)
    n_chunks = (2 * n + N_EXPERTS * (MOE_CHUNK - 1)) // MOE_CHUNK
    src_idx, chunk_expert, chunk_valid, pos0, pos1 = _routing_tables(route, n_chunks, (n_chunks + 2) * MOE_CHUNK)
    ys = _moe_ffn(h3, src_idx, chunk_expert, chunk_valid, moe_w1[0], moe_w3[0], moe_w2[0], n_chunks)
    gates = jnp.stack([route[2], route[3]], axis=1)
    out = _combine(ys, pos0, pos1, gates, x3, lat(1, 5), ln_w[1, 1], ln_b[1, 1], alpha, t)
    return out.reshape(b, t, d)
```

```python
import functools
import math

import numpy as np
import jax
import jax.numpy as jnp
from jax import lax
from jax.experimental import pallas as pl
from jax.experimental.pallas import tpu as pltpu

F32 = jnp.float32
BF16 = jnp.bfloat16

HEAD = 64
GROUP = 4 * HEAD
CHUNK = 64
LN_EPS = 1e-5
GN_EPS = 64e-5
POS_BASE = 10000.0
POOL_WINDOWS = (2, 4, 8, 16)
N_EXPERTS = 8
MOE_CHUNK = 1024
MOE_SUB = 256
VMEM_LIMIT_BYTES = 56 * 1024 * 1024
MOE_VMEM_LIMIT_BYTES = 60 * 1024 * 1024


def _params(semantics, vmem_limit_bytes=VMEM_LIMIT_BYTES):
    return pltpu.CompilerParams(dimension_semantics=semantics, vmem_limit_bytes=vmem_limit_bytes)


def _tile(n, preferred):
    t = min(preferred, n)
    while n % t:
        t //= 2
    return t


def _dot(a, b):
    return jnp.dot(a, b, preferred_element_type=F32)


def _dot_nt(a, b):
    return lax.dot_general(a, b, (((1,), (1,)), ((), ())), preferred_element_type=F32)


def _dot_tn(a, b):
    return lax.dot_general(a, b, (((0,), (0,)), ((), ())), preferred_element_type=F32)


def _split2(x):
    hi = x.astype(BF16)
    lo = (x - hi.astype(F32)).astype(BF16)
    return hi, lo


def _split3(x):
    hi = x.astype(BF16)
    r1 = x - hi.astype(F32)
    mid = r1.astype(BF16)
    lo = (r1 - mid.astype(F32)).astype(BF16)
    return hi, mid, lo


def _layer_norm(z, w, b):
    mu = jnp.mean(z, axis=-1, keepdims=True)
    zc = z - mu
    var = jnp.mean(zc * zc, axis=-1, keepdims=True)
    return zc * lax.rsqrt(var + LN_EPS) * w + b


def _adaln_kernel(c_ref, w_ref, b_ref, o_ref):
    c = c_ref[...]
    a = (c * jax.nn.sigmoid(c)).astype(BF16)
    o_ref[0] = _dot(a, w_ref[0].astype(BF16)) + b_ref[0]


def _adaln(cond, w_mod, b_mod, tn=1024):
    depth, d, n = w_mod.shape
    tn = _tile(n, tn)
    rows = cond.shape[0]
    return pl.pallas_call(
        _adaln_kernel,
        out_shape=jax.ShapeDtypeStruct((depth, rows, n), F32),
        grid=(depth, n // tn),
        in_specs=[pl.BlockSpec((rows, d), lambda l, j: (0, 0)),
                  pl.BlockSpec((1, d, tn), lambda l, j: (l, 0, j)),
                  pl.BlockSpec((1, 1, tn), lambda l, j: (l, 0, j))],
        out_specs=pl.BlockSpec((1, rows, tn), lambda l, j: (l, 0, j)),
        compiler_params=_params(("parallel", "parallel")),
        name="adaln",
    )(cond, w_mod, b_mod.reshape(depth, 1, n))


def _mm_kernel(a_ref, w_ref, b_ref, o_ref, *, act):
    acc = _dot(a_ref[...], w_ref[...].astype(BF16)) + b_ref[...]
    if act == "tanh":
        acc = jnp.tanh(acc)
    elif act == "sigmoid":
        acc = jax.nn.sigmoid(acc)
    o_ref[...] = acc.astype(o_ref.dtype)


def _mm(a, w, bias=None, *, act=None, out_dtype=F32, tm=2304, tn=512):
    m, k = a.shape
    n = w.shape[1]
    tm = _tile(m, tm)
    tn = _tile(n, tn)
    if bias is None:
        bias = jnp.zeros((1, n), F32)
    return pl.pallas_call(
        functools.partial(_mm_kernel, act=act),
        out_shape=jax.ShapeDtypeStruct((m, n), out_dtype),
        grid=(m // tm, n // tn),
        in_specs=[pl.BlockSpec((tm, k), lambda i, j: (i, 0)),
                  pl.BlockSpec((k, tn), lambda i, j: (0, j)),
                  pl.BlockSpec((1, tn), lambda i, j: (0, j))],
        out_specs=pl.BlockSpec((tm, tn), lambda i, j: (i, j)),
        compiler_params=_params(("parallel", "arbitrary")),
        name="matmul",
    )(a, w, bias.reshape(1, n))


def _mix_kernel(ctx_ref, x_ref, xp_ref, xn_ref, pos_ref, pp_ref, pn_ref,
                sc_ref, sh_ref, csc_ref, csh_ref, mu_ref,
                o0, o1, o2, o3, o4, o5, *, n_lat_tiles):
    s = pl.program_id(1)
    is_ctx = s == 0
    tm = x_ref.shape[1]
    scale = jnp.where(is_ctx, csc_ref[0], sc_ref[0]) + 1.0
    shift = jnp.where(is_ctx, csh_ref[0], sh_ref[0])
    src = jnp.where(is_ctx, ctx_ref[0], x_ref[0] + pos_ref[...])
    h = src * scale + shift
    has_prev = s > 1
    has_next = jnp.logical_and(s >= 1, s < n_lat_tiles)
    h_prev = jnp.where(has_prev, (xp_ref[0] + pp_ref[...]) * scale + shift, 0.0)[7:8]
    h_next = jnp.where(has_next, (xn_ref[0] + pn_ref[...]) * scale + shift, 0.0)[0:1]
    row = lax.broadcasted_iota(jnp.int32, h.shape, 0)
    h_m1 = jnp.where(row == 0, h_prev, pltpu.roll(h, 1, axis=0))
    h_p1 = jnp.where(row == tm - 1, h_next, pltpu.roll(h, tm - 1, axis=0))
    xx = 0.5 * (h_m1 + h_p1) - h
    for n, o_ref in enumerate((o0, o1, o2, o3, o4, o5)):
        o_ref[0] = (h + xx * mu_ref[n:n + 1]).astype(o_ref.dtype)


def _rwkv_mix(ctx, x, pos, sc, sh, csc, csh, mu):
    b, t, d = x.shape
    tm = ctx.shape[1]
    assert t % tm == 0 and tm % 8 == 0
    n_lat = t // tm
    r8 = tm // 8
    lat = lambda bi, s: (bi, jnp.maximum(s - 1, 0), 0)
    prev8 = lambda bi, s: (bi, jnp.maximum((s - 1) * r8 - 1, 0), 0)
    next8 = lambda bi, s: (bi, jnp.minimum(jnp.maximum(s, 1) * r8, t // 8 - 1), 0)
    vec = pl.BlockSpec((1, 1, d), lambda bi, s: (bi, 0, 0))
    cvec = pl.BlockSpec((1, 1, d), lambda bi, s: (0, 0, 0))
    out_sds = jax.ShapeDtypeStruct((b, tm + t, d), BF16)
    return pl.pallas_call(
        functools.partial(_mix_kernel, n_lat_tiles=n_lat),
        out_shape=[out_sds] * 6,
        grid=(b, n_lat + 1),
        in_specs=[pl.BlockSpec((1, tm, d), lambda bi, s: (bi, 0, 0)),
                  pl.BlockSpec((1, tm, d), lat),
                  pl.BlockSpec((1, 8, d), prev8),
                  pl.BlockSpec((1, 8, d), next8),
                  pl.BlockSpec((tm, d), lambda bi, s: (jnp.maximum(s - 1, 0), 0)),
                  pl.BlockSpec((8, d), lambda bi, s: (jnp.maximum((s - 1) * r8 - 1, 0), 0)),
                  pl.BlockSpec((8, d), lambda bi, s: (jnp.minimum(jnp.maximum(s, 1) * r8, t // 8 - 1), 0)),
                  vec, vec, cvec, cvec,
                  pl.BlockSpec((6, d), lambda bi, s: (0, 0))],
        out_specs=[pl.BlockSpec((1, tm, d), lambda bi, s: (bi, s, 0))] * 6,
        compiler_params=_params(("parallel", "parallel")),
        name="rwkv_mix",
    )(ctx, x, x, x, pos, pos, pos, sc, sh, csc, csh, mu)


def _block_diag(x, bmask):
    xb = x.astype(BF16)
    zero = jnp.zeros((HEAD, GROUP // 2), BF16)
    rows = []
    for h in range(GROUP // HEAD):
        t = h // 2
        blk = xb[:, t * 128:(t + 1) * 128] * bmask[h * HEAD:(h + 1) * HEAD, t * 128:(t + 1) * 128]
        rows.append(jnp.concatenate([blk, zero] if t == 0 else [zero, blk], axis=1))
    return jnp.concatenate(rows, axis=0)


def _fold_heads(full):
    lane = lax.broadcasted_iota(jnp.int32, (HEAD, 128), 1)
    tiles = []
    for t in range(GROUP // 128):
        even = full[(2 * t) * HEAD:(2 * t + 1) * HEAD, t * 128:(t + 1) * 128]
        odd = full[(2 * t + 1) * HEAD:(2 * t + 2) * HEAD, t * 128:(t + 1) * 128]
        tiles.append(jnp.where(lane < HEAD, even, odd))
    return jnp.concatenate(tiles, axis=1)


SUB = 16


def _block_diag16(x, bmask16):
    xb = x.astype(BF16)
    zero = jnp.zeros((SUB, GROUP // 2), BF16)
    rows = []
    for b in range(GROUP // SUB):
        t = b // (128 // SUB)
        blk = xb[:, t * 128:(t + 1) * 128] * bmask16[b * SUB:(b + 1) * SUB, t * 128:(t + 1) * 128]
        rows.append(jnp.concatenate([blk, zero] if t == 0 else [zero, blk], axis=1))
    return jnp.concatenate(rows, axis=0)


def _dot3_bd16(a, b, bmask16):
    rows = a.shape[0]
    a_hi, a_lo = _split2(a)
    b_hi, b_lo = _split2(b)
    main = _dot(jnp.concatenate([a_hi, a_lo], axis=0), _block_diag16(b_hi, bmask16))
    return main[:rows] + main[rows:] + _dot(a_hi, _block_diag16(b_lo, bmask16))


def _unit_triangular_inverse(l_mats, bmask, bmask16, eye16, diag16, off_a, off_b):
    nq = CHUNK // SUB
    l16 = [sum(l[q * SUB:(q + 1) * SUB] * diag16[q * SUB:(q + 1) * SUB] for q in range(nq)).astype(BF16)
           for l in l_mats]
    t16 = [eye16 + x.astype(F32) for x in l16]
    l_pow = [_dot(x, _block_diag16(x, bmask16)) for x in l16]
    for _ in range(2):
        both = [_dot3_bd16(jnp.concatenate([t, lp], axis=0), lp, bmask16) for t, lp in zip(t16, l_pow)]
        t16 = [t + bo[:SUB] for t, bo in zip(t16, both)]
        l_pow = [bo[SUB:] for bo in both]
    t16 = [t + _dot3_bd16(t, lp, bmask16) for t, lp in zip(t16, l_pow)]
    d = [jnp.concatenate([t] * nq, axis=0) * diag16 for t in t16]
    for off in (off_a, off_b):
        x = [_dot(di.astype(BF16), _block_diag(l * off, bmask)) for di, l in zip(d, l_mats)]
        d = [di + _dot(xi.astype(BF16), _block_diag(di, bmask)) for di, xi in zip(d, x)]
    return d


PAIR = 2


def _scan_kernel(r_ref, k_ref, v_ref, zw_ref, za_ref, kk_ref, ka_ref,
                 bmask_ref, bmask16_ref, tri_ref, dm_ref, cm_ref, eye16_ref,
                 y_ref, s_ref, p_ref, rh_ref, q_ref, yl_ref):
    ng = r_ref.shape[2] // GROUP
    bmask = bmask_ref[...]
    di = pl.program_id(0)

    @pl.when(pl.program_id(2) == 0)
    def _():
        s_ref[...] = jnp.zeros_like(s_ref)
        p_ref[...] = jnp.zeros_like(p_ref)
        rh_ref[...] = jnp.zeros_like(rh_ref)
        q_ref[...] = jnp.zeros_like(q_ref)
        yl_ref[...] = jnp.zeros_like(yl_ref)

    for step in range(PAIR):
        h = jnp.where(di == 0, step, PAIR - 1 - step)
        row0 = pl.multiple_of(h * CHUNK, CHUNK)
        for g in range(ng):
            cols = slice(g * GROUP, (g + 1) * GROUP)
            s_bd = _block_diag(s_ref[:, cols], bmask)
            out = _dot(jnp.concatenate([p_ref[h, :, cols], rh_ref[h, :, cols]], axis=0), s_bd)
            s_ref[:, cols] = out[:CHUNK] + q_ref[h, :, cols]
            y_ref[0, 0, pl.ds(row0, CHUNK), cols] = (out[CHUNK:] + yl_ref[h, :, cols]).astype(y_ref.dtype)

    eye = cm_ref[0]
    m_strict = dm_ref[0, 0]
    m_incl = dm_ref[0, 1]
    chains = [(h, g) for h in range(PAIR) for g in range(ng)]
    pieces = lambda x: [x[h * CHUNK:(h + 1) * CHUNK, g * GROUP:(g + 1) * GROUP] for h, g in chains]
    halves = lambda f, x: jnp.concatenate([f(x[h * CHUNK:(h + 1) * CHUNK]) for h in range(PAIR)], axis=0)
    bd = lambda x: _block_diag(x, bmask)
    stack = lambda x, y: jnp.concatenate([x, y], axis=0)

    r = r_ref[0].astype(F32)
    k = k_ref[0].astype(F32)
    v = v_ref[0].astype(F32)
    lw = (-math.exp(-0.5)) * jax.nn.sigmoid(zw_ref[0].astype(F32))
    a = jax.nn.sigmoid(za_ref[0].astype(F32))
    kkr = k * kk_ref[...]
    sq = jnp.concatenate(pieces(kkr * kkr), axis=0)
    sq_hi, sq_lo = _split2(sq)
    ssq = _dot(sq_hi, bmask) + _dot(sq_lo, bmask)
    ssq = jnp.concatenate(
        [jnp.concatenate([ssq[(h * ng + g) * CHUNK:(h * ng + g + 1) * CHUNK] for g in range(ng)], axis=1)
         for h in range(PAIR)], axis=0)
    kk = kkr * lax.rsqrt(jnp.maximum(ssq, 1e-24))
    kd = k * (1.0 + (a - 1.0) * ka_ref[...])
    bb = kk * a
    tri = tri_ref[0]

    def cumulative(x):
        x_hi, x_lo = _split2(x)
        return _dot(tri, x_hi) + _dot(tri, x_lo)

    g_cum = halves(cumulative, lw)
    g_end = halves(lambda x: jnp.broadcast_to(jnp.sum(x, axis=0, keepdims=True), x.shape), lw)
    e_neg = jnp.exp(-g_cum)
    e_end = jnp.exp(g_end - g_cum)
    a_t = pieces(-kk * jnp.exp(g_cum - lw))
    r_t = pieces(r * jnp.exp(g_cum))
    b_t = pieces(bb * e_neg)
    k_t = pieces(kd * e_neg)
    b_h = pieces((bb * e_end).astype(BF16))
    k_h = pieces((kd * e_end).astype(BF16))
    v_g = pieces(v)
    decay_end = [x[0:1] for x in pieces(jnp.exp(g_end))]

    ar = [stack(x, y).astype(BF16) for x, y in zip(a_t, r_t)]
    mb = [_dot_nt(x, bd(y)) for x, y in zip(ar, b_t)]
    mk = [_dot_nt(x, bd(y)) for x, y in zip(ar, k_t)]
    l_mat = [x[:CHUNK] * m_strict for x in mb]
    m_rb = [(x[CHUNK:] * m_incl).astype(BF16) for x in mb]
    m_k = [stack(x[:CHUNK] * m_strict, x[CHUNK:] * m_incl).astype(BF16) for x in mk]
    mv = [_dot(x, bd(y)) for x, y in zip(m_k, v_g)]
    t_mat = _unit_triangular_inverse(l_mat, bmask, bmask16_ref[...], eye16_ref[...],
                                     cm_ref[1], cm_ref[2], cm_ref[3])
    t_g = [stack(t, _dot(m, bd(t))).astype(BF16) for t, m in zip(t_mat, m_rb)]
    ta = [_dot(x, bd(y)) for x, y in zip(t_g, a_t)]
    tu = [_dot(x, bd(y[:CHUNK])) for x, y in zip(t_g, mv)]
    a_h = [x[:CHUNK].astype(BF16) for x in ta]
    u_0 = [x[:CHUNK] for x in tu]
    rh = [x + y[CHUNK:] for x, y in zip(r_t, ta)]
    yl = [x[CHUNK:] + y[CHUNK:] for x, y in zip(tu, mv)]
    p_full = [_dot_tn(x, y) for x, y in zip(b_h, a_h)]
    q_full = [_dot_tn(stack(x, y), stack(u, w).astype(BF16)) for x, y, u, w in zip(b_h, k_h, u_0, v_g)]
    for i, (h, g) in enumerate(chains):
        cols = slice(g * GROUP, (g + 1) * GROUP)
        p_ref[h, :, cols] = (_fold_heads(p_full[i]) + eye * decay_end[i]).astype(p_ref.dtype)
        rh_ref[h, :, cols] = rh[i].astype(rh_ref.dtype)
        q_ref[h, :, cols] = _fold_heads(q_full[i])
        yl_ref[h, :, cols] = yl[i]


def _scan_consts():
    lane = np.arange(GROUP)
    bmask = (lane[:, None] // HEAD == lane[None, :] // HEAD).astype(np.float32)
    i = np.arange(CHUNK)[:, None]
    j = np.arange(CHUNK)[None, :]
    jl = (lane % HEAD)[None, :]
    tri = np.stack([(j <= i), (j >= i)]).astype(np.float32)
    dir_masks = np.stack([np.stack([(jl < i), (jl <= i)]),
                          np.stack([(jl > i), (jl >= i)])]).astype(np.float32)
    same16 = (jl // SUB == i // SUB)
    same32 = (jl // (2 * SUB) == i // (2 * SUB))
    common = np.stack([(jl == i), same16, same32 & ~same16, ~same32]).astype(np.float32)
    bmask16 = (lane[:, None] // SUB == lane[None, :] // SUB).astype(np.float32)
    eye16 = ((lane % SUB)[None, :] == np.arange(SUB)[:, None]).astype(np.float32)
    return (jnp.asarray(bmask, BF16), jnp.asarray(bmask16, BF16), jnp.asarray(tri, BF16),
            jnp.asarray(dir_masks, F32), jnp.asarray(common, F32), jnp.asarray(eye16, F32))


def _scan(r, k, v, zw, za, k_k, k_a, n_ctx_chunks):
    b, l, d = r.shape
    rows = PAIR * CHUNK
    npair = l // rows
    assert l % rows == 0 and n_ctx_chunks % PAIR == 0
    n_ctx = n_ctx_chunks // PAIR
    bmask, bmask16, tri, dir_masks, common, eye16 = _scan_consts()

    def pair_of(di, s):
        back = jnp.where(s < n_ctx, n_ctx - 1 - s, npair - 1 + n_ctx - s)
        return jnp.where(di == 0, s, back)

    fold_pair = lambda di, s: pair_of(di, jnp.minimum(s, npair - 1))
    apply_pair = lambda di, s: pair_of(di, jnp.maximum(s - 1, 0))
    tok = pl.BlockSpec((1, rows, d), lambda di, bi, s: (bi, fold_pair(di, s), 0))
    tok2 = pl.BlockSpec((1, rows, d), lambda di, bi, s: (bi, fold_pair(di, s), di))
    par = pl.BlockSpec((1, d), lambda di, bi, s: (0, 0))
    return pl.pallas_call(
        _scan_kernel,
        out_shape=jax.ShapeDtypeStruct((2, b, l, d), BF16),
        grid=(2, b, npair + 1),
        in_specs=[tok, tok, tok, tok2, tok2, par, par,
                  pl.BlockSpec((GROUP, GROUP), lambda di, bi, s: (0, 0)),
                  pl.BlockSpec((GROUP, GROUP), lambda di, bi, s: (0, 0)),
                  pl.BlockSpec((1, CHUNK, CHUNK), lambda di, bi, s: (di, 0, 0)),
                  pl.BlockSpec((1, 2, CHUNK, GROUP), lambda di, bi, s: (di, 0, 0, 0)),
                  pl.BlockSpec((4, CHUNK, GROUP), lambda di, bi, s: (0, 0, 0)),
                  pl.BlockSpec((SUB, GROUP), lambda di, bi, s: (0, 0))],
        out_specs=pl.BlockSpec((1, 1, rows, d), lambda di, bi, s: (di, bi, apply_pair(di, s), 0)),
        scratch_shapes=[pltpu.VMEM((CHUNK, d), F32), pltpu.VMEM((PAIR, CHUNK, d), BF16),
                        pltpu.VMEM((PAIR, CHUNK, d), BF16), pltpu.VMEM((PAIR, CHUNK, d), F32),
                        pltpu.VMEM((PAIR, CHUNK, d), F32)],
        compiler_params=_params(("parallel", "parallel", "arbitrary")),
        name="scan",
    )(r, k, v, zw, za, k_k.reshape(1, d), k_a.reshape(1, d), bmask, bmask16, tri, dir_masks, common, eye16)


def _head_sum(x, ones_bd):
    hi, lo = _split2(x)
    return _dot(hi, ones_bd) + _dot(lo, ones_bd)


def _rwkv_post_kernel(y_ref, r_ref, k_ref, v_ref, za0_ref, za1_ref, gate_ref,
                      ka_ref, rk_ref, gnw_ref, gnb_ref, ones_ref, o_ref):
    ones_bd = ones_ref[...]
    tm, d = o_ref.shape[1], o_ref.shape[2]
    ng = d // GROUP
    to_rows = lambda x: jnp.concatenate([x[:, g * GROUP:(g + 1) * GROUP] for g in range(ng)], axis=0)
    to_cols = lambda x: jnp.concatenate([x[g * tm:(g + 1) * tm] for g in range(ng)], axis=1)
    head_mean = lambda x: to_cols(_head_sum(to_rows(x), ones_bd)) * (1.0 / HEAD)
    y = y_ref[0, 0].astype(F32) + y_ref[1, 0].astype(F32)
    yc = y - head_mean(y)
    var = head_mean(yc * yc)
    o = yc * lax.rsqrt(var + GN_EPS) * gnw_ref[...] + gnb_ref[...]
    a_sum = jax.nn.sigmoid(za0_ref[0].astype(F32)) + jax.nn.sigmoid(za1_ref[0].astype(F32))
    r = r_ref[0].astype(F32)
    k_sum = k_ref[0].astype(F32) * (2.0 + (a_sum - 2.0) * ka_ref[...])
    bonus = head_mean(r * k_sum * rk_ref[...]) * float(HEAD) * v_ref[0].astype(F32)
    o_ref[0] = ((o + bonus) * gate_ref[0].astype(F32)).astype(o_ref.dtype)


def _rwkv_post(y, r, k, v, za, gate, k_a, r_k, gn_w, gn_b, n_ctx, tm=256):
    _, b, l, d = y.shape
    t = l - n_ctx
    off = n_ctx // tm
    ones_bd = _scan_consts()[0]
    tok = pl.BlockSpec((1, tm, d), lambda bi, s: (bi, s + off, 0))
    par = pl.BlockSpec((1, d), lambda bi, s: (0, 0))
    return pl.pallas_call(
        _rwkv_post_kernel,
        out_shape=jax.ShapeDtypeStruct((b, t, d), BF16),
        grid=(b, t // tm),
        in_specs=[pl.BlockSpec((2, 1, tm, d), lambda bi, s: (0, bi, s + off, 0)),
                  tok, tok, tok,
                  pl.BlockSpec((1, tm, d), lambda bi, s: (bi, s + off, 0)),
                  pl.BlockSpec((1, tm, d), lambda bi, s: (bi, s + off, 1)),
                  tok, par, par, par, par,
                  pl.BlockSpec((GROUP, GROUP), lambda bi, s: (0, 0))],
        out_specs=pl.BlockSpec((1, tm, d), lambda bi, s: (bi, s, 0)),
        compiler_params=_params(("parallel", "parallel")),
        name="rwkv_post",
    )(y, r, k, v, za, za, gate, k_a.reshape(1, d), r_k.reshape(1, d), gn_w.reshape(1, d),
      gn_b.reshape(1, d), ones_bd)


def _proj_norm_kernel(a_ref, w_ref, x_ref, pos_ref, g_ref, lnw_ref, lnb_ref, sc_ref, sh_ref,
                      o_ref, h_ref, acc_ref, *, alpha):
    j = pl.program_id(1)
    nj = acc_ref.shape[0]
    tn = acc_ref.shape[2]
    acc_ref[j] = _dot(a_ref[...], w_ref[...].astype(BF16))

    @pl.when(j == nj - 1)
    def _():
        gate = g_ref[0]
        for jj in range(nj):
            cols = slice(jj * tn, (jj + 1) * tn)
            o_ref[:, cols] = alpha * (x_ref[:, cols] + pos_ref[:, cols]) + gate[:, cols] * acc_ref[jj]
        x_new = _layer_norm(o_ref[...], lnw_ref[...], lnb_ref[...])
        o_ref[...] = x_new
        h_ref[...] = (x_new * (1.0 + sc_ref[0]) + sh_ref[0]).astype(h_ref.dtype)


def _proj_norm(a, w, x, pos, gate, ln_w, ln_b, sc, sh, alpha, tm=512, tn=512):
    m, k = a.shape
    d = w.shape[1]
    t = pos.shape[0]
    tm = _tile(t, tm)
    tn = _tile(d, tn)
    tpb = t // tm
    vec = pl.BlockSpec((1, 1, d), lambda i, j: (i // tpb, 0, 0))
    return pl.pallas_call(
        functools.partial(_proj_norm_kernel, alpha=alpha),
        out_shape=[jax.ShapeDtypeStruct((m, d), F32), jax.ShapeDtypeStruct((m, d), BF16)],
        grid=(m // tm, d // tn),
        in_specs=[pl.BlockSpec((tm, k), lambda i, j: (i, 0)),
                  pl.BlockSpec((k, tn), lambda i, j: (0, j)),
                  pl.BlockSpec((tm, d), lambda i, j: (i, 0)),
                  pl.BlockSpec((tm, d), lambda i, j: (i % tpb, 0)),
                  vec,
                  pl.BlockSpec((1, d), lambda i, j: (0, 0)),
                  pl.BlockSpec((1, d), lambda i, j: (0, 0)),
                  vec, vec],
        out_specs=[pl.BlockSpec((tm, d), lambda i, j: (i, 0))] * 2,
        scratch_shapes=[pltpu.VMEM((d // tn, tm, tn), F32)],
        compiler_params=_params(("parallel", "arbitrary")),
        name="proj_norm",
    )(a, w, x, pos, gate, ln_w.reshape(1, d), ln_b.reshape(1, d), sc, sh)


def _swiglu_halves(h, w1_ref, w3_ref, w2_ref, lead):
    tf = w1_ref.shape[-1]
    halves = [slice(0, tf // 2), slice(tf // 2, tf)]
    a1 = [_dot(h, w1_ref[lead + (slice(None), c)].astype(BF16)) for c in halves]
    a3 = [_dot(h, w3_ref[lead + (slice(None), c)].astype(BF16)) for c in halves]
    u = [(x * jax.nn.sigmoid(x) * y).astype(BF16) for x, y in zip(a1, a3)]
    y = [_dot(x, w2_ref[lead + (c, slice(None))].astype(BF16)) for x, c in zip(u, halves)]
    return y[0] + y[1]


def _ffn_kernel(h_ref, w1_ref, w3_ref, w2_ref, o_ref, acc_ref):
    j = pl.program_id(1)

    @pl.when(j == 0)
    def _():
        acc_ref[...] = jnp.zeros_like(acc_ref)

    acc_ref[...] += _swiglu_halves(h_ref[...], w1_ref, w3_ref, w2_ref, ())

    @pl.when(j == pl.num_programs(1) - 1)
    def _():
        o_ref[...] = acc_ref[...].astype(o_ref.dtype)


def _ffn(h, w1, w3, w2, tm=1024, tf=512):
    m, d = h.shape
    f = w1.shape[1]
    tm = _tile(m, tm)
    tf = _tile(f, tf)
    return pl.pallas_call(
        _ffn_kernel,
        out_shape=jax.ShapeDtypeStruct((m, d), BF16),
        grid=(m // tm, f // tf),
        in_specs=[pl.BlockSpec((tm, d), lambda i, j: (i, 0)),
                  pl.BlockSpec((d, tf), lambda i, j: (0, j)),
                  pl.BlockSpec((d, tf), lambda i, j: (0, j)),
                  pl.BlockSpec((tf, d), lambda i, j: (j, 0))],
        out_specs=pl.BlockSpec((tm, d), lambda i, j: (i, 0)),
        scratch_shapes=[pltpu.VMEM((tm, d), F32)],
        compiler_params=_params(("parallel", "arbitrary")),
        name="ffn",
    )(h, w1, w3, w2)


def _residual_norm_kernel(x_ref, y_ref, g_ref, lnw_ref, lnb_ref, o_ref, *, alpha):
    z = alpha * x_ref[...] + g_ref[0] * y_ref[...].astype(F32)
    o_ref[...] = _layer_norm(z, lnw_ref[...], lnb_ref[...])


def _residual_norm(x, y, gate, ln_w, ln_b, alpha, rows_per_batch, tm=512):
    m, d = x.shape
    tm = _tile(rows_per_batch, tm)
    tpb = rows_per_batch // tm
    tok = pl.BlockSpec((tm, d), lambda i: (i, 0))
    row = pl.BlockSpec((1, d), lambda i: (0, 0))
    return pl.pallas_call(
        functools.partial(_residual_norm_kernel, alpha=alpha),
        out_shape=jax.ShapeDtypeStruct((m, d), F32),
        grid=(m // tm,),
        in_specs=[tok, tok, pl.BlockSpec((1, 1, d), lambda i: (i // tpb, 0, 0)), row, row],
        out_specs=tok,
        compiler_params=_params(("parallel",)),
        name="residual_norm",
    )(x, y, gate, ln_w.reshape(1, d), ln_b.reshape(1, d))


def _shift_down(x, s, row):
    return jnp.where(row >= s, pltpu.roll(x, s, axis=0), 0.0)


def _shift_up(x, s, row):
    t = x.shape[0]
    return jnp.where(row < t - s, pltpu.roll(x, t - s, axis=0), 0.0)


def _pool_kernel(x_ref, sc_ref, sh_ref, w_ref, scale_ref, o_ref):
    g = pl.program_id(0)
    t = x_ref.shape[1]
    h = x_ref[0] * (1.0 + sc_ref[0]) + sh_ref[0]
    row = lax.broadcasted_iota(jnp.int32, h.shape, 0)
    w = w_ref[0].astype(BF16)
    for gi, win in enumerate(POOL_WINDOWS):
        @pl.when(g == gi)
        def _(win=win):
            half = win // 2
            back = h
            fwd = h
            m = 1
            while m < half:
                back = back + _shift_down(back, m, row)
                fwd = fwd + _shift_up(fwd, m, row)
                m *= 2
            total = _shift_down(back, 1, row) + fwd
            count = (jnp.minimum(row + half, t) - jnp.maximum(row - half, 0)).astype(F32)
            pooled = (total / count - h).astype(BF16)
            o_ref[0] = _dot(pooled, w) * scale_ref[...]


def _pool(x, sc, sh, w_pool, scale):
    b, t, d = x.shape
    ng, p, _ = w_pool.shape
    vec = pl.BlockSpec((1, 1, p), lambda g, bi: (bi, 0, g))
    return pl.pallas_call(
        _pool_kernel,
        out_shape=jax.ShapeDtypeStruct((b, t, d), F32),
        grid=(ng, b),
        in_specs=[pl.BlockSpec((1, t, p), lambda g, bi: (bi, 0, g)), vec, vec,
                  pl.BlockSpec((1, p, p), lambda g, bi: (g, 0, 0)),
                  pl.BlockSpec((1, p), lambda g, bi: (0, g))],
        out_specs=pl.BlockSpec((1, t, p), lambda g, bi: (bi, 0, g)),
        compiler_params=_params(("parallel", "parallel")),
        name="pool",
    )(x, sc, sh, w_pool, scale.reshape(1, d))


def _route_kernel(x_ref, y_ref, g_ref, lnw_ref, lnb_ref, sc_ref, sh_ref, rt_ref,
                  xo_ref, h_ref, route_ref, *, alpha):
    x = _layer_norm(alpha * x_ref[...] + g_ref[0] * y_ref[...], lnw_ref[...], lnb_ref[...])
    xo_ref[...] = x
    h = x * (1.0 + sc_ref[0]) + sh_ref[0]
    h_ref[...] = h
    r1, r2, r3 = _split3(rt_ref[...])
    h1, h2, h3 = _split3(h)
    logits = (_dot_nt(r1, h1) + (_dot_nt(r1, h2) + _dot_nt(r2, h1))
              + (_dot_nt(r1, h3) + _dot_nt(r2, h2) + _dot_nt(r3, h1)))
    mx = jnp.max(logits, axis=0, keepdims=True)
    e = jnp.exp(logits - mx)
    p = e / jnp.sum(e, axis=0, keepdims=True)
    idx = lax.broadcasted_iota(jnp.int32, p.shape, 0)
    p1 = jnp.max(p, axis=0, keepdims=True)
    i1 = jnp.min(jnp.where(p == p1, idx, N_EXPERTS), axis=0, keepdims=True)
    rest = jnp.where(idx == i1, -1.0, p)
    p2 = jnp.max(rest, axis=0, keepdims=True)
    i2 = jnp.min(jnp.where(rest == p2, idx, N_EXPERTS), axis=0, keepdims=True)
    den = p1 + p2
    out = jnp.where(idx == 0, i1.astype(F32), 0.0)
    out = jnp.where(idx == 1, i2.astype(F32), out)
    out = jnp.where(idx == 2, p1 / den, out)
    out = jnp.where(idx == 3, p2 / den, out)
    route_ref[...] = out


def _route(x, y, gate, ln_w, ln_b, sc, sh, router, alpha, rows_per_batch, tm=256):
    m, d = x.shape
    tpb = rows_per_batch // tm
    vec = pl.BlockSpec((1, 1, d), lambda i: (i // tpb, 0, 0))
    row = pl.BlockSpec((1, d), lambda i: (0, 0))
    tok = pl.BlockSpec((tm, d), lambda i: (i, 0))
    return pl.pallas_call(
        functools.partial(_route_kernel, alpha=alpha),
        out_shape=[jax.ShapeDtypeStruct((m, d), F32), jax.ShapeDtypeStruct((m, d), F32),
                   jax.ShapeDtypeStruct((N_EXPERTS, m), F32)],
        grid=(m // tm,),
        in_specs=[tok, tok, vec, row, row, vec, vec,
                  pl.BlockSpec((N_EXPERTS, d), lambda i: (0, 0))],
        out_specs=[tok, tok, pl.BlockSpec((N_EXPERTS, tm), lambda i: (0, i))],
        compiler_params=_params(("parallel",)),
        name="route",
    )(x, y, gate, ln_w.reshape(1, d), ln_b.reshape(1, d), sc, sh, router.T)


def _row_copy(src_hbm, dst_vmem, sem, src_row, dst_row):
    return pltpu.make_async_copy(src_hbm.at[pl.ds(src_row, 1)], dst_vmem.at[pl.ds(dst_row, 1)], sem)


def _moe_kernel(ce_ref, nv_ref, idx_ref, h_ref, w1_ref, w3_ref, w2_ref, o_ref,
                stage_ref, work_ref, gather_sem, *, share):
    c = pl.program_id(0)
    j = pl.program_id(1)
    n_chunks = pl.num_programs(0)
    nj = pl.num_programs(1)
    nv = nv_ref[c]
    full = work_ref.shape[0]
    stage_rows = stage_ref.shape[0]

    def gather(chunk, r):
        return _row_copy(h_ref, stage_ref, gather_sem, idx_ref[chunk * full + r], r)

    def wait_all_gathers():
        def body(r, carry):
            gather(0, r).wait()
            return carry
        lax.fori_loop(0, stage_rows, body, 0, unroll=8)

    @pl.when(j == 0)
    def _():
        @pl.when(c == 0)
        def _():
            def body(r, carry):
                gather(0, r).start()
                return carry
            lax.fori_loop(0, stage_rows, body, 0, unroll=8)

        o_ref[...] = jnp.zeros_like(o_ref)
        wait_all_gathers()
        work_ref[...] = stage_ref[0:full, :].astype(work_ref.dtype)

    def request_next_share():
        for i in range(share):
            gather(c + 1, j * share + i).start()

    def swiglu_rows(rows):
        o_ref[rows, :] += _swiglu_halves(work_ref[rows, :], w1_ref, w3_ref, w2_ref, (0,))

    n_sub = (nv + (MOE_SUB - 1)) // MOE_SUB
    for k in range(full // MOE_SUB + 1):
        @pl.when(n_sub == k)
        def _(k=k):
            request_next_share()
            if k:
                swiglu_rows(slice(0, k * MOE_SUB))

    @pl.when(jnp.logical_and(j == nj - 1, c == n_chunks - 1))
    def _():
        wait_all_gathers()


def _moe_ffn(h, src_idx, chunk_expert, chunk_valid, w1, w3, w2, n_chunks, tf=512):
    d = h.shape[1]
    f = w1.shape[2]
    tf = _tile(f, tf)
    nj = f // tf
    share = -(-MOE_CHUNK // nj)
    share = -(-share // 8) * 8
    stage_rows = share * nj
    assert src_idx.shape[0] >= n_chunks * MOE_CHUNK + stage_rows

    def jeff(c, j, nv):
        return jnp.where(nv[c] > 0, j, nj - 1)

    return pl.pallas_call(
        functools.partial(_moe_kernel, share=share),
        out_shape=jax.ShapeDtypeStruct((n_chunks * MOE_CHUNK, d), F32),
        grid_spec=pltpu.PrefetchScalarGridSpec(
            num_scalar_prefetch=3,
            grid=(n_chunks, nj),
            in_specs=[pl.BlockSpec(memory_space=pl.ANY),
                      pl.BlockSpec((1, d, tf), lambda c, j, ce, nv, ix: (ce[c], 0, jeff(c, j, nv))),
                      pl.BlockSpec((1, d, tf), lambda c, j, ce, nv, ix: (ce[c], 0, jeff(c, j, nv))),
                      pl.BlockSpec((1, tf, d), lambda c, j, ce, nv, ix: (ce[c], jeff(c, j, nv), 0))],
            out_specs=pl.BlockSpec((MOE_CHUNK, d), lambda c, j, ce, nv, ix: (c, 0)),
            scratch_shapes=[pltpu.VMEM((stage_rows, d), F32), pltpu.VMEM((MOE_CHUNK, d), BF16),
                            pltpu.SemaphoreType.DMA]),
        compiler_params=_params(("arbitrary", "arbitrary"), MOE_VMEM_LIMIT_BYTES),
        name="moe_ffn",
    )(chunk_expert, chunk_valid, src_idx, h, w1, w3, w2)


def _combine_kernel(p0_ref, p1_ref, ys_ref, x_ref, gates_ref, g_ref, lnw_ref, lnb_ref,
                    o_ref, b0_ref, b1_ref, sem, *, alpha):
    i = pl.program_id(0)
    n = pl.num_programs(0)
    rows = b0_ref.shape[1]
    slot = i % 2

    def request(tile, to_slot):
        def body(r, carry):
            _row_copy(ys_ref, b0_ref.at[to_slot], sem.at[to_slot], p0_ref[tile * rows + r], r).start()
            _row_copy(ys_ref, b1_ref.at[to_slot], sem.at[to_slot], p1_ref[tile * rows + r], r).start()
            return carry
        lax.fori_loop(0, rows, body, 0, unroll=4)

    @pl.when(i == 0)
    def _():
        request(0, 0)

    @pl.when(i + 1 < n)
    def _():
        request(jnp.minimum(i + 1, n - 1), 1 - slot)

    def wait(r, carry):
        _row_copy(ys_ref, b0_ref.at[slot], sem.at[slot], 0, r).wait()
        _row_copy(ys_ref, b1_ref.at[slot], sem.at[slot], 0, r).wait()
        return carry

    lax.fori_loop(0, rows, wait, 0, unroll=4)
    gates = gates_ref[...]
    y = gates[:, 0:1] * b0_ref[slot] + gates[:, 1:2] * b1_ref[slot]
    z = alpha * x_ref[...] + g_ref[0] * y
    o_ref[...] = _layer_norm(z, lnw_ref[...], lnb_ref[...])


def _combine(ys, pos0, pos1, gates, x, gate_vec, ln_w, ln_b, alpha, rows_per_batch, rows=256):
    m, d = x.shape
    tpb = rows_per_batch // rows
    tok = lambda i, a, b: (i, 0)
    return pl.pallas_call(
        functools.partial(_combine_kernel, alpha=alpha),
        out_shape=jax.ShapeDtypeStruct((m, d), F32),
        grid_spec=pltpu.PrefetchScalarGridSpec(
            num_scalar_prefetch=2,
            grid=(m // rows,),
            in_specs=[pl.BlockSpec(memory_space=pl.ANY),
                      pl.BlockSpec((rows, d), tok),
                      pl.BlockSpec((rows, 2), tok),
                      pl.BlockSpec((1, 1, d), lambda i, a, b: (i // tpb, 0, 0)),
                      pl.BlockSpec((1, d), lambda i, a, b: (0, 0)),
                      pl.BlockSpec((1, d), lambda i, a, b: (0, 0))],
            out_specs=pl.BlockSpec((rows, d), tok),
            scratch_shapes=[pltpu.VMEM((2, rows, d), F32), pltpu.VMEM((2, rows, d), F32),
                            pltpu.SemaphoreType.DMA((2,))]),
        compiler_params=_params(("arbitrary",)),
        name="moe_combine",
    )(pos0, pos1, ys, x, gates, gate_vec, ln_w.reshape(1, d), ln_b.reshape(1, d))


def _routing_tables(route, n_chunks, table_len):
    n = route.shape[1]
    experts = jnp.concatenate([route[0], route[1]]).astype(jnp.int32)
    onehot = (experts[:, None] == jnp.arange(N_EXPERTS, dtype=jnp.int32)[None, :]).astype(jnp.int32)
    csum = jnp.cumsum(onehot, axis=0)
    rank = jnp.sum((csum - 1) * onehot, axis=1)
    counts = csum[-1]
    chunks_e = (counts + MOE_CHUNK - 1) // MOE_CHUNK
    rows_e = jnp.maximum((counts + chunks_e * 8 - 1) // jnp.maximum(chunks_e * 8, 1) * 8, 8)
    chunk_end = jnp.cumsum(chunks_e)
    chunk_start = chunk_end - chunks_e
    dest = (chunk_start[experts] + rank // rows_e[experts]) * MOE_CHUNK + rank % rows_e[experts]
    token = jnp.arange(2 * n, dtype=jnp.int32) % n
    src_idx = jnp.zeros((table_len,), jnp.int32).at[dest].set(token)
    cid = jnp.arange(n_chunks, dtype=jnp.int32)
    used = cid < chunk_end[-1]
    last_used = jnp.maximum(chunk_end[-1] - 1, 0)
    ce = jnp.sum((jnp.minimum(cid, last_used)[:, None] >= chunk_end[None, :]).astype(jnp.int32), axis=1)
    ce = jnp.minimum(ce, N_EXPERTS - 1)
    nvalid = jnp.clip(counts[ce] - (cid - chunk_start[ce]) * rows_e[ce], 0, rows_e[ce])
    nvalid = jnp.where(used, nvalid, 0).astype(jnp.int32)
    return src_idx, ce, nvalid, dest[:n], dest[n:]


def _position_embedding(rows, width, d):
    quarter = d // 4
    omega = 1.0 / (POS_BASE ** (jnp.arange(quarter, dtype=F32) / quarter))
    ar = jnp.arange(rows, dtype=F32)[:, None] * omega[None, :]
    ac = jnp.arange(width, dtype=F32)[:, None] * omega[None, :]
    row_part = jnp.repeat(jnp.concatenate([jnp.sin(ar), jnp.cos(ar)], axis=-1), width, axis=0)
    col_part = jnp.tile(jnp.concatenate([jnp.sin(ac), jnp.cos(ac)], axis=-1), (rows, 1))
    return jnp.concatenate([row_part, col_part], axis=-1)


def _block_diag2(w):
    z = jnp.zeros_like(w[0])
    return jnp.concatenate([jnp.concatenate([w[0], z], axis=1), jnp.concatenate([z, w[1]], axis=1)], axis=0)


def kernel(x, c, ctx, c_ctx, w_mod, b_mod, ln_w, ln_b, rwkv_mu, rwkv_w_r, rwkv_w_k, rwkv_w_v, rwkv_w_o, rwkv_decay_w0, rwkv_decay_w1, rwkv_decay_w2, rwkv_iclr_a0, rwkv_iclr_a1, rwkv_iclr_a2, rwkv_gate_g1, rwkv_gate_g2, rwkv_k_k, rwkv_k_a, rwkv_r_k, rwkv_gn_w, rwkv_gn_b, pool_w, pool_scale, ffn_w1, ffn_w3, ffn_w2, moe_router, moe_w1, moe_w3, moe_w2):
    b, t, d = x.shape
    n_ctx = ctx.shape[1]
    depth = w_mod.shape[0]
    assert depth == 2 and rwkv_mu.shape[0] == 1 and pool_w.shape[0] == 1
    alpha = (2.0 * depth) ** 0.25
    grid_w = 64
    l = n_ctx + t
    n = b * t

    cond = jnp.zeros((8, d), F32).at[:b].set(c).at[b].set(c_ctx)
    mod = _adaln(cond, w_mod, b_mod).reshape(depth, 8, 6, d)
    lat = lambda layer, which: mod[layer, :b, which].reshape(b, 1, d)
    cvec = lambda layer, which: mod[layer, b, which].reshape(1, 1, d)

    pos = _position_embedding(t // grid_w, grid_w, d)

    xr, xw, xk, xv, xa, xg = _rwkv_mix(ctx, x, pos, lat(0, 1), lat(0, 0), cvec(0, 1), cvec(0, 0), rwkv_mu[0])
    flat = lambda a: a.reshape(b * l, a.shape[-1])
    r = _mm(flat(xr), rwkv_w_r[0], out_dtype=BF16)
    k = _mm(flat(xk), rwkv_w_k[0], out_dtype=BF16)
    v = _mm(flat(xv), rwkv_w_v[0], out_dtype=BF16)
    dw1 = jnp.concatenate([rwkv_decay_w1[0, 0], rwkv_decay_w1[0, 1]], axis=1)
    ia1 = jnp.concatenate([rwkv_iclr_a1[0, 0], rwkv_iclr_a1[0, 1]], axis=1)
    lora_w = _mm(flat(xw), dw1, act="tanh", out_dtype=BF16)
    zw = _mm(lora_w, _block_diag2(rwkv_decay_w2[0]), rwkv_decay_w0[0].reshape(1, 2 * d), out_dtype=BF16)
    lora_a = _mm(flat(xa), ia1, out_dtype=BF16)
    za = _mm(lora_a, _block_diag2(rwkv_iclr_a2[0]), rwkv_iclr_a0[0].reshape(1, 2 * d), out_dtype=BF16)
    lora_g = _mm(flat(xg), rwkv_gate_g1[0], act="sigmoid", out_dtype=BF16)
    gate = _mm(lora_g, rwkv_gate_g2[0], out_dtype=BF16)
    seq = lambda a: a.reshape(b, l, a.shape[-1])
    y_scan = _scan(seq(r), seq(k), seq(v), seq(zw), seq(za), rwkv_k_k[0], rwkv_k_a[0], n_ctx // CHUNK)
    og = _rwkv_post(y_scan, seq(r), seq(k), seq(v), seq(za), seq(gate), rwkv_k_a[0],
                    rwkv_r_k[0].reshape(d), rwkv_gn_w[0], rwkv_gn_b[0], n_ctx)
    x1, h1 = _proj_norm(og.reshape(n, d), rwkv_w_o[0].astype(BF16), x.reshape(n, d), pos, lat(0, 2),
                        ln_w[0, 0], ln_b[0, 0], lat(0, 4), lat(0, 3), alpha)
    y_ffn = _ffn(h1, ffn_w1[0], ffn_w3[0], ffn_w2[0])
    x2 = _residual_norm(x1, y_ffn, lat(0, 5), ln_w[0, 1], ln_b[0, 1], alpha, t)

    y_pool = _pool(x2.reshape(b, t, d), lat(1, 1), lat(1, 0), pool_w[0], pool_scale[0])
    x3, h3, route = _route(x2, y_pool.reshape(n, d), lat(1, 2), ln_w[1, 0], ln_b[1, 0],
                           lat(1, 4), lat(1, 3), moe_router[0], alpha, t)
    n_chunks = (2 * n + N_EXPERTS * (MOE_CHUNK - 1)) // MOE_CHUNK
    src_idx, chunk_expert, chunk_valid, pos0, pos1 = _routing_tables(route, n_chunks, (n_chunks + 2) * MOE_CHUNK)
    ys = _moe_ffn(h3, src_idx, chunk_expert, chunk_valid, moe_w1[0], moe_w3[0], moe_w2[0], n_chunks)
    gates = jnp.stack([route[2], route[3]], axis=1)
    out = _combine(ys, pos0, pos1, gates, x3, lat(1, 5), ln_w[1, 1], ln_b[1, 1], alpha, t)
    return out.reshape(b, t, d)
```

```python
import functools
import math

import numpy as np
import jax
import jax.numpy as jnp
from jax import lax
from jax.experimental import pallas as pl
from jax.experimental.pallas import tpu as pltpu

F32 = jnp.float32
BF16 = jnp.bfloat16

HEAD = 64
GROUP = 4 * HEAD
CHUNK = 64
LN_EPS = 1e-5
GN_EPS = 64e-5
POS_BASE = 10000.0
POOL_WINDOWS = (2, 4, 8, 16)
N_EXPERTS = 8
MOE_CHUNK = 1024
MOE_SUB = 256
VMEM_LIMIT_BYTES = 56 * 1024 * 1024
MOE_VMEM_LIMIT_BYTES = 60 * 1024 * 1024


def _params(semantics, vmem_limit_bytes=VMEM_LIMIT_BYTES):
    return pltpu.CompilerParams(dimension_semantics=semantics, vmem_limit_bytes=vmem_limit_bytes)


def _tile(n, preferred):
    t = min(preferred, n)
    while n % t:
        t //= 2
    return t


def _dot(a, b):
    return jnp.dot(a, b, preferred_element_type=F32)


def _dot_nt(a, b):
    return lax.dot_general(a, b, (((1,), (1,)), ((), ())), preferred_element_type=F32)


def _dot_tn(a, b):
    return lax.dot_general(a, b, (((0,), (0,)), ((), ())), preferred_element_type=F32)


def _split2(x):
    hi = x.astype(BF16)
    lo = (x - hi.astype(F32)).astype(BF16)
    return hi, lo


def _split3(x):
    hi = x.astype(BF16)
    r1 = x - hi.astype(F32)
    mid = r1.astype(BF16)
    lo = (r1 - mid.astype(F32)).astype(BF16)
    return hi, mid, lo


def _layer_norm(z, w, b):
    mu = jnp.mean(z, axis=-1, keepdims=True)
    zc = z - mu
    var = jnp.mean(zc * zc, axis=-1, keepdims=True)
    return zc * lax.rsqrt(var + LN_EPS) * w + b


def _adaln_kernel(c_ref, w_ref, b_ref, o_ref):
    c = c_ref[...]
    a = (c * jax.nn.sigmoid(c)).astype(BF16)
    o_ref[0] = _dot(a, w_ref[0].astype(BF16)) + b_ref[0]


def _adaln(cond, w_mod, b_mod, tn=1024):
    depth, d, n = w_mod.shape
    tn = _tile(n, tn)
    rows = cond.shape[0]
    return pl.pallas_call(
        _adaln_kernel,
        out_shape=jax.ShapeDtypeStruct((depth, rows, n), F32),
        grid=(depth, n // tn),
        in_specs=[pl.BlockSpec((rows, d), lambda l, j: (0, 0)),
                  pl.BlockSpec((1, d, tn), lambda l, j: (l, 0, j)),
                  pl.BlockSpec((1, 1, tn), lambda l, j: (l, 0, j))],
        out_specs=pl.BlockSpec((1, rows, tn), lambda l, j: (l, 0, j)),
        compiler_params=_params(("parallel", "parallel")),
        name="adaln",
    )(cond, w_mod, b_mod.reshape(depth, 1, n))


def _mm_kernel(a_ref, w_ref, b_ref, o_ref, *, act):
    acc = _dot(a_ref[...], w_ref[...].astype(BF16)) + b_ref[...]
    if act == "tanh":
        acc = jnp.tanh(acc)
    elif act == "sigmoid":
        acc = jax.nn.sigmoid(acc)
    o_ref[...] = acc.astype(o_ref.dtype)


def _mm(a, w, bias=None, *, act=None, out_dtype=F32, tm=2304, tn=512):
    m, k = a.shape
    n = w.shape[1]
    tm = _tile(m, tm)
    tn = _tile(n, tn)
    if bias is None:
        bias = jnp.zeros((1, n), F32)
    return pl.pallas_call(
        functools.partial(_mm_kernel, act=act),
        out_shape=jax.ShapeDtypeStruct((m, n), out_dtype),
        grid=(m // tm, n // tn),
        in_specs=[pl.BlockSpec((tm, k), lambda i, j: (i, 0)),
                  pl.BlockSpec((k, tn), lambda i, j: (0, j)),
                  pl.BlockSpec((1, tn), lambda i, j: (0, j))],
        out_specs=pl.BlockSpec((tm, tn), lambda i, j: (i, j)),
        compiler_params=_params(("parallel", "arbitrary")),
        name="matmul",
    )(a, w, bias.reshape(1, n))


def _mix_kernel(ctx_ref, x_ref, xp_ref, xn_ref, pos_ref, pp_ref, pn_ref,
                sc_ref, sh_ref, csc_ref, csh_ref, mu_ref,
                o0, o1, o2, o3, o4, o5, *, n_lat_tiles):
    s = pl.program_id(1)
    is_ctx = s == 0
    tm = x_ref.shape[1]
    scale = jnp.where(is_ctx, csc_ref[0], sc_ref[0]) + 1.0
    shift = jnp.where(is_ctx, csh_ref[0], sh_ref[0])
    src = jnp.where(is_ctx, ctx_ref[0], x_ref[0] + pos_ref[...])
    h = src * scale + shift
    has_prev = s > 1
    has_next = jnp.logical_and(s >= 1, s < n_lat_tiles)
    h_prev = jnp.where(has_prev, (xp_ref[0] + pp_ref[...]) * scale + shift, 0.0)[7:8]
    h_next = jnp.where(has_next, (xn_ref[0] + pn_ref[...]) * scale + shift, 0.0)[0:1]
    row = lax.broadcasted_iota(jnp.int32, h.shape, 0)
    h_m1 = jnp.where(row == 0, h_prev, pltpu.roll(h, 1, axis=0))
    h_p1 = jnp.where(row == tm - 1, h_next, pltpu.roll(h, tm - 1, axis=0))
    xx = 0.5 * (h_m1 + h_p1) - h
    for n, o_ref in enumerate((o0, o1, o2, o3, o4, o5)):
        o_ref[0] = (h + xx * mu_ref[n:n + 1]).astype(o_ref.dtype)


def _rwkv_mix(ctx, x, pos, sc, sh, csc, csh, mu):
    b, t, d = x.shape
    tm = ctx.shape[1]
    assert t % tm == 0 and tm % 8 == 0
    n_lat = t // tm
    r8 = tm // 8
    lat = lambda bi, s: (bi, jnp.maximum(s - 1, 0), 0)
    prev8 = lambda bi, s: (bi, jnp.maximum((s - 1) * r8 - 1, 0), 0)
    next8 = lambda bi, s: (bi, jnp.minimum(jnp.maximum(s, 1) * r8, t // 8 - 1), 0)
    vec = pl.BlockSpec((1, 1, d), lambda bi, s: (bi, 0, 0))
    cvec = pl.BlockSpec((1, 1, d), lambda bi, s: (0, 0, 0))
    out_sds = jax.ShapeDtypeStruct((b, tm + t, d), BF16)
    return pl.pallas_call(
        functools.partial(_mix_kernel, n_lat_tiles=n_lat),
        out_shape=[out_sds] * 6,
        grid=(b, n_lat + 1),
        in_specs=[pl.BlockSpec((1, tm, d), lambda bi, s: (bi, 0, 0)),
                  pl.BlockSpec((1, tm, d), lat),
                  pl.BlockSpec((1, 8, d), prev8),
                  pl.BlockSpec((1, 8, d), next8),
                  pl.BlockSpec((tm, d), lambda bi, s: (jnp.maximum(s - 1, 0), 0)),
                  pl.BlockSpec((8, d), lambda bi, s: (jnp.maximum((s - 1) * r8 - 1, 0), 0)),
                  pl.BlockSpec((8, d), lambda bi, s: (jnp.minimum(jnp.maximum(s, 1) * r8, t // 8 - 1), 0)),
                  vec, vec, cvec, cvec,
                  pl.BlockSpec((6, d), lambda bi, s: (0, 0))],
        out_specs=[pl.BlockSpec((1, tm, d), lambda bi, s: (bi, s, 0))] * 6,
        compiler_params=_params(("parallel", "parallel")),
        name="rwkv_mix",
    )(ctx, x, x, x, pos, pos, pos, sc, sh, csc, csh, mu)


def _block_diag(x, bmask):
    xb = x.astype(BF16)
    zero = jnp.zeros((HEAD, GROUP // 2), BF16)
    rows = []
    for h in range(GROUP // HEAD):
        t = h // 2
        blk = xb[:, t * 128:(t + 1) * 128] * bmask[h * HEAD:(h + 1) * HEAD, t * 128:(t + 1) * 128]
        rows.append(jnp.concatenate([blk, zero] if t == 0 else [zero, blk], axis=1))
    return jnp.concatenate(rows, axis=0)


def _fold_heads(full):
    lane = lax.broadcasted_iota(jnp.int32, (HEAD, 128), 1)
    tiles = []
    for t in range(GROUP // 128):
        even = full[(2 * t) * HEAD:(2 * t + 1) * HEAD, t * 128:(t + 1) * 128]
        odd = full[(2 * t + 1) * HEAD:(2 * t + 2) * HEAD, t * 128:(t + 1) * 128]
        tiles.append(jnp.where(lane < HEAD, even, odd))
    return jnp.concatenate(tiles, axis=1)


SUB = 16


def _block_diag16(x, bmask16):
    xb = x.astype(BF16)
    zero = jnp.zeros((SUB, GROUP // 2), BF16)
    rows = []
    for b in range(GROUP // SUB):
        t = b // (128 // SUB)
        blk = xb[:, t * 128:(t + 1) * 128] * bmask16[b * SUB:(b + 1) * SUB, t * 128:(t + 1) * 128]
        rows.append(jnp.concatenate([blk, zero] if t == 0 else [zero, blk], axis=1))
    return jnp.concatenate(rows, axis=0)


def _dot3_bd16(a, b, bmask16):
    rows = a.shape[0]
    a_hi, a_lo = _split2(a)
    b_hi, b_lo = _split2(b)
    main = _dot(jnp.concatenate([a_hi, a_lo], axis=0), _block_diag16(b_hi, bmask16))
    return main[:rows] + main[rows:] + _dot(a_hi, _block_diag16(b_lo, bmask16))


def _unit_triangular_inverse(l_mats, bmask, bmask16, eye16, diag16, off_a, off_b):
    nq = CHUNK // SUB
    l16 = [sum(l[q * SUB:(q + 1) * SUB] * diag16[q * SUB:(q + 1) * SUB] for q in range(nq)).astype(BF16)
           for l in l_mats]
    t16 = [eye16 + x.astype(F32) for x in l16]
    l_pow = [_dot(x, _block_diag16(x, bmask16)) for x in l16]
    for _ in range(2):
        both = [_dot3_bd16(jnp.concatenate([t, lp], axis=0), lp, bmask16) for t, lp in zip(t16, l_pow)]
        t16 = [t + bo[:SUB] for t, bo in zip(t16, both)]
        l_pow = [bo[SUB:] for bo in both]
    t16 = [t + _dot3_bd16(t, lp, bmask16) for t, lp in zip(t16, l_pow)]
    d = [jnp.concatenate([t] * nq, axis=0) * diag16 for t in t16]
    for off in (off_a, off_b):
        x = [_dot(di.astype(BF16), _block_diag(l * off, bmask)) for di, l in zip(d, l_mats)]
        d = [di + _dot(xi.astype(BF16), _block_diag(di, bmask)) for di, xi in zip(d, x)]
    return d


PAIR = 2


def _scan_kernel(r_ref, k_ref, v_ref, zw_ref, za_ref, kk_ref, ka_ref,
                 bmask_ref, bmask16_ref, tri_ref, dm_ref, cm_ref, eye16_ref,
                 y_ref, s_ref, p_ref, rh_ref, q_ref, yl_ref):
    ng = r_ref.shape[2] // GROUP
    bmask = bmask_ref[...]
    di = pl.program_id(0)

    @pl.when(pl.program_id(2) == 0)
    def _():
        s_ref[...] = jnp.zeros_like(s_ref)
        p_ref[...] = jnp.zeros_like(p_ref)
        rh_ref[...] = jnp.zeros_like(rh_ref)
        q_ref[...] = jnp.zeros_like(q_ref)
        yl_ref[...] = jnp.zeros_like(yl_ref)

    for step in range(PAIR):
        h = jnp.where(di == 0, step, PAIR - 1 - step)
        row0 = pl.multiple_of(h * CHUNK, CHUNK)
        for g in range(ng):
            cols = slice(g * GROUP, (g + 1) * GROUP)
            s_bd = _block_diag(s_ref[:, cols], bmask)
            out = _dot(jnp.concatenate([p_ref[h, :, cols], rh_ref[h, :, cols]], axis=0), s_bd)
            s_ref[:, cols] = out[:CHUNK] + q_ref[h, :, cols]
            y_ref[0, 0, pl.ds(row0, CHUNK), cols] = (out[CHUNK:] + yl_ref[h, :, cols]).astype(y_ref.dtype)

    eye = cm_ref[0]
    m_strict = dm_ref[0, 0]
    m_incl = dm_ref[0, 1]
    chains = [(h, g) for h in range(PAIR) for g in range(ng)]
    pieces = lambda x: [x[h * CHUNK:(h + 1) * CHUNK, g * GROUP:(g + 1) * GROUP] for h, g in chains]
    halves = lambda f, x: jnp.concatenate([f(x[h * CHUNK:(h + 1) * CHUNK]) for h in range(PAIR)], axis=0)
    bd = lambda x: _block_diag(x, bmask)
    stack = lambda x, y: jnp.concatenate([x, y], axis=0)

    r = r_ref[0].astype(F32)
    k = k_ref[0].astype(F32)
    v = v_ref[0].astype(F32)
    lw = (-math.exp(-0.5)) * jax.nn.sigmoid(zw_ref[0].astype(F32))
    a = jax.nn.sigmoid(za_ref[0].astype(F32))
    kkr = k * kk_ref[...]
    sq = jnp.concatenate(pieces(kkr * kkr), axis=0)
    sq_hi, sq_lo = _split2(sq)
    ssq = _dot(sq_hi, bmask) + _dot(sq_lo, bmask)
    ssq = jnp.concatenate(
        [jnp.concatenate([ssq[(h * ng + g) * CHUNK:(h * ng + g + 1) * CHUNK] for g in range(ng)], axis=1)
         for h in range(PAIR)], axis=0)
    kk = kkr * lax.rsqrt(jnp.maximum(ssq, 1e-24))
    kd = k * (1.0 + (a - 1.0) * ka_ref[...])
    bb = kk * a
    tri = tri_ref[0]

    def cumulative(x):
        x_hi, x_lo = _split2(x)
        return _dot(tri, x_hi) + _dot(tri, x_lo)

    g_cum = halves(cumulative, lw)
    g_end = halves(lambda x: jnp.broadcast_to(jnp.sum(x, axis=0, keepdims=True), x.shape), lw)
    e_neg = jnp.exp(-g_cum)
    e_end = jnp.exp(g_end - g_cum)
    a_t = pieces(-kk * jnp.exp(g_cum - lw))
    r_t = pieces(r * jnp.exp(g_cum))
    b_t = pieces(bb * e_neg)
    k_t = pieces(kd * e_neg)
    b_h = pieces((bb * e_end).astype(BF16))
    k_h = pieces((kd * e_end).astype(BF16))
    v_g = pieces(v)
    decay_end = [x[0:1] for x in pieces(jnp.exp(g_end))]

    ar = [stack(x, y).astype(BF16) for x, y in zip(a_t, r_t)]
    mb = [_dot_nt(x, bd(y)) for x, y in zip(ar, b_t)]
    mk = [_dot_nt(x, bd(y)) for x, y in zip(ar, k_t)]
    l_mat = [x[:CHUNK] * m_strict for x in mb]
    m_rb = [(x[CHUNK:] * m_incl).astype(BF16) for x in mb]
    m_k = [stack(x[:CHUNK] * m_strict, x[CHUNK:] * m_incl).astype(BF16) for x in mk]
    mv = [_dot(x, bd(y)) for x, y in zip(m_k, v_g)]
    t_mat = _unit_triangular_inverse(l_mat, bmask, bmask16_ref[...], eye16_ref[...],
                                     cm_ref[1], cm_ref[2], cm_ref[3])
    t_g = [stack(t, _dot(m, bd(t))).astype(BF16) for t, m in zip(t_mat, m_rb)]
    ta = [_dot(x, bd(y)) for x, y in zip(t_g, a_t)]
    tu = [_dot(x, bd(y[:CHUNK])) for x, y in zip(t_g, mv)]
    a_h = [x[:CHUNK].astype(BF16) for x in ta]
    u_0 = [x[:CHUNK] for x in tu]
    rh = [x + y[CHUNK:] for x, y in zip(r_t, ta)]
    yl = [x[CHUNK:] + y[CHUNK:] for x, y in zip(tu, mv)]
    p_full = [_dot_tn(x, y) for x, y in zip(b_h, a_h)]
    q_full = [_dot_tn(stack(x, y), stack(u, w).astype(BF16)) for x, y, u, w in zip(b_h, k_h, u_0, v_g)]
    for i, (h, g) in enumerate(chains):
        cols = slice(g * GROUP, (g + 1) * GROUP)
        p_ref[h, :, cols] = (_fold_heads(p_full[i]) + eye * decay_end[i]).astype(p_ref.dtype)
        rh_ref[h, :, cols] = rh[i].astype(rh_ref.dtype)
        q_ref[h, :, cols] = _fold_heads(q_full[i])
        yl_ref[h, :, cols] = yl[i]


def _scan_consts():
    lane = np.arange(GROUP)
    bmask = (lane[:, None] // HEAD == lane[None, :] // HEAD).astype(np.float32)
    i = np.arange(CHUNK)[:, None]
    j = np.arange(CHUNK)[None, :]
    jl = (lane % HEAD)[None, :]
    tri = np.stack([(j <= i), (j >= i)]).astype(np.float32)
    dir_masks = np.stack([np.stack([(jl < i), (jl <= i)]),
                          np.stack([(jl > i), (jl >= i)])]).astype(np.float32)
    same16 = (jl // SUB == i // SUB)
    same32 = (jl // (2 * SUB) == i // (2 * SUB))
    common = np.stack([(jl == i), same16, same32 & ~same16, ~same32]).astype(np.float32)
    bmask16 = (lane[:, None] // SUB == lane[None, :] // SUB).astype(np.float32)
    eye16 = ((lane % SUB)[None, :] == np.arange(SUB)[:, None]).astype(np.float32)
    return (jnp.asarray(bmask, BF16), jnp.asarray(bmask16, BF16), jnp.asarray(tri, BF16),
            jnp.asarray(dir_masks, F32), jnp.asarray(common, F32), jnp.asarray(eye16, F32))


def _scan(r, k, v, zw, za, k_k, k_a, n_ctx_chunks):
    b, l, d = r.shape
    rows = PAIR * CHUNK
    npair = l // rows
    assert l % rows == 0 and n_ctx_chunks % PAIR == 0
    n_ctx = n_ctx_chunks // PAIR
    bmask, bmask16, tri, dir_masks, common, eye16 = _scan_consts()

    def pair_of(di, s):
        back = jnp.where(s < n_ctx, n_ctx - 1 - s, npair - 1 + n_ctx - s)
        return jnp.where(di == 0, s, back)

    fold_pair = lambda di, s: pair_of(di, jnp.minimum(s, npair - 1))
    apply_pair = lambda di, s: pair_of(di, jnp.maximum(s - 1, 0))
    tok = pl.BlockSpec((1, rows, d), lambda di, bi, s: (bi, fold_pair(di, s), 0))
    tok2 = pl.BlockSpec((1, rows, d), lambda di, bi, s: (bi, fold_pair(di, s), di))
    par = pl.BlockSpec((1, d), lambda di, bi, s: (0, 0))
    return pl.pallas_call(
        _scan_kernel,
        out_shape=jax.ShapeDtypeStruct((2, b, l, d), BF16),
        grid=(2, b, npair + 1),
        in_specs=[tok, tok, tok, tok2, tok2, par, par,
                  pl.BlockSpec((GROUP, GROUP), lambda di, bi, s: (0, 0)),
                  pl.BlockSpec((GROUP, GROUP), lambda di, bi, s: (0, 0)),
                  pl.BlockSpec((1, CHUNK, CHUNK), lambda di, bi, s: (di, 0, 0)),
                  pl.BlockSpec((1, 2, CHUNK, GROUP), lambda di, bi, s: (di, 0, 0, 0)),
                  pl.BlockSpec((4, CHUNK, GROUP), lambda di, bi, s: (0, 0, 0)),
                  pl.BlockSpec((SUB, GROUP), lambda di, bi, s: (0, 0))],
        out_specs=pl.BlockSpec((1, 1, rows, d), lambda di, bi, s: (di, bi, apply_pair(di, s), 0)),
        scratch_shapes=[pltpu.VMEM((CHUNK, d), F32), pltpu.VMEM((PAIR, CHUNK, d), BF16),
                        pltpu.VMEM((PAIR, CHUNK, d), BF16), pltpu.VMEM((PAIR, CHUNK, d), F32),
                        pltpu.VMEM((PAIR, CHUNK, d), F32)],
        compiler_params=_params(("parallel", "parallel", "arbitrary")),
        name="scan",
    )(r, k, v, zw, za, k_k.reshape(1, d), k_a.reshape(1, d), bmask, bmask16, tri, dir_masks, common, eye16)


def _head_sum(x, ones_bd):
    hi, lo = _split2(x)
    return _dot(hi, ones_bd) + _dot(lo, ones_bd)


def _rwkv_post_kernel(y_ref, r_ref, k_ref, v_ref, za0_ref, za1_ref, gate_ref,
                      ka_ref, rk_ref, gnw_ref, gnb_ref, ones_ref, o_ref):
    ones_bd = ones_ref[...]
    tm, d = o_ref.shape[1], o_ref.shape[2]
    ng = d // GROUP
    to_rows = lambda x: jnp.concatenate([x[:, g * GROUP:(g + 1) * GROUP] for g in range(ng)], axis=0)
    to_cols = lambda x: jnp.concatenate([x[g * tm:(g + 1) * tm] for g in range(ng)], axis=1)
    head_mean = lambda x: to_cols(_head_sum(to_rows(x), ones_bd)) * (1.0 / HEAD)
    y = y_ref[0, 0].astype(F32) + y_ref[1, 0].astype(F32)
    yc = y - head_mean(y)
    var = head_mean(yc * yc)
    o = yc * lax.rsqrt(var + GN_EPS) * gnw_ref[...] + gnb_ref[...]
    a_sum = jax.nn.sigmoid(za0_ref[0].astype(F32)) + jax.nn.sigmoid(za1_ref[0].astype(F32))
    r = r_ref[0].astype(F32)
    k_sum = k_ref[0].astype(F32) * (2.0 + (a_sum - 2.0) * ka_ref[...])
    bonus = head_mean(r * k_sum * rk_ref[...]) * float(HEAD) * v_ref[0].astype(F32)
    o_ref[0] = ((o + bonus) * gate_ref[0].astype(F32)).astype(o_ref.dtype)


def _rwkv_post(y, r, k, v, za, gate, k_a, r_k, gn_w, gn_b, n_ctx, tm=256):
    _, b, l, d = y.shape
    t = l - n_ctx
    off = n_ctx // tm
    ones_bd = _scan_consts()[0]
    tok = pl.BlockSpec((1, tm, d), lambda bi, s: (bi, s + off, 0))
    par = pl.BlockSpec((1, d), lambda bi, s: (0, 0))
    return pl.pallas_call(
        _rwkv_post_kernel,
        out_shape=jax.ShapeDtypeStruct((b, t, d), BF16),
        grid=(b, t // tm),
        in_specs=[pl.BlockSpec((2, 1, tm, d), lambda bi, s: (0, bi, s + off, 0)),
                  tok, tok, tok,
                  pl.BlockSpec((1, tm, d), lambda bi, s: (bi, s + off, 0)),
                  pl.BlockSpec((1, tm, d), lambda bi, s: (bi, s + off, 1)),
                  tok, par, par, par, par,
                  pl.BlockSpec((GROUP, GROUP), lambda bi, s: (0, 0))],
        out_specs=pl.BlockSpec((1, tm, d), lambda bi, s: (bi, s, 0)),
        compiler_params=_params(("parallel", "parallel")),
        name="rwkv_post",
    )(y, r, k, v, za, za, gate, k_a.reshape(1, d), r_k.reshape(1, d), gn_w.reshape(1, d),
      gn_b.reshape(1, d), ones_bd)


def _proj_norm_kernel(a_ref, w_ref, x_ref, pos_ref, g_ref, lnw_ref, lnb_ref, sc_ref, sh_ref,
                      o_ref, h_ref, acc_ref, *, alpha):
    j = pl.program_id(1)
    nj = acc_ref.shape[0]
    tn = acc_ref.shape[2]
    acc_ref[j] = _dot(a_ref[...], w_ref[...].astype(BF16))

    @pl.when(j == nj - 1)
    def _():
        gate = g_ref[0]
        for jj in range(nj):
            cols = slice(jj * tn, (jj + 1) * tn)
            o_ref[:, cols] = alpha * (x_ref[:, cols] + pos_ref[:, cols]) + gate[:, cols] * acc_ref[jj]
        x_new = _layer_norm(o_ref[...], lnw_ref[...], lnb_ref[...])
        o_ref[...] = x_new
        h_ref[...] = (x_new * (1.0 + sc_ref[0]) + sh_ref[0]).astype(h_ref.dtype)


def _proj_norm(a, w, x, pos, gate, ln_w, ln_b, sc, sh, alpha, tm=512, tn=512):
    m, k = a.shape
    d = w.shape[1]
    t = pos.shape[0]
    tm = _tile(t, tm)
    tn = _tile(d, tn)
    tpb = t // tm
    vec = pl.BlockSpec((1, 1, d), lambda i, j: (i // tpb, 0, 0))
    return pl.pallas_call(
        functools.partial(_proj_norm_kernel, alpha=alpha),
        out_shape=[jax.ShapeDtypeStruct((m, d), F32), jax.ShapeDtypeStruct((m, d), BF16)],
        grid=(m // tm, d // tn),
        in_specs=[pl.BlockSpec((tm, k), lambda i, j: (i, 0)),
                  pl.BlockSpec((k, tn), lambda i, j: (0, j)),
                  pl.BlockSpec((tm, d), lambda i, j: (i, 0)),
                  pl.BlockSpec((tm, d), lambda i, j: (i % tpb, 0)),
                  vec,
                  pl.BlockSpec((1, d), lambda i, j: (0, 0)),
                  pl.BlockSpec((1, d), lambda i, j: (0, 0)),
                  vec, vec],
        out_specs=[pl.BlockSpec((tm, d), lambda i, j: (i, 0))] * 2,
        scratch_shapes=[pltpu.VMEM((d // tn, tm, tn), F32)],
        compiler_params=_params(("parallel", "arbitrary")),
        name="proj_norm",
    )(a, w, x, pos, gate, ln_w.reshape(1, d), ln_b.reshape(1, d), sc, sh)


def _swiglu_halves(h, w1_ref, w3_ref, w2_ref, lead):
    tf = w1_ref.shape[-1]
    halves = [slice(0, tf // 2), slice(tf // 2, tf)]
    a1 = [_dot(h, w1_ref[lead + (slice(None), c)].astype(BF16)) for c in halves]
    a3 = [_dot(h, w3_ref[lead + (slice(None), c)].astype(BF16)) for c in halves]
    u = [(x * jax.nn.sigmoid(x) * y).astype(BF16) for x, y in zip(a1, a3)]
    y = [_dot(x, w2_ref[lead + (c, slice(None))].astype(BF16)) for x, c in zip(u, halves)]
    return y[0] + y[1]


def _ffn_kernel(h_ref, w1_ref, w3_ref, w2_ref, o_ref, acc_ref):
    j = pl.program_id(1)

    @pl.when(j == 0)
    def _():
        acc_ref[...] = jnp.zeros_like(acc_ref)

    acc_ref[...] += _swiglu_halves(h_ref[...], w1_ref, w3_ref, w2_ref, ())

    @pl.when(j == pl.num_programs(1) - 1)
    def _():
        o_ref[...] = acc_ref[...].astype(o_ref.dtype)


def _ffn(h, w1, w3, w2, tm=1024, tf=512):
    m, d = h.shape
    f = w1.shape[1]
    tm = _tile(m, tm)
    tf = _tile(f, tf)
    return pl.pallas_call(
        _ffn_kernel,
        out_shape=jax.ShapeDtypeStruct((m, d), BF16),
        grid=(m // tm, f // tf),
        in_specs=[pl.BlockSpec((tm, d), lambda i, j: (i, 0)),
                  pl.BlockSpec((d, tf), lambda i, j: (0, j)),
                  pl.BlockSpec((d, tf), lambda i, j: (0, j)),
                  pl.BlockSpec((tf, d), lambda i, j: (j, 0))],
        out_specs=pl.BlockSpec((tm, d), lambda i, j: (i, 0)),
        scratch_shapes=[pltpu.VMEM((tm, d), F32)],
        compiler_params=_params(("parallel", "arbitrary")),
        name="ffn",
    )(h, w1, w3, w2)


def _residual_norm_kernel(x_ref, y_ref, g_ref, lnw_ref, lnb_ref, o_ref, *, alpha):
    z = alpha * x_ref[...] + g_ref[0] * y_ref[...].astype(F32)
    o_ref[...] = _layer_norm(z, lnw_ref[...], lnb_ref[...])


def _residual_norm(x, y, gate, ln_w, ln_b, alpha, rows_per_batch, tm=512):
    m, d = x.shape
    tm = _tile(rows_per_batch, tm)
    tpb = rows_per_batch // tm
    tok = pl.BlockSpec((tm, d), lambda i: (i, 0))
    row = pl.BlockSpec((1, d), lambda i: (0, 0))
    return pl.pallas_call(
        functools.partial(_residual_norm_kernel, alpha=alpha),
        out_shape=jax.ShapeDtypeStruct((m, d), F32),
        grid=(m // tm,),
        in_specs=[tok, tok, pl.BlockSpec((1, 1, d), lambda i: (i // tpb, 0, 0)), row, row],
        out_specs=tok,
        compiler_params=_params(("parallel",)),
        name="residual_norm",
    )(x, y, gate, ln_w.reshape(1, d), ln_b.reshape(1, d))


def _shift_down(x, s, row):
    return jnp.where(row >= s, pltpu.roll(x, s, axis=0), 0.0)


def _shift_up(x, s, row):
    t = x.shape[0]
    return jnp.where(row < t - s, pltpu.roll(x, t - s, axis=0), 0.0)


def _pool_kernel(x_ref, sc_ref, sh_ref, w_ref, scale_ref, o_ref):
    g = pl.program_id(0)
    t = x_ref.shape[1]
    h = x_ref[0] * (1.0 + sc_ref[0]) + sh_ref[0]
    row = lax.broadcasted_iota(jnp.int32, h.shape, 0)
    w = w_ref[0].astype(BF16)
    for gi, win in enumerate(POOL_WINDOWS):
        @pl.when(g == gi)
        def _(win=win):
            half = win // 2
            back = h
            fwd = h
            m = 1
            while m < half:
                back = back + _shift_down(back, m, row)
                fwd = fwd + _shift_up(fwd, m, row)
                m *= 2
            total = _shift_down(back, 1, row) + fwd
            count = (jnp.minimum(row + half, t) - jnp.maximum(row - half, 0)).astype(F32)
            pooled = (total / count - h).astype(BF16)
            o_ref[0] = _dot(pooled, w) * scale_ref[...]


def _pool(x, sc, sh, w_pool, scale):
    b, t, d = x.shape
    ng, p, _ = w_pool.shape
    vec = pl.BlockSpec((1, 1, p), lambda g, bi: (bi, 0, g))
    return pl.pallas_call(
        _pool_kernel,
        out_shape=jax.ShapeDtypeStruct((b, t, d), F32),
        grid=(ng, b),
        in_specs=[pl.BlockSpec((1, t, p), lambda g, bi: (bi, 0, g)), vec, vec,
                  pl.BlockSpec((1, p, p), lambda g, bi: (g, 0, 0)),
                  pl.BlockSpec((1, p), lambda g, bi: (0, g))],
        out_specs=pl.BlockSpec((1, t, p), lambda g, bi: (bi, 0, g)),
        compiler_params=_params(("parallel", "parallel")),
        name="pool",
    )(x, sc, sh, w_pool, scale.reshape(1, d))


def _route_kernel(x_ref, y_ref, g_ref, lnw_ref, lnb_ref, sc_ref, sh_ref, rt_ref,
                  xo_ref, h_ref, route_ref, *, alpha):
    x = _layer_norm(alpha * x_ref[...] + g_ref[0] * y_ref[...], lnw_ref[...], lnb_ref[...])
    xo_ref[...] = x
    h = x * (1.0 + sc_ref[0]) + sh_ref[0]
    h_ref[...] = h
    r1, r2, r3 = _split3(rt_ref[...])
    h1, h2, h3 = _split3(h)
    logits = (_dot_nt(r1, h1) + (_dot_nt(r1, h2) + _dot_nt(r2, h1))
              + (_dot_nt(r1, h3) + _dot_nt(r2, h2) + _dot_nt(r3, h1)))
    mx = jnp.max(logits, axis=0, keepdims=True)
    e = jnp.exp(logits - mx)
    p = e / jnp.sum(e, axis=0, keepdims=True)
    idx = lax.broadcasted_iota(jnp.int32, p.shape, 0)
    p1 = jnp.max(p, axis=0, keepdims=True)
    i1 = jnp.min(jnp.where(p == p1, idx, N_EXPERTS), axis=0, keepdims=True)
    rest = jnp.where(idx == i1, -1.0, p)
    p2 = jnp.max(rest, axis=0, keepdims=True)
    i2 = jnp.min(jnp.where(rest == p2, idx, N_EXPERTS), axis=0, keepdims=True)
    den = p1 + p2
    out = jnp.where(idx == 0, i1.astype(F32), 0.0)
    out = jnp.where(idx == 1, i2.astype(F32), out)
    out = jnp.where(idx == 2, p1 / den, out)
    out = jnp.where(idx == 3, p2 / den, out)
    route_ref[...] = out


def _route(x, y, gate, ln_w, ln_b, sc, sh, router, alpha, rows_per_batch, tm=256):
    m, d = x.shape
    tpb = rows_per_batch // tm
    vec = pl.BlockSpec((1, 1, d), lambda i: (i // tpb, 0, 0))
    row = pl.BlockSpec((1, d), lambda i: (0, 0))
    tok = pl.BlockSpec((tm, d), lambda i: (i, 0))
    return pl.pallas_call(
        functools.partial(_route_kernel, alpha=alpha),
        out_shape=[jax.ShapeDtypeStruct((m, d), F32), jax.ShapeDtypeStruct((m, d), F32),
                   jax.ShapeDtypeStruct((N_EXPERTS, m), F32)],
        grid=(m // tm,),
        in_specs=[tok, tok, vec, row, row, vec, vec,
                  pl.BlockSpec((N_EXPERTS, d), lambda i: (0, 0))],
        out_specs=[tok, tok, pl.BlockSpec((N_EXPERTS, tm), lambda i: (0, i))],
        compiler_params=_params(("parallel",)),
        name="route",
    )(x, y, gate, ln_w.reshape(1, d), ln_b.reshape(1, d), sc, sh, router.T)


def _row_copy(src_hbm, dst_vmem, sem, src_row, dst_row):
    return pltpu.make_async_copy(src_hbm.at[pl.ds(src_row, 1)], dst_vmem.at[pl.ds(dst_row, 1)], sem)


def _moe_kernel(ce_ref, nv_ref, idx_ref, h_ref, w1_ref, w3_ref, w2_ref, o_ref,
                stage_ref, work_ref, gather_sem, *, share):
    c = pl.program_id(0)
    j = pl.program_id(1)
    n_chunks = pl.num_programs(0)
    nj = pl.num_programs(1)
    nv = nv_ref[c]
    full = work_ref.shape[0]
    stage_rows = stage_ref.shape[0]

    def gather(chunk, r):
        return _row_copy(h_ref, stage_ref, gather_sem, idx_ref[chunk * full + r], r)

    def wait_all_gathers():
        def body(r, carry):
            gather(0, r).wait()
            return carry
        lax.fori_loop(0, stage_rows, body, 0, unroll=8)

    @pl.when(j == 0)
    def _():
        @pl.when(c == 0)
        def _():
            def body(r, carry):
                gather(0, r).start()
                return carry
            lax.fori_loop(0, stage_rows, body, 0, unroll=8)

        o_ref[...] = jnp.zeros_like(o_ref)
        wait_all_gathers()
        work_ref[...] = stage_ref[0:full, :].astype(work_ref.dtype)

    def request_next_share():
        for i in range(share):
            gather(c + 1, j * share + i).start()

    def swiglu_rows(rows):
        o_ref[rows, :] += _swiglu_halves(work_ref[rows, :], w1_ref, w3_ref, w2_ref, (0,))

    n_sub = (nv + (MOE_SUB - 1)) // MOE_SUB
    for k in range(full // MOE_SUB + 1):
        @pl.when(n_sub == k)
        def _(k=k):
            request_next_share()
            if k:
                swiglu_rows(slice(0, k * MOE_SUB))

    @pl.when(jnp.logical_and(j == nj - 1, c == n_chunks - 1))
    def _():
        wait_all_gathers()


def _moe_ffn(h, src_idx, chunk_expert, chunk_valid, w1, w3, w2, n_chunks, tf=512):
    d = h.shape[1]
    f = w1.shape[2]
    tf = _tile(f, tf)
    nj = f // tf
    share = -(-MOE_CHUNK // nj)
    share = -(-share // 8) * 8
    stage_rows = share * nj
    assert src_idx.shape[0] >= n_chunks * MOE_CHUNK + stage_rows

    def jeff(c, j, nv):
        return jnp.where(nv[c] > 0, j, nj - 1)

    return pl.pallas_call(
        functools.partial(_moe_kernel, share=share),
        out_shape=jax.ShapeDtypeStruct((n_chunks * MOE_CHUNK, d), F32),
        grid_spec=pltpu.PrefetchScalarGridSpec(
            num_scalar_prefetch=3,
            grid=(n_chunks, nj),
            in_specs=[pl.BlockSpec(memory_space=pl.ANY),
                      pl.BlockSpec((1, d, tf), lambda c, j, ce, nv, ix: (ce[c], 0, jeff(c, j, nv))),
                      pl.BlockSpec((1, d, tf), lambda c, j, ce, nv, ix: (ce[c], 0, jeff(c, j, nv))),
                      pl.BlockSpec((1, tf, d), lambda c, j, ce, nv, ix: (ce[c], jeff(c, j, nv), 0))],
            out_specs=pl.BlockSpec((MOE_CHUNK, d), lambda c, j, ce, nv, ix: (c, 0)),
            scratch_shapes=[pltpu.VMEM((stage_rows, d), F32), pltpu.VMEM((MOE_CHUNK, d), BF16),
                            pltpu.SemaphoreType.DMA]),
        compiler_params=_params(("arbitrary", "arbitrary"), MOE_VMEM_LIMIT_BYTES),
        name="moe_ffn",
    )(chunk_expert, chunk_valid, src_idx, h, w1, w3, w2)


def _combine_kernel(p0_ref, p1_ref, ys_ref, x_ref, gates_ref, g_ref, lnw_ref, lnb_ref,
                    o_ref, b0_ref, b1_ref, sem, *, alpha):
    i = pl.program_id(0)
    n = pl.num_programs(0)
    rows = b0_ref.shape[1]
    slot = i % 2

    def copies(tile, to_slot, r):
        return (_row_copy(ys_ref, b0_ref.at[to_slot], sem.at[to_slot], p0_ref[tile * rows + r], r),
                _row_copy(ys_ref, b1_ref.at[to_slot], sem.at[to_slot], p1_ref[tile * rows + r], r))

    def wait_slot(to_slot):
        def body(r, carry):
            for cp in copies(0, to_slot, r):
                cp.wait()
            return carry
        lax.fori_loop(0, rows, body, 0, unroll=4)

    @pl.when(i == 0)
    def _():
        def body(r, carry):
            for cp in copies(0, 0, r):
                cp.start()
            return carry
        lax.fori_loop(0, rows, body, 0, unroll=4)

    wait_slot(slot)
    for r in range(rows):
        for cp in copies(i + 1, 1 - slot, r):
            cp.start()
    gates = gates_ref[...]
    y = gates[:, 0:1] * b0_ref[slot] + gates[:, 1:2] * b1_ref[slot]
    z = alpha * x_ref[...] + g_ref[0] * y
    o_ref[...] = _layer_norm(z, lnw_ref[...], lnb_ref[...])

    @pl.when(i == n - 1)
    def _():
        wait_slot(1 - slot)


def _combine(ys, pos0, pos1, gates, x, gate_vec, ln_w, ln_b, alpha, rows_per_batch, rows=256):
    m, d = x.shape
    tpb = rows_per_batch // rows
    tok = lambda i, a, b: (i, 0)
    return pl.pallas_call(
        functools.partial(_combine_kernel, alpha=alpha),
        out_shape=jax.ShapeDtypeStruct((m, d), F32),
        grid_spec=pltpu.PrefetchScalarGridSpec(
            num_scalar_prefetch=2,
            grid=(m // rows,),
            in_specs=[pl.BlockSpec(memory_space=pl.ANY),
                      pl.BlockSpec((rows, d), tok),
                      pl.BlockSpec((rows, 2), tok),
                      pl.BlockSpec((1, 1, d), lambda i, a, b: (i // tpb, 0, 0)),
                      pl.BlockSpec((1, d), lambda i, a, b: (0, 0)),
                      pl.BlockSpec((1, d), lambda i, a, b: (0, 0))],
            out_specs=pl.BlockSpec((rows, d), tok),
            scratch_shapes=[pltpu.VMEM((2, rows, d), F32), pltpu.VMEM((2, rows, d), F32),
                            pltpu.SemaphoreType.DMA((2,))]),
        compiler_params=_params(("arbitrary",)),
        name="moe_combine",
    )(pos0, pos1, ys, x, gates, gate_vec, ln_w.reshape(1, d), ln_b.reshape(1, d))


def _routing_tables(route, n_chunks, table_len):
    n = route.shape[1]
    experts = jnp.concatenate([route[0], route[1]]).astype(jnp.int32)
    onehot = (experts[:, None] == jnp.arange(N_EXPERTS, dtype=jnp.int32)[None, :]).astype(jnp.int32)
    csum = jnp.cumsum(onehot, axis=0)
    rank = jnp.sum((csum - 1) * onehot, axis=1)
    counts = csum[-1]
    chunks_e = (counts + MOE_CHUNK - 1) // MOE_CHUNK
    chunk_end = jnp.cumsum(chunks_e)
    chunk_start = chunk_end - chunks_e
    dest = (chunk_start * MOE_CHUNK)[experts] + rank
    token = jnp.arange(2 * n, dtype=jnp.int32) % n
    src_idx = jnp.zeros((table_len,), jnp.int32).at[dest].set(token)
    cid = jnp.arange(n_chunks, dtype=jnp.int32)
    used = cid < chunk_end[-1]
    last_used = jnp.maximum(chunk_end[-1] - 1, 0)
    ce = jnp.sum((jnp.minimum(cid, last_used)[:, None] >= chunk_end[None, :]).astype(jnp.int32), axis=1)
    ce = jnp.minimum(ce, N_EXPERTS - 1)
    nvalid = jnp.clip(counts[ce] - (cid - chunk_start[ce]) * MOE_CHUNK, 0, MOE_CHUNK)
    nvalid = jnp.where(used, nvalid, 0).astype(jnp.int32)
    return src_idx, ce, nvalid, dest[:n], dest[n:]


def _position_embedding(rows, width, d):
    quarter = d // 4
    omega = 1.0 / (POS_BASE ** (jnp.arange(quarter, dtype=F32) / quarter))
    ar = jnp.arange(rows, dtype=F32)[:, None] * omega[None, :]
    ac = jnp.arange(width, dtype=F32)[:, None] * omega[None, :]
    row_part = jnp.repeat(jnp.concatenate([jnp.sin(ar), jnp.cos(ar)], axis=-1), width, axis=0)
    col_part = jnp.tile(jnp.concatenate([jnp.sin(ac), jnp.cos(ac)], axis=-1), (rows, 1))
    return jnp.concatenate([row_part, col_part], axis=-1)


def _block_diag2(w):
    z = jnp.zeros_like(w[0])
    return jnp.concatenate([jnp.concatenate([w[0], z], axis=1), jnp.concatenate([z, w[1]], axis=1)], axis=0)


def kernel(x, c, ctx, c_ctx, w_mod, b_mod, ln_w, ln_b, rwkv_mu, rwkv_w_r, rwkv_w_k, rwkv_w_v, rwkv_w_o, rwkv_decay_w0, rwkv_decay_w1, rwkv_decay_w2, rwkv_iclr_a0, rwkv_iclr_a1, rwkv_iclr_a2, rwkv_gate_g1, rwkv_gate_g2, rwkv_k_k, rwkv_k_a, rwkv_r_k, rwkv_gn_w, rwkv_gn_b, pool_w, pool_scale, ffn_w1, ffn_w3, ffn_w2, moe_router, moe_w1, moe_w3, moe_w2):
    b, t, d = x.shape
    n_ctx = ctx.shape[1]
    depth = w_mod.shape[0]
    assert depth == 2 and rwkv_mu.shape[0] == 1 and pool_w.shape[0] == 1
    alpha = (2.0 * depth) ** 0.25
    grid_w = 64
    l = n_ctx + t
    n = b * t

    cond = jnp.zeros((8, d), F32).at[:b].set(c).at[b].set(c_ctx)
    mod = _adaln(cond, w_mod, b_mod).reshape(depth, 8, 6, d)
    lat = lambda layer, which: mod[layer, :b, which].reshape(b, 1, d)
    cvec = lambda layer, which: mod[layer, b, which].reshape(1, 1, d)

    pos = _position_embedding(t // grid_w, grid_w, d)

    xr, xw, xk, xv, xa, xg = _rwkv_mix(ctx, x, pos, lat(0, 1), lat(0, 0), cvec(0, 1), cvec(0, 0), rwkv_mu[0])
    flat = lambda a: a.reshape(b * l, a.shape[-1])
    r = _mm(flat(xr), rwkv_w_r[0], out_dtype=BF16)
    k = _mm(flat(xk), rwkv_w_k[0], out_dtype=BF16)
    v = _mm(flat(xv), rwkv_w_v[0], out_dtype=BF16)
    dw1 = jnp.concatenate([rwkv_decay_w1[0, 0], rwkv_decay_w1[0, 1]], axis=1)
    ia1 = jnp.concatenate([rwkv_iclr_a1[0, 0], rwkv_iclr_a1[0, 1]], axis=1)
    lora_w = _mm(flat(xw), dw1, act="tanh", out_dtype=BF16)
    zw = _mm(lora_w, _block_diag2(rwkv_decay_w2[0]), rwkv_decay_w0[0].reshape(1, 2 * d), out_dtype=BF16)
    lora_a = _mm(flat(xa), ia1, out_dtype=BF16)
    za = _mm(lora_a, _block_diag2(rwkv_iclr_a2[0]), rwkv_iclr_a0[0].reshape(1, 2 * d), out_dtype=BF16)
    lora_g = _mm(flat(xg), rwkv_gate_g1[0], act="sigmoid", out_dtype=BF16)
    gate = _mm(lora_g, rwkv_gate_g2[0], out_dtype=BF16)
    seq = lambda a: a.reshape(b, l, a.shape[-1])
    y_scan = _scan(seq(r), seq(k), seq(v), seq(zw), seq(za), rwkv_k_k[0], rwkv_k_a[0], n_ctx // CHUNK)
    og = _rwkv_post(y_scan, seq(r), seq(k), seq(v), seq(za), seq(gate), rwkv_k_a[0],
                    rwkv_r_k[0].reshape(d), rwkv_gn_w[0], rwkv_gn_b[0], n_ctx)
    x1, h1 = _proj_norm(og.reshape(n, d), rwkv_w_o[0].astype(BF16), x.reshape(n, d), pos, lat(0, 2),
                        ln_w[0, 0], ln_b[0, 0], lat(0, 4), lat(0, 3), alpha)
    y_ffn = _ffn(h1, ffn_w1[0], ffn_w3[0], ffn_w2[0])
    x2 = _residual_norm(x1, y_ffn, lat(0, 5), ln_w[0, 1], ln_b[0, 1], alpha, t)

    y_pool = _pool(x2.reshape(b, t, d), lat(1, 1), lat(1, 0), pool_w[0], pool_scale[0])
    x3, h3, route = _route(x2, y_pool.reshape(n, d), lat(1, 2), ln_w[1, 0], ln_b[1, 0],
                           lat(1, 4), lat(1, 3), moe_router[0], alpha, t)
    n_chunks = (2 * n + N_EXPERTS * (MOE_CHUNK - 1)) // MOE_CHUNK
    src_idx, chunk_expert, chunk_valid, pos0, pos1 = _routing_tables(route, n_chunks, (n_chunks + 2) * MOE_CHUNK)
    ys = _moe_ffn(h3, src_idx, chunk_expert, chunk_valid, moe_w1[0], moe_w3[0], moe_w2[0], n_chunks)
    gates = jnp.stack([route[2], route[3]], axis=1)
    spare = jnp.zeros((256,), jnp.int32)
    out = _combine(ys, jnp.concatenate([pos0, spare]), jnp.concatenate([pos1, spare]), gates, x3,
                   lat(1, 5), ln_w[1, 1], ln_b[1, 1], alpha, t)
    return out.reshape(b, t, d)
```

```python
import functools
import math

import numpy as np
import jax
import jax.numpy as jnp
from jax import lax
from jax.experimental import pallas as pl
from jax.experimental.pallas import tpu as pltpu

F32 = jnp.float32
BF16 = jnp.bfloat16

HEAD = 64
GROUP = 4 * HEAD
CHUNK = 64
LN_EPS = 1e-5
GN_EPS = 64e-5
POS_BASE = 10000.0
POOL_WINDOWS = (2, 4, 8, 16)
N_EXPERTS = 8
MOE_CHUNK = 1024
MOE_SUB = 256
VMEM_LIMIT_BYTES = 56 * 1024 * 1024
MOE_VMEM_LIMIT_BYTES = 60 * 1024 * 1024


def _params(semantics, vmem_limit_bytes=VMEM_LIMIT_BYTES):
    return pltpu.CompilerParams(dimension_semantics=semantics, vmem_limit_bytes=vmem_limit_bytes)


def _tile(n, preferred):
    t = min(preferred, n)
    while n % t:
        t //= 2
    return t


def _dot(a, b):
    return jnp.dot(a, b, preferred_element_type=F32)


def _dot_nt(a, b):
    return lax.dot_general(a, b, (((1,), (1,)), ((), ())), preferred_element_type=F32)


def _dot_tn(a, b):
    return lax.dot_general(a, b, (((0,), (0,)), ((), ())), preferred_element_type=F32)


def _split2(x):
    hi = x.astype(BF16)
    lo = (x - hi.astype(F32)).astype(BF16)
    return hi, lo


def _split3(x):
    hi = x.astype(BF16)
    r1 = x - hi.astype(F32)
    mid = r1.astype(BF16)
    lo = (r1 - mid.astype(F32)).astype(BF16)
    return hi, mid, lo


def _layer_norm(z, w, b):
    mu = jnp.mean(z, axis=-1, keepdims=True)
    zc = z - mu
    var = jnp.mean(zc * zc, axis=-1, keepdims=True)
    return zc * lax.rsqrt(var + LN_EPS) * w + b


def _adaln_kernel(c_ref, w_ref, b_ref, o_ref):
    c = c_ref[...]
    a = (c * jax.nn.sigmoid(c)).astype(BF16)
    o_ref[0] = _dot(a, w_ref[0].astype(BF16)) + b_ref[0]


def _adaln(cond, w_mod, b_mod, tn=1024):
    depth, d, n = w_mod.shape
    tn = _tile(n, tn)
    rows = cond.shape[0]
    return pl.pallas_call(
        _adaln_kernel,
        out_shape=jax.ShapeDtypeStruct((depth, rows, n), F32),
        grid=(depth, n // tn),
        in_specs=[pl.BlockSpec((rows, d), lambda l, j: (0, 0)),
                  pl.BlockSpec((1, d, tn), lambda l, j: (l, 0, j)),
                  pl.BlockSpec((1, 1, tn), lambda l, j: (l, 0, j))],
        out_specs=pl.BlockSpec((1, rows, tn), lambda l, j: (l, 0, j)),
        compiler_params=_params(("parallel", "parallel")),
        name="adaln",
    )(cond, w_mod, b_mod.reshape(depth, 1, n))


def _mm_kernel(a_ref, w_ref, b_ref, o_ref, *, act):
    acc = _dot(a_ref[...], w_ref[...].astype(BF16)) + b_ref[...]
    if act == "tanh":
        acc = jnp.tanh(acc)
    elif act == "sigmoid":
        acc = jax.nn.sigmoid(acc)
    o_ref[...] = acc.astype(o_ref.dtype)


def _mm(a, w, bias=None, *, act=None, out_dtype=F32, tm=2304, tn=512):
    m, k = a.shape
    n = w.shape[1]
    tm = _tile(m, tm)
    tn = _tile(n, tn)
    if bias is None:
        bias = jnp.zeros((1, n), F32)
    return pl.pallas_call(
        functools.partial(_mm_kernel, act=act),
        out_shape=jax.ShapeDtypeStruct((m, n), out_dtype),
        grid=(m // tm, n // tn),
        in_specs=[pl.BlockSpec((tm, k), lambda i, j: (i, 0)),
                  pl.BlockSpec((k, tn), lambda i, j: (0, j)),
                  pl.BlockSpec((1, tn), lambda i, j: (0, j))],
        out_specs=pl.BlockSpec((tm, tn), lambda i, j: (i, j)),
        compiler_params=_params(("parallel", "arbitrary")),
        name="matmul",
    )(a, w, bias.reshape(1, n))


def _mix_kernel(ctx_ref, x_ref, xp_ref, xn_ref, pos_ref, pp_ref, pn_ref,
                sc_ref, sh_ref, csc_ref, csh_ref, mu_ref,
                o0, o1, o2, o3, o4, o5, *, n_lat_tiles):
    s = pl.program_id(0)
    is_ctx = s == 0
    tm = x_ref.shape[1]
    scale = jnp.where(is_ctx, csc_ref[0], sc_ref[0]) + 1.0
    shift = jnp.where(is_ctx, csh_ref[0], sh_ref[0])
    src = jnp.where(is_ctx, ctx_ref[0], x_ref[0] + pos_ref[...])
    h = src * scale + shift
    has_prev = s > 1
    has_next = jnp.logical_and(s >= 1, s < n_lat_tiles)
    h_prev = jnp.where(has_prev, (xp_ref[0] + pp_ref[...]) * scale + shift, 0.0)[7:8]
    h_next = jnp.where(has_next, (xn_ref[0] + pn_ref[...]) * scale + shift, 0.0)[0:1]
    row = lax.broadcasted_iota(jnp.int32, h.shape, 0)
    h_m1 = jnp.where(row == 0, h_prev, pltpu.roll(h, 1, axis=0))
    h_p1 = jnp.where(row == tm - 1, h_next, pltpu.roll(h, tm - 1, axis=0))
    xx = 0.5 * (h_m1 + h_p1) - h
    for n, o_ref in enumerate((o0, o1, o2, o3, o4, o5)):
        o_ref[0] = (h + xx * mu_ref[n:n + 1]).astype(o_ref.dtype)


def _rwkv_mix(ctx, x, pos, sc, sh, csc, csh, mu):
    b, t, d = x.shape
    tm = ctx.shape[1]
    assert t % tm == 0 and tm % 8 == 0
    n_lat = t // tm
    r8 = tm // 8
    lat = lambda s, bi: (bi, jnp.maximum(s - 1, 0), 0)
    prev8 = lambda s, bi: (bi, jnp.maximum((s - 1) * r8 - 1, 0), 0)
    next8 = lambda s, bi: (bi, jnp.minimum(jnp.maximum(s, 1) * r8, t // 8 - 1), 0)
    vec = pl.BlockSpec((1, 1, d), lambda s, bi: (bi, 0, 0))
    cvec = pl.BlockSpec((1, 1, d), lambda s, bi: (0, 0, 0))
    out_sds = jax.ShapeDtypeStruct((b, tm + t, d), BF16)
    return pl.pallas_call(
        functools.partial(_mix_kernel, n_lat_tiles=n_lat),
        out_shape=[out_sds] * 6,
        grid=(n_lat + 1, b),
        in_specs=[pl.BlockSpec((1, tm, d), lambda s, bi: (bi, 0, 0)),
                  pl.BlockSpec((1, tm, d), lat),
                  pl.BlockSpec((1, 8, d), prev8),
                  pl.BlockSpec((1, 8, d), next8),
                  pl.BlockSpec((tm, d), lambda s, bi: (jnp.maximum(s - 1, 0), 0)),
                  pl.BlockSpec((8, d), lambda s, bi: (jnp.maximum((s - 1) * r8 - 1, 0), 0)),
                  pl.BlockSpec((8, d), lambda s, bi: (jnp.minimum(jnp.maximum(s, 1) * r8, t // 8 - 1), 0)),
                  vec, vec, cvec, cvec,
                  pl.BlockSpec((6, d), lambda s, bi: (0, 0))],
        out_specs=[pl.BlockSpec((1, tm, d), lambda s, bi: (bi, s, 0))] * 6,
        compiler_params=_params(("parallel", "parallel")),
        name="rwkv_mix",
    )(ctx, x, x, x, pos, pos, pos, sc, sh, csc, csh, mu)


def _block_diag(x, bmask):
    xb = x.astype(BF16)
    zero = jnp.zeros((HEAD, GROUP // 2), BF16)
    rows = []
    for h in range(GROUP // HEAD):
        t = h // 2
        blk = xb[:, t * 128:(t + 1) * 128] * bmask[h * HEAD:(h + 1) * HEAD, t * 128:(t + 1) * 128]
        rows.append(jnp.concatenate([blk, zero] if t == 0 else [zero, blk], axis=1))
    return jnp.concatenate(rows, axis=0)


def _fold_heads(full):
    lane = lax.broadcasted_iota(jnp.int32, (HEAD, 128), 1)
    tiles = []
    for t in range(GROUP // 128):
        even = full[(2 * t) * HEAD:(2 * t + 1) * HEAD, t * 128:(t + 1) * 128]
        odd = full[(2 * t + 1) * HEAD:(2 * t + 2) * HEAD, t * 128:(t + 1) * 128]
        tiles.append(jnp.where(lane < HEAD, even, odd))
    return jnp.concatenate(tiles, axis=1)


SUB = 16


def _block_diag16(x, bmask16):
    xb = x.astype(BF16)
    zero = jnp.zeros((SUB, GROUP // 2), BF16)
    rows = []
    for b in range(GROUP // SUB):
        t = b // (128 // SUB)
        blk = xb[:, t * 128:(t + 1) * 128] * bmask16[b * SUB:(b + 1) * SUB, t * 128:(t + 1) * 128]
        rows.append(jnp.concatenate([blk, zero] if t == 0 else [zero, blk], axis=1))
    return jnp.concatenate(rows, axis=0)


def _dot3_bd16(a, b, bmask16):
    rows = a.shape[0]
    a_hi, a_lo = _split2(a)
    b_hi, b_lo = _split2(b)
    main = _dot(jnp.concatenate([a_hi, a_lo], axis=0), _block_diag16(b_hi, bmask16))
    return main[:rows] + main[rows:] + _dot(a_hi, _block_diag16(b_lo, bmask16))


def _unit_triangular_inverse(l_mats, bmask, bmask16, eye16, diag16, off_a, off_b):
    nq = CHUNK // SUB
    l16 = [sum(l[q * SUB:(q + 1) * SUB] * diag16[q * SUB:(q + 1) * SUB] for q in range(nq)).astype(BF16)
           for l in l_mats]
    t16 = [eye16 + x.astype(F32) for x in l16]
    l_pow = [_dot(x, _block_diag16(x, bmask16)) for x in l16]
    for _ in range(2):
        both = [_dot3_bd16(jnp.concatenate([t, lp], axis=0), lp, bmask16) for t, lp in zip(t16, l_pow)]
        t16 = [t + bo[:SUB] for t, bo in zip(t16, both)]
        l_pow = [bo[SUB:] for bo in both]
    t16 = [t + _dot3_bd16(t, lp, bmask16) for t, lp in zip(t16, l_pow)]
    d = [jnp.concatenate([t] * nq, axis=0) * diag16 for t in t16]
    for off in (off_a, off_b):
        x = [_dot(di.astype(BF16), _block_diag(l * off, bmask)) for di, l in zip(d, l_mats)]
        d = [di + _dot(xi.astype(BF16), _block_diag(di, bmask)) for di, xi in zip(d, x)]
    return d


PAIR = 2


def _scan_kernel(r_ref, k_ref, v_ref, zw_ref, za_ref, kk_ref, ka_ref,
                 bmask_ref, bmask16_ref, tri_ref, dm_ref, cm_ref, eye16_ref,
                 y_ref, s_ref, p_ref, rh_ref, q_ref, yl_ref):
    ng = r_ref.shape[2] // GROUP
    bmask = bmask_ref[...]
    di = pl.program_id(0)

    @pl.when(pl.program_id(2) == 0)
    def _():
        s_ref[...] = jnp.zeros_like(s_ref)
        p_ref[...] = jnp.zeros_like(p_ref)
        rh_ref[...] = jnp.zeros_like(rh_ref)
        q_ref[...] = jnp.zeros_like(q_ref)
        yl_ref[...] = jnp.zeros_like(yl_ref)

    for step in range(PAIR):
        h = jnp.where(di == 0, step, PAIR - 1 - step)
        row0 = pl.multiple_of(h * CHUNK, CHUNK)
        for g in range(ng):
            cols = slice(g * GROUP, (g + 1) * GROUP)
            s_bd = _block_diag(s_ref[:, cols], bmask)
            out = _dot(jnp.concatenate([p_ref[h, :, cols], rh_ref[h, :, cols]], axis=0), s_bd)
            s_ref[:, cols] = out[:CHUNK] + q_ref[h, :, cols]
            y_ref[0, 0, pl.ds(row0, CHUNK), cols] = (out[CHUNK:] + yl_ref[h, :, cols]).astype(y_ref.dtype)

    eye = cm_ref[0]
    m_strict = dm_ref[0, 0]
    m_incl = dm_ref[0, 1]
    chains = [(h, g) for h in range(PAIR) for g in range(ng)]
    pieces = lambda x: [x[h * CHUNK:(h + 1) * CHUNK, g * GROUP:(g + 1) * GROUP] for h, g in chains]
    halves = lambda f, x: jnp.concatenate([f(x[h * CHUNK:(h + 1) * CHUNK]) for h in range(PAIR)], axis=0)
    bd = lambda x: _block_diag(x, bmask)
    stack = lambda x, y: jnp.concatenate([x, y], axis=0)

    r = r_ref[0].astype(F32)
    k = k_ref[0].astype(F32)
    v = v_ref[0].astype(F32)
    lw = (-math.exp(-0.5)) * jax.nn.sigmoid(zw_ref[0].astype(F32))
    a = jax.nn.sigmoid(za_ref[0].astype(F32))
    kkr = k * kk_ref[...]
    sq = jnp.concatenate(pieces(kkr * kkr), axis=0)
    sq_hi, sq_lo = _split2(sq)
    ssq = _dot(sq_hi, bmask) + _dot(sq_lo, bmask)
    ssq = jnp.concatenate(
        [jnp.concatenate([ssq[(h * ng + g) * CHUNK:(h * ng + g + 1) * CHUNK] for g in range(ng)], axis=1)
         for h in range(PAIR)], axis=0)
    kk = kkr * lax.rsqrt(jnp.maximum(ssq, 1e-24))
    kd = k * (1.0 + (a - 1.0) * ka_ref[...])
    bb = kk * a
    tri = tri_ref[0]

    def cumulative(x):
        x_hi, x_lo = _split2(x)
        return _dot(tri, x_hi) + _dot(tri, x_lo)

    g_cum = halves(cumulative, lw)
    g_end = halves(lambda x: jnp.broadcast_to(jnp.sum(x, axis=0, keepdims=True), x.shape), lw)
    e_neg = jnp.exp(-g_cum)
    e_end = jnp.exp(g_end - g_cum)
    a_t = pieces(-kk * jnp.exp(g_cum - lw))
    r_t = pieces(r * jnp.exp(g_cum))
    b_t = pieces(bb * e_neg)
    k_t = pieces(kd * e_neg)
    b_h = pieces((bb * e_end).astype(BF16))
    k_h = pieces((kd * e_end).astype(BF16))
    v_g = pieces(v)
    decay_end = [x[0:1] for x in pieces(jnp.exp(g_end))]

    ar = [stack(x, y).astype(BF16) for x, y in zip(a_t, r_t)]
    mb = [_dot_nt(x, bd(y)) for x, y in zip(ar, b_t)]
    mk = [_dot_nt(x, bd(y)) for x, y in zip(ar, k_t)]
    l_mat = [x[:CHUNK] * m_strict for x in mb]
    m_rb = [(x[CHUNK:] * m_incl).astype(BF16) for x in mb]
    m_k = [stack(x[:CHUNK] * m_strict, x[CHUNK:] * m_incl).astype(BF16) for x in mk]
    mv = [_dot(x, bd(y)) for x, y in zip(m_k, v_g)]
    t_mat = _unit_triangular_inverse(l_mat, bmask, bmask16_ref[...], eye16_ref[...],
                                     cm_ref[1], cm_ref[2], cm_ref[3])
    t_g = [stack(t, _dot(m, bd(t))).astype(BF16) for t, m in zip(t_mat, m_rb)]
    ta = [_dot(x, bd(y)) for x, y in zip(t_g, a_t)]
    tu = [_dot(x, bd(y[:CHUNK])) for x, y in zip(t_g, mv)]
    a_h = [x[:CHUNK].astype(BF16) for x in ta]
    u_0 = [x[:CHUNK] for x in tu]
    rh = [x + y[CHUNK:] for x, y in zip(r_t, ta)]
    yl = [x[CHUNK:] + y[CHUNK:] for x, y in zip(tu, mv)]
    p_full = [_dot_tn(x, y) for x, y in zip(b_h, a_h)]
    q_full = [_dot_tn(stack(x, y), stack(u, w).astype(BF16)) for x, y, u, w in zip(b_h, k_h, u_0, v_g)]
    for i, (h, g) in enumerate(chains):
        cols = slice(g * GROUP, (g + 1) * GROUP)
        p_ref[h, :, cols] = (_fold_heads(p_full[i]) + eye * decay_end[i]).astype(p_ref.dtype)
        rh_ref[h, :, cols] = rh[i].astype(rh_ref.dtype)
        q_ref[h, :, cols] = _fold_heads(q_full[i])
        yl_ref[h, :, cols] = yl[i]


def _scan_consts():
    lane = np.arange(GROUP)
    bmask = (lane[:, None] // HEAD == lane[None, :] // HEAD).astype(np.float32)
    i = np.arange(CHUNK)[:, None]
    j = np.arange(CHUNK)[None, :]
    jl = (lane % HEAD)[None, :]
    tri = np.stack([(j <= i), (j >= i)]).astype(np.float32)
    dir_masks = np.stack([np.stack([(jl < i), (jl <= i)]),
                          np.stack([(jl > i), (jl >= i)])]).astype(np.float32)
    same16 = (jl // SUB == i // SUB)
    same32 = (jl // (2 * SUB) == i // (2 * SUB))
    common = np.stack([(jl == i), same16, same32 & ~same16, ~same32]).astype(np.float32)
    bmask16 = (lane[:, None] // SUB == lane[None, :] // SUB).astype(np.float32)
    eye16 = ((lane % SUB)[None, :] == np.arange(SUB)[:, None]).astype(np.float32)
    return (jnp.asarray(bmask, BF16), jnp.asarray(bmask16, BF16), jnp.asarray(tri, BF16),
            jnp.asarray(dir_masks, F32), jnp.asarray(common, F32), jnp.asarray(eye16, F32))


def _scan(r, k, v, zw, za, k_k, k_a, n_ctx_chunks):
    b, l, d = r.shape
    rows = PAIR * CHUNK
    npair = l // rows
    assert l % rows == 0 and n_ctx_chunks % PAIR == 0
    n_ctx = n_ctx_chunks // PAIR
    bmask, bmask16, tri, dir_masks, common, eye16 = _scan_consts()

    def pair_of(di, s):
        back = jnp.where(s < n_ctx, n_ctx - 1 - s, npair - 1 + n_ctx - s)
        return jnp.where(di == 0, s, back)

    fold_pair = lambda di, s: pair_of(di, jnp.minimum(s, npair - 1))
    apply_pair = lambda di, s: pair_of(di, jnp.maximum(s - 1, 0))
    tok = pl.BlockSpec((1, rows, d), lambda di, bi, s: (bi, fold_pair(di, s), 0))
    tok2 = pl.BlockSpec((1, rows, d), lambda di, bi, s: (bi, fold_pair(di, s), di))
    par = pl.BlockSpec((1, d), lambda di, bi, s: (0, 0))
    return pl.pallas_call(
        _scan_kernel,
        out_shape=jax.ShapeDtypeStruct((2, b, l, d), BF16),
        grid=(2, b, npair + 1),
        in_specs=[tok, tok, tok, tok2, tok2, par, par,
                  pl.BlockSpec((GROUP, GROUP), lambda di, bi, s: (0, 0)),
                  pl.BlockSpec((GROUP, GROUP), lambda di, bi, s: (0, 0)),
                  pl.BlockSpec((1, CHUNK, CHUNK), lambda di, bi, s: (di, 0, 0)),
                  pl.BlockSpec((1, 2, CHUNK, GROUP), lambda di, bi, s: (di, 0, 0, 0)),
                  pl.BlockSpec((4, CHUNK, GROUP), lambda di, bi, s: (0, 0, 0)),
                  pl.BlockSpec((SUB, GROUP), lambda di, bi, s: (0, 0))],
        out_specs=pl.BlockSpec((1, 1, rows, d), lambda di, bi, s: (di, bi, apply_pair(di, s), 0)),
        scratch_shapes=[pltpu.VMEM((CHUNK, d), F32), pltpu.VMEM((PAIR, CHUNK, d), BF16),
                        pltpu.VMEM((PAIR, CHUNK, d), BF16), pltpu.VMEM((PAIR, CHUNK, d), F32),
                        pltpu.VMEM((PAIR, CHUNK, d), F32)],
        compiler_params=_params(("parallel", "parallel", "arbitrary")),
        name="scan",
    )(r, k, v, zw, za, k_k.reshape(1, d), k_a.reshape(1, d), bmask, bmask16, tri, dir_masks, common, eye16)


def _head_sum(x, ones_bd):
    hi, lo = _split2(x)
    return _dot(hi, ones_bd) + _dot(lo, ones_bd)


def _rwkv_post_kernel(y_ref, r_ref, k_ref, v_ref, za0_ref, za1_ref, gate_ref,
                      ka_ref, rk_ref, gnw_ref, gnb_ref, ones_ref, o_ref):
    ones_bd = ones_ref[...]
    tm, d = o_ref.shape[1], o_ref.shape[2]
    ng = d // GROUP
    to_rows = lambda x: jnp.concatenate([x[:, g * GROUP:(g + 1) * GROUP] for g in range(ng)], axis=0)
    to_cols = lambda x: jnp.concatenate([x[g * tm:(g + 1) * tm] for g in range(ng)], axis=1)
    head_mean = lambda x: to_cols(_head_sum(to_rows(x), ones_bd)) * (1.0 / HEAD)
    y = y_ref[0, 0].astype(F32) + y_ref[1, 0].astype(F32)
    yc = y - head_mean(y)
    var = head_mean(yc * yc)
    o = yc * lax.rsqrt(var + GN_EPS) * gnw_ref[...] + gnb_ref[...]
    a_sum = jax.nn.sigmoid(za0_ref[0].astype(F32)) + jax.nn.sigmoid(za1_ref[0].astype(F32))
    r = r_ref[0].astype(F32)
    k_sum = k_ref[0].astype(F32) * (2.0 + (a_sum - 2.0) * ka_ref[...])
    bonus = head_mean(r * k_sum * rk_ref[...]) * float(HEAD) * v_ref[0].astype(F32)
    o_ref[0] = ((o + bonus) * gate_ref[0].astype(F32)).astype(o_ref.dtype)


def _rwkv_post(y, r, k, v, za, gate, k_a, r_k, gn_w, gn_b, n_ctx, tm=256):
    _, b, l, d = y.shape
    t = l - n_ctx
    off = n_ctx // tm
    ones_bd = _scan_consts()[0]
    tok = pl.BlockSpec((1, tm, d), lambda bi, s: (bi, s + off, 0))
    par = pl.BlockSpec((1, d), lambda bi, s: (0, 0))
    return pl.pallas_call(
        _rwkv_post_kernel,
        out_shape=jax.ShapeDtypeStruct((b, t, d), BF16),
        grid=(b, t // tm),
        in_specs=[pl.BlockSpec((2, 1, tm, d), lambda bi, s: (0, bi, s + off, 0)),
                  tok, tok, tok,
                  pl.BlockSpec((1, tm, d), lambda bi, s: (bi, s + off, 0)),
                  pl.BlockSpec((1, tm, d), lambda bi, s: (bi, s + off, 1)),
                  tok, par, par, par, par,
                  pl.BlockSpec((GROUP, GROUP), lambda bi, s: (0, 0))],
        out_specs=pl.BlockSpec((1, tm, d), lambda bi, s: (bi, s, 0)),
        compiler_params=_params(("parallel", "parallel")),
        name="rwkv_post",
    )(y, r, k, v, za, za, gate, k_a.reshape(1, d), r_k.reshape(1, d), gn_w.reshape(1, d),
      gn_b.reshape(1, d), ones_bd)


def _proj_norm_kernel(a_ref, w_ref, x_ref, pos_ref, g_ref, lnw_ref, lnb_ref, sc_ref, sh_ref,
                      o_ref, h_ref, acc_ref, *, alpha):
    j = pl.program_id(2)
    nj = acc_ref.shape[0]
    tn = acc_ref.shape[2]
    acc_ref[j] = _dot(a_ref[...], w_ref[...].astype(BF16))

    @pl.when(j == nj - 1)
    def _():
        gate = g_ref[0]
        for jj in range(nj):
            cols = slice(jj * tn, (jj + 1) * tn)
            o_ref[:, cols] = alpha * (x_ref[:, cols] + pos_ref[:, cols]) + gate[:, cols] * acc_ref[jj]
        x_new = _layer_norm(o_ref[...], lnw_ref[...], lnb_ref[...])
        o_ref[...] = x_new
        h_ref[...] = (x_new * (1.0 + sc_ref[0]) + sh_ref[0]).astype(h_ref.dtype)


def _proj_norm(a, w, x, pos, gate, ln_w, ln_b, sc, sh, alpha, tm=512, tn=512):
    m, k = a.shape
    d = w.shape[1]
    t = pos.shape[0]
    tm = _tile(t, tm)
    tn = _tile(d, tn)
    tpb = t // tm
    nb = m // t
    rows = lambda ti, bi, j: (bi * tpb + ti, 0)
    vec = pl.BlockSpec((1, 1, d), lambda ti, bi, j: (bi, 0, 0))
    return pl.pallas_call(
        functools.partial(_proj_norm_kernel, alpha=alpha),
        out_shape=[jax.ShapeDtypeStruct((m, d), F32), jax.ShapeDtypeStruct((m, d), BF16)],
        grid=(tpb, nb, d // tn),
        in_specs=[pl.BlockSpec((tm, k), rows),
                  pl.BlockSpec((k, tn), lambda ti, bi, j: (0, j)),
                  pl.BlockSpec((tm, d), rows),
                  pl.BlockSpec((tm, d), lambda ti, bi, j: (ti, 0)),
                  vec,
                  pl.BlockSpec((1, d), lambda ti, bi, j: (0, 0)),
                  pl.BlockSpec((1, d), lambda ti, bi, j: (0, 0)),
                  vec, vec],
        out_specs=[pl.BlockSpec((tm, d), rows)] * 2,
        scratch_shapes=[pltpu.VMEM((d // tn, tm, tn), F32)],
        compiler_params=_params(("parallel", "parallel", "arbitrary")),
        name="proj_norm",
    )(a, w, x, pos, gate, ln_w.reshape(1, d), ln_b.reshape(1, d), sc, sh)


def _swiglu_halves(h, w1_ref, w3_ref, w2_ref, lead):
    tf = w1_ref.shape[-1]
    halves = [slice(0, tf // 2), slice(tf // 2, tf)]
    a1 = [_dot(h, w1_ref[lead + (slice(None), c)].astype(BF16)) for c in halves]
    a3 = [_dot(h, w3_ref[lead + (slice(None), c)].astype(BF16)) for c in halves]
    u = [(x * jax.nn.sigmoid(x) * y).astype(BF16) for x, y in zip(a1, a3)]
    y = [_dot(x, w2_ref[lead + (c, slice(None))].astype(BF16)) for x, c in zip(u, halves)]
    return y[0] + y[1]


def _ffn_kernel(h_ref, w1_ref, w3_ref, w2_ref, o_ref, acc_ref):
    j = pl.program_id(1)

    @pl.when(j == 0)
    def _():
        acc_ref[...] = jnp.zeros_like(acc_ref)

    acc_ref[...] += _swiglu_halves(h_ref[...], w1_ref, w3_ref, w2_ref, ())

    @pl.when(j == pl.num_programs(1) - 1)
    def _():
        o_ref[...] = acc_ref[...].astype(o_ref.dtype)


def _ffn(h, w1, w3, w2, tm=1024, tf=512):
    m, d = h.shape
    f = w1.shape[1]
    tm = _tile(m, tm)
    tf = _tile(f, tf)
    return pl.pallas_call(
        _ffn_kernel,
        out_shape=jax.ShapeDtypeStruct((m, d), BF16),
        grid=(m // tm, f // tf),
        in_specs=[pl.BlockSpec((tm, d), lambda i, j: (i, 0)),
                  pl.BlockSpec((d, tf), lambda i, j: (0, j)),
                  pl.BlockSpec((d, tf), lambda i, j: (0, j)),
                  pl.BlockSpec((tf, d), lambda i, j: (j, 0))],
        out_specs=pl.BlockSpec((tm, d), lambda i, j: (i, 0)),
        scratch_shapes=[pltpu.VMEM((tm, d), F32)],
        compiler_params=_params(("parallel", "arbitrary")),
        name="ffn",
    )(h, w1, w3, w2)


def _residual_norm_kernel(x_ref, y_ref, g_ref, lnw_ref, lnb_ref, o_ref, *, alpha):
    z = alpha * x_ref[...] + g_ref[0] * y_ref[...].astype(F32)
    o_ref[...] = _layer_norm(z, lnw_ref[...], lnb_ref[...])


def _residual_norm(x, y, gate, ln_w, ln_b, alpha, rows_per_batch, tm=512):
    m, d = x.shape
    tm = _tile(rows_per_batch, tm)
    tpb = rows_per_batch // tm
    tok = pl.BlockSpec((tm, d), lambda i: (i, 0))
    row = pl.BlockSpec((1, d), lambda i: (0, 0))
    return pl.pallas_call(
        functools.partial(_residual_norm_kernel, alpha=alpha),
        out_shape=jax.ShapeDtypeStruct((m, d), F32),
        grid=(m // tm,),
        in_specs=[tok, tok, pl.BlockSpec((1, 1, d), lambda i: (i // tpb, 0, 0)), row, row],
        out_specs=tok,
        compiler_params=_params(("parallel",)),
        name="residual_norm",
    )(x, y, gate, ln_w.reshape(1, d), ln_b.reshape(1, d))


def _shift_down(x, s, row):
    return jnp.where(row >= s, pltpu.roll(x, s, axis=0), 0.0)


def _shift_up(x, s, row):
    t = x.shape[0]
    return jnp.where(row < t - s, pltpu.roll(x, t - s, axis=0), 0.0)


def _pool_kernel(x_ref, sc_ref, sh_ref, w_ref, scale_ref, o_ref):
    g = pl.program_id(0)
    t = x_ref.shape[1]
    h = x_ref[0] * (1.0 + sc_ref[0]) + sh_ref[0]
    row = lax.broadcasted_iota(jnp.int32, h.shape, 0)
    w = w_ref[0].astype(BF16)
    for gi, win in enumerate(POOL_WINDOWS):
        @pl.when(g == gi)
        def _(win=win):
            half = win // 2
            back = h
            fwd = h
            m = 1
            while m < half:
                back = back + _shift_down(back, m, row)
                fwd = fwd + _shift_up(fwd, m, row)
                m *= 2
            total = _shift_down(back, 1, row) + fwd
            count = (jnp.minimum(row + half, t) - jnp.maximum(row - half, 0)).astype(F32)
            pooled = (total / count - h).astype(BF16)
            o_ref[0] = _dot(pooled, w) * scale_ref[...]


def _pool(x, sc, sh, w_pool, scale):
    b, t, d = x.shape
    ng, p, _ = w_pool.shape
    vec = pl.BlockSpec((1, 1, p), lambda g, bi: (bi, 0, g))
    return pl.pallas_call(
        _pool_kernel,
        out_shape=jax.ShapeDtypeStruct((b, t, d), F32),
        grid=(ng, b),
        in_specs=[pl.BlockSpec((1, t, p), lambda g, bi: (bi, 0, g)), vec, vec,
                  pl.BlockSpec((1, p, p), lambda g, bi: (g, 0, 0)),
                  pl.BlockSpec((1, p), lambda g, bi: (0, g))],
        out_specs=pl.BlockSpec((1, t, p), lambda g, bi: (bi, 0, g)),
        compiler_params=_params(("parallel", "parallel")),
        name="pool",
    )(x, sc, sh, w_pool, scale.reshape(1, d))


def _route_kernel(x_ref, y_ref, g_ref, lnw_ref, lnb_ref, sc_ref, sh_ref, rt_ref,
                  xo_ref, h_ref, route_ref, *, alpha):
    x = _layer_norm(alpha * x_ref[...] + g_ref[0] * y_ref[...], lnw_ref[...], lnb_ref[...])
    xo_ref[...] = x
    h = x * (1.0 + sc_ref[0]) + sh_ref[0]
    h_ref[...] = h
    r1, r2, r3 = _split3(rt_ref[...])
    h1, h2, h3 = _split3(h)
    logits = (_dot_nt(r1, h1) + (_dot_nt(r1, h2) + _dot_nt(r2, h1))
              + (_dot_nt(r1, h3) + _dot_nt(r2, h2) + _dot_nt(r3, h1)))
    mx = jnp.max(logits, axis=0, keepdims=True)
    e = jnp.exp(logits - mx)
    p = e / jnp.sum(e, axis=0, keepdims=True)
    idx = lax.broadcasted_iota(jnp.int32, p.shape, 0)
    p1 = jnp.max(p, axis=0, keepdims=True)
    i1 = jnp.min(jnp.where(p == p1, idx, N_EXPERTS), axis=0, keepdims=True)
    rest = jnp.where(idx == i1, -1.0, p)
    p2 = jnp.max(rest, axis=0, keepdims=True)
    i2 = jnp.min(jnp.where(rest == p2, idx, N_EXPERTS), axis=0, keepdims=True)
    den = p1 + p2
    out = jnp.where(idx == 0, i1.astype(F32), 0.0)
    out = jnp.where(idx == 1, i2.astype(F32), out)
    out = jnp.where(idx == 2, p1 / den, out)
    out = jnp.where(idx == 3, p2 / den, out)
    route_ref[...] = out


def _route(x, y, gate, ln_w, ln_b, sc, sh, router, alpha, rows_per_batch, tm=256):
    m, d = x.shape
    tpb = rows_per_batch // tm
    vec = pl.BlockSpec((1, 1, d), lambda i: (i // tpb, 0, 0))
    row = pl.BlockSpec((1, d), lambda i: (0, 0))
    tok = pl.BlockSpec((tm, d), lambda i: (i, 0))
    return pl.pallas_call(
        functools.partial(_route_kernel, alpha=alpha),
        out_shape=[jax.ShapeDtypeStruct((m, d), F32), jax.ShapeDtypeStruct((m, d), F32),
                   jax.ShapeDtypeStruct((N_EXPERTS, m), F32)],
        grid=(m // tm,),
        in_specs=[tok, tok, vec, row, row, vec, vec,
                  pl.BlockSpec((N_EXPERTS, d), lambda i: (0, 0))],
        out_specs=[tok, tok, pl.BlockSpec((N_EXPERTS, tm), lambda i: (0, i))],
        compiler_params=_params(("parallel",)),
        name="route",
    )(x, y, gate, ln_w.reshape(1, d), ln_b.reshape(1, d), sc, sh, router.T)


def _row_copy(src_hbm, dst_vmem, sem, src_row, dst_row):
    return pltpu.make_async_copy(src_hbm.at[pl.ds(src_row, 1)], dst_vmem.at[pl.ds(dst_row, 1)], sem)


def _moe_kernel(ce_ref, nv_ref, idx_ref, h_ref, w1_ref, w3_ref, w2_ref, o_ref,
                stage_ref, work_ref, gather_sem, *, share):
    c = pl.program_id(0)
    j = pl.program_id(1)
    n_chunks = pl.num_programs(0)
    nj = pl.num_programs(1)
    nv = nv_ref[c]
    full = work_ref.shape[0]
    stage_rows = stage_ref.shape[0]

    def gather(chunk, r):
        return _row_copy(h_ref, stage_ref, gather_sem, idx_ref[chunk * full + r], r)

    def wait_all_gathers():
        def body(r, carry):
            gather(0, r).wait()
            return carry
        lax.fori_loop(0, stage_rows, body, 0, unroll=8)

    @pl.when(j == 0)
    def _():
        @pl.when(c == 0)
        def _():
            def body(r, carry):
                gather(0, r).start()
                return carry
            lax.fori_loop(0, stage_rows, body, 0, unroll=8)

        o_ref[...] = jnp.zeros_like(o_ref)
        wait_all_gathers()
        work_ref[...] = stage_ref[0:full, :].astype(work_ref.dtype)

    def request_next_share():
        for i in range(share):
            gather(c + 1, j * share + i).start(priority=1)

    def swiglu_rows(rows):
        o_ref[rows, :] += _swiglu_halves(work_ref[rows, :], w1_ref, w3_ref, w2_ref, (0,))

    n_sub = (nv + (MOE_SUB - 1)) // MOE_SUB
    for k in range(full // MOE_SUB + 1):
        @pl.when(n_sub == k)
        def _(k=k):
            request_next_share()
            if k:
                swiglu_rows(slice(0, k * MOE_SUB))

    @pl.when(jnp.logical_and(j == nj - 1, c == n_chunks - 1))
    def _():
        wait_all_gathers()


def _moe_ffn(h, src_idx, chunk_expert, chunk_valid, w1, w3, w2, n_chunks, tf=512):
    d = h.shape[1]
    f = w1.shape[2]
    tf = _tile(f, tf)
    nj = f // tf
    share = -(-MOE_CHUNK // nj)
    share = -(-share // 8) * 8
    stage_rows = share * nj
    assert src_idx.shape[0] >= n_chunks * MOE_CHUNK + stage_rows

    def jeff(c, j, nv):
        return jnp.where(nv[c] > 0, j, nj - 1)

    return pl.pallas_call(
        functools.partial(_moe_kernel, share=share),
        out_shape=jax.ShapeDtypeStruct((n_chunks * MOE_CHUNK, d), F32),
        grid_spec=pltpu.PrefetchScalarGridSpec(
            num_scalar_prefetch=3,
            grid=(n_chunks, nj),
            in_specs=[pl.BlockSpec(memory_space=pl.ANY),
                      pl.BlockSpec((1, d, tf), lambda c, j, ce, nv, ix: (ce[c], 0, jeff(c, j, nv))),
                      pl.BlockSpec((1, d, tf), lambda c, j, ce, nv, ix: (ce[c], 0, jeff(c, j, nv))),
                      pl.BlockSpec((1, tf, d), lambda c, j, ce, nv, ix: (ce[c], jeff(c, j, nv), 0))],
            out_specs=pl.BlockSpec((MOE_CHUNK, d), lambda c, j, ce, nv, ix: (c, 0)),
            scratch_shapes=[pltpu.VMEM((stage_rows, d), F32), pltpu.VMEM((MOE_CHUNK, d), BF16),
                            pltpu.SemaphoreType.DMA]),
        compiler_params=_params(("arbitrary", "arbitrary"), MOE_VMEM_LIMIT_BYTES),
        name="moe_ffn",
    )(chunk_expert, chunk_valid, src_idx, h, w1, w3, w2)


def _combine_kernel(p0_ref, p1_ref, ys_ref, x_ref, gates_ref, g_ref, lnw_ref, lnb_ref,
                    o_ref, b0_ref, b1_ref, sem, *, alpha):
    i = pl.program_id(0)
    n = pl.num_programs(0)
    rows = b0_ref.shape[1]
    slot = i % 2

    def copies(tile, to_slot, r):
        return (_row_copy(ys_ref, b0_ref.at[to_slot], sem.at[to_slot], p0_ref[tile * rows + r], r),
                _row_copy(ys_ref, b1_ref.at[to_slot], sem.at[to_slot], p1_ref[tile * rows + r], r))

    def wait_slot(to_slot):
        def body(r, carry):
            for cp in copies(0, to_slot, r):
                cp.wait()
            return carry
        lax.fori_loop(0, rows, body, 0, unroll=4)

    @pl.when(i == 0)
    def _():
        def body(r, carry):
            for cp in copies(0, 0, r):
                cp.start()
            return carry
        lax.fori_loop(0, rows, body, 0, unroll=4)

    wait_slot(slot)
    for r in range(rows):
        for cp in copies(i + 1, 1 - slot, r):
            cp.start()
    gates = gates_ref[...]
    y = gates[:, 0:1] * b0_ref[slot] + gates[:, 1:2] * b1_ref[slot]
    z = alpha * x_ref[...] + g_ref[0] * y
    o_ref[...] = _layer_norm(z, lnw_ref[...], lnb_ref[...])

    @pl.when(i == n - 1)
    def _():
        wait_slot(1 - slot)


def _combine(ys, pos0, pos1, gates, x, gate_vec, ln_w, ln_b, alpha, rows_per_batch, rows=256):
    m, d = x.shape
    tpb = rows_per_batch // rows
    tok = lambda i, a, b: (i, 0)
    return pl.pallas_call(
        functools.partial(_combine_kernel, alpha=alpha),
        out_shape=jax.ShapeDtypeStruct((m, d), F32),
        grid_spec=pltpu.PrefetchScalarGridSpec(
            num_scalar_prefetch=2,
            grid=(m // rows,),
            in_specs=[pl.BlockSpec(memory_space=pl.ANY),
                      pl.BlockSpec((rows, d), tok),
                      pl.BlockSpec((rows, 2), tok),
                      pl.BlockSpec((1, 1, d), lambda i, a, b: (i // tpb, 0, 0)),
                      pl.BlockSpec((1, d), lambda i, a, b: (0, 0)),
                      pl.BlockSpec((1, d), lambda i, a, b: (0, 0))],
            out_specs=pl.BlockSpec((rows, d), tok),
            scratch_shapes=[pltpu.VMEM((2, rows, d), F32), pltpu.VMEM((2, rows, d), F32),
                            pltpu.SemaphoreType.DMA((2,))]),
        compiler_params=_params(("arbitrary",)),
        name="moe_combine",
    )(pos0, pos1, ys, x, gates, gate_vec, ln_w.reshape(1, d), ln_b.reshape(1, d))


def _routing_tables(route, n_chunks, table_len):
    n = route.shape[1]
    experts = jnp.concatenate([route[0], route[1]]).astype(jnp.int32)
    onehot = (experts[:, None] == jnp.arange(N_EXPERTS, dtype=jnp.int32)[None, :]).astype(jnp.int32)
    csum = jnp.cumsum(onehot, axis=0)
    rank = jnp.sum((csum - 1) * onehot, axis=1)
    counts = csum[-1]
    chunks_e = (counts + MOE_CHUNK - 1) // MOE_CHUNK
    chunk_end = jnp.cumsum(chunks_e)
    chunk_start = chunk_end - chunks_e
    dest = (chunk_start * MOE_CHUNK)[experts] + rank
    token = jnp.arange(2 * n, dtype=jnp.int32) % n
    src_idx = jnp.zeros((table_len,), jnp.int32).at[dest].set(token)
    cid = jnp.arange(n_chunks, dtype=jnp.int32)
    used = cid < chunk_end[-1]
    last_used = jnp.maximum(chunk_end[-1] - 1, 0)
    ce = jnp.sum((jnp.minimum(cid, last_used)[:, None] >= chunk_end[None, :]).astype(jnp.int32), axis=1)
    ce = jnp.minimum(ce, N_EXPERTS - 1)
    nvalid = jnp.clip(counts[ce] - (cid - chunk_start[ce]) * MOE_CHUNK, 0, MOE_CHUNK)
    nvalid = jnp.where(used, nvalid, 0).astype(jnp.int32)
    return src_idx, ce, nvalid, dest[:n], dest[n:]


def _position_embedding(rows, width, d):
    quarter = d // 4
    omega = 1.0 / (POS_BASE ** (jnp.arange(quarter, dtype=F32) / quarter))
    ar = jnp.arange(rows, dtype=F32)[:, None] * omega[None, :]
    ac = jnp.arange(width, dtype=F32)[:, None] * omega[None, :]
    row_part = jnp.repeat(jnp.concatenate([jnp.sin(ar), jnp.cos(ar)], axis=-1), width, axis=0)
    col_part = jnp.tile(jnp.concatenate([jnp.sin(ac), jnp.cos(ac)], axis=-1), (rows, 1))
    return jnp.concatenate([row_part, col_part], axis=-1)


def _block_diag2(w):
    z = jnp.zeros_like(w[0])
    return jnp.concatenate([jnp.concatenate([w[0], z], axis=1), jnp.concatenate([z, w[1]], axis=1)], axis=0)


def kernel(x, c, ctx, c_ctx, w_mod, b_mod, ln_w, ln_b, rwkv_mu, rwkv_w_r, rwkv_w_k, rwkv_w_v, rwkv_w_o, rwkv_decay_w0, rwkv_decay_w1, rwkv_decay_w2, rwkv_iclr_a0, rwkv_iclr_a1, rwkv_iclr_a2, rwkv_gate_g1, rwkv_gate_g2, rwkv_k_k, rwkv_k_a, rwkv_r_k, rwkv_gn_w, rwkv_gn_b, pool_w, pool_scale, ffn_w1, ffn_w3, ffn_w2, moe_router, moe_w1, moe_w3, moe_w2):
    b, t, d = x.shape
    n_ctx = ctx.shape[1]
    depth = w_mod.shape[0]
    assert depth == 2 and rwkv_mu.shape[0] == 1 and pool_w.shape[0] == 1
    alpha = (2.0 * depth) ** 0.25
    grid_w = 64
    l = n_ctx + t
    n = b * t

    cond = jnp.zeros((8, d), F32).at[:b].set(c).at[b].set(c_ctx)
    mod = _adaln(cond, w_mod, b_mod).reshape(depth, 8, 6, d)
    lat = lambda layer, which: mod[layer, :b, which].reshape(b, 1, d)
    cvec = lambda layer, which: mod[layer, b, which].reshape(1, 1, d)

    pos = _position_embedding(t // grid_w, grid_w, d)

    xr, xw, xk, xv, xa, xg = _rwkv_mix(ctx, x, pos, lat(0, 1), lat(0, 0), cvec(0, 1), cvec(0, 0), rwkv_mu[0])
    flat = lambda a: a.reshape(b * l, a.shape[-1])
    r = _mm(flat(xr), rwkv_w_r[0], out_dtype=BF16)
    k = _mm(flat(xk), rwkv_w_k[0], out_dtype=BF16)
    v = _mm(flat(xv), rwkv_w_v[0], out_dtype=BF16)
    dw1 = jnp.concatenate([rwkv_decay_w1[0, 0], rwkv_decay_w1[0, 1]], axis=1)
    ia1 = jnp.concatenate([rwkv_iclr_a1[0, 0], rwkv_iclr_a1[0, 1]], axis=1)
    lora_w = _mm(flat(xw), dw1, act="tanh", out_dtype=BF16)
    zw = _mm(lora_w, _block_diag2(rwkv_decay_w2[0]), rwkv_decay_w0[0].reshape(1, 2 * d), out_dtype=BF16, tn=2048)
    lora_a = _mm(flat(xa), ia1, out_dtype=BF16)
    za = _mm(lora_a, _block_diag2(rwkv_iclr_a2[0]), rwkv_iclr_a0[0].reshape(1, 2 * d), out_dtype=BF16, tn=2048)
    lora_g = _mm(flat(xg), rwkv_gate_g1[0], act="sigmoid", out_dtype=BF16)
    gate = _mm(lora_g, rwkv_gate_g2[0], out_dtype=BF16, tn=2048)
    seq = lambda a: a.reshape(b, l, a.shape[-1])
    y_scan = _scan(seq(r), seq(k), seq(v), seq(zw), seq(za), rwkv_k_k[0], rwkv_k_a[0], n_ctx // CHUNK)
    og = _rwkv_post(y_scan, seq(r), seq(k), seq(v), seq(za), seq(gate), rwkv_k_a[0],
                    rwkv_r_k[0].reshape(d), rwkv_gn_w[0], rwkv_gn_b[0], n_ctx)
    x1, h1 = _proj_norm(og.reshape(n, d), rwkv_w_o[0].astype(BF16), x.reshape(n, d), pos, lat(0, 2),
                        ln_w[0, 0], ln_b[0, 0], lat(0, 4), lat(0, 3), alpha)
    y_ffn = _ffn(h1, ffn_w1[0], ffn_w3[0], ffn_w2[0])
    x2 = _residual_norm(x1, y_ffn, lat(0, 5), ln_w[0, 1], ln_b[0, 1], alpha, t)

    y_pool = _pool(x2.reshape(b, t, d), lat(1, 1), lat(1, 0), pool_w[0], pool_scale[0])
    x3, h3, route = _route(x2, y_pool.reshape(n, d), lat(1, 2), ln_w[1, 0], ln_b[1, 0],
                           lat(1, 4), lat(1, 3), moe_router[0], alpha, t)
    n_chunks = (2 * n + N_EXPERTS * (MOE_CHUNK - 1)) // MOE_CHUNK
    src_idx, chunk_expert, chunk_valid, pos0, pos1 = _routing_tables(route, n_chunks, (n_chunks + 2) * MOE_CHUNK)
    ys = _moe_ffn(h3, src_idx, chunk_expert, chunk_valid, moe_w1[0], moe_w3[0], moe_w2[0], n_chunks)
    gates = jnp.stack([route[2], route[3]], axis=1)
    spare = jnp.zeros((256,), jnp.int32)
    out = _combine(ys, jnp.concatenate([pos0, spare]), jnp.concatenate([pos1, spare]), gates, x3,
                   lat(1, 5), ln_w[1, 1], ln_b[1, 1], alpha, t)
    return out.reshape(b, t, d)
```

```python
import functools
import math

import numpy as np
import jax
import jax.numpy as jnp
from jax import lax
from jax.experimental import pallas as pl
from jax.experimental.pallas import tpu as pltpu

F32 = jnp.float32
BF16 = jnp.bfloat16

HEAD = 64
GROUP = 4 * HEAD
CHUNK = 64
LN_EPS = 1e-5
GN_EPS = 64e-5
POS_BASE = 10000.0
POOL_WINDOWS = (2, 4, 8, 16)
N_EXPERTS = 8
MOE_CHUNK = 1088
MOE_SUB = 256
VMEM_LIMIT_BYTES = 56 * 1024 * 1024
MOE_VMEM_LIMIT_BYTES = 62 * 1024 * 1024


def _params(semantics, vmem_limit_bytes=VMEM_LIMIT_BYTES):
    return pltpu.CompilerParams(dimension_semantics=semantics, vmem_limit_bytes=vmem_limit_bytes)


def _tile(n, preferred):
    t = min(preferred, n)
    while n % t:
        t //= 2
    return t


def _dot(a, b):
    return jnp.dot(a, b, preferred_element_type=F32)


def _dot_nt(a, b):
    return lax.dot_general(a, b, (((1,), (1,)), ((), ())), preferred_element_type=F32)


def _dot_tn(a, b):
    return lax.dot_general(a, b, (((0,), (0,)), ((), ())), preferred_element_type=F32)


def _split2(x):
    hi = x.astype(BF16)
    lo = (x - hi.astype(F32)).astype(BF16)
    return hi, lo


def _split3(x):
    hi = x.astype(BF16)
    r1 = x - hi.astype(F32)
    mid = r1.astype(BF16)
    lo = (r1 - mid.astype(F32)).astype(BF16)
    return hi, mid, lo


def _layer_norm(z, w, b):
    mu = jnp.mean(z, axis=-1, keepdims=True)
    zc = z - mu
    var = jnp.mean(zc * zc, axis=-1, keepdims=True)
    return zc * lax.rsqrt(var + LN_EPS) * w + b


def _adaln_kernel(c_ref, w_ref, b_ref, o_ref):
    c = c_ref[...]
    a = (c * jax.nn.sigmoid(c)).astype(BF16)
    o_ref[0] = _dot(a, w_ref[0].astype(BF16)) + b_ref[0]


def _adaln(cond, w_mod, b_mod, tn=1024):
    depth, d, n = w_mod.shape
    tn = _tile(n, tn)
    rows = cond.shape[0]
    return pl.pallas_call(
        _adaln_kernel,
        out_shape=jax.ShapeDtypeStruct((depth, rows, n), F32),
        grid=(depth, n // tn),
        in_specs=[pl.BlockSpec((rows, d), lambda l, j: (0, 0)),
                  pl.BlockSpec((1, d, tn), lambda l, j: (l, 0, j)),
                  pl.BlockSpec((1, 1, tn), lambda l, j: (l, 0, j))],
        out_specs=pl.BlockSpec((1, rows, tn), lambda l, j: (l, 0, j)),
        compiler_params=_params(("parallel", "parallel")),
        name="adaln",
    )(cond, w_mod, b_mod.reshape(depth, 1, n))


def _mm_kernel(a_ref, w_ref, b_ref, o_ref, *, act):
    acc = _dot(a_ref[...], w_ref[...].astype(BF16)) + b_ref[...]
    if act == "tanh":
        acc = jnp.tanh(acc)
    elif act == "sigmoid":
        acc = jax.nn.sigmoid(acc)
    o_ref[...] = acc.astype(o_ref.dtype)


def _mm(a, w, bias=None, *, act=None, out_dtype=F32, tm=2304, tn=512):
    m, k = a.shape
    n = w.shape[1]
    tm = _tile(m, tm)
    tn = _tile(n, tn)
    if bias is None:
        bias = jnp.zeros((1, n), F32)
    return pl.pallas_call(
        functools.partial(_mm_kernel, act=act),
        out_shape=jax.ShapeDtypeStruct((m, n), out_dtype),
        grid=(m // tm, n // tn),
        in_specs=[pl.BlockSpec((tm, k), lambda i, j: (i, 0)),
                  pl.BlockSpec((k, tn), lambda i, j: (0, j)),
                  pl.BlockSpec((1, tn), lambda i, j: (0, j))],
        out_specs=pl.BlockSpec((tm, tn), lambda i, j: (i, j)),
        compiler_params=_params(("parallel", "arbitrary")),
        name="matmul",
    )(a, w, bias.reshape(1, n))


def _mix_kernel(ctx_ref, x_ref, xp_ref, xn_ref, pos_ref, pp_ref, pn_ref,
                sc_ref, sh_ref, csc_ref, csh_ref, mu_ref,
                o0, o1, o2, o3, o4, o5, *, n_lat_tiles):
    s = pl.program_id(0)
    is_ctx = s == 0
    tm = x_ref.shape[1]
    scale = jnp.where(is_ctx, csc_ref[0], sc_ref[0]) + 1.0
    shift = jnp.where(is_ctx, csh_ref[0], sh_ref[0])
    src = jnp.where(is_ctx, ctx_ref[0], x_ref[0] + pos_ref[...])
    h = src * scale + shift
    has_prev = s > 1
    has_next = jnp.logical_and(s >= 1, s < n_lat_tiles)
    h_prev = jnp.where(has_prev, (xp_ref[0] + pp_ref[...]) * scale + shift, 0.0)[7:8]
    h_next = jnp.where(has_next, (xn_ref[0] + pn_ref[...]) * scale + shift, 0.0)[0:1]
    row = lax.broadcasted_iota(jnp.int32, h.shape, 0)
    h_m1 = jnp.where(row == 0, h_prev, pltpu.roll(h, 1, axis=0))
    h_p1 = jnp.where(row == tm - 1, h_next, pltpu.roll(h, tm - 1, axis=0))
    xx = 0.5 * (h_m1 + h_p1) - h
    for n, o_ref in enumerate((o0, o1, o2, o3, o4, o5)):
        o_ref[0] = (h + xx * mu_ref[n:n + 1]).astype(o_ref.dtype)


def _rwkv_mix(ctx, x, pos, sc, sh, csc, csh, mu):
    b, t, d = x.shape
    tm = ctx.shape[1]
    assert t % tm == 0 and tm % 8 == 0
    n_lat = t // tm
    r8 = tm // 8
    lat = lambda s, bi: (bi, jnp.maximum(s - 1, 0), 0)
    prev8 = lambda s, bi: (bi, jnp.maximum((s - 1) * r8 - 1, 0), 0)
    next8 = lambda s, bi: (bi, jnp.minimum(jnp.maximum(s, 1) * r8, t // 8 - 1), 0)
    vec = pl.BlockSpec((1, 1, d), lambda s, bi: (bi, 0, 0))
    cvec = pl.BlockSpec((1, 1, d), lambda s, bi: (0, 0, 0))
    out_sds = jax.ShapeDtypeStruct((b, tm + t, d), BF16)
    return pl.pallas_call(
        functools.partial(_mix_kernel, n_lat_tiles=n_lat),
        out_shape=[out_sds] * 6,
        grid=(n_lat + 1, b),
        in_specs=[pl.BlockSpec((1, tm, d), lambda s, bi: (bi, 0, 0)),
                  pl.BlockSpec((1, tm, d), lat),
                  pl.BlockSpec((1, 8, d), prev8),
                  pl.BlockSpec((1, 8, d), next8),
                  pl.BlockSpec((tm, d), lambda s, bi: (jnp.maximum(s - 1, 0), 0)),
                  pl.BlockSpec((8, d), lambda s, bi: (jnp.maximum((s - 1) * r8 - 1, 0), 0)),
                  pl.BlockSpec((8, d), lambda s, bi: (jnp.minimum(jnp.maximum(s, 1) * r8, t // 8 - 1), 0)),
                  vec, vec, cvec, cvec,
                  pl.BlockSpec((6, d), lambda s, bi: (0, 0))],
        out_specs=[pl.BlockSpec((1, tm, d), lambda s, bi: (bi, s, 0))] * 6,
        compiler_params=_params(("parallel", "parallel")),
        name="rwkv_mix",
    )(ctx, x, x, x, pos, pos, pos, sc, sh, csc, csh, mu)


def _block_diag(x, bmask):
    xb = x.astype(BF16)
    zero = jnp.zeros((HEAD, GROUP // 2), BF16)
    rows = []
    for h in range(GROUP // HEAD):
        t = h // 2
        blk = xb[:, t * 128:(t + 1) * 128] * bmask[h * HEAD:(h + 1) * HEAD, t * 128:(t + 1) * 128]
        rows.append(jnp.concatenate([blk, zero] if t == 0 else [zero, blk], axis=1))
    return jnp.concatenate(rows, axis=0)


def _fold_heads(full):
    lane = lax.broadcasted_iota(jnp.int32, (HEAD, 128), 1)
    tiles = []
    for t in range(GROUP // 128):
        even = full[(2 * t) * HEAD:(2 * t + 1) * HEAD, t * 128:(t + 1) * 128]
        odd = full[(2 * t + 1) * HEAD:(2 * t + 2) * HEAD, t * 128:(t + 1) * 128]
        tiles.append(jnp.where(lane < HEAD, even, odd))
    return jnp.concatenate(tiles, axis=1)


SUB = 16


def _block_diag16(x, bmask16):
    xb = x.astype(BF16)
    zero = jnp.zeros((SUB, GROUP // 2), BF16)
    rows = []
    for b in range(GROUP // SUB):
        t = b // (128 // SUB)
        blk = xb[:, t * 128:(t + 1) * 128] * bmask16[b * SUB:(b + 1) * SUB, t * 128:(t + 1) * 128]
        rows.append(jnp.concatenate([blk, zero] if t == 0 else [zero, blk], axis=1))
    return jnp.concatenate(rows, axis=0)


def _dot3_bd16(a, b, bmask16):
    rows = a.shape[0]
    a_hi, a_lo = _split2(a)
    b_hi, b_lo = _split2(b)
    main = _dot(jnp.concatenate([a_hi, a_lo], axis=0), _block_diag16(b_hi, bmask16))
    return main[:rows] + main[rows:] + _dot(a_hi, _block_diag16(b_lo, bmask16))


def _unit_triangular_inverse(l_mats, bmask, bmask16, eye16, diag16, off_a, off_b):
    nq = CHUNK // SUB
    l16 = [sum(l[q * SUB:(q + 1) * SUB] * diag16[q * SUB:(q + 1) * SUB] for q in range(nq)).astype(BF16)
           for l in l_mats]
    t16 = [eye16 + x.astype(F32) for x in l16]
    l_pow = [_dot(x, _block_diag16(x, bmask16)) for x in l16]
    for _ in range(2):
        both = [_dot3_bd16(jnp.concatenate([t, lp], axis=0), lp, bmask16) for t, lp in zip(t16, l_pow)]
        t16 = [t + bo[:SUB] for t, bo in zip(t16, both)]
        l_pow = [bo[SUB:] for bo in both]
    t16 = [t + _dot3_bd16(t, lp, bmask16) for t, lp in zip(t16, l_pow)]
    d = [jnp.concatenate([t] * nq, axis=0) * diag16 for t in t16]
    for off in (off_a, off_b):
        x = [_dot(di.astype(BF16), _block_diag(l * off, bmask)) for di, l in zip(d, l_mats)]
        d = [di + _dot(xi.astype(BF16), _block_diag(di, bmask)) for di, xi in zip(d, x)]
    return d


PAIR = 2


def _scan_kernel(r_ref, k_ref, v_ref, zw_ref, za_ref, kk_ref, ka_ref,
                 bmask_ref, bmask16_ref, tri_ref, dm_ref, cm_ref, eye16_ref,
                 y_ref, s_ref, p_ref, rh_ref, q_ref, yl_ref):
    ng = r_ref.shape[2] // GROUP
    bmask = bmask_ref[...]
    di = pl.program_id(0)

    @pl.when(pl.program_id(2) == 0)
    def _():
        s_ref[...] = jnp.zeros_like(s_ref)
        p_ref[...] = jnp.zeros_like(p_ref)
        rh_ref[...] = jnp.zeros_like(rh_ref)
        q_ref[...] = jnp.zeros_like(q_ref)
        yl_ref[...] = jnp.zeros_like(yl_ref)

    for step in range(PAIR):
        h = jnp.where(di == 0, step, PAIR - 1 - step)
        row0 = pl.multiple_of(h * CHUNK, CHUNK)
        for g in range(ng):
            cols = slice(g * GROUP, (g + 1) * GROUP)
            s_bd = _block_diag(s_ref[:, cols], bmask)
            out = _dot(jnp.concatenate([p_ref[h, :, cols], rh_ref[h, :, cols]], axis=0), s_bd)
            s_ref[:, cols] = out[:CHUNK] + q_ref[h, :, cols]
            y_ref[0, 0, pl.ds(row0, CHUNK), cols] = (out[CHUNK:] + yl_ref[h, :, cols]).astype(y_ref.dtype)

    eye = cm_ref[0]
    m_strict = dm_ref[0, 0]
    m_incl = dm_ref[0, 1]
    chains = [(h, g) for h in range(PAIR) for g in range(ng)]
    pieces = lambda x: [x[h * CHUNK:(h + 1) * CHUNK, g * GROUP:(g + 1) * GROUP] for h, g in chains]
    halves = lambda f, x: jnp.concatenate([f(x[h * CHUNK:(h + 1) * CHUNK]) for h in range(PAIR)], axis=0)
    bd = lambda x: _block_diag(x, bmask)
    stack = lambda x, y: jnp.concatenate([x, y], axis=0)

    r = r_ref[0].astype(F32)
    k = k_ref[0].astype(F32)
    v = v_ref[0].astype(F32)
    lw = (-math.exp(-0.5)) * jax.nn.sigmoid(zw_ref[0].astype(F32))
    a = jax.nn.sigmoid(za_ref[0].astype(F32))
    kkr = k * kk_ref[...]
    sq = jnp.concatenate(pieces(kkr * kkr), axis=0)
    sq_hi, sq_lo = _split2(sq)
    ssq = _dot(sq_hi, bmask) + _dot(sq_lo, bmask)
    ssq = jnp.concatenate(
        [jnp.concatenate([ssq[(h * ng + g) * CHUNK:(h * ng + g + 1) * CHUNK] for g in range(ng)], axis=1)
         for h in range(PAIR)], axis=0)
    kk = kkr * lax.rsqrt(jnp.maximum(ssq, 1e-24))
    kd = k * (1.0 + (a - 1.0) * ka_ref[...])
    bb = kk * a
    tri = tri_ref[0]

    def cumulative(x):
        x_hi, x_lo = _split2(x)
        return _dot(tri, x_hi) + _dot(tri, x_lo)

    g_cum = halves(cumulative, lw)
    g_end = halves(lambda x: jnp.broadcast_to(jnp.sum(x, axis=0, keepdims=True), x.shape), lw)
    e_neg = jnp.exp(-g_cum)
    e_end = jnp.exp(g_end - g_cum)
    a_t = pieces(-kk * jnp.exp(g_cum - lw))
    r_t = pieces(r * jnp.exp(g_cum))
    b_t = pieces(bb * e_neg)
    k_t = pieces(kd * e_neg)
    b_h = pieces((bb * e_end).astype(BF16))
    k_h = pieces((kd * e_end).astype(BF16))
    v_g = pieces(v)
    decay_end = [x[0:1] for x in pieces(jnp.exp(g_end))]

    ar = [stack(x, y).astype(BF16) for x, y in zip(a_t, r_t)]
    mb = [_dot_nt(x, bd(y)) for x, y in zip(ar, b_t)]
    mk = [_dot_nt(x, bd(y)) for x, y in zip(ar, k_t)]
    l_mat = [x[:CHUNK] * m_strict for x in mb]
    m_rb = [(x[CHUNK:] * m_incl).astype(BF16) for x in mb]
    m_k = [stack(x[:CHUNK] * m_strict, x[CHUNK:] * m_incl).astype(BF16) for x in mk]
    mv = [_dot(x, bd(y)) for x, y in zip(m_k, v_g)]
    t_mat = _unit_triangular_inverse(l_mat, bmask, bmask16_ref[...], eye16_ref[...],
                                     cm_ref[1], cm_ref[2], cm_ref[3])
    t_g = [stack(t, _dot(m, bd(t))).astype(BF16) for t, m in zip(t_mat, m_rb)]
    ta = [_dot(x, bd(y)) for x, y in zip(t_g, a_t)]
    tu = [_dot(x, bd(y[:CHUNK])) for x, y in zip(t_g, mv)]
    a_h = [x[:CHUNK].astype(BF16) for x in ta]
    u_0 = [x[:CHUNK] for x in tu]
    rh = [x + y[CHUNK:] for x, y in zip(r_t, ta)]
    yl = [x[CHUNK:] + y[CHUNK:] for x, y in zip(tu, mv)]
    p_full = [_dot_tn(x, y) for x, y in zip(b_h, a_h)]
    q_full = [_dot_tn(stack(x, y), stack(u, w).astype(BF16)) for x, y, u, w in zip(b_h, k_h, u_0, v_g)]
    for i, (h, g) in enumerate(chains):
        cols = slice(g * GROUP, (g + 1) * GROUP)
        p_ref[h, :, cols] = (_fold_heads(p_full[i]) + eye * decay_end[i]).astype(p_ref.dtype)
        rh_ref[h, :, cols] = rh[i].astype(rh_ref.dtype)
        q_ref[h, :, cols] = _fold_heads(q_full[i])
        yl_ref[h, :, cols] = yl[i]


def _scan_consts():
    lane = np.arange(GROUP)
    bmask = (lane[:, None] // HEAD == lane[None, :] // HEAD).astype(np.float32)
    i = np.arange(CHUNK)[:, None]
    j = np.arange(CHUNK)[None, :]
    jl = (lane % HEAD)[None, :]
    tri = np.stack([(j <= i), (j >= i)]).astype(np.float32)
    dir_masks = np.stack([np.stack([(jl < i), (jl <= i)]),
                          np.stack([(jl > i), (jl >= i)])]).astype(np.float32)
    same16 = (jl // SUB == i // SUB)
    same32 = (jl // (2 * SUB) == i // (2 * SUB))
    common = np.stack([(jl == i), same16, same32 & ~same16, ~same32]).astype(np.float32)
    bmask16 = (lane[:, None] // SUB == lane[None, :] // SUB).astype(np.float32)
    eye16 = ((lane % SUB)[None, :] == np.arange(SUB)[:, None]).astype(np.float32)
    return (jnp.asarray(bmask, BF16), jnp.asarray(bmask16, BF16), jnp.asarray(tri, BF16),
            jnp.asarray(dir_masks, F32), jnp.asarray(common, F32), jnp.asarray(eye16, F32))


def _scan(r, k, v, zw, za, k_k, k_a, n_ctx_chunks):
    b, l, d = r.shape
    rows = PAIR * CHUNK
    npair = l // rows
    assert l % rows == 0 and n_ctx_chunks % PAIR == 0
    n_ctx = n_ctx_chunks // PAIR
    bmask, bmask16, tri, dir_masks, common, eye16 = _scan_consts()

    def pair_of(di, s):
        back = jnp.where(s < n_ctx, n_ctx - 1 - s, npair - 1 + n_ctx - s)
        return jnp.where(di == 0, s, back)

    fold_pair = lambda di, s: pair_of(di, jnp.minimum(s, npair - 1))
    apply_pair = lambda di, s: pair_of(di, jnp.maximum(s - 1, 0))
    tok = pl.BlockSpec((1, rows, d), lambda di, bi, s: (bi, fold_pair(di, s), 0))
    tok2 = pl.BlockSpec((1, rows, d), lambda di, bi, s: (bi, fold_pair(di, s), di))
    par = pl.BlockSpec((1, d), lambda di, bi, s: (0, 0))
    return pl.pallas_call(
        _scan_kernel,
        out_shape=jax.ShapeDtypeStruct((2, b, l, d), BF16),
        grid=(2, b, npair + 1),
        in_specs=[tok, tok, tok, tok2, tok2, par, par,
                  pl.BlockSpec((GROUP, GROUP), lambda di, bi, s: (0, 0)),
                  pl.BlockSpec((GROUP, GROUP), lambda di, bi, s: (0, 0)),
                  pl.BlockSpec((1, CHUNK, CHUNK), lambda di, bi, s: (di, 0, 0)),
                  pl.BlockSpec((1, 2, CHUNK, GROUP), lambda di, bi, s: (di, 0, 0, 0)),
                  pl.BlockSpec((4, CHUNK, GROUP), lambda di, bi, s: (0, 0, 0)),
                  pl.BlockSpec((SUB, GROUP), lambda di, bi, s: (0, 0))],
        out_specs=pl.BlockSpec((1, 1, rows, d), lambda di, bi, s: (di, bi, apply_pair(di, s), 0)),
        scratch_shapes=[pltpu.VMEM((CHUNK, d), F32), pltpu.VMEM((PAIR, CHUNK, d), BF16),
                        pltpu.VMEM((PAIR, CHUNK, d), BF16), pltpu.VMEM((PAIR, CHUNK, d), F32),
                        pltpu.VMEM((PAIR, CHUNK, d), F32)],
        compiler_params=_params(("parallel", "parallel", "arbitrary")),
        name="scan",
    )(r, k, v, zw, za, k_k.reshape(1, d), k_a.reshape(1, d), bmask, bmask16, tri, dir_masks, common, eye16)


def _head_sum(x, ones_bd):
    hi, lo = _split2(x)
    return _dot(hi, ones_bd) + _dot(lo, ones_bd)


def _rwkv_post_kernel(y_ref, r_ref, k_ref, v_ref, za0_ref, za1_ref, gate_ref,
                      ka_ref, rk_ref, gnw_ref, gnb_ref, ones_ref, o_ref):
    ones_bd = ones_ref[...]
    tm, d = o_ref.shape[1], o_ref.shape[2]
    ng = d // GROUP
    to_rows = lambda x: jnp.concatenate([x[:, g * GROUP:(g + 1) * GROUP] for g in range(ng)], axis=0)
    to_cols = lambda x: jnp.concatenate([x[g * tm:(g + 1) * tm] for g in range(ng)], axis=1)
    head_mean = lambda x: to_cols(_head_sum(to_rows(x), ones_bd)) * (1.0 / HEAD)
    y = y_ref[0, 0].astype(F32) + y_ref[1, 0].astype(F32)
    yc = y - head_mean(y)
    var = head_mean(yc * yc)
    o = yc * lax.rsqrt(var + GN_EPS) * gnw_ref[...] + gnb_ref[...]
    a_sum = jax.nn.sigmoid(za0_ref[0].astype(F32)) + jax.nn.sigmoid(za1_ref[0].astype(F32))
    r = r_ref[0].astype(F32)
    k_sum = k_ref[0].astype(F32) * (2.0 + (a_sum - 2.0) * ka_ref[...])
    bonus = head_mean(r * k_sum * rk_ref[...]) * float(HEAD) * v_ref[0].astype(F32)
    o_ref[0] = ((o + bonus) * gate_ref[0].astype(F32)).astype(o_ref.dtype)


def _rwkv_post(y, r, k, v, za, gate, k_a, r_k, gn_w, gn_b, n_ctx, tm=256):
    _, b, l, d = y.shape
    t = l - n_ctx
    off = n_ctx // tm
    ones_bd = _scan_consts()[0]
    tok = pl.BlockSpec((1, tm, d), lambda bi, s: (bi, s + off, 0))
    par = pl.BlockSpec((1, d), lambda bi, s: (0, 0))
    return pl.pallas_call(
        _rwkv_post_kernel,
        out_shape=jax.ShapeDtypeStruct((b, t, d), BF16),
        grid=(b, t // tm),
        in_specs=[pl.BlockSpec((2, 1, tm, d), lambda bi, s: (0, bi, s + off, 0)),
                  tok, tok, tok,
                  pl.BlockSpec((1, tm, d), lambda bi, s: (bi, s + off, 0)),
                  pl.BlockSpec((1, tm, d), lambda bi, s: (bi, s + off, 1)),
                  tok, par, par, par, par,
                  pl.BlockSpec((GROUP, GROUP), lambda bi, s: (0, 0))],
        out_specs=pl.BlockSpec((1, tm, d), lambda bi, s: (bi, s, 0)),
        compiler_params=_params(("parallel", "parallel")),
        name="rwkv_post",
    )(y, r, k, v, za, za, gate, k_a.reshape(1, d), r_k.reshape(1, d), gn_w.reshape(1, d),
      gn_b.reshape(1, d), ones_bd)


def _proj_norm_kernel(a_ref, w_ref, x_ref, pos_ref, g_ref, lnw_ref, lnb_ref, sc_ref, sh_ref,
                      o_ref, h_ref, acc_ref, *, alpha):
    j = pl.program_id(2)
    nj = acc_ref.shape[0]
    tn = acc_ref.shape[2]
    acc_ref[j] = _dot(a_ref[...], w_ref[...].astype(BF16))

    @pl.when(j == nj - 1)
    def _():
        gate = g_ref[0]
        for jj in range(nj):
            cols = slice(jj * tn, (jj + 1) * tn)
            o_ref[:, cols] = alpha * (x_ref[:, cols] + pos_ref[:, cols]) + gate[:, cols] * acc_ref[jj]
        x_new = _layer_norm(o_ref[...], lnw_ref[...], lnb_ref[...])
        o_ref[...] = x_new
        h_ref[...] = (x_new * (1.0 + sc_ref[0]) + sh_ref[0]).astype(h_ref.dtype)


def _proj_norm(a, w, x, pos, gate, ln_w, ln_b, sc, sh, alpha, tm=512, tn=512):
    m, k = a.shape
    d = w.shape[1]
    t = pos.shape[0]
    tm = _tile(t, tm)
    tn = _tile(d, tn)
    tpb = t // tm
    nb = m // t
    rows = lambda ti, bi, j: (bi * tpb + ti, 0)
    vec = pl.BlockSpec((1, 1, d), lambda ti, bi, j: (bi, 0, 0))
    return pl.pallas_call(
        functools.partial(_proj_norm_kernel, alpha=alpha),
        out_shape=[jax.ShapeDtypeStruct((m, d), F32), jax.ShapeDtypeStruct((m, d), BF16)],
        grid=(tpb, nb, d // tn),
        in_specs=[pl.BlockSpec((tm, k), rows),
                  pl.BlockSpec((k, tn), lambda ti, bi, j: (0, j)),
                  pl.BlockSpec((tm, d), rows),
                  pl.BlockSpec((tm, d), lambda ti, bi, j: (ti, 0)),
                  vec,
                  pl.BlockSpec((1, d), lambda ti, bi, j: (0, 0)),
                  pl.BlockSpec((1, d), lambda ti, bi, j: (0, 0)),
                  vec, vec],
        out_specs=[pl.BlockSpec((tm, d), rows)] * 2,
        scratch_shapes=[pltpu.VMEM((d // tn, tm, tn), F32)],
        compiler_params=_params(("parallel", "parallel", "arbitrary")),
        name="proj_norm",
    )(a, w, x, pos, gate, ln_w.reshape(1, d), ln_b.reshape(1, d), sc, sh)


def _swiglu_halves(h, w1_ref, w3_ref, w2_ref, lead):
    tf = w1_ref.shape[-1]
    halves = [slice(0, tf // 2), slice(tf // 2, tf)]
    a1 = [_dot(h, w1_ref[lead + (slice(None), c)].astype(BF16)) for c in halves]
    a3 = [_dot(h, w3_ref[lead + (slice(None), c)].astype(BF16)) for c in halves]
    u = [(x * jax.nn.sigmoid(x) * y).astype(BF16) for x, y in zip(a1, a3)]
    y = [_dot(x, w2_ref[lead + (c, slice(None))].astype(BF16)) for x, c in zip(u, halves)]
    return y[0] + y[1]


def _ffn_kernel(h_ref, w1_ref, w3_ref, w2_ref, o_ref, acc_ref):
    j = pl.program_id(1)

    @pl.when(j == 0)
    def _():
        acc_ref[...] = jnp.zeros_like(acc_ref)

    acc_ref[...] += _swiglu_halves(h_ref[...], w1_ref, w3_ref, w2_ref, ())

    @pl.when(j == pl.num_programs(1) - 1)
    def _():
        o_ref[...] = acc_ref[...].astype(o_ref.dtype)


def _ffn(h, w1, w3, w2, tm=1024, tf=512):
    m, d = h.shape
    f = w1.shape[1]
    tm = _tile(m, tm)
    tf = _tile(f, tf)
    return pl.pallas_call(
        _ffn_kernel,
        out_shape=jax.ShapeDtypeStruct((m, d), BF16),
        grid=(m // tm, f // tf),
        in_specs=[pl.BlockSpec((tm, d), lambda i, j: (i, 0)),
                  pl.BlockSpec((d, tf), lambda i, j: (0, j)),
                  pl.BlockSpec((d, tf), lambda i, j: (0, j)),
                  pl.BlockSpec((tf, d), lambda i, j: (j, 0))],
        out_specs=pl.BlockSpec((tm, d), lambda i, j: (i, 0)),
        scratch_shapes=[pltpu.VMEM((tm, d), F32)],
        compiler_params=_params(("parallel", "arbitrary")),
        name="ffn",
    )(h, w1, w3, w2)


def _residual_norm_kernel(x_ref, y_ref, g_ref, lnw_ref, lnb_ref, o_ref, *, alpha):
    z = alpha * x_ref[...] + g_ref[0] * y_ref[...].astype(F32)
    o_ref[...] = _layer_norm(z, lnw_ref[...], lnb_ref[...])


def _residual_norm(x, y, gate, ln_w, ln_b, alpha, rows_per_batch, tm=512):
    m, d = x.shape
    tm = _tile(rows_per_batch, tm)
    tpb = rows_per_batch // tm
    tok = pl.BlockSpec((tm, d), lambda i: (i, 0))
    row = pl.BlockSpec((1, d), lambda i: (0, 0))
    return pl.pallas_call(
        functools.partial(_residual_norm_kernel, alpha=alpha),
        out_shape=jax.ShapeDtypeStruct((m, d), F32),
        grid=(m // tm,),
        in_specs=[tok, tok, pl.BlockSpec((1, 1, d), lambda i: (i // tpb, 0, 0)), row, row],
        out_specs=tok,
        compiler_params=_params(("parallel",)),
        name="residual_norm",
    )(x, y, gate, ln_w.reshape(1, d), ln_b.reshape(1, d))


def _shift_down(x, s, row):
    return jnp.where(row >= s, pltpu.roll(x, s, axis=0), 0.0)


def _shift_up(x, s, row):
    t = x.shape[0]
    return jnp.where(row < t - s, pltpu.roll(x, t - s, axis=0), 0.0)


def _pool_kernel(x_ref, sc_ref, sh_ref, w_ref, scale_ref, o_ref):
    g = pl.program_id(0)
    t = x_ref.shape[1]
    h = x_ref[0] * (1.0 + sc_ref[0]) + sh_ref[0]
    row = lax.broadcasted_iota(jnp.int32, h.shape, 0)
    w = w_ref[0].astype(BF16)
    for gi, win in enumerate(POOL_WINDOWS):
        @pl.when(g == gi)
        def _(win=win):
            half = win // 2
            back = h
            fwd = h
            m = 1
            while m < half:
                back = back + _shift_down(back, m, row)
                fwd = fwd + _shift_up(fwd, m, row)
                m *= 2
            total = _shift_down(back, 1, row) + fwd
            count = (jnp.minimum(row + half, t) - jnp.maximum(row - half, 0)).astype(F32)
            pooled = (total / count - h).astype(BF16)
            o_ref[0] = _dot(pooled, w) * scale_ref[...]


def _pool(x, sc, sh, w_pool, scale):
    b, t, d = x.shape
    ng, p, _ = w_pool.shape
    vec = pl.BlockSpec((1, 1, p), lambda g, bi: (bi, 0, g))
    return pl.pallas_call(
        _pool_kernel,
        out_shape=jax.ShapeDtypeStruct((b, t, d), F32),
        grid=(ng, b),
        in_specs=[pl.BlockSpec((1, t, p), lambda g, bi: (bi, 0, g)), vec, vec,
                  pl.BlockSpec((1, p, p), lambda g, bi: (g, 0, 0)),
                  pl.BlockSpec((1, p), lambda g, bi: (0, g))],
        out_specs=pl.BlockSpec((1, t, p), lambda g, bi: (bi, 0, g)),
        compiler_params=_params(("parallel", "parallel")),
        name="pool",
    )(x, sc, sh, w_pool, scale.reshape(1, d))


def _route_kernel(x_ref, y_ref, g_ref, lnw_ref, lnb_ref, sc_ref, sh_ref, rt_ref,
                  xo_ref, h_ref, route_ref, *, alpha):
    x = _layer_norm(alpha * x_ref[...] + g_ref[0] * y_ref[...], lnw_ref[...], lnb_ref[...])
    xo_ref[...] = x
    h = x * (1.0 + sc_ref[0]) + sh_ref[0]
    h_ref[...] = h
    r1, r2, r3 = _split3(rt_ref[...])
    h1, h2, h3 = _split3(h)
    logits = (_dot_nt(r1, h1) + (_dot_nt(r1, h2) + _dot_nt(r2, h1))
              + (_dot_nt(r1, h3) + _dot_nt(r2, h2) + _dot_nt(r3, h1)))
    mx = jnp.max(logits, axis=0, keepdims=True)
    e = jnp.exp(logits - mx)
    p = e / jnp.sum(e, axis=0, keepdims=True)
    idx = lax.broadcasted_iota(jnp.int32, p.shape, 0)
    p1 = jnp.max(p, axis=0, keepdims=True)
    i1 = jnp.min(jnp.where(p == p1, idx, N_EXPERTS), axis=0, keepdims=True)
    rest = jnp.where(idx == i1, -1.0, p)
    p2 = jnp.max(rest, axis=0, keepdims=True)
    i2 = jnp.min(jnp.where(rest == p2, idx, N_EXPERTS), axis=0, keepdims=True)
    den = p1 + p2
    out = jnp.where(idx == 0, i1.astype(F32), 0.0)
    out = jnp.where(idx == 1, i2.astype(F32), out)
    out = jnp.where(idx == 2, p1 / den, out)
    out = jnp.where(idx == 3, p2 / den, out)
    route_ref[...] = out


def _route(x, y, gate, ln_w, ln_b, sc, sh, router, alpha, rows_per_batch, tm=256):
    m, d = x.shape
    tpb = rows_per_batch // tm
    vec = pl.BlockSpec((1, 1, d), lambda i: (i // tpb, 0, 0))
    row = pl.BlockSpec((1, d), lambda i: (0, 0))
    tok = pl.BlockSpec((tm, d), lambda i: (i, 0))
    return pl.pallas_call(
        functools.partial(_route_kernel, alpha=alpha),
        out_shape=[jax.ShapeDtypeStruct((m, d), F32), jax.ShapeDtypeStruct((m, d), F32),
                   jax.ShapeDtypeStruct((N_EXPERTS, m), F32)],
        grid=(m // tm,),
        in_specs=[tok, tok, vec, row, row, vec, vec,
                  pl.BlockSpec((N_EXPERTS, d), lambda i: (0, 0))],
        out_specs=[tok, tok, pl.BlockSpec((N_EXPERTS, tm), lambda i: (0, i))],
        compiler_params=_params(("parallel",)),
        name="route",
    )(x, y, gate, ln_w.reshape(1, d), ln_b.reshape(1, d), sc, sh, router.T)


def _row_copy(src_hbm, dst_vmem, sem, src_row, dst_row):
    return pltpu.make_async_copy(src_hbm.at[pl.ds(src_row, 1)], dst_vmem.at[pl.ds(dst_row, 1)], sem)


def _moe_kernel(ce_ref, nv_ref, idx_ref, h_ref, w1_ref, w3_ref, w2_ref, o_ref,
                stage_ref, work_ref, gather_sem, *, share):
    c = pl.program_id(0)
    j = pl.program_id(1)
    n_chunks = pl.num_programs(0)
    nj = pl.num_programs(1)
    nv = nv_ref[c]
    full = work_ref.shape[0]
    stage_rows = stage_ref.shape[0]

    def gather(chunk, r):
        return _row_copy(h_ref, stage_ref, gather_sem, idx_ref[chunk * full + r], r)

    def wait_all_gathers():
        def body(r, carry):
            gather(0, r).wait()
            return carry
        lax.fori_loop(0, stage_rows, body, 0, unroll=8)

    @pl.when(j == 0)
    def _():
        @pl.when(c == 0)
        def _():
            def body(r, carry):
                gather(0, r).start()
                return carry
            lax.fori_loop(0, stage_rows, body, 0, unroll=8)

        o_ref[...] = jnp.zeros_like(o_ref)
        wait_all_gathers()
        work_ref[...] = stage_ref[0:full, :].astype(work_ref.dtype)

    def request_next_share():
        for i in range(share):
            gather(c + 1, j * share + i).start(priority=1)

    def swiglu_rows(rows):
        o_ref[rows, :] += _swiglu_halves(work_ref[rows, :], w1_ref, w3_ref, w2_ref, (0,))

    n_sub = (nv + (MOE_SUB - 1)) // MOE_SUB
    for k in range(-(-full // MOE_SUB) + 1):
        @pl.when(n_sub == k)
        def _(k=k):
            request_next_share()
            if k:
                swiglu_rows(slice(0, min(k * MOE_SUB, full)))

    @pl.when(jnp.logical_and(j == nj - 1, c == n_chunks - 1))
    def _():
        wait_all_gathers()


def _moe_ffn(h, src_idx, chunk_expert, chunk_valid, w1, w3, w2, n_chunks, tf=512):
    d = h.shape[1]
    f = w1.shape[2]
    tf = _tile(f, tf)
    nj = f // tf
    share = -(-MOE_CHUNK // nj)
    share = -(-share // 8) * 8
    stage_rows = share * nj
    assert src_idx.shape[0] >= n_chunks * MOE_CHUNK + stage_rows

    def jeff(c, j, nv):
        return jnp.where(nv[c] > 0, j, nj - 1)

    return pl.pallas_call(
        functools.partial(_moe_kernel, share=share),
        out_shape=jax.ShapeDtypeStruct((n_chunks * MOE_CHUNK, d), F32),
        grid_spec=pltpu.PrefetchScalarGridSpec(
            num_scalar_prefetch=3,
            grid=(n_chunks, nj),
            in_specs=[pl.BlockSpec(memory_space=pl.ANY),
                      pl.BlockSpec((1, d, tf), lambda c, j, ce, nv, ix: (ce[c], 0, jeff(c, j, nv))),
                      pl.BlockSpec((1, d, tf), lambda c, j, ce, nv, ix: (ce[c], 0, jeff(c, j, nv))),
                      pl.BlockSpec((1, tf, d), lambda c, j, ce, nv, ix: (ce[c], jeff(c, j, nv), 0))],
            out_specs=pl.BlockSpec((MOE_CHUNK, d), lambda c, j, ce, nv, ix: (c, 0)),
            scratch_shapes=[pltpu.VMEM((stage_rows, d), F32), pltpu.VMEM((MOE_CHUNK, d), BF16),
                            pltpu.SemaphoreType.DMA]),
        compiler_params=_params(("arbitrary", "arbitrary"), MOE_VMEM_LIMIT_BYTES),
        name="moe_ffn",
    )(chunk_expert, chunk_valid, src_idx, h, w1, w3, w2)


def _combine_kernel(p0_ref, p1_ref, ys_ref, x_ref, gates_ref, g_ref, lnw_ref, lnb_ref,
                    o_ref, b0_ref, b1_ref, sem, *, alpha):
    i = pl.program_id(0)
    n = pl.num_programs(0)
    rows = b0_ref.shape[1]
    slot = i % 2

    def copies(tile, to_slot, r):
        return (_row_copy(ys_ref, b0_ref.at[to_slot], sem.at[to_slot], p0_ref[tile * rows + r], r),
                _row_copy(ys_ref, b1_ref.at[to_slot], sem.at[to_slot], p1_ref[tile * rows + r], r))

    def wait_slot(to_slot):
        def body(r, carry):
            for cp in copies(0, to_slot, r):
                cp.wait()
            return carry
        lax.fori_loop(0, rows, body, 0, unroll=4)

    @pl.when(i == 0)
    def _():
        def body(r, carry):
            for cp in copies(0, 0, r):
                cp.start()
            return carry
        lax.fori_loop(0, rows, body, 0, unroll=4)

    wait_slot(slot)
    for r in range(rows):
        for cp in copies(i + 1, 1 - slot, r):
            cp.start()
    gates = gates_ref[...]
    y = gates[:, 0:1] * b0_ref[slot] + gates[:, 1:2] * b1_ref[slot]
    z = alpha * x_ref[...] + g_ref[0] * y
    o_ref[...] = _layer_norm(z, lnw_ref[...], lnb_ref[...])

    @pl.when(i == n - 1)
    def _():
        wait_slot(1 - slot)


def _combine(ys, pos0, pos1, gates, x, gate_vec, ln_w, ln_b, alpha, rows_per_batch, rows=256):
    m, d = x.shape
    tpb = rows_per_batch // rows
    tok = lambda i, a, b: (i, 0)
    return pl.pallas_call(
        functools.partial(_combine_kernel, alpha=alpha),
        out_shape=jax.ShapeDtypeStruct((m, d), F32),
        grid_spec=pltpu.PrefetchScalarGridSpec(
            num_scalar_prefetch=2,
            grid=(m // rows,),
            in_specs=[pl.BlockSpec(memory_space=pl.ANY),
                      pl.BlockSpec((rows, d), tok),
                      pl.BlockSpec((rows, 2), tok),
                      pl.BlockSpec((1, 1, d), lambda i, a, b: (i // tpb, 0, 0)),
                      pl.BlockSpec((1, d), lambda i, a, b: (0, 0)),
                      pl.BlockSpec((1, d), lambda i, a, b: (0, 0))],
            out_specs=pl.BlockSpec((rows, d), tok),
            scratch_shapes=[pltpu.VMEM((2, rows, d), F32), pltpu.VMEM((2, rows, d), F32),
                            pltpu.SemaphoreType.DMA((2,))]),
        compiler_params=_params(("arbitrary",)),
        name="moe_combine",
    )(pos0, pos1, ys, x, gates, gate_vec, ln_w.reshape(1, d), ln_b.reshape(1, d))


def _routing_tables(route, n_chunks, table_len):
    n = route.shape[1]
    experts = jnp.concatenate([route[0], route[1]]).astype(jnp.int32)
    onehot = (experts[:, None] == jnp.arange(N_EXPERTS, dtype=jnp.int32)[None, :]).astype(jnp.int32)
    csum = jnp.cumsum(onehot, axis=0)
    rank = jnp.sum((csum - 1) * onehot, axis=1)
    counts = csum[-1]
    chunks_e = (counts + MOE_CHUNK - 1) // MOE_CHUNK
    chunk_end = jnp.cumsum(chunks_e)
    chunk_start = chunk_end - chunks_e
    dest = (chunk_start * MOE_CHUNK)[experts] + rank
    token = jnp.arange(2 * n, dtype=jnp.int32) % n
    src_idx = jnp.zeros((table_len,), jnp.int32).at[dest].set(token)
    cid = jnp.arange(n_chunks, dtype=jnp.int32)
    used = cid < chunk_end[-1]
    last_used = jnp.maximum(chunk_end[-1] - 1, 0)
    ce = jnp.sum((jnp.minimum(cid, last_used)[:, None] >= chunk_end[None, :]).astype(jnp.int32), axis=1)
    ce = jnp.minimum(ce, N_EXPERTS - 1)
    nvalid = jnp.clip(counts[ce] - (cid - chunk_start[ce]) * MOE_CHUNK, 0, MOE_CHUNK)
    nvalid = jnp.where(used, nvalid, 0).astype(jnp.int32)
    return src_idx, ce, nvalid, dest[:n], dest[n:]


def _position_embedding(rows, width, d):
    quarter = d // 4
    omega = 1.0 / (POS_BASE ** (jnp.arange(quarter, dtype=F32) / quarter))
    ar = jnp.arange(rows, dtype=F32)[:, None] * omega[None, :]
    ac = jnp.arange(width, dtype=F32)[:, None] * omega[None, :]
    row_part = jnp.repeat(jnp.concatenate([jnp.sin(ar), jnp.cos(ar)], axis=-1), width, axis=0)
    col_part = jnp.tile(jnp.concatenate([jnp.sin(ac), jnp.cos(ac)], axis=-1), (rows, 1))
    return jnp.concatenate([row_part, col_part], axis=-1)


def _block_diag2(w):
    z = jnp.zeros_like(w[0])
    return jnp.concatenate([jnp.concatenate([w[0], z], axis=1), jnp.concatenate([z, w[1]], axis=1)], axis=0)


def kernel(x, c, ctx, c_ctx, w_mod, b_mod, ln_w, ln_b, rwkv_mu, rwkv_w_r, rwkv_w_k, rwkv_w_v, rwkv_w_o, rwkv_decay_w0, rwkv_decay_w1, rwkv_decay_w2, rwkv_iclr_a0, rwkv_iclr_a1, rwkv_iclr_a2, rwkv_gate_g1, rwkv_gate_g2, rwkv_k_k, rwkv_k_a, rwkv_r_k, rwkv_gn_w, rwkv_gn_b, pool_w, pool_scale, ffn_w1, ffn_w3, ffn_w2, moe_router, moe_w1, moe_w3, moe_w2):
    b, t, d = x.shape
    n_ctx = ctx.shape[1]
    depth = w_mod.shape[0]
    assert depth == 2 and rwkv_mu.shape[0] == 1 and pool_w.shape[0] == 1
    alpha = (2.0 * depth) ** 0.25
    grid_w = 64
    l = n_ctx + t
    n = b * t

    cond = jnp.zeros((8, d), F32).at[:b].set(c).at[b].set(c_ctx)
    mod = _adaln(cond, w_mod, b_mod).reshape(depth, 8, 6, d)
    lat = lambda layer, which: mod[layer, :b, which].reshape(b, 1, d)
    cvec = lambda layer, which: mod[layer, b, which].reshape(1, 1, d)

    pos = _position_embedding(t // grid_w, grid_w, d)

    xr, xw, xk, xv, xa, xg = _rwkv_mix(ctx, x, pos, lat(0, 1), lat(0, 0), cvec(0, 1), cvec(0, 0), rwkv_mu[0])
    flat = lambda a: a.reshape(b * l, a.shape[-1])
    r = _mm(flat(xr), rwkv_w_r[0], out_dtype=BF16)
    k = _mm(flat(xk), rwkv_w_k[0], out_dtype=BF16)
    v = _mm(flat(xv), rwkv_w_v[0], out_dtype=BF16)
    dw1 = jnp.concatenate([rwkv_decay_w1[0, 0], rwkv_decay_w1[0, 1]], axis=1)
    ia1 = jnp.concatenate([rwkv_iclr_a1[0, 0], rwkv_iclr_a1[0, 1]], axis=1)
    lora_w = _mm(flat(xw), dw1, act="tanh", out_dtype=BF16)
    zw = _mm(lora_w, _block_diag2(rwkv_decay_w2[0]), rwkv_decay_w0[0].reshape(1, 2 * d), out_dtype=BF16, tn=2048)
    lora_a = _mm(flat(xa), ia1, out_dtype=BF16)
    za = _mm(lora_a, _block_diag2(rwkv_iclr_a2[0]), rwkv_iclr_a0[0].reshape(1, 2 * d), out_dtype=BF16, tn=2048)
    lora_g = _mm(flat(xg), rwkv_gate_g1[0], act="sigmoid", out_dtype=BF16)
    gate = _mm(lora_g, rwkv_gate_g2[0], out_dtype=BF16, tn=2048)
    seq = lambda a: a.reshape(b, l, a.shape[-1])
    y_scan = _scan(seq(r), seq(k), seq(v), seq(zw), seq(za), rwkv_k_k[0], rwkv_k_a[0], n_ctx // CHUNK)
    og = _rwkv_post(y_scan, seq(r), seq(k), seq(v), seq(za), seq(gate), rwkv_k_a[0],
                    rwkv_r_k[0].reshape(d), rwkv_gn_w[0], rwkv_gn_b[0], n_ctx)
    x1, h1 = _proj_norm(og.reshape(n, d), rwkv_w_o[0].astype(BF16), x.reshape(n, d), pos, lat(0, 2),
                        ln_w[0, 0], ln_b[0, 0], lat(0, 4), lat(0, 3), alpha)
    y_ffn = _ffn(h1, ffn_w1[0], ffn_w3[0], ffn_w2[0])
    x2 = _residual_norm(x1, y_ffn, lat(0, 5), ln_w[0, 1], ln_b[0, 1], alpha, t)

    y_pool = _pool(x2.reshape(b, t, d), lat(1, 1), lat(1, 0), pool_w[0], pool_scale[0])
    x3, h3, route = _route(x2, y_pool.reshape(n, d), lat(1, 2), ln_w[1, 0], ln_b[1, 0],
                           lat(1, 4), lat(1, 3), moe_router[0], alpha, t)
    n_chunks = (2 * n + N_EXPERTS * (MOE_CHUNK - 1)) // MOE_CHUNK
    src_idx, chunk_expert, chunk_valid, pos0, pos1 = _routing_tables(route, n_chunks, (n_chunks + 2) * MOE_CHUNK)
    ys = _moe_ffn(h3, src_idx, chunk_expert, chunk_valid, moe_w1[0], moe_w3[0], moe_w2[0], n_chunks)
    gates = jnp.stack([route[2], route[3]], axis=1)
    spare = jnp.zeros((256,), jnp.int32)
    out = _combine(ys, jnp.concatenate([pos0, spare]), jnp.concatenate([pos1, spare]), gates, x3,
                   lat(1, 5), ln_w[1, 1], ln_b[1, 1], alpha, t)
    return out.reshape(b, t, d)
```

```python
import functools
import math

import numpy as np
import jax
import jax.numpy as jnp
from jax import lax
from jax.experimental import pallas as pl
from jax.experimental.pallas import tpu as pltpu

F32 = jnp.float32
BF16 = jnp.bfloat16

HEAD = 64
GROUP = 4 * HEAD
CHUNK = 64
LN_EPS = 1e-5
GN_EPS = 64e-5
POS_BASE = 10000.0
POOL_WINDOWS = (2, 4, 8, 16)
N_EXPERTS = 8
MOE_CHUNK = 1024
MOE_SUB = 256
VMEM_LIMIT_BYTES = 56 * 1024 * 1024
MOE_VMEM_LIMIT_BYTES = 60 * 1024 * 1024


def _params(semantics, vmem_limit_bytes=VMEM_LIMIT_BYTES):
    return pltpu.CompilerParams(dimension_semantics=semantics, vmem_limit_bytes=vmem_limit_bytes)


def _tile(n, preferred):
    t = min(preferred, n)
    while n % t:
        t //= 2
    return t


def _dot(a, b):
    return jnp.dot(a, b, preferred_element_type=F32)


def _dot_nt(a, b):
    return lax.dot_general(a, b, (((1,), (1,)), ((), ())), preferred_element_type=F32)


def _dot_tn(a, b):
    return lax.dot_general(a, b, (((0,), (0,)), ((), ())), preferred_element_type=F32)


def _split2(x):
    hi = x.astype(BF16)
    lo = (x - hi.astype(F32)).astype(BF16)
    return hi, lo


def _split3(x):
    hi = x.astype(BF16)
    r1 = x - hi.astype(F32)
    mid = r1.astype(BF16)
    lo = (r1 - mid.astype(F32)).astype(BF16)
    return hi, mid, lo


def _layer_norm(z, w, b):
    mu = jnp.mean(z, axis=-1, keepdims=True)
    zc = z - mu
    var = jnp.mean(zc * zc, axis=-1, keepdims=True)
    return zc * lax.rsqrt(var + LN_EPS) * w + b


def _adaln_kernel(c_ref, w_ref, b_ref, o_ref):
    c = c_ref[...]
    a = (c * jax.nn.sigmoid(c)).astype(BF16)
    o_ref[0] = _dot(a, w_ref[0].astype(BF16)) + b_ref[0]


def _adaln(cond, w_mod, b_mod, tn=1024):
    depth, d, n = w_mod.shape
    tn = _tile(n, tn)
    rows = cond.shape[0]
    return pl.pallas_call(
        _adaln_kernel,
        out_shape=jax.ShapeDtypeStruct((depth, rows, n), F32),
        grid=(depth, n // tn),
        in_specs=[pl.BlockSpec((rows, d), lambda l, j: (0, 0)),
                  pl.BlockSpec((1, d, tn), lambda l, j: (l, 0, j)),
                  pl.BlockSpec((1, 1, tn), lambda l, j: (l, 0, j))],
        out_specs=pl.BlockSpec((1, rows, tn), lambda l, j: (l, 0, j)),
        compiler_params=_params(("parallel", "parallel")),
        name="adaln",
    )(cond, w_mod, b_mod.reshape(depth, 1, n))


def _mm_kernel(a_ref, w_ref, b_ref, o_ref, *, act):
    acc = _dot(a_ref[...], w_ref[...].astype(BF16)) + b_ref[...]
    if act == "tanh":
        acc = jnp.tanh(acc)
    elif act == "sigmoid":
        acc = jax.nn.sigmoid(acc)
    o_ref[...] = acc.astype(o_ref.dtype)


def _mm(a, w, bias=None, *, act=None, out_dtype=F32, tm=2304, tn=512):
    m, k = a.shape
    n = w.shape[1]
    tm = _tile(m, tm)
    tn = _tile(n, tn)
    if bias is None:
        bias = jnp.zeros((1, n), F32)
    return pl.pallas_call(
        functools.partial(_mm_kernel, act=act),
        out_shape=jax.ShapeDtypeStruct((m, n), out_dtype),
        grid=(m // tm, n // tn),
        in_specs=[pl.BlockSpec((tm, k), lambda i, j: (i, 0)),
                  pl.BlockSpec((k, tn), lambda i, j: (0, j)),
                  pl.BlockSpec((1, tn), lambda i, j: (0, j))],
        out_specs=pl.BlockSpec((tm, tn), lambda i, j: (i, j)),
        compiler_params=_params(("parallel", "arbitrary")),
        name="matmul",
    )(a, w, bias.reshape(1, n))


def _mix_kernel(ctx_ref, x_ref, xp_ref, xn_ref, pos_ref, pp_ref, pn_ref,
                sc_ref, sh_ref, csc_ref, csh_ref, mu_ref,
                o0, o1, o2, o3, o4, o5, *, n_lat_tiles):
    s = pl.program_id(0)
    is_ctx = s == 0
    tm = x_ref.shape[1]
    scale = jnp.where(is_ctx, csc_ref[0], sc_ref[0]) + 1.0
    shift = jnp.where(is_ctx, csh_ref[0], sh_ref[0])
    src = jnp.where(is_ctx, ctx_ref[0], x_ref[0] + pos_ref[...])
    h = src * scale + shift
    has_prev = s > 1
    has_next = jnp.logical_and(s >= 1, s < n_lat_tiles)
    h_prev = jnp.where(has_prev, (xp_ref[0] + pp_ref[...]) * scale + shift, 0.0)[7:8]
    h_next = jnp.where(has_next, (xn_ref[0] + pn_ref[...]) * scale + shift, 0.0)[0:1]
    row = lax.broadcasted_iota(jnp.int32, h.shape, 0)
    h_m1 = jnp.where(row == 0, h_prev, pltpu.roll(h, 1, axis=0))
    h_p1 = jnp.where(row == tm - 1, h_next, pltpu.roll(h, tm - 1, axis=0))
    xx = 0.5 * (h_m1 + h_p1) - h
    for n, o_ref in enumerate((o0, o1, o2, o3, o4, o5)):
        o_ref[0] = (h + xx * mu_ref[n:n + 1]).astype(o_ref.dtype)


def _rwkv_mix(ctx, x, pos, sc, sh, csc, csh, mu):
    b, t, d = x.shape
    tm = ctx.shape[1]
    assert t % tm == 0 and tm % 8 == 0
    n_lat = t // tm
    r8 = tm // 8
    lat = lambda s, bi: (bi, jnp.maximum(s - 1, 0), 0)
    prev8 = lambda s, bi: (bi, jnp.maximum((s - 1) * r8 - 1, 0), 0)
    next8 = lambda s, bi: (bi, jnp.minimum(jnp.maximum(s, 1) * r8, t // 8 - 1), 0)
    vec = pl.BlockSpec((1, 1, d), lambda s, bi: (bi, 0, 0))
    cvec = pl.BlockSpec((1, 1, d), lambda s, bi: (0, 0, 0))
    out_sds = jax.ShapeDtypeStruct((b, tm + t, d), BF16)
    return pl.pallas_call(
        functools.partial(_mix_kernel, n_lat_tiles=n_lat),
        out_shape=[out_sds] * 6,
        grid=(n_lat + 1, b),
        in_specs=[pl.BlockSpec((1, tm, d), lambda s, bi: (bi, 0, 0)),
                  pl.BlockSpec((1, tm, d), lat),
                  pl.BlockSpec((1, 8, d), prev8),
                  pl.BlockSpec((1, 8, d), next8),
                  pl.BlockSpec((tm, d), lambda s, bi: (jnp.maximum(s - 1, 0), 0)),
                  pl.BlockSpec((8, d), lambda s, bi: (jnp.maximum((s - 1) * r8 - 1, 0), 0)),
                  pl.BlockSpec((8, d), lambda s, bi: (jnp.minimum(jnp.maximum(s, 1) * r8, t // 8 - 1), 0)),
                  vec, vec, cvec, cvec,
                  pl.BlockSpec((6, d), lambda s, bi: (0, 0))],
        out_specs=[pl.BlockSpec((1, tm, d), lambda s, bi: (bi, s, 0))] * 6,
        compiler_params=_params(("parallel", "parallel")),
        name="rwkv_mix",
    )(ctx, x, x, x, pos, pos, pos, sc, sh, csc, csh, mu)


def _block_diag(x, bmask):
    xb = x.astype(BF16)
    zero = jnp.zeros((HEAD, GROUP // 2), BF16)
    rows = []
    for h in range(GROUP // HEAD):
        t = h // 2
        blk = xb[:, t * 128:(t + 1) * 128] * bmask[h * HEAD:(h + 1) * HEAD, t * 128:(t + 1) * 128]
        rows.append(jnp.concatenate([blk, zero] if t == 0 else [zero, blk], axis=1))
    return jnp.concatenate(rows, axis=0)


def _fold_heads(full):
    lane = lax.broadcasted_iota(jnp.int32, (HEAD, 128), 1)
    tiles = []
    for t in range(GROUP // 128):
        even = full[(2 * t) * HEAD:(2 * t + 1) * HEAD, t * 128:(t + 1) * 128]
        odd = full[(2 * t + 1) * HEAD:(2 * t + 2) * HEAD, t * 128:(t + 1) * 128]
        tiles.append(jnp.where(lane < HEAD, even, odd))
    return jnp.concatenate(tiles, axis=1)


def _dot_bd_pair(x, y1, y2, bmask, transposed=False):
    out1, out2 = [], []
    for t in range(GROUP // 128):
        lanes = slice(t * 128, (t + 1) * 128)
        mask = bmask[lanes, lanes]
        b1 = jnp.concatenate([y1[:, lanes].astype(BF16)] * 2, axis=0) * mask
        b2 = jnp.concatenate([y2[:, lanes].astype(BF16)] * 2, axis=0) * mask
        if transposed:
            z = _dot_nt(x[:, lanes], jnp.concatenate([b1, b2], axis=0))
        else:
            z = _dot(x[:, lanes], jnp.concatenate([b1, b2], axis=1))
        out1.append(z[:, :128])
        out2.append(z[:, 128:])
    return jnp.concatenate(out1, axis=1), jnp.concatenate(out2, axis=1)


SUB = 16


def _square_bd16(x, bmask16):
    per_tile = 128 // SUB
    tiles = [slice(t * 128, (t + 1) * 128) for t in range(GROUP // 128)]
    lhs = jnp.concatenate([x[:, lanes] for lanes in tiles], axis=0)
    rhs = jnp.concatenate([jnp.concatenate([x[:, lanes]] * per_tile, axis=0) * bmask16[lanes, lanes]
                           for lanes in tiles], axis=1)
    z = _dot(lhs, rhs)
    return jnp.concatenate([z[t * SUB:(t + 1) * SUB, lanes] for t, lanes in enumerate(tiles)], axis=1)


def _dot3_bd16(a, b, bmask16):
    rows = a.shape[0]
    a_hi, a_lo = _split2(a)
    b_hi, b_lo = _split2(b)
    lhs = jnp.concatenate([a_hi, a_lo], axis=0)
    per_tile = 128 // SUB
    outs = []
    for t in range(GROUP // 128):
        lanes = slice(t * 128, (t + 1) * 128)
        mask = bmask16[lanes, lanes]
        rhs = jnp.concatenate([jnp.concatenate([b_hi[:, lanes]] * per_tile, axis=0) * mask,
                               jnp.concatenate([b_lo[:, lanes]] * per_tile, axis=0) * mask], axis=1)
        z = _dot(lhs[:, lanes], rhs)
        outs.append(z[:rows, :128] + z[rows:, :128] + z[:rows, 128:])
    return jnp.concatenate(outs, axis=1)


def _unit_triangular_inverse(l_mats, bmask, bmask16, eye16, diag16, off_a, off_b):
    nq = CHUNK // SUB
    l16 = [sum(l[q * SUB:(q + 1) * SUB] * diag16[q * SUB:(q + 1) * SUB] for q in range(nq)).astype(BF16)
           for l in l_mats]
    t16 = [eye16 + x.astype(F32) for x in l16]
    l_pow = [_square_bd16(x, bmask16) for x in l16]
    for _ in range(2):
        both = [_dot3_bd16(jnp.concatenate([t, lp], axis=0), lp, bmask16) for t, lp in zip(t16, l_pow)]
        t16 = [t + bo[:SUB] for t, bo in zip(t16, both)]
        l_pow = [bo[SUB:] for bo in both]
    t16 = [t + _dot3_bd16(t, lp, bmask16) for t, lp in zip(t16, l_pow)]
    d = [jnp.concatenate([t] * nq, axis=0) * diag16 for t in t16]
    for off in (off_a, off_b):
        x = [_dot(di.astype(BF16), _block_diag(l * off, bmask)) for di, l in zip(d, l_mats)]
        d = [di + _dot(xi.astype(BF16), _block_diag(di, bmask)) for di, xi in zip(d, x)]
    return d


PAIR = 2


def _scan_kernel(r_ref, k_ref, v_ref, zw_ref, za_ref, kk_ref, ka_ref,
                 bmask_ref, bmask16_ref, tri_ref, dm_ref, cm_ref, eye16_ref,
                 y_ref, s_ref, p_ref, rh_ref, q_ref, yl_ref):
    ng = r_ref.shape[2] // GROUP
    bmask = bmask_ref[...]
    di = pl.program_id(0)

    @pl.when(pl.program_id(2) == 0)
    def _():
        s_ref[...] = jnp.zeros_like(s_ref)
        p_ref[...] = jnp.zeros_like(p_ref)
        rh_ref[...] = jnp.zeros_like(rh_ref)
        q_ref[...] = jnp.zeros_like(q_ref)
        yl_ref[...] = jnp.zeros_like(yl_ref)

    for step in range(PAIR):
        h = jnp.where(di == 0, step, PAIR - 1 - step)
        row0 = pl.multiple_of(h * CHUNK, CHUNK)
        for g in range(ng):
            cols = slice(g * GROUP, (g + 1) * GROUP)
            s_bd = _block_diag(s_ref[:, cols], bmask)
            out = _dot(jnp.concatenate([p_ref[h, :, cols], rh_ref[h, :, cols]], axis=0), s_bd)
            s_ref[:, cols] = out[:CHUNK] + q_ref[h, :, cols]
            y_ref[0, 0, pl.ds(row0, CHUNK), cols] = (out[CHUNK:] + yl_ref[h, :, cols]).astype(y_ref.dtype)

    eye = cm_ref[0]
    m_strict = dm_ref[0, 0]
    m_incl = dm_ref[0, 1]
    chains = [(h, g) for h in range(PAIR) for g in range(ng)]
    pieces = lambda x: [x[h * CHUNK:(h + 1) * CHUNK, g * GROUP:(g + 1) * GROUP] for h, g in chains]
    halves = lambda f, x: jnp.concatenate([f(x[h * CHUNK:(h + 1) * CHUNK]) for h in range(PAIR)], axis=0)
    bd = lambda x: _block_diag(x, bmask)
    stack = lambda x, y: jnp.concatenate([x, y], axis=0)

    r = r_ref[0].astype(F32)
    k = k_ref[0].astype(F32)
    v = v_ref[0].astype(F32)
    lw = (-math.exp(-0.5)) * jax.nn.sigmoid(zw_ref[0].astype(F32))
    a = jax.nn.sigmoid(za_ref[0].astype(F32))
    kkr = k * kk_ref[...]
    sq = jnp.concatenate(pieces(kkr * kkr), axis=0)
    sq_hi, sq_lo = _split2(sq)
    ssq = _dot(sq_hi, bmask) + _dot(sq_lo, bmask)
    ssq = jnp.concatenate(
        [jnp.concatenate([ssq[(h * ng + g) * CHUNK:(h * ng + g + 1) * CHUNK] for g in range(ng)], axis=1)
         for h in range(PAIR)], axis=0)
    kk = kkr * lax.rsqrt(jnp.maximum(ssq, 1e-24))
    kd = k * (1.0 + (a - 1.0) * ka_ref[...])
    bb = kk * a
    tri = tri_ref[0]

    def cumulative(x):
        x_hi, x_lo = _split2(x)
        return _dot(tri, x_hi) + _dot(tri, x_lo)

    g_cum = halves(cumulative, lw)
    g_end = halves(lambda x: jnp.broadcast_to(jnp.sum(x, axis=0, keepdims=True), x.shape), lw)
    e_neg = jnp.exp(-g_cum)
    e_end = jnp.exp(g_end - g_cum)
    a_t = pieces(-kk * jnp.exp(g_cum - lw))
    r_t = pieces(r * jnp.exp(g_cum))
    b_t = pieces(bb * e_neg)
    k_t = pieces(kd * e_neg)
    b_h = pieces((bb * e_end).astype(BF16))
    k_h = pieces((kd * e_end).astype(BF16))
    v_g = pieces(v)
    decay_end = [x[0:1] for x in pieces(jnp.exp(g_end))]

    ar = [stack(x, y).astype(BF16) for x, y in zip(a_t, r_t)]
    mbk = [_dot_bd_pair(x, y, z, bmask, transposed=True) for x, y, z in zip(ar, b_t, k_t)]
    mb = [x[0] for x in mbk]
    mk = [x[1] for x in mbk]
    l_mat = [x[:CHUNK] * m_strict for x in mb]
    m_rb = [(x[CHUNK:] * m_incl).astype(BF16) for x in mb]
    m_k = [stack(x[:CHUNK] * m_strict, x[CHUNK:] * m_incl).astype(BF16) for x in mk]
    mv = [_dot(x, bd(y)) for x, y in zip(m_k, v_g)]
    t_mat = _unit_triangular_inverse(l_mat, bmask, bmask16_ref[...], eye16_ref[...],
                                     cm_ref[1], cm_ref[2], cm_ref[3])
    t_g = [stack(t, _dot(m, bd(t))).astype(BF16) for t, m in zip(t_mat, m_rb)]
    tau = [_dot_bd_pair(x, y, z[:CHUNK], bmask) for x, y, z in zip(t_g, a_t, mv)]
    ta = [x[0] for x in tau]
    tu = [x[1] for x in tau]
    a_h = [x[:CHUNK].astype(BF16) for x in ta]
    u_0 = [x[:CHUNK] for x in tu]
    rh = [x + y[CHUNK:] for x, y in zip(r_t, ta)]
    yl = [x[CHUNK:] + y[CHUNK:] for x, y in zip(tu, mv)]
    p_full = [_dot_tn(x, y) for x, y in zip(b_h, a_h)]
    q_full = [_dot_tn(stack(x, y), stack(u, w).astype(BF16)) for x, y, u, w in zip(b_h, k_h, u_0, v_g)]
    for i, (h, g) in enumerate(chains):
        cols = slice(g * GROUP, (g + 1) * GROUP)
        p_ref[h, :, cols] = (_fold_heads(p_full[i]) + eye * decay_end[i]).astype(p_ref.dtype)
        rh_ref[h, :, cols] = rh[i].astype(rh_ref.dtype)
        q_ref[h, :, cols] = _fold_heads(q_full[i])
        yl_ref[h, :, cols] = yl[i]


def _scan_consts():
    lane = np.arange(GROUP)
    bmask = (lane[:, None] // HEAD == lane[None, :] // HEAD).astype(np.float32)
    i = np.arange(CHUNK)[:, None]
    j = np.arange(CHUNK)[None, :]
    jl = (lane % HEAD)[None, :]
    tri = np.stack([(j <= i), (j >= i)]).astype(np.float32)
    dir_masks = np.stack([np.stack([(jl < i), (jl <= i)]),
                          np.stack([(jl > i), (jl >= i)])]).astype(np.float32)
    same16 = (jl // SUB == i // SUB)
    same32 = (jl // (2 * SUB) == i // (2 * SUB))
    common = np.stack([(jl == i), same16, same32 & ~same16, ~same32]).astype(np.float32)
    bmask16 = (lane[:, None] // SUB == lane[None, :] // SUB).astype(np.float32)
    eye16 = ((lane % SUB)[None, :] == np.arange(SUB)[:, None]).astype(np.float32)
    return (jnp.asarray(bmask, BF16), jnp.asarray(bmask16, BF16), jnp.asarray(tri, BF16),
            jnp.asarray(dir_masks, F32), jnp.asarray(common, F32), jnp.asarray(eye16, F32))


def _scan(r, k, v, zw, za, k_k, k_a, n_ctx_chunks):
    b, l, d = r.shape
    rows = PAIR * CHUNK
    npair = l // rows
    assert l % rows == 0 and n_ctx_chunks % PAIR == 0
    n_ctx = n_ctx_chunks // PAIR
    bmask, bmask16, tri, dir_masks, common, eye16 = _scan_consts()

    def pair_of(di, s):
        back = jnp.where(s < n_ctx, n_ctx - 1 - s, npair - 1 + n_ctx - s)
        return jnp.where(di == 0, s, back)

    fold_pair = lambda di, s: pair_of(di, jnp.minimum(s, npair - 1))
    apply_pair = lambda di, s: pair_of(di, jnp.maximum(s - 1, 0))
    tok = pl.BlockSpec((1, rows, d), lambda di, bi, s: (bi, fold_pair(di, s), 0))
    tok2 = pl.BlockSpec((1, rows, d), lambda di, bi, s: (bi, fold_pair(di, s), di))
    par = pl.BlockSpec((1, d), lambda di, bi, s: (0, 0))
    return pl.pallas_call(
        _scan_kernel,
        out_shape=jax.ShapeDtypeStruct((2, b, l, d), BF16),
        grid=(2, b, npair + 1),
        in_specs=[tok, tok, tok, tok2, tok2, par, par,
                  pl.BlockSpec((GROUP, GROUP), lambda di, bi, s: (0, 0)),
                  pl.BlockSpec((GROUP, GROUP), lambda di, bi, s: (0, 0)),
                  pl.BlockSpec((1, CHUNK, CHUNK), lambda di, bi, s: (di, 0, 0)),
                  pl.BlockSpec((1, 2, CHUNK, GROUP), lambda di, bi, s: (di, 0, 0, 0)),
                  pl.BlockSpec((4, CHUNK, GROUP), lambda di, bi, s: (0, 0, 0)),
                  pl.BlockSpec((SUB, GROUP), lambda di, bi, s: (0, 0))],
        out_specs=pl.BlockSpec((1, 1, rows, d), lambda di, bi, s: (di, bi, apply_pair(di, s), 0)),
        scratch_shapes=[pltpu.VMEM((CHUNK, d), F32), pltpu.VMEM((PAIR, CHUNK, d), BF16),
                        pltpu.VMEM((PAIR, CHUNK, d), BF16), pltpu.VMEM((PAIR, CHUNK, d), F32),
                        pltpu.VMEM((PAIR, CHUNK, d), F32)],
        compiler_params=_params(("parallel", "parallel", "arbitrary")),
        name="scan",
    )(r, k, v, zw, za, k_k.reshape(1, d), k_a.reshape(1, d), bmask, bmask16, tri, dir_masks, common, eye16)


def _head_sum(x, ones_bd):
    hi, lo = _split2(x)
    return _dot(hi, ones_bd) + _dot(lo, ones_bd)


def _rwkv_post_kernel(y_ref, r_ref, k_ref, v_ref, za0_ref, za1_ref, gate_ref,
                      ka_ref, rk_ref, gnw_ref, gnb_ref, ones_ref, o_ref):
    ones_bd = ones_ref[...]
    tm, d = o_ref.shape[1], o_ref.shape[2]
    ng = d // GROUP
    to_rows = lambda x: jnp.concatenate([x[:, g * GROUP:(g + 1) * GROUP] for g in range(ng)], axis=0)
    to_cols = lambda x: jnp.concatenate([x[g * tm:(g + 1) * tm] for g in range(ng)], axis=1)
    head_mean = lambda x: to_cols(_head_sum(to_rows(x), ones_bd)) * (1.0 / HEAD)
    y = y_ref[0, 0].astype(F32) + y_ref[1, 0].astype(F32)
    yc = y - head_mean(y)
    var = head_mean(yc * yc)
    o = yc * lax.rsqrt(var + GN_EPS) * gnw_ref[...] + gnb_ref[...]
    a_sum = jax.nn.sigmoid(za0_ref[0].astype(F32)) + jax.nn.sigmoid(za1_ref[0].astype(F32))
    r = r_ref[0].astype(F32)
    k_sum = k_ref[0].astype(F32) * (2.0 + (a_sum - 2.0) * ka_ref[...])
    bonus = head_mean(r * k_sum * rk_ref[...]) * float(HEAD) * v_ref[0].astype(F32)
    o_ref[0] = ((o + bonus) * gate_ref[0].astype(F32)).astype(o_ref.dtype)


def _rwkv_post(y, r, k, v, za, gate, k_a, r_k, gn_w, gn_b, n_ctx, tm=256):
    _, b, l, d = y.shape
    t = l - n_ctx
    off = n_ctx // tm
    ones_bd = _scan_consts()[0]
    tok = pl.BlockSpec((1, tm, d), lambda bi, s: (bi, s + off, 0))
    par = pl.BlockSpec((1, d), lambda bi, s: (0, 0))
    return pl.pallas_call(
        _rwkv_post_kernel,
        out_shape=jax.ShapeDtypeStruct((b, t, d), BF16),
        grid=(b, t // tm),
        in_specs=[pl.BlockSpec((2, 1, tm, d), lambda bi, s: (0, bi, s + off, 0)),
                  tok, tok, tok,
                  pl.BlockSpec((1, tm, d), lambda bi, s: (bi, s + off, 0)),
                  pl.BlockSpec((1, tm, d), lambda bi, s: (bi, s + off, 1)),
                  tok, par, par, par, par,
                  pl.BlockSpec((GROUP, GROUP), lambda bi, s: (0, 0))],
        out_specs=pl.BlockSpec((1, tm, d), lambda bi, s: (bi, s, 0)),
        compiler_params=_params(("parallel", "parallel")),
        name="rwkv_post",
    )(y, r, k, v, za, za, gate, k_a.reshape(1, d), r_k.reshape(1, d), gn_w.reshape(1, d),
      gn_b.reshape(1, d), ones_bd)


def _proj_norm_kernel(a_ref, w_ref, x_ref, pos_ref, g_ref, lnw_ref, lnb_ref, sc_ref, sh_ref,
                      o_ref, h_ref, acc_ref, *, alpha):
    j = pl.program_id(2)
    nj = acc_ref.shape[0]
    tn = acc_ref.shape[2]
    acc_ref[j] = _dot(a_ref[...], w_ref[...].astype(BF16))

    @pl.when(j == nj - 1)
    def _():
        gate = g_ref[0]
        for jj in range(nj):
            cols = slice(jj * tn, (jj + 1) * tn)
            o_ref[:, cols] = alpha * (x_ref[:, cols] + pos_ref[:, cols]) + gate[:, cols] * acc_ref[jj]
        x_new = _layer_norm(o_ref[...], lnw_ref[...], lnb_ref[...])
        o_ref[...] = x_new
        h_ref[...] = (x_new * (1.0 + sc_ref[0]) + sh_ref[0]).astype(h_ref.dtype)


def _proj_norm(a, w, x, pos, gate, ln_w, ln_b, sc, sh, alpha, tm=512, tn=512):
    m, k = a.shape
    d = w.shape[1]
    t = pos.shape[0]
    tm = _tile(t, tm)
    tn = _tile(d, tn)
    tpb = t // tm
    nb = m // t
    rows = lambda ti, bi, j: (bi * tpb + ti, 0)
    vec = pl.BlockSpec((1, 1, d), lambda ti, bi, j: (bi, 0, 0))
    return pl.pallas_call(
        functools.partial(_proj_norm_kernel, alpha=alpha),
        out_shape=[jax.ShapeDtypeStruct((m, d), F32), jax.ShapeDtypeStruct((m, d), BF16)],
        grid=(tpb, nb, d // tn),
        in_specs=[pl.BlockSpec((tm, k), rows),
                  pl.BlockSpec((k, tn), lambda ti, bi, j: (0, j)),
                  pl.BlockSpec((tm, d), rows),
                  pl.BlockSpec((tm, d), lambda ti, bi, j: (ti, 0)),
                  vec,
                  pl.BlockSpec((1, d), lambda ti, bi, j: (0, 0)),
                  pl.BlockSpec((1, d), lambda ti, bi, j: (0, 0)),
                  vec, vec],
        out_specs=[pl.BlockSpec((tm, d), rows)] * 2,
        scratch_shapes=[pltpu.VMEM((d // tn, tm, tn), F32)],
        compiler_params=_params(("parallel", "parallel", "arbitrary")),
        name="proj_norm",
    )(a, w, x, pos, gate, ln_w.reshape(1, d), ln_b.reshape(1, d), sc, sh)


def _swiglu_halves(h, w1_ref, w3_ref, w2_ref, lead):
    tf = w1_ref.shape[-1]
    halves = [slice(0, tf // 2), slice(tf // 2, tf)]
    a1 = [_dot(h, w1_ref[lead + (slice(None), c)].astype(BF16)) for c in halves]
    a3 = [_dot(h, w3_ref[lead + (slice(None), c)].astype(BF16)) for c in halves]
    u = [(x * jax.nn.sigmoid(x) * y).astype(BF16) for x, y in zip(a1, a3)]
    y = [_dot(x, w2_ref[lead + (c, slice(None))].astype(BF16)) for x, c in zip(u, halves)]
    return y[0] + y[1]


def _ffn_kernel(h_ref, w1_ref, w3_ref, w2_ref, o_ref, acc_ref):
    j = pl.program_id(1)

    @pl.when(j == 0)
    def _():
        acc_ref[...] = jnp.zeros_like(acc_ref)

    acc_ref[...] += _swiglu_halves(h_ref[...], w1_ref, w3_ref, w2_ref, ())

    @pl.when(j == pl.num_programs(1) - 1)
    def _():
        o_ref[...] = acc_ref[...].astype(o_ref.dtype)


def _ffn(h, w1, w3, w2, tm=1024, tf=512):
    m, d = h.shape
    f = w1.shape[1]
    tm = _tile(m, tm)
    tf = _tile(f, tf)
    return pl.pallas_call(
        _ffn_kernel,
        out_shape=jax.ShapeDtypeStruct((m, d), BF16),
        grid=(m // tm, f // tf),
        in_specs=[pl.BlockSpec((tm, d), lambda i, j: (i, 0)),
                  pl.BlockSpec((d, tf), lambda i, j: (0, j)),
                  pl.BlockSpec((d, tf), lambda i, j: (0, j)),
                  pl.BlockSpec((tf, d), lambda i, j: (j, 0))],
        out_specs=pl.BlockSpec((tm, d), lambda i, j: (i, 0)),
        scratch_shapes=[pltpu.VMEM((tm, d), F32)],
        compiler_params=_params(("parallel", "arbitrary")),
        name="ffn",
    )(h, w1, w3, w2)


def _residual_norm_kernel(x_ref, y_ref, g_ref, lnw_ref, lnb_ref, o_ref, *, alpha):
    z = alpha * x_ref[...] + g_ref[0] * y_ref[...].astype(F32)
    o_ref[...] = _layer_norm(z, lnw_ref[...], lnb_ref[...])


def _residual_norm(x, y, gate, ln_w, ln_b, alpha, rows_per_batch, tm=512):
    m, d = x.shape
    tm = _tile(rows_per_batch, tm)
    tpb = rows_per_batch // tm
    tok = pl.BlockSpec((tm, d), lambda i: (i, 0))
    row = pl.BlockSpec((1, d), lambda i: (0, 0))
    return pl.pallas_call(
        functools.partial(_residual_norm_kernel, alpha=alpha),
        out_shape=jax.ShapeDtypeStruct((m, d), F32),
        grid=(m // tm,),
        in_specs=[tok, tok, pl.BlockSpec((1, 1, d), lambda i: (i // tpb, 0, 0)), row, row],
        out_specs=tok,
        compiler_params=_params(("parallel",)),
        name="residual_norm",
    )(x, y, gate, ln_w.reshape(1, d), ln_b.reshape(1, d))


def _shift_down(x, s, row):
    return jnp.where(row >= s, pltpu.roll(x, s, axis=0), 0.0)


def _shift_up(x, s, row):
    t = x.shape[0]
    return jnp.where(row < t - s, pltpu.roll(x, t - s, axis=0), 0.0)


def _pool_kernel(x_ref, sc_ref, sh_ref, w_ref, scale_ref, o_ref):
    g = pl.program_id(0)
    t = x_ref.shape[1]
    h = x_ref[0] * (1.0 + sc_ref[0]) + sh_ref[0]
    row = lax.broadcasted_iota(jnp.int32, h.shape, 0)
    w = w_ref[0].astype(BF16)
    for gi, win in enumerate(POOL_WINDOWS):
        @pl.when(g == gi)
        def _(win=win):
            half = win // 2
            back = h
            fwd = h
            m = 1
            while m < half:
                back = back + _shift_down(back, m, row)
                fwd = fwd + _shift_up(fwd, m, row)
                m *= 2
            total = _shift_down(back, 1, row) + fwd
            count = (jnp.minimum(row + half, t) - jnp.maximum(row - half, 0)).astype(F32)
            pooled = (total / count - h).astype(BF16)
            o_ref[0] = _dot(pooled, w) * scale_ref[...]


def _pool(x, sc, sh, w_pool, scale):
    b, t, d = x.shape
    ng, p, _ = w_pool.shape
    vec = pl.BlockSpec((1, 1, p), lambda g, bi: (bi, 0, g))
    return pl.pallas_call(
        _pool_kernel,
        out_shape=jax.ShapeDtypeStruct((b, t, d), F32),
        grid=(ng, b),
        in_specs=[pl.BlockSpec((1, t, p), lambda g, bi: (bi, 0, g)), vec, vec,
                  pl.BlockSpec((1, p, p), lambda g, bi: (g, 0, 0)),
                  pl.BlockSpec((1, p), lambda g, bi: (0, g))],
        out_specs=pl.BlockSpec((1, t, p), lambda g, bi: (bi, 0, g)),
        compiler_params=_params(("parallel", "parallel")),
        name="pool",
    )(x, sc, sh, w_pool, scale.reshape(1, d))


def _route_kernel(x_ref, y_ref, g_ref, lnw_ref, lnb_ref, sc_ref, sh_ref, rt_ref,
                  xo_ref, h_ref, route_ref, *, alpha):
    x = _layer_norm(alpha * x_ref[...] + g_ref[0] * y_ref[...], lnw_ref[...], lnb_ref[...])
    xo_ref[...] = x
    h = x * (1.0 + sc_ref[0]) + sh_ref[0]
    h_ref[...] = h
    r1, r2, r3 = _split3(rt_ref[...])
    h1, h2, h3 = _split3(h)
    logits = (_dot_nt(r1, h1) + (_dot_nt(r1, h2) + _dot_nt(r2, h1))
              + (_dot_nt(r1, h3) + _dot_nt(r2, h2) + _dot_nt(r3, h1)))
    mx = jnp.max(logits, axis=0, keepdims=True)
    e = jnp.exp(logits - mx)
    p = e / jnp.sum(e, axis=0, keepdims=True)
    idx = lax.broadcasted_iota(jnp.int32, p.shape, 0)
    p1 = jnp.max(p, axis=0, keepdims=True)
    i1 = jnp.min(jnp.where(p == p1, idx, N_EXPERTS), axis=0, keepdims=True)
    rest = jnp.where(idx == i1, -1.0, p)
    p2 = jnp.max(rest, axis=0, keepdims=True)
    i2 = jnp.min(jnp.where(rest == p2, idx, N_EXPERTS), axis=0, keepdims=True)
    den = p1 + p2
    out = jnp.where(idx == 0, i1.astype(F32), 0.0)
    out = jnp.where(idx == 1, i2.astype(F32), out)
    out = jnp.where(idx == 2, p1 / den, out)
    out = jnp.where(idx == 3, p2 / den, out)
    route_ref[...] = out


def _route(x, y, gate, ln_w, ln_b, sc, sh, router, alpha, rows_per_batch, tm=256):
    m, d = x.shape
    tpb = rows_per_batch // tm
    vec = pl.BlockSpec((1, 1, d), lambda i: (i // tpb, 0, 0))
    row = pl.BlockSpec((1, d), lambda i: (0, 0))
    tok = pl.BlockSpec((tm, d), lambda i: (i, 0))
    return pl.pallas_call(
        functools.partial(_route_kernel, alpha=alpha),
        out_shape=[jax.ShapeDtypeStruct((m, d), F32), jax.ShapeDtypeStruct((m, d), F32),
                   jax.ShapeDtypeStruct((N_EXPERTS, m), F32)],
        grid=(m // tm,),
        in_specs=[tok, tok, vec, row, row, vec, vec,
                  pl.BlockSpec((N_EXPERTS, d), lambda i: (0, 0))],
        out_specs=[tok, tok, pl.BlockSpec((N_EXPERTS, tm), lambda i: (0, i))],
        compiler_params=_params(("parallel",)),
        name="route",
    )(x, y, gate, ln_w.reshape(1, d), ln_b.reshape(1, d), sc, sh, router.T)


def _row_copy(src_hbm, dst_vmem, sem, src_row, dst_row):
    return pltpu.make_async_copy(src_hbm.at[pl.ds(src_row, 1)], dst_vmem.at[pl.ds(dst_row, 1)], sem)


def _moe_kernel(ce_ref, nv_ref, idx_ref, h_ref, w1_ref, w3_ref, w2_ref, o_ref,
                stage_ref, work_ref, gather_sem, *, share):
    c = pl.program_id(0)
    j = pl.program_id(1)
    n_chunks = pl.num_programs(0)
    nj = pl.num_programs(1)
    nv = nv_ref[c]
    full = work_ref.shape[0]
    stage_rows = stage_ref.shape[0]

    def gather(chunk, r):
        return _row_copy(h_ref, stage_ref, gather_sem, idx_ref[chunk * full + r], r)

    def wait_all_gathers():
        def body(r, carry):
            gather(0, r).wait()
            return carry
        lax.fori_loop(0, stage_rows, body, 0, unroll=8)

    @pl.when(j == 0)
    def _():
        @pl.when(c == 0)
        def _():
            def body(r, carry):
                gather(0, r).start()
                return carry
            lax.fori_loop(0, stage_rows, body, 0, unroll=8)

        o_ref[...] = jnp.zeros_like(o_ref)
        wait_all_gathers()
        work_ref[...] = stage_ref[0:full, :].astype(work_ref.dtype)

    def request_next_share():
        for i in range(share):
            gather(c + 1, j * share + i).start(priority=1)

    def swiglu_rows(rows):
        o_ref[rows, :] += _swiglu_halves(work_ref[rows, :], w1_ref, w3_ref, w2_ref, (0,))

    n_sub = (nv + (MOE_SUB - 1)) // MOE_SUB
    for k in range(full // MOE_SUB + 1):
        @pl.when(n_sub == k)
        def _(k=k):
            request_next_share()
            if k:
                swiglu_rows(slice(0, k * MOE_SUB))

    @pl.when(jnp.logical_and(j == nj - 1, c == n_chunks - 1))
    def _():
        wait_all_gathers()


def _moe_ffn(h, src_idx, chunk_expert, chunk_valid, w1, w3, w2, n_chunks, tf=512):
    d = h.shape[1]
    f = w1.shape[2]
    tf = _tile(f, tf)
    nj = f // tf
    share = -(-MOE_CHUNK // nj)
    share = -(-share // 8) * 8
    stage_rows = share * nj
    assert src_idx.shape[0] >= n_chunks * MOE_CHUNK + stage_rows

    def jeff(c, j, nv):
        return jnp.where(nv[c] > 0, j, nj - 1)

    return pl.pallas_call(
        functools.partial(_moe_kernel, share=share),
        out_shape=jax.ShapeDtypeStruct((n_chunks * MOE_CHUNK, d), F32),
        grid_spec=pltpu.PrefetchScalarGridSpec(
            num_scalar_prefetch=3,
            grid=(n_chunks, nj),
            in_specs=[pl.BlockSpec(memory_space=pl.ANY),
                      pl.BlockSpec((1, d, tf), lambda c, j, ce, nv, ix: (ce[c], 0, jeff(c, j, nv))),
                      pl.BlockSpec((1, d, tf), lambda c, j, ce, nv, ix: (ce[c], 0, jeff(c, j, nv))),
                      pl.BlockSpec((1, tf, d), lambda c, j, ce, nv, ix: (ce[c], jeff(c, j, nv), 0))],
            out_specs=pl.BlockSpec((MOE_CHUNK, d), lambda c, j, ce, nv, ix: (c, 0)),
            scratch_shapes=[pltpu.VMEM((stage_rows, d), F32), pltpu.VMEM((MOE_CHUNK, d), BF16),
                            pltpu.SemaphoreType.DMA]),
        compiler_params=_params(("arbitrary", "arbitrary"), MOE_VMEM_LIMIT_BYTES),
        name="moe_ffn",
    )(chunk_expert, chunk_valid, src_idx, h, w1, w3, w2)


def _combine_kernel(p0_ref, p1_ref, ys_ref, x_ref, gates_ref, g_ref, lnw_ref, lnb_ref,
                    o_ref, b0_ref, b1_ref, sem, *, alpha):
    i = pl.program_id(0)
    n = pl.num_programs(0)
    rows = b0_ref.shape[1]
    slot = i % 2

    def copies(tile, to_slot, r):
        return (_row_copy(ys_ref, b0_ref.at[to_slot], sem.at[to_slot], p0_ref[tile * rows + r], r),
                _row_copy(ys_ref, b1_ref.at[to_slot], sem.at[to_slot], p1_ref[tile * rows + r], r))

    def wait_slot(to_slot):
        def body(r, carry):
            for cp in copies(0, to_slot, r):
                cp.wait()
            return carry
        lax.fori_loop(0, rows, body, 0, unroll=4)

    @pl.when(i == 0)
    def _():
        def body(r, carry):
            for cp in copies(0, 0, r):
                cp.start()
            return carry
        lax.fori_loop(0, rows, body, 0, unroll=4)

    wait_slot(slot)
    for r in range(rows):
        for cp in copies(i + 1, 1 - slot, r):
            cp.start()
    gates = gates_ref[...]
    y = gates[:, 0:1] * b0_ref[slot] + gates[:, 1:2] * b1_ref[slot]
    z = alpha * x_ref[...] + g_ref[0] * y
    o_ref[...] = _layer_norm(z, lnw_ref[...], lnb_ref[...])

    @pl.when(i == n - 1)
    def _():
        wait_slot(1 - slot)


def _combine(ys, pos0, pos1, gates, x, gate_vec, ln_w, ln_b, alpha, rows_per_batch, rows=256):
    m, d = x.shape
    tpb = rows_per_batch // rows
    tok = lambda i, a, b: (i, 0)
    return pl.pallas_call(
        functools.partial(_combine_kernel, alpha=alpha),
        out_shape=jax.ShapeDtypeStruct((m, d), F32),
        grid_spec=pltpu.PrefetchScalarGridSpec(
            num_scalar_prefetch=2,
            grid=(m // rows,),
            in_specs=[pl.BlockSpec(memory_space=pl.ANY),
                      pl.BlockSpec((rows, d), tok),
                      pl.BlockSpec((rows, 2), tok),
                      pl.BlockSpec((1, 1, d), lambda i, a, b: (i // tpb, 0, 0)),
                      pl.BlockSpec((1, d), lambda i, a, b: (0, 0)),
                      pl.BlockSpec((1, d), lambda i, a, b: (0, 0))],
            out_specs=pl.BlockSpec((rows, d), tok),
            scratch_shapes=[pltpu.VMEM((2, rows, d), F32), pltpu.VMEM((2, rows, d), F32),
                            pltpu.SemaphoreType.DMA((2,))]),
        compiler_params=_params(("arbitrary",)),
        name="moe_combine",
    )(pos0, pos1, ys, x, gates, gate_vec, ln_w.reshape(1, d), ln_b.reshape(1, d))


def _routing_tables(route, n_chunks, table_len):
    n = route.shape[1]
    experts = jnp.concatenate([route[0], route[1]]).astype(jnp.int32)
    onehot = (experts[:, None] == jnp.arange(N_EXPERTS, dtype=jnp.int32)[None, :]).astype(jnp.int32)
    csum = jnp.cumsum(onehot, axis=0)
    rank = jnp.sum((csum - 1) * onehot, axis=1)
    counts = csum[-1]
    chunks_e = (counts + MOE_CHUNK - 1) // MOE_CHUNK
    chunk_end = jnp.cumsum(chunks_e)
    chunk_start = chunk_end - chunks_e
    dest = (chunk_start * MOE_CHUNK)[experts] + rank
    token = jnp.arange(2 * n, dtype=jnp.int32) % n
    src_idx = jnp.zeros((table_len,), jnp.int32).at[dest].set(token)
    cid = jnp.arange(n_chunks, dtype=jnp.int32)
    used = cid < chunk_end[-1]
    last_used = jnp.maximum(chunk_end[-1] - 1, 0)
    ce = jnp.sum((jnp.minimum(cid, last_used)[:, None] >= chunk_end[None, :]).astype(jnp.int32), axis=1)
    ce = jnp.minimum(ce, N_EXPERTS - 1)
    nvalid = jnp.clip(counts[ce] - (cid - chunk_start[ce]) * MOE_CHUNK, 0, MOE_CHUNK)
    nvalid = jnp.where(used, nvalid, 0).astype(jnp.int32)
    return src_idx, ce, nvalid, dest[:n], dest[n:]


def _position_embedding(rows, width, d):
    quarter = d // 4
    omega = 1.0 / (POS_BASE ** (jnp.arange(quarter, dtype=F32) / quarter))
    ar = jnp.arange(rows, dtype=F32)[:, None] * omega[None, :]
    ac = jnp.arange(width, dtype=F32)[:, None] * omega[None, :]
    row_part = jnp.repeat(jnp.concatenate([jnp.sin(ar), jnp.cos(ar)], axis=-1), width, axis=0)
    col_part = jnp.tile(jnp.concatenate([jnp.sin(ac), jnp.cos(ac)], axis=-1), (rows, 1))
    return jnp.concatenate([row_part, col_part], axis=-1)


def _block_diag2(w):
    z = jnp.zeros_like(w[0])
    return jnp.concatenate([jnp.concatenate([w[0], z], axis=1), jnp.concatenate([z, w[1]], axis=1)], axis=0)


def kernel(x, c, ctx, c_ctx, w_mod, b_mod, ln_w, ln_b, rwkv_mu, rwkv_w_r, rwkv_w_k, rwkv_w_v, rwkv_w_o, rwkv_decay_w0, rwkv_decay_w1, rwkv_decay_w2, rwkv_iclr_a0, rwkv_iclr_a1, rwkv_iclr_a2, rwkv_gate_g1, rwkv_gate_g2, rwkv_k_k, rwkv_k_a, rwkv_r_k, rwkv_gn_w, rwkv_gn_b, pool_w, pool_scale, ffn_w1, ffn_w3, ffn_w2, moe_router, moe_w1, moe_w3, moe_w2):
    b, t, d = x.shape
    n_ctx = ctx.shape[1]
    depth = w_mod.shape[0]
    assert depth == 2 and rwkv_mu.shape[0] == 1 and pool_w.shape[0] == 1
    alpha = (2.0 * depth) ** 0.25
    grid_w = 64
    l = n_ctx + t
    n = b * t

    cond = jnp.zeros((8, d), F32).at[:b].set(c).at[b].set(c_ctx)
    mod = _adaln(cond, w_mod, b_mod).reshape(depth, 8, 6, d)
    lat = lambda layer, which: mod[layer, :b, which].reshape(b, 1, d)
    cvec = lambda layer, which: mod[layer, b, which].reshape(1, 1, d)

    pos = _position_embedding(t // grid_w, grid_w, d)

    xr, xw, xk, xv, xa, xg = _rwkv_mix(ctx, x, pos, lat(0, 1), lat(0, 0), cvec(0, 1), cvec(0, 0), rwkv_mu[0])
    flat = lambda a: a.reshape(b * l, a.shape[-1])
    r = _mm(flat(xr), rwkv_w_r[0], out_dtype=BF16)
    k = _mm(flat(xk), rwkv_w_k[0], out_dtype=BF16)
    v = _mm(flat(xv), rwkv_w_v[0], out_dtype=BF16)
    dw1 = jnp.concatenate([rwkv_decay_w1[0, 0], rwkv_decay_w1[0, 1]], axis=1)
    ia1 = jnp.concatenate([rwkv_iclr_a1[0, 0], rwkv_iclr_a1[0, 1]], axis=1)
    lora_w = _mm(flat(xw), dw1, act="tanh", out_dtype=BF16)
    zw = _mm(lora_w, _block_diag2(rwkv_decay_w2[0]), rwkv_decay_w0[0].reshape(1, 2 * d), out_dtype=BF16, tn=2048)
    lora_a = _mm(flat(xa), ia1, out_dtype=BF16)
    za = _mm(lora_a, _block_diag2(rwkv_iclr_a2[0]), rwkv_iclr_a0[0].reshape(1, 2 * d), out_dtype=BF16, tn=2048)
    lora_g = _mm(flat(xg), rwkv_gate_g1[0], act="sigmoid", out_dtype=BF16)
    gate = _mm(lora_g, rwkv_gate_g2[0], out_dtype=BF16, tn=2048)
    seq = lambda a: a.reshape(b, l, a.shape[-1])
    y_scan = _scan(seq(r), seq(k), seq(v), seq(zw), seq(za), rwkv_k_k[0], rwkv_k_a[0], n_ctx // CHUNK)
    og = _rwkv_post(y_scan, seq(r), seq(k), seq(v), seq(za), seq(gate), rwkv_k_a[0],
                    rwkv_r_k[0].reshape(d), rwkv_gn_w[0], rwkv_gn_b[0], n_ctx)
    x1, h1 = _proj_norm(og.reshape(n, d), rwkv_w_o[0].astype(BF16), x.reshape(n, d), pos, lat(0, 2),
                        ln_w[0, 0], ln_b[0, 0], lat(0, 4), lat(0, 3), alpha)
    y_ffn = _ffn(h1, ffn_w1[0], ffn_w3[0], ffn_w2[0])
    x2 = _residual_norm(x1, y_ffn, lat(0, 5), ln_w[0, 1], ln_b[0, 1], alpha, t)

    y_pool = _pool(x2.reshape(b, t, d), lat(1, 1), lat(1, 0), pool_w[0], pool_scale[0])
    x3, h3, route = _route(x2, y_pool.reshape(n, d), lat(1, 2), ln_w[1, 0], ln_b[1, 0],
                           lat(1, 4), lat(1, 3), moe_router[0], alpha, t)
    n_chunks = (2 * n + N_EXPERTS * (MOE_CHUNK - 1)) // MOE_CHUNK
    src_idx, chunk_expert, chunk_valid, pos0, pos1 = _routing_tables(route, n_chunks, (n_chunks + 2) * MOE_CHUNK)
    ys = _moe_ffn(h3, src_idx, chunk_expert, chunk_valid, moe_w1[0], moe_w3[0], moe_w2[0], n_chunks)
    gates = jnp.stack([route[2], route[3]], axis=1)
    spare = jnp.zeros((256,), jnp.int32)
    out = _combine(ys, jnp.concatenate([pos0, spare]), jnp.concatenate([pos1, spare]), gates, x3,
                   lat(1, 5), ln_w[1, 1], ln_b[1, 1], alpha, t)
    return out.reshape(b, t, d)
```

```python
import functools
import math

import numpy as np
import jax
import jax.numpy as jnp
from jax import lax
from jax.experimental import pallas as pl
from jax.experimental.pallas import tpu as pltpu

F32 = jnp.float32
BF16 = jnp.bfloat16

LANES = 128
HEAD = 64
GROUP = 4 * HEAD
CHUNK = 64
LN_EPS = 1e-5
GN_EPS = 64e-5
POS_BASE = 10000.0
POOL_WINDOWS = (2, 4, 8, 16)
N_EXPERTS = 8
MOE_CHUNK = 1024
MOE_SUB = 256
VMEM_LIMIT_BYTES = 56 * 1024 * 1024
MOE_VMEM_LIMIT_BYTES = 60 * 1024 * 1024


def _params(semantics, vmem_limit_bytes=VMEM_LIMIT_BYTES):
    return pltpu.CompilerParams(dimension_semantics=semantics, vmem_limit_bytes=vmem_limit_bytes)


def _tile(n, preferred):
    t = min(preferred, n)
    while n % t:
        t //= 2
    return t


def _dot(a, b):
    return jnp.dot(a, b, preferred_element_type=F32)


def _dot_nt(a, b):
    return lax.dot_general(a, b, (((1,), (1,)), ((), ())), preferred_element_type=F32)


def _dot_tn(a, b):
    return lax.dot_general(a, b, (((0,), (0,)), ((), ())), preferred_element_type=F32)


def _split2(x):
    hi = x.astype(BF16)
    lo = (x - hi.astype(F32)).astype(BF16)
    return hi, lo


def _split3(x):
    hi = x.astype(BF16)
    r1 = x - hi.astype(F32)
    mid = r1.astype(BF16)
    lo = (r1 - mid.astype(F32)).astype(BF16)
    return hi, mid, lo


def _layer_norm(z, w, b):
    mu = jnp.mean(z, axis=-1, keepdims=True)
    zc = z - mu
    var = jnp.mean(zc * zc, axis=-1, keepdims=True)
    return zc * lax.rsqrt(var + LN_EPS) * w + b


def _adaln_kernel(c_ref, w_ref, b_ref, o_ref):
    c = c_ref[...]
    a = (c * jax.nn.sigmoid(c)).astype(BF16)
    o_ref[0] = _dot(a, w_ref[0].astype(BF16)) + b_ref[0]


def _adaln(cond, w_mod, b_mod, tn=1024):
    depth, d, n = w_mod.shape
    tn = _tile(n, tn)
    rows = cond.shape[0]
    return pl.pallas_call(
        _adaln_kernel,
        out_shape=jax.ShapeDtypeStruct((depth, rows, n), F32),
        grid=(depth, n // tn),
        in_specs=[pl.BlockSpec((rows, d), lambda l, j: (0, 0)),
                  pl.BlockSpec((1, d, tn), lambda l, j: (l, 0, j)),
                  pl.BlockSpec((1, 1, tn), lambda l, j: (l, 0, j))],
        out_specs=pl.BlockSpec((1, rows, tn), lambda l, j: (l, 0, j)),
        compiler_params=_params(("parallel", "parallel")),
        name="adaln",
    )(cond, w_mod, b_mod.reshape(depth, 1, n))


def _mm_kernel(a_ref, w_ref, b_ref, o_ref, *, act):
    acc = _dot(a_ref[...], w_ref[...].astype(BF16)) + b_ref[...]
    if act == "tanh":
        acc = jnp.tanh(acc)
    elif act == "sigmoid":
        acc = jax.nn.sigmoid(acc)
    o_ref[...] = acc.astype(o_ref.dtype)


def _mm(a, w, bias=None, *, act=None, out_dtype=F32, tm=2304, tn=512):
    m, k = a.shape
    n = w.shape[1]
    tm = _tile(m, tm)
    tn = _tile(n, tn)
    if bias is None:
        bias = jnp.zeros((1, n), F32)
    return pl.pallas_call(
        functools.partial(_mm_kernel, act=act),
        out_shape=jax.ShapeDtypeStruct((m, n), out_dtype),
        grid=(m // tm, n // tn),
        in_specs=[pl.BlockSpec((tm, k), lambda i, j: (i, 0)),
                  pl.BlockSpec((k, tn), lambda i, j: (0, j)),
                  pl.BlockSpec((1, tn), lambda i, j: (0, j))],
        out_specs=pl.BlockSpec((tm, tn), lambda i, j: (i, j)),
        compiler_params=_params(("parallel", "arbitrary")),
        name="matmul",
    )(a, w, bias.reshape(1, n))


def _mix_kernel(ctx_ref, x_ref, xp_ref, xn_ref, pos_ref, pp_ref, pn_ref,
                sc_ref, sh_ref, csc_ref, csh_ref, mu_ref,
                o0, o1, o2, o3, o4, o5, *, n_lat_tiles):
    s = pl.program_id(0)
    is_ctx = s == 0
    tm = x_ref.shape[1]
    scale = jnp.where(is_ctx, csc_ref[0], sc_ref[0]) + 1.0
    shift = jnp.where(is_ctx, csh_ref[0], sh_ref[0])
    src = jnp.where(is_ctx, ctx_ref[0], x_ref[0] + pos_ref[...])
    h = src * scale + shift
    has_prev = s > 1
    has_next = jnp.logical_and(s >= 1, s < n_lat_tiles)
    h_prev = jnp.where(has_prev, (xp_ref[0] + pp_ref[...]) * scale + shift, 0.0)[7:8]
    h_next = jnp.where(has_next, (xn_ref[0] + pn_ref[...]) * scale + shift, 0.0)[0:1]
    row = lax.broadcasted_iota(jnp.int32, h.shape, 0)
    h_m1 = jnp.where(row == 0, h_prev, pltpu.roll(h, 1, axis=0))
    h_p1 = jnp.where(row == tm - 1, h_next, pltpu.roll(h, tm - 1, axis=0))
    xx = 0.5 * (h_m1 + h_p1) - h
    for n, o_ref in enumerate((o0, o1, o2, o3, o4, o5)):
        o_ref[0] = (h + xx * mu_ref[n:n + 1]).astype(o_ref.dtype)


def _rwkv_mix(ctx, x, pos, sc, sh, csc, csh, mu):
    b, t, d = x.shape
    tm = ctx.shape[1]
    assert t % tm == 0 and tm % 8 == 0
    n_lat = t // tm
    r8 = tm // 8
    lat = lambda s, bi: (bi, jnp.maximum(s - 1, 0), 0)
    prev8 = lambda s, bi: (bi, jnp.maximum((s - 1) * r8 - 1, 0), 0)
    next8 = lambda s, bi: (bi, jnp.minimum(jnp.maximum(s, 1) * r8, t // 8 - 1), 0)
    vec = pl.BlockSpec((1, 1, d), lambda s, bi: (bi, 0, 0))
    cvec = pl.BlockSpec((1, 1, d), lambda s, bi: (0, 0, 0))
    out_sds = jax.ShapeDtypeStruct((b, tm + t, d), BF16)
    return pl.pallas_call(
        functools.partial(_mix_kernel, n_lat_tiles=n_lat),
        out_shape=[out_sds] * 6,
        grid=(n_lat + 1, b),
        in_specs=[pl.BlockSpec((1, tm, d), lambda s, bi: (bi, 0, 0)),
                  pl.BlockSpec((1, tm, d), lat),
                  pl.BlockSpec((1, 8, d), prev8),
                  pl.BlockSpec((1, 8, d), next8),
                  pl.BlockSpec((tm, d), lambda s, bi: (jnp.maximum(s - 1, 0), 0)),
                  pl.BlockSpec((8, d), lambda s, bi: (jnp.maximum((s - 1) * r8 - 1, 0), 0)),
                  pl.BlockSpec((8, d), lambda s, bi: (jnp.minimum(jnp.maximum(s, 1) * r8, t // 8 - 1), 0)),
                  vec, vec, cvec, cvec,
                  pl.BlockSpec((6, d), lambda s, bi: (0, 0))],
        out_specs=[pl.BlockSpec((1, tm, d), lambda s, bi: (bi, s, 0))] * 6,
        compiler_params=_params(("parallel", "parallel")),
        name="rwkv_mix",
    )(ctx, x, x, x, pos, pos, pos, sc, sh, csc, csh, mu)


def _block_diag(x, bmask):
    xb = x.astype(BF16)
    zero = jnp.zeros((HEAD, GROUP // 2), BF16)
    rows = []
    for h in range(GROUP // HEAD):
        t = h // 2
        blk = xb[:, t * LANES:(t + 1) * LANES] * bmask[h * HEAD:(h + 1) * HEAD, t * LANES:(t + 1) * LANES]
        rows.append(jnp.concatenate([blk, zero] if t == 0 else [zero, blk], axis=1))
    return jnp.concatenate(rows, axis=0)


def _fold_heads(full):
    lane = lax.broadcasted_iota(jnp.int32, (HEAD, LANES), 1)
    tiles = []
    for t in range(GROUP // LANES):
        even = full[(2 * t) * HEAD:(2 * t + 1) * HEAD, t * LANES:(t + 1) * LANES]
        odd = full[(2 * t + 1) * HEAD:(2 * t + 2) * HEAD, t * LANES:(t + 1) * LANES]
        tiles.append(jnp.where(lane < HEAD, even, odd))
    return jnp.concatenate(tiles, axis=1)


def _dot_bd_pair(x, y1, y2, bmask, transposed=False):
    out1, out2 = [], []
    for t in range(GROUP // LANES):
        lanes = slice(t * LANES, (t + 1) * LANES)
        mask = bmask[lanes, lanes]
        b1 = jnp.concatenate([y1[:, lanes].astype(BF16)] * 2, axis=0) * mask
        b2 = jnp.concatenate([y2[:, lanes].astype(BF16)] * 2, axis=0) * mask
        if transposed:
            z = _dot_nt(x[:, lanes], jnp.concatenate([b1, b2], axis=0))
        else:
            z = _dot(x[:, lanes], jnp.concatenate([b1, b2], axis=1))
        out1.append(z[:, :LANES])
        out2.append(z[:, LANES:])
    return jnp.concatenate(out1, axis=1), jnp.concatenate(out2, axis=1)


SUB = 16


def _square_bd16(x, bmask16):
    per_tile = LANES // SUB
    tiles = [slice(t * LANES, (t + 1) * LANES) for t in range(GROUP // LANES)]
    lhs = jnp.concatenate([x[:, lanes] for lanes in tiles], axis=0)
    rhs = jnp.concatenate([jnp.concatenate([x[:, lanes]] * per_tile, axis=0) * bmask16[lanes, lanes]
                           for lanes in tiles], axis=1)
    z = _dot(lhs, rhs)
    return jnp.concatenate([z[t * SUB:(t + 1) * SUB, lanes] for t, lanes in enumerate(tiles)], axis=1)


def _dot3_bd16(a, b, bmask16):
    rows = a.shape[0]
    a_hi, a_lo = _split2(a)
    b_hi, b_lo = _split2(b)
    lhs = jnp.concatenate([a_hi, a_lo], axis=0)
    per_tile = LANES // SUB
    outs = []
    for t in range(GROUP // LANES):
        lanes = slice(t * LANES, (t + 1) * LANES)
        mask = bmask16[lanes, lanes]
        rhs = jnp.concatenate([jnp.concatenate([b_hi[:, lanes]] * per_tile, axis=0) * mask,
                               jnp.concatenate([b_lo[:, lanes]] * per_tile, axis=0) * mask], axis=1)
        z = _dot(lhs[:, lanes], rhs)
        outs.append(z[:rows, :LANES] + z[rows:, :LANES] + z[:rows, LANES:])
    return jnp.concatenate(outs, axis=1)


def _unit_triangular_inverse(l_mats, bmask, bmask16, eye16, diag16, off_a, off_b):
    nq = CHUNK // SUB
    l16 = [sum(l[q * SUB:(q + 1) * SUB] * diag16[q * SUB:(q + 1) * SUB] for q in range(nq)).astype(BF16)
           for l in l_mats]
    t16 = [eye16 + x.astype(F32) for x in l16]
    l_pow = [_square_bd16(x, bmask16) for x in l16]
    for _ in range(2):
        both = [_dot3_bd16(jnp.concatenate([t, lp], axis=0), lp, bmask16) for t, lp in zip(t16, l_pow)]
        t16 = [t + bo[:SUB] for t, bo in zip(t16, both)]
        l_pow = [bo[SUB:] for bo in both]
    t16 = [t + _dot3_bd16(t, lp, bmask16) for t, lp in zip(t16, l_pow)]
    d = [jnp.concatenate([t] * nq, axis=0) * diag16 for t in t16]
    for off in (off_a, off_b):
        x = [_dot(di.astype(BF16), _block_diag(l * off, bmask)) for di, l in zip(d, l_mats)]
        d = [di + _dot(xi.astype(BF16), _block_diag(di, bmask)) for di, xi in zip(d, x)]
    return d


PAIR = 2


def _scan_kernel(r_ref, k_ref, v_ref, zw_ref, za_ref, kk_ref, ka_ref,
                 bmask_ref, bmask16_ref, tri_ref, dm_ref, cm_ref, eye16_ref,
                 y_ref, s_ref, p_ref, rh_ref, q_ref, yl_ref):
    ng = r_ref.shape[2] // GROUP
    bmask = bmask_ref[...]
    di = pl.program_id(0)

    @pl.when(pl.program_id(2) == 0)
    def _():
        s_ref[...] = jnp.zeros_like(s_ref)
        p_ref[...] = jnp.zeros_like(p_ref)
        rh_ref[...] = jnp.zeros_like(rh_ref)
        q_ref[...] = jnp.zeros_like(q_ref)
        yl_ref[...] = jnp.zeros_like(yl_ref)

    for step in range(PAIR):
        h = jnp.where(di == 0, step, PAIR - 1 - step)
        row0 = pl.multiple_of(h * CHUNK, CHUNK)
        for g in range(ng):
            cols = slice(g * GROUP, (g + 1) * GROUP)
            s_bd = _block_diag(s_ref[:, cols], bmask)
            out = _dot(jnp.concatenate([p_ref[h, :, cols], rh_ref[h, :, cols]], axis=0), s_bd)
            s_ref[:, cols] = out[:CHUNK] + q_ref[h, :, cols]
            y_ref[0, 0, pl.ds(row0, CHUNK), cols] = (out[CHUNK:] + yl_ref[h, :, cols]).astype(y_ref.dtype)

    eye = cm_ref[0]
    m_strict = dm_ref[0, 0]
    m_incl = dm_ref[0, 1]
    chains = [(h, g) for h in range(PAIR) for g in range(ng)]
    pieces = lambda x: [x[h * CHUNK:(h + 1) * CHUNK, g * GROUP:(g + 1) * GROUP] for h, g in chains]
    halves = lambda f, x: jnp.concatenate([f(x[h * CHUNK:(h + 1) * CHUNK]) for h in range(PAIR)], axis=0)
    bd = lambda x: _block_diag(x, bmask)
    stack = lambda x, y: jnp.concatenate([x, y], axis=0)

    r = r_ref[0].astype(F32)
    k = k_ref[0].astype(F32)
    v = v_ref[0].astype(F32)
    lw = (-math.exp(-0.5)) * jax.nn.sigmoid(zw_ref[0].astype(F32))
    a = jax.nn.sigmoid(za_ref[0].astype(F32))
    kkr = k * kk_ref[...]
    sq = jnp.concatenate(pieces(kkr * kkr), axis=0)
    sq_hi, sq_lo = _split2(sq)
    ssq = _dot(sq_hi, bmask) + _dot(sq_lo, bmask)
    ssq = jnp.concatenate(
        [jnp.concatenate([ssq[(h * ng + g) * CHUNK:(h * ng + g + 1) * CHUNK] for g in range(ng)], axis=1)
         for h in range(PAIR)], axis=0)
    kk = kkr * lax.rsqrt(jnp.maximum(ssq, 1e-24))
    kd = k * (1.0 + (a - 1.0) * ka_ref[...])
    bb = kk * a
    tri = tri_ref[0]

    def cumulative(x):
        x_hi, x_lo = _split2(x)
        return _dot(tri, x_hi) + _dot(tri, x_lo)

    g_cum = halves(cumulative, lw)
    g_end = halves(lambda x: jnp.broadcast_to(jnp.sum(x, axis=0, keepdims=True), x.shape), lw)
    e_neg = jnp.exp(-g_cum)
    e_end = jnp.exp(g_end - g_cum)
    a_t = pieces(-kk * jnp.exp(g_cum - lw))
    r_t = pieces(r * jnp.exp(g_cum))
    b_t = pieces(bb * e_neg)
    k_t = pieces(kd * e_neg)
    b_h = pieces((bb * e_end).astype(BF16))
    k_h = pieces((kd * e_end).astype(BF16))
    v_g = pieces(v)
    decay_end = [x[0:1] for x in pieces(jnp.exp(g_end))]

    ar = [stack(x, y).astype(BF16) for x, y in zip(a_t, r_t)]
    mbk = [_dot_bd_pair(x, y, z, bmask, transposed=True) for x, y, z in zip(ar, b_t, k_t)]
    mb = [x[0] for x in mbk]
    mk = [x[1] for x in mbk]
    l_mat = [x[:CHUNK] * m_strict for x in mb]
    m_rb = [(x[CHUNK:] * m_incl).astype(BF16) for x in mb]
    m_k = [stack(x[:CHUNK] * m_strict, x[CHUNK:] * m_incl).astype(BF16) for x in mk]
    mv = [_dot(x, bd(y)) for x, y in zip(m_k, v_g)]
    t_mat = _unit_triangular_inverse(l_mat, bmask, bmask16_ref[...], eye16_ref[...],
                                     cm_ref[1], cm_ref[2], cm_ref[3])
    t_g = [stack(t, _dot(m, bd(t))).astype(BF16) for t, m in zip(t_mat, m_rb)]
    tau = [_dot_bd_pair(x, y, z[:CHUNK], bmask) for x, y, z in zip(t_g, a_t, mv)]
    ta = [x[0] for x in tau]
    tu = [x[1] for x in tau]
    a_h = [x[:CHUNK].astype(BF16) for x in ta]
    u_0 = [x[:CHUNK] for x in tu]
    rh = [x + y[CHUNK:] for x, y in zip(r_t, ta)]
    yl = [x[CHUNK:] + y[CHUNK:] for x, y in zip(tu, mv)]
    p_full = [_dot_tn(x, y) for x, y in zip(b_h, a_h)]
    q_full = [_dot_tn(stack(x, y), stack(u, w).astype(BF16)) for x, y, u, w in zip(b_h, k_h, u_0, v_g)]
    for i, (h, g) in enumerate(chains):
        cols = slice(g * GROUP, (g + 1) * GROUP)
        p_ref[h, :, cols] = (_fold_heads(p_full[i]) + eye * decay_end[i]).astype(p_ref.dtype)
        rh_ref[h, :, cols] = rh[i].astype(rh_ref.dtype)
        q_ref[h, :, cols] = _fold_heads(q_full[i])
        yl_ref[h, :, cols] = yl[i]


def _scan_consts():
    lane = np.arange(GROUP)
    bmask = (lane[:, None] // HEAD == lane[None, :] // HEAD).astype(np.float32)
    i = np.arange(CHUNK)[:, None]
    j = np.arange(CHUNK)[None, :]
    jl = (lane % HEAD)[None, :]
    tri = np.stack([(j <= i), (j >= i)]).astype(np.float32)
    dir_masks = np.stack([np.stack([(jl < i), (jl <= i)]),
                          np.stack([(jl > i), (jl >= i)])]).astype(np.float32)
    same16 = (jl // SUB == i // SUB)
    same32 = (jl // (2 * SUB) == i // (2 * SUB))
    common = np.stack([(jl == i), same16, same32 & ~same16, ~same32]).astype(np.float32)
    bmask16 = (lane[:, None] // SUB == lane[None, :] // SUB).astype(np.float32)
    eye16 = ((lane % SUB)[None, :] == np.arange(SUB)[:, None]).astype(np.float32)
    return (jnp.asarray(bmask, BF16), jnp.asarray(bmask16, BF16), jnp.asarray(tri, BF16),
            jnp.asarray(dir_masks, F32), jnp.asarray(common, F32), jnp.asarray(eye16, F32))


def _scan(r, k, v, zw, za, k_k, k_a, n_ctx_chunks):
    b, l, d = r.shape
    rows = PAIR * CHUNK
    npair = l // rows
    assert l % rows == 0 and n_ctx_chunks % PAIR == 0
    n_ctx = n_ctx_chunks // PAIR
    bmask, bmask16, tri, dir_masks, common, eye16 = _scan_consts()

    def pair_of(di, s):
        back = jnp.where(s < n_ctx, n_ctx - 1 - s, npair - 1 + n_ctx - s)
        return jnp.where(di == 0, s, back)

    fold_pair = lambda di, s: pair_of(di, jnp.minimum(s, npair - 1))
    apply_pair = lambda di, s: pair_of(di, jnp.maximum(s - 1, 0))
    tok = pl.BlockSpec((1, rows, d), lambda di, bi, s: (bi, fold_pair(di, s), 0))
    tok2 = pl.BlockSpec((1, rows, d), lambda di, bi, s: (bi, fold_pair(di, s), di))
    par = pl.BlockSpec((1, d), lambda di, bi, s: (0, 0))
    return pl.pallas_call(
        _scan_kernel,
        out_shape=jax.ShapeDtypeStruct((2, b, l, d), BF16),
        grid=(2, b, npair + 1),
        in_specs=[tok, tok, tok, tok2, tok2, par, par,
                  pl.BlockSpec((GROUP, GROUP), lambda di, bi, s: (0, 0)),
                  pl.BlockSpec((GROUP, GROUP), lambda di, bi, s: (0, 0)),
                  pl.BlockSpec((1, CHUNK, CHUNK), lambda di, bi, s: (di, 0, 0)),
                  pl.BlockSpec((1, 2, CHUNK, GROUP), lambda di, bi, s: (di, 0, 0, 0)),
                  pl.BlockSpec((4, CHUNK, GROUP), lambda di, bi, s: (0, 0, 0)),
                  pl.BlockSpec((SUB, GROUP), lambda di, bi, s: (0, 0))],
        out_specs=pl.BlockSpec((1, 1, rows, d), lambda di, bi, s: (di, bi, apply_pair(di, s), 0)),
        scratch_shapes=[pltpu.VMEM((CHUNK, d), F32), pltpu.VMEM((PAIR, CHUNK, d), BF16),
                        pltpu.VMEM((PAIR, CHUNK, d), BF16), pltpu.VMEM((PAIR, CHUNK, d), F32),
                        pltpu.VMEM((PAIR, CHUNK, d), F32)],
        compiler_params=_params(("parallel", "parallel", "arbitrary")),
        name="scan",
    )(r, k, v, zw, za, k_k.reshape(1, d), k_a.reshape(1, d), bmask, bmask16, tri, dir_masks, common, eye16)


def _head_sum(x, ones_bd):
    hi, lo = _split2(x)
    return _dot(hi, ones_bd) + _dot(lo, ones_bd)


def _rwkv_post_kernel(y_ref, r_ref, k_ref, v_ref, za0_ref, za1_ref, gate_ref,
                      ka_ref, rk_ref, gnw_ref, gnb_ref, ones_ref, o_ref):
    ones_bd = ones_ref[...]
    tm, d = o_ref.shape[1], o_ref.shape[2]
    ng = d // GROUP
    to_rows = lambda x: jnp.concatenate([x[:, g * GROUP:(g + 1) * GROUP] for g in range(ng)], axis=0)
    to_cols = lambda x: jnp.concatenate([x[g * tm:(g + 1) * tm] for g in range(ng)], axis=1)
    head_mean = lambda x: to_cols(_head_sum(to_rows(x), ones_bd)) * (1.0 / HEAD)
    y = y_ref[0, 0].astype(F32) + y_ref[1, 0].astype(F32)
    yc = y - head_mean(y)
    var = head_mean(yc * yc)
    o = yc * lax.rsqrt(var + GN_EPS) * gnw_ref[...] + gnb_ref[...]
    a_sum = jax.nn.sigmoid(za0_ref[0].astype(F32)) + jax.nn.sigmoid(za1_ref[0].astype(F32))
    r = r_ref[0].astype(F32)
    k_sum = k_ref[0].astype(F32) * (2.0 + (a_sum - 2.0) * ka_ref[...])
    bonus = head_mean(r * k_sum * rk_ref[...]) * float(HEAD) * v_ref[0].astype(F32)
    o_ref[0] = ((o + bonus) * gate_ref[0].astype(F32)).astype(o_ref.dtype)


def _rwkv_post(y, r, k, v, za, gate, k_a, r_k, gn_w, gn_b, n_ctx, tm=256):
    _, b, l, d = y.shape
    t = l - n_ctx
    off = n_ctx // tm
    ones_bd = _scan_consts()[0]
    tok = pl.BlockSpec((1, tm, d), lambda bi, s: (bi, s + off, 0))
    par = pl.BlockSpec((1, d), lambda bi, s: (0, 0))
    return pl.pallas_call(
        _rwkv_post_kernel,
        out_shape=jax.ShapeDtypeStruct((b, t, d), BF16),
        grid=(b, t // tm),
        in_specs=[pl.BlockSpec((2, 1, tm, d), lambda bi, s: (0, bi, s + off, 0)),
                  tok, tok, tok,
                  pl.BlockSpec((1, tm, d), lambda bi, s: (bi, s + off, 0)),
                  pl.BlockSpec((1, tm, d), lambda bi, s: (bi, s + off, 1)),
                  tok, par, par, par, par,
                  pl.BlockSpec((GROUP, GROUP), lambda bi, s: (0, 0))],
        out_specs=pl.BlockSpec((1, tm, d), lambda bi, s: (bi, s, 0)),
        compiler_params=_params(("parallel", "parallel")),
        name="rwkv_post",
    )(y, r, k, v, za, za, gate, k_a.reshape(1, d), r_k.reshape(1, d), gn_w.reshape(1, d),
      gn_b.reshape(1, d), ones_bd)


def _proj_norm_kernel(a_ref, w_ref, x_ref, pos_ref, g_ref, lnw_ref, lnb_ref, sc_ref, sh_ref,
                      o_ref, h_ref, acc_ref, *, alpha):
    j = pl.program_id(2)
    nj = acc_ref.shape[0]
    tn = acc_ref.shape[2]
    acc_ref[j] = _dot(a_ref[...], w_ref[...].astype(BF16))

    @pl.when(j == nj - 1)
    def _():
        gate = g_ref[0]
        for jj in range(nj):
            cols = slice(jj * tn, (jj + 1) * tn)
            o_ref[:, cols] = alpha * (x_ref[:, cols] + pos_ref[:, cols]) + gate[:, cols] * acc_ref[jj]
        x_new = _layer_norm(o_ref[...], lnw_ref[...], lnb_ref[...])
        o_ref[...] = x_new
        h_ref[...] = (x_new * (1.0 + sc_ref[0]) + sh_ref[0]).astype(h_ref.dtype)


def _proj_norm(a, w, x, pos, gate, ln_w, ln_b, sc, sh, alpha, tm=512, tn=2048):
    m, k = a.shape
    d = w.shape[1]
    t = pos.shape[0]
    tm = _tile(t, tm)
    tn = _tile(d, tn)
    tpb = t // tm
    nb = m // t
    rows = lambda ti, bi, j: (bi * tpb + ti, 0)
    vec = pl.BlockSpec((1, 1, d), lambda ti, bi, j: (bi, 0, 0))
    return pl.pallas_call(
        functools.partial(_proj_norm_kernel, alpha=alpha),
        out_shape=[jax.ShapeDtypeStruct((m, d), F32), jax.ShapeDtypeStruct((m, d), BF16)],
        grid=(tpb, nb, d // tn),
        in_specs=[pl.BlockSpec((tm, k), rows),
                  pl.BlockSpec((k, tn), lambda ti, bi, j: (0, j)),
                  pl.BlockSpec((tm, d), rows),
                  pl.BlockSpec((tm, d), lambda ti, bi, j: (ti, 0)),
                  vec,
                  pl.BlockSpec((1, d), lambda ti, bi, j: (0, 0)),
                  pl.BlockSpec((1, d), lambda ti, bi, j: (0, 0)),
                  vec, vec],
        out_specs=[pl.BlockSpec((tm, d), rows)] * 2,
        scratch_shapes=[pltpu.VMEM((d // tn, tm, tn), F32)],
        compiler_params=_params(("parallel", "parallel", "arbitrary")),
        name="proj_norm",
    )(a, w, x, pos, gate, ln_w.reshape(1, d), ln_b.reshape(1, d), sc, sh)


def _swiglu_halves(h, w1_ref, w3_ref, w2_ref, lead):
    tf = w1_ref.shape[-1]
    halves = [slice(0, tf // 2), slice(tf // 2, tf)]
    a1 = [_dot(h, w1_ref[lead + (slice(None), c)].astype(BF16)) for c in halves]
    a3 = [_dot(h, w3_ref[lead + (slice(None), c)].astype(BF16)) for c in halves]
    u = [(x * jax.nn.sigmoid(x) * y).astype(BF16) for x, y in zip(a1, a3)]
    y = [_dot(x, w2_ref[lead + (c, slice(None))].astype(BF16)) for x, c in zip(u, halves)]
    return y[0] + y[1]


def _ffn_kernel(h_ref, w1_ref, w3_ref, w2_ref, o_ref, acc_ref):
    j = pl.program_id(1)

    @pl.when(j == 0)
    def _():
        acc_ref[...] = jnp.zeros_like(acc_ref)

    acc_ref[...] += _swiglu_halves(h_ref[...], w1_ref, w3_ref, w2_ref, ())

    @pl.when(j == pl.num_programs(1) - 1)
    def _():
        o_ref[...] = acc_ref[...].astype(o_ref.dtype)


def _ffn(h, w1, w3, w2, tm=1024, tf=512):
    m, d = h.shape
    f = w1.shape[1]
    tm = _tile(m, tm)
    tf = _tile(f, tf)
    return pl.pallas_call(
        _ffn_kernel,
        out_shape=jax.ShapeDtypeStruct((m, d), BF16),
        grid=(m // tm, f // tf),
        in_specs=[pl.BlockSpec((tm, d), lambda i, j: (i, 0)),
                  pl.BlockSpec((d, tf), lambda i, j: (0, j)),
                  pl.BlockSpec((d, tf), lambda i, j: (0, j)),
                  pl.BlockSpec((tf, d), lambda i, j: (j, 0))],
        out_specs=pl.BlockSpec((tm, d), lambda i, j: (i, 0)),
        scratch_shapes=[pltpu.VMEM((tm, d), F32)],
        compiler_params=_params(("parallel", "arbitrary")),
        name="ffn",
    )(h, w1, w3, w2)


def _residual_norm_kernel(x_ref, y_ref, g_ref, lnw_ref, lnb_ref, o_ref, *, alpha):
    z = alpha * x_ref[...] + g_ref[0] * y_ref[...].astype(F32)
    o_ref[...] = _layer_norm(z, lnw_ref[...], lnb_ref[...])


def _residual_norm(x, y, gate, ln_w, ln_b, alpha, rows_per_batch, tm=512):
    m, d = x.shape
    tm = _tile(rows_per_batch, tm)
    tpb = rows_per_batch // tm
    tok = pl.BlockSpec((tm, d), lambda i: (i, 0))
    row = pl.BlockSpec((1, d), lambda i: (0, 0))
    return pl.pallas_call(
        functools.partial(_residual_norm_kernel, alpha=alpha),
        out_shape=jax.ShapeDtypeStruct((m, d), F32),
        grid=(m // tm,),
        in_specs=[tok, tok, pl.BlockSpec((1, 1, d), lambda i: (i // tpb, 0, 0)), row, row],
        out_specs=tok,
        compiler_params=_params(("parallel",)),
        name="residual_norm",
    )(x, y, gate, ln_w.reshape(1, d), ln_b.reshape(1, d))


def _shift_down(x, s, row):
    return jnp.where(row >= s, pltpu.roll(x, s, axis=0), 0.0)


def _shift_up(x, s, row):
    t = x.shape[0]
    return jnp.where(row < t - s, pltpu.roll(x, t - s, axis=0), 0.0)


def _pool_kernel(x_ref, sc_ref, sh_ref, w_ref, scale_ref, o_ref):
    g = pl.program_id(0)
    t = x_ref.shape[1]
    h = x_ref[0] * (1.0 + sc_ref[0]) + sh_ref[0]
    row = lax.broadcasted_iota(jnp.int32, h.shape, 0)
    w = w_ref[0].astype(BF16)
    for gi, win in enumerate(POOL_WINDOWS):
        @pl.when(g == gi)
        def _(win=win):
            half = win // 2
            back = h
            fwd = h
            m = 1
            while m < half:
                back = back + _shift_down(back, m, row)
                fwd = fwd + _shift_up(fwd, m, row)
                m *= 2
            total = _shift_down(back, 1, row) + fwd
            count = (jnp.minimum(row + half, t) - jnp.maximum(row - half, 0)).astype(F32)
            pooled = (total / count - h).astype(BF16)
            o_ref[0] = _dot(pooled, w) * scale_ref[...]


def _pool(x, sc, sh, w_pool, scale):
    b, t, d = x.shape
    ng, p, _ = w_pool.shape
    vec = pl.BlockSpec((1, 1, p), lambda g, bi: (bi, 0, g))
    return pl.pallas_call(
        _pool_kernel,
        out_shape=jax.ShapeDtypeStruct((b, t, d), F32),
        grid=(ng, b),
        in_specs=[pl.BlockSpec((1, t, p), lambda g, bi: (bi, 0, g)), vec, vec,
                  pl.BlockSpec((1, p, p), lambda g, bi: (g, 0, 0)),
                  pl.BlockSpec((1, p), lambda g, bi: (0, g))],
        out_specs=pl.BlockSpec((1, t, p), lambda g, bi: (bi, 0, g)),
        compiler_params=_params(("parallel", "parallel")),
        name="pool",
    )(x, sc, sh, w_pool, scale.reshape(1, d))


def _route_kernel(x_ref, y_ref, g_ref, lnw_ref, lnb_ref, sc_ref, sh_ref, rt_ref,
                  xo_ref, h_ref, route_ref, *, alpha):
    x = _layer_norm(alpha * x_ref[...] + g_ref[0] * y_ref[...], lnw_ref[...], lnb_ref[...])
    xo_ref[...] = x
    h = x * (1.0 + sc_ref[0]) + sh_ref[0]
    h_ref[...] = h
    r1, r2, r3 = _split3(rt_ref[...])
    h1, h2, h3 = _split3(h)
    logits = (_dot_nt(r1, h1) + (_dot_nt(r1, h2) + _dot_nt(r2, h1))
              + (_dot_nt(r1, h3) + _dot_nt(r2, h2) + _dot_nt(r3, h1)))
    mx = jnp.max(logits, axis=0, keepdims=True)
    e = jnp.exp(logits - mx)
    p = e / jnp.sum(e, axis=0, keepdims=True)
    idx = lax.broadcasted_iota(jnp.int32, p.shape, 0)
    p1 = jnp.max(p, axis=0, keepdims=True)
    i1 = jnp.min(jnp.where(p == p1, idx, N_EXPERTS), axis=0, keepdims=True)
    rest = jnp.where(idx == i1, -1.0, p)
    p2 = jnp.max(rest, axis=0, keepdims=True)
    i2 = jnp.min(jnp.where(rest == p2, idx, N_EXPERTS), axis=0, keepdims=True)
    den = p1 + p2
    out = jnp.where(idx == 0, i1.astype(F32), 0.0)
    out = jnp.where(idx == 1, i2.astype(F32), out)
    out = jnp.where(idx == 2, p1 / den, out)
    out = jnp.where(idx == 3, p2 / den, out)
    route_ref[...] = out


def _route(x, y, gate, ln_w, ln_b, sc, sh, router, alpha, rows_per_batch, tm=256):
    m, d = x.shape
    tpb = rows_per_batch // tm
    vec = pl.BlockSpec((1, 1, d), lambda i: (i // tpb, 0, 0))
    row = pl.BlockSpec((1, d), lambda i: (0, 0))
    tok = pl.BlockSpec((tm, d), lambda i: (i, 0))
    return pl.pallas_call(
        functools.partial(_route_kernel, alpha=alpha),
        out_shape=[jax.ShapeDtypeStruct((m, d), F32), jax.ShapeDtypeStruct((m, d), F32),
                   jax.ShapeDtypeStruct((N_EXPERTS, m), F32)],
        grid=(m // tm,),
        in_specs=[tok, tok, vec, row, row, vec, vec,
                  pl.BlockSpec((N_EXPERTS, d), lambda i: (0, 0))],
        out_specs=[tok, tok, pl.BlockSpec((N_EXPERTS, tm), lambda i: (0, i))],
        compiler_params=_params(("parallel",)),
        name="route",
    )(x, y, gate, ln_w.reshape(1, d), ln_b.reshape(1, d), sc, sh, router.T)


def _row_copy(src_hbm, dst_vmem, sem, src_row, dst_row):
    return pltpu.make_async_copy(src_hbm.at[pl.ds(src_row, 1)], dst_vmem.at[pl.ds(dst_row, 1)], sem)


def _moe_kernel(ce_ref, nv_ref, idx_ref, h_ref, w1_ref, w3_ref, w2_ref, o_ref,
                stage_ref, work_ref, gather_sem, *, share):
    c = pl.program_id(0)
    j = pl.program_id(1)
    n_chunks = pl.num_programs(0)
    nj = pl.num_programs(1)
    nv = nv_ref[c]
    full = work_ref.shape[0]
    stage_rows = stage_ref.shape[0]

    def gather(chunk, r):
        return _row_copy(h_ref, stage_ref, gather_sem, idx_ref[chunk * full + r], r)

    def wait_all_gathers():
        def body(r, carry):
            gather(0, r).wait()
            return carry
        lax.fori_loop(0, stage_rows, body, 0, unroll=8)

    @pl.when(j == 0)
    def _():
        @pl.when(c == 0)
        def _():
            def body(r, carry):
                gather(0, r).start()
                return carry
            lax.fori_loop(0, stage_rows, body, 0, unroll=8)

        o_ref[...] = jnp.zeros_like(o_ref)
        wait_all_gathers()
        work_ref[...] = stage_ref[0:full, :].astype(work_ref.dtype)

    def request_next_share():
        for i in range(share):
            gather(c + 1, j * share + i).start(priority=1)

    def swiglu_rows(rows):
        o_ref[rows, :] += _swiglu_halves(work_ref[rows, :], w1_ref, w3_ref, w2_ref, (0,))

    n_sub = (nv + (MOE_SUB - 1)) // MOE_SUB
    for k in range(full // MOE_SUB + 1):
        @pl.when(n_sub == k)
        def _(k=k):
            request_next_share()
            if k:
                swiglu_rows(slice(0, k * MOE_SUB))

    @pl.when(jnp.logical_and(j == nj - 1, c == n_chunks - 1))
    def _():
        wait_all_gathers()


def _moe_ffn(h, src_idx, chunk_expert, chunk_valid, w1, w3, w2, n_chunks, tf=512):
    d = h.shape[1]
    f = w1.shape[2]
    tf = _tile(f, tf)
    nj = f // tf
    share = -(-MOE_CHUNK // nj)
    share = -(-share // 8) * 8
    stage_rows = share * nj
    assert src_idx.shape[0] >= n_chunks * MOE_CHUNK + stage_rows

    def jeff(c, j, nv):
        return jnp.where(nv[c] > 0, j, nj - 1)

    return pl.pallas_call(
        functools.partial(_moe_kernel, share=share),
        out_shape=jax.ShapeDtypeStruct((n_chunks * MOE_CHUNK, d), F32),
        grid_spec=pltpu.PrefetchScalarGridSpec(
            num_scalar_prefetch=3,
            grid=(n_chunks, nj),
            in_specs=[pl.BlockSpec(memory_space=pl.ANY),
                      pl.BlockSpec((1, d, tf), lambda c, j, ce, nv, ix: (ce[c], 0, jeff(c, j, nv))),
                      pl.BlockSpec((1, d, tf), lambda c, j, ce, nv, ix: (ce[c], 0, jeff(c, j, nv))),
                      pl.BlockSpec((1, tf, d), lambda c, j, ce, nv, ix: (ce[c], jeff(c, j, nv), 0))],
            out_specs=pl.BlockSpec((MOE_CHUNK, d), lambda c, j, ce, nv, ix: (c, 0)),
            scratch_shapes=[pltpu.VMEM((stage_rows, d), F32), pltpu.VMEM((MOE_CHUNK, d), BF16),
                            pltpu.SemaphoreType.DMA]),
        compiler_params=_params(("arbitrary", "arbitrary"), MOE_VMEM_LIMIT_BYTES),
        name="moe_ffn",
    )(chunk_expert, chunk_valid, src_idx, h, w1, w3, w2)


def _combine_kernel(p0_ref, p1_ref, ys_ref, x_ref, gates_ref, g_ref, lnw_ref, lnb_ref,
                    o_ref, b0_ref, b1_ref, sem, *, alpha):
    i = pl.program_id(0)
    n = pl.num_programs(0)
    rows = b0_ref.shape[1]
    slot = i % 2

    def copies(tile, to_slot, r):
        return (_row_copy(ys_ref, b0_ref.at[to_slot], sem.at[to_slot], p0_ref[tile * rows + r], r),
                _row_copy(ys_ref, b1_ref.at[to_slot], sem.at[to_slot], p1_ref[tile * rows + r], r))

    def wait_slot(to_slot):
        def body(r, carry):
            for cp in copies(0, to_slot, r):
                cp.wait()
            return carry
        lax.fori_loop(0, rows, body, 0, unroll=4)

    @pl.when(i == 0)
    def _():
        def body(r, carry):
            for cp in copies(0, 0, r):
                cp.start()
            return carry
        lax.fori_loop(0, rows, body, 0, unroll=4)

    wait_slot(slot)
    for r in range(rows):
        for cp in copies(i + 1, 1 - slot, r):
            cp.start()
    gates = gates_ref[...]
    y = gates[:, 0:1] * b0_ref[slot] + gates[:, 1:2] * b1_ref[slot]
    z = alpha * x_ref[...] + g_ref[0] * y
    o_ref[...] = _layer_norm(z, lnw_ref[...], lnb_ref[...])

    @pl.when(i == n - 1)
    def _():
        wait_slot(1 - slot)


def _combine(ys, pos0, pos1, gates, x, gate_vec, ln_w, ln_b, alpha, rows_per_batch, rows=256):
    m, d = x.shape
    tpb = rows_per_batch // rows
    tok = lambda i, a, b: (i, 0)
    return pl.pallas_call(
        functools.partial(_combine_kernel, alpha=alpha),
        out_shape=jax.ShapeDtypeStruct((m, d), F32),
        grid_spec=pltpu.PrefetchScalarGridSpec(
            num_scalar_prefetch=2,
            grid=(m // rows,),
            in_specs=[pl.BlockSpec(memory_space=pl.ANY),
                      pl.BlockSpec((rows, d), tok),
                      pl.BlockSpec((rows, 2), tok),
                      pl.BlockSpec((1, 1, d), lambda i, a, b: (i // tpb, 0, 0)),
                      pl.BlockSpec((1, d), lambda i, a, b: (0, 0)),
                      pl.BlockSpec((1, d), lambda i, a, b: (0, 0))],
            out_specs=pl.BlockSpec((rows, d), tok),
            scratch_shapes=[pltpu.VMEM((2, rows, d), F32), pltpu.VMEM((2, rows, d), F32),
                            pltpu.SemaphoreType.DMA((2,))]),
        compiler_params=_params(("arbitrary",)),
        name="moe_combine",
    )(pos0, pos1, ys, x, gates, gate_vec, ln_w.reshape(1, d), ln_b.reshape(1, d))


def _routing_tables(route, n_chunks, table_len):
    n = route.shape[1]
    experts = jnp.concatenate([route[0], route[1]]).astype(jnp.int32)
    onehot = (experts[:, None] == jnp.arange(N_EXPERTS, dtype=jnp.int32)[None, :]).astype(jnp.int32)
    csum = jnp.cumsum(onehot, axis=0)
    rank = jnp.sum((csum - 1) * onehot, axis=1)
    counts = csum[-1]
    chunks_e = (counts + MOE_CHUNK - 1) // MOE_CHUNK
    chunk_end = jnp.cumsum(chunks_e)
    chunk_start = chunk_end - chunks_e
    dest = (chunk_start * MOE_CHUNK)[experts] + rank
    token = jnp.arange(2 * n, dtype=jnp.int32) % n
    src_idx = jnp.zeros((table_len,), jnp.int32).at[dest].set(token)
    cid = jnp.arange(n_chunks, dtype=jnp.int32)
    used = cid < chunk_end[-1]
    last_used = jnp.maximum(chunk_end[-1] - 1, 0)
    ce = jnp.sum((jnp.minimum(cid, last_used)[:, None] >= chunk_end[None, :]).astype(jnp.int32), axis=1)
    ce = jnp.minimum(ce, N_EXPERTS - 1)
    nvalid = jnp.clip(counts[ce] - (cid - chunk_start[ce]) * MOE_CHUNK, 0, MOE_CHUNK)
    nvalid = jnp.where(used, nvalid, 0).astype(jnp.int32)
    return src_idx, ce, nvalid, dest[:n], dest[n:]


def _position_embedding(rows, width, d):
    quarter = d // 4
    omega = 1.0 / (POS_BASE ** (jnp.arange(quarter, dtype=F32) / quarter))
    ar = jnp.arange(rows, dtype=F32)[:, None] * omega[None, :]
    ac = jnp.arange(width, dtype=F32)[:, None] * omega[None, :]
    row_part = jnp.repeat(jnp.concatenate([jnp.sin(ar), jnp.cos(ar)], axis=-1), width, axis=0)
    col_part = jnp.tile(jnp.concatenate([jnp.sin(ac), jnp.cos(ac)], axis=-1), (rows, 1))
    return jnp.concatenate([row_part, col_part], axis=-1)


def _block_diag2(w):
    z = jnp.zeros_like(w[0])
    return jnp.concatenate([jnp.concatenate([w[0], z], axis=1), jnp.concatenate([z, w[1]], axis=1)], axis=0)


def kernel(x, c, ctx, c_ctx, w_mod, b_mod, ln_w, ln_b, rwkv_mu, rwkv_w_r, rwkv_w_k, rwkv_w_v, rwkv_w_o, rwkv_decay_w0, rwkv_decay_w1, rwkv_decay_w2, rwkv_iclr_a0, rwkv_iclr_a1, rwkv_iclr_a2, rwkv_gate_g1, rwkv_gate_g2, rwkv_k_k, rwkv_k_a, rwkv_r_k, rwkv_gn_w, rwkv_gn_b, pool_w, pool_scale, ffn_w1, ffn_w3, ffn_w2, moe_router, moe_w1, moe_w3, moe_w2):
    b, t, d = x.shape
    n_ctx = ctx.shape[1]
    depth = w_mod.shape[0]
    assert depth == 2 and rwkv_mu.shape[0] == 1 and pool_w.shape[0] == 1
    alpha = (2.0 * depth) ** 0.25
    grid_w = 64
    l = n_ctx + t
    n = b * t

    cond = jnp.zeros((8, d), F32).at[:b].set(c).at[b].set(c_ctx)
    mod = _adaln(cond, w_mod, b_mod).reshape(depth, 8, 6, d)
    lat = lambda layer, which: mod[layer, :b, which].reshape(b, 1, d)
    cvec = lambda layer, which: mod[layer, b, which].reshape(1, 1, d)

    pos = _position_embedding(t // grid_w, grid_w, d)

    xr, xw, xk, xv, xa, xg = _rwkv_mix(ctx, x, pos, lat(0, 1), lat(0, 0), cvec(0, 1), cvec(0, 0), rwkv_mu[0])
    flat = lambda a: a.reshape(b * l, a.shape[-1])
    r = _mm(flat(xr), rwkv_w_r[0], out_dtype=BF16)
    k = _mm(flat(xk), rwkv_w_k[0], out_dtype=BF16)
    v = _mm(flat(xv), rwkv_w_v[0], out_dtype=BF16)
    dw1 = jnp.concatenate([rwkv_decay_w1[0, 0], rwkv_decay_w1[0, 1]], axis=1)
    ia1 = jnp.concatenate([rwkv_iclr_a1[0, 0], rwkv_iclr_a1[0, 1]], axis=1)
    lora_w = _mm(flat(xw), dw1, act="tanh", out_dtype=BF16)
    zw = _mm(lora_w, _block_diag2(rwkv_decay_w2[0]), rwkv_decay_w0[0].reshape(1, 2 * d), out_dtype=BF16, tn=2048)
    lora_a = _mm(flat(xa), ia1, out_dtype=BF16)
    za = _mm(lora_a, _block_diag2(rwkv_iclr_a2[0]), rwkv_iclr_a0[0].reshape(1, 2 * d), out_dtype=BF16, tn=2048)
    lora_g = _mm(flat(xg), rwkv_gate_g1[0], act="sigmoid", out_dtype=BF16)
    gate = _mm(lora_g, rwkv_gate_g2[0], out_dtype=BF16, tn=2048)
    seq = lambda a: a.reshape(b, l, a.shape[-1])
    y_scan = _scan(seq(r), seq(k), seq(v), seq(zw), seq(za), rwkv_k_k[0], rwkv_k_a[0], n_ctx // CHUNK)
    og = _rwkv_post(y_scan, seq(r), seq(k), seq(v), seq(za), seq(gate), rwkv_k_a[0],
                    rwkv_r_k[0].reshape(d), rwkv_gn_w[0], rwkv_gn_b[0], n_ctx)
    x1, h1 = _proj_norm(og.reshape(n, d), rwkv_w_o[0].astype(BF16), x.reshape(n, d), pos, lat(0, 2),
                        ln_w[0, 0], ln_b[0, 0], lat(0, 4), lat(0, 3), alpha)
    y_ffn = _ffn(h1, ffn_w1[0], ffn_w3[0], ffn_w2[0])
    x2 = _residual_norm(x1, y_ffn, lat(0, 5), ln_w[0, 1], ln_b[0, 1], alpha, t)

    y_pool = _pool(x2.reshape(b, t, d), lat(1, 1), lat(1, 0), pool_w[0], pool_scale[0])
    x3, h3, route = _route(x2, y_pool.reshape(n, d), lat(1, 2), ln_w[1, 0], ln_b[1, 0],
                           lat(1, 4), lat(1, 3), moe_router[0], alpha, t)
    n_chunks = (2 * n + N_EXPERTS * (MOE_CHUNK - 1)) // MOE_CHUNK
    src_idx, chunk_expert, chunk_valid, pos0, pos1 = _routing_tables(route, n_chunks, (n_chunks + 2) * MOE_CHUNK)
    ys = _moe_ffn(h3, src_idx, chunk_expert, chunk_valid, moe_w1[0], moe_w3[0], moe_w2[0], n_chunks)
    gates = jnp.stack([route[2], route[3]], axis=1)
    spare = jnp.zeros((256,), jnp.int32)
    out = _combine(ys, jnp.concatenate([pos0, spare]), jnp.concatenate([pos1, spare]), gates, x3,
                   lat(1, 5), ln_w[1, 1], ln_b[1, 1], alpha, t)
    return out.reshape(b, t, d)
```

```python
import functools
import math

import numpy as np
import jax
import jax.numpy as jnp
from jax import lax
from jax.experimental import pallas as pl
from jax.experimental.pallas import tpu as pltpu

F32 = jnp.float32
BF16 = jnp.bfloat16

LANES = 128
HEAD = 64
GROUP = 4 * HEAD
CHUNK = 64
LN_EPS = 1e-5
GN_EPS = 64e-5
POS_BASE = 10000.0
POOL_WINDOWS = (2, 4, 8, 16)
N_EXPERTS = 8
MOE_CHUNK = 1024
MOE_SUB = 256
VMEM_LIMIT_BYTES = 56 * 1024 * 1024
MOE_VMEM_LIMIT_BYTES = 60 * 1024 * 1024


def _params(semantics, vmem_limit_bytes=VMEM_LIMIT_BYTES):
    return pltpu.CompilerParams(dimension_semantics=semantics, vmem_limit_bytes=vmem_limit_bytes)


def _tile(n, preferred):
    t = min(preferred, n)
    while n % t:
        t //= 2
    return t


def _dot(a, b):
    return jnp.dot(a, b, preferred_element_type=F32)


def _dot_nt(a, b):
    return lax.dot_general(a, b, (((1,), (1,)), ((), ())), preferred_element_type=F32)


def _dot_tn(a, b):
    return lax.dot_general(a, b, (((0,), (0,)), ((), ())), preferred_element_type=F32)


def _split2(x):
    hi = x.astype(BF16)
    lo = (x - hi.astype(F32)).astype(BF16)
    return hi, lo


def _split3(x):
    hi = x.astype(BF16)
    r1 = x - hi.astype(F32)
    mid = r1.astype(BF16)
    lo = (r1 - mid.astype(F32)).astype(BF16)
    return hi, mid, lo


def _layer_norm(z, w, b):
    mu = jnp.mean(z, axis=-1, keepdims=True)
    zc = z - mu
    var = jnp.mean(zc * zc, axis=-1, keepdims=True)
    return zc * lax.rsqrt(var + LN_EPS) * w + b


def _adaln_kernel(c_ref, w_ref, b_ref, o_ref):
    c = c_ref[...]
    a = (c * jax.nn.sigmoid(c)).astype(BF16)
    o_ref[0] = _dot(a, w_ref[0].astype(BF16)) + b_ref[0]


def _adaln(cond, w_mod, b_mod, tn=1024):
    depth, d, n = w_mod.shape
    tn = _tile(n, tn)
    rows = cond.shape[0]
    return pl.pallas_call(
        _adaln_kernel,
        out_shape=jax.ShapeDtypeStruct((depth, rows, n), F32),
        grid=(depth, n // tn),
        in_specs=[pl.BlockSpec((rows, d), lambda l, j: (0, 0)),
                  pl.BlockSpec((1, d, tn), lambda l, j: (l, 0, j)),
                  pl.BlockSpec((1, 1, tn), lambda l, j: (l, 0, j))],
        out_specs=pl.BlockSpec((1, rows, tn), lambda l, j: (l, 0, j)),
        compiler_params=_params(("parallel", "parallel")),
        name="adaln",
    )(cond, w_mod, b_mod.reshape(depth, 1, n))


def _mm_kernel(a_ref, w_ref, b_ref, o_ref, *, act):
    acc = _dot(a_ref[...], w_ref[...].astype(BF16)) + b_ref[...]
    if act == "tanh":
        acc = jnp.tanh(acc)
    elif act == "sigmoid":
        acc = jax.nn.sigmoid(acc)
    o_ref[...] = acc.astype(o_ref.dtype)


def _mm(a, w, bias=None, *, act=None, out_dtype=F32, tm=2304, tn=512):
    m, k = a.shape
    n = w.shape[1]
    tm = _tile(m, tm)
    tn = _tile(n, tn)
    if bias is None:
        bias = jnp.zeros((1, n), F32)
    return pl.pallas_call(
        functools.partial(_mm_kernel, act=act),
        out_shape=jax.ShapeDtypeStruct((m, n), out_dtype),
        grid=(m // tm, n // tn),
        in_specs=[pl.BlockSpec((tm, k), lambda i, j: (i, 0)),
                  pl.BlockSpec((k, tn), lambda i, j: (0, j)),
                  pl.BlockSpec((1, tn), lambda i, j: (0, j))],
        out_specs=pl.BlockSpec((tm, tn), lambda i, j: (i, j)),
        compiler_params=_params(("parallel", "arbitrary")),
        name="matmul",
    )(a, w, bias.reshape(1, n))


def _mix_kernel(ctx_ref, x_ref, xp_ref, xn_ref, pos_ref, pp_ref, pn_ref,
                sc_ref, sh_ref, csc_ref, csh_ref, mu_ref,
                o0, o1, o2, o3, o4, o5, *, n_lat_tiles):
    s = pl.program_id(0)
    is_ctx = s == 0
    tm = x_ref.shape[1]
    scale = jnp.where(is_ctx, csc_ref[0], sc_ref[0]) + 1.0
    shift = jnp.where(is_ctx, csh_ref[0], sh_ref[0])
    src = jnp.where(is_ctx, ctx_ref[0], x_ref[0] + pos_ref[...])
    h = src * scale + shift
    has_prev = s > 1
    has_next = jnp.logical_and(s >= 1, s < n_lat_tiles)
    h_prev = jnp.where(has_prev, (xp_ref[0] + pp_ref[...]) * scale + shift, 0.0)[7:8]
    h_next = jnp.where(has_next, (xn_ref[0] + pn_ref[...]) * scale + shift, 0.0)[0:1]
    row = lax.broadcasted_iota(jnp.int32, h.shape, 0)
    h_m1 = jnp.where(row == 0, h_prev, pltpu.roll(h, 1, axis=0))
    h_p1 = jnp.where(row == tm - 1, h_next, pltpu.roll(h, tm - 1, axis=0))
    xx = 0.5 * (h_m1 + h_p1) - h
    for n, o_ref in enumerate((o0, o1, o2, o3, o4, o5)):
        o_ref[0] = (h + xx * mu_ref[n:n + 1]).astype(o_ref.dtype)


def _rwkv_mix(ctx, x, pos, sc, sh, csc, csh, mu):
    b, t, d = x.shape
    tm = ctx.shape[1]
    assert t % tm == 0 and tm % 8 == 0
    n_lat = t // tm
    r8 = tm // 8
    lat = lambda s, bi: (bi, jnp.maximum(s - 1, 0), 0)
    prev8 = lambda s, bi: (bi, jnp.maximum((s - 1) * r8 - 1, 0), 0)
    next8 = lambda s, bi: (bi, jnp.minimum(jnp.maximum(s, 1) * r8, t // 8 - 1), 0)
    vec = pl.BlockSpec((1, 1, d), lambda s, bi: (bi, 0, 0))
    cvec = pl.BlockSpec((1, 1, d), lambda s, bi: (0, 0, 0))
    out_sds = jax.ShapeDtypeStruct((b, tm + t, d), BF16)
    return pl.pallas_call(
        functools.partial(_mix_kernel, n_lat_tiles=n_lat),
        out_shape=[out_sds] * 6,
        grid=(n_lat + 1, b),
        in_specs=[pl.BlockSpec((1, tm, d), lambda s, bi: (bi, 0, 0)),
                  pl.BlockSpec((1, tm, d), lat),
                  pl.BlockSpec((1, 8, d), prev8),
                  pl.BlockSpec((1, 8, d), next8),
                  pl.BlockSpec((tm, d), lambda s, bi: (jnp.maximum(s - 1, 0), 0)),
                  pl.BlockSpec((8, d), lambda s, bi: (jnp.maximum((s - 1) * r8 - 1, 0), 0)),
                  pl.BlockSpec((8, d), lambda s, bi: (jnp.minimum(jnp.maximum(s, 1) * r8, t // 8 - 1), 0)),
                  vec, vec, cvec, cvec,
                  pl.BlockSpec((6, d), lambda s, bi: (0, 0))],
        out_specs=[pl.BlockSpec((1, tm, d), lambda s, bi: (bi, s, 0))] * 6,
        compiler_params=_params(("parallel", "parallel")),
        name="rwkv_mix",
    )(ctx, x, x, x, pos, pos, pos, sc, sh, csc, csh, mu)


def _block_diag(x, bmask):
    xb = x.astype(BF16)
    zero = jnp.zeros((HEAD, GROUP // 2), BF16)
    rows = []
    for h in range(GROUP // HEAD):
        t = h // 2
        blk = xb[:, t * LANES:(t + 1) * LANES] * bmask[h * HEAD:(h + 1) * HEAD, t * LANES:(t + 1) * LANES]
        rows.append(jnp.concatenate([blk, zero] if t == 0 else [zero, blk], axis=1))
    return jnp.concatenate(rows, axis=0)


def _fold_heads(full):
    lane = lax.broadcasted_iota(jnp.int32, (HEAD, LANES), 1)
    tiles = []
    for t in range(GROUP // LANES):
        even = full[(2 * t) * HEAD:(2 * t + 1) * HEAD, t * LANES:(t + 1) * LANES]
        odd = full[(2 * t + 1) * HEAD:(2 * t + 2) * HEAD, t * LANES:(t + 1) * LANES]
        tiles.append(jnp.where(lane < HEAD, even, odd))
    return jnp.concatenate(tiles, axis=1)


def _dot_bd_pair(x, y1, y2, bmask, transposed=False):
    out1, out2 = [], []
    for t in range(GROUP // LANES):
        lanes = slice(t * LANES, (t + 1) * LANES)
        mask = bmask[lanes, lanes]
        b1 = jnp.concatenate([y1[:, lanes].astype(BF16)] * 2, axis=0) * mask
        b2 = jnp.concatenate([y2[:, lanes].astype(BF16)] * 2, axis=0) * mask
        if transposed:
            z = _dot_nt(x[:, lanes], jnp.concatenate([b1, b2], axis=0))
        else:
            z = _dot(x[:, lanes], jnp.concatenate([b1, b2], axis=1))
        out1.append(z[:, :LANES])
        out2.append(z[:, LANES:])
    return jnp.concatenate(out1, axis=1), jnp.concatenate(out2, axis=1)


SUB = 16


def _square_bd16(x, bmask16):
    per_tile = LANES // SUB
    tiles = [slice(t * LANES, (t + 1) * LANES) for t in range(GROUP // LANES)]
    lhs = jnp.concatenate([x[:, lanes] for lanes in tiles], axis=0)
    rhs = jnp.concatenate([jnp.concatenate([x[:, lanes]] * per_tile, axis=0) * bmask16[lanes, lanes]
                           for lanes in tiles], axis=1)
    z = _dot(lhs, rhs)
    return jnp.concatenate([z[t * SUB:(t + 1) * SUB, lanes] for t, lanes in enumerate(tiles)], axis=1)


def _dot3_bd16(a, b, bmask16):
    rows = a.shape[0]
    a_hi, a_lo = _split2(a)
    b_hi, b_lo = _split2(b)
    lhs = jnp.concatenate([a_hi, a_lo], axis=0)
    per_tile = LANES // SUB
    outs = []
    for t in range(GROUP // LANES):
        lanes = slice(t * LANES, (t + 1) * LANES)
        mask = bmask16[lanes, lanes]
        rhs = jnp.concatenate([jnp.concatenate([b_hi[:, lanes]] * per_tile, axis=0) * mask,
                               jnp.concatenate([b_lo[:, lanes]] * per_tile, axis=0) * mask], axis=1)
        z = _dot(lhs[:, lanes], rhs)
        outs.append(z[:rows, :LANES] + z[rows:, :LANES] + z[:rows, LANES:])
    return jnp.concatenate(outs, axis=1)


def _unit_triangular_inverse(l_mats, bmask, bmask16, eye16, diag16, off_a, off_b):
    nq = CHUNK // SUB
    l16 = [sum(l[q * SUB:(q + 1) * SUB] * diag16[q * SUB:(q + 1) * SUB] for q in range(nq)).astype(BF16)
           for l in l_mats]
    t16 = [eye16 + x.astype(F32) for x in l16]
    l_pow = [_square_bd16(x, bmask16) for x in l16]
    for _ in range(2):
        both = [_dot3_bd16(jnp.concatenate([t, lp], axis=0), lp, bmask16) for t, lp in zip(t16, l_pow)]
        t16 = [t + bo[:SUB] for t, bo in zip(t16, both)]
        l_pow = [bo[SUB:] for bo in both]
    t16 = [t + _dot3_bd16(t, lp, bmask16) for t, lp in zip(t16, l_pow)]
    d = [jnp.concatenate([t] * nq, axis=0) * diag16 for t in t16]
    for off in (off_a, off_b):
        x = [_dot(di.astype(BF16), _block_diag(l * off, bmask)) for di, l in zip(d, l_mats)]
        d = [di + _dot(xi.astype(BF16), _block_diag(di, bmask)) for di, xi in zip(d, x)]
    return d


PAIR = 2


def _scan_kernel(r_ref, k_ref, v_ref, zw_ref, za_ref, kk_ref, ka_ref,
                 bmask_ref, bmask16_ref, tri_ref, dm_ref, cm_ref, eye16_ref,
                 y_ref, s_ref, p_ref, rh_ref, q_ref, yl_ref):
    ng = r_ref.shape[2] // GROUP
    bmask = bmask_ref[...]
    di = pl.program_id(0)

    @pl.when(pl.program_id(2) == 0)
    def _():
        s_ref[...] = jnp.zeros_like(s_ref)
        p_ref[...] = jnp.zeros_like(p_ref)
        rh_ref[...] = jnp.zeros_like(rh_ref)
        q_ref[...] = jnp.zeros_like(q_ref)
        yl_ref[...] = jnp.zeros_like(yl_ref)

    for step in range(PAIR):
        h = jnp.where(di == 0, step, PAIR - 1 - step)
        row0 = pl.multiple_of(h * CHUNK, CHUNK)
        for g in range(ng):
            cols = slice(g * GROUP, (g + 1) * GROUP)
            s_bd = _block_diag(s_ref[:, cols], bmask)
            out = _dot(jnp.concatenate([p_ref[h, :, cols], rh_ref[h, :, cols]], axis=0), s_bd)
            s_ref[:, cols] = out[:CHUNK] + q_ref[h, :, cols]
            y_ref[0, 0, pl.ds(row0, CHUNK), cols] = (out[CHUNK:] + yl_ref[h, :, cols]).astype(y_ref.dtype)

    eye = cm_ref[0]
    m_strict = dm_ref[0, 0]
    m_incl = dm_ref[0, 1]
    chains = [(h, g) for h in range(PAIR) for g in range(ng)]
    pieces = lambda x: [x[h * CHUNK:(h + 1) * CHUNK, g * GROUP:(g + 1) * GROUP] for h, g in chains]
    halves = lambda f, x: jnp.concatenate([f(x[h * CHUNK:(h + 1) * CHUNK]) for h in range(PAIR)], axis=0)
    bd = lambda x: _block_diag(x, bmask)
    stack = lambda x, y: jnp.concatenate([x, y], axis=0)

    r = r_ref[0].astype(F32)
    k = k_ref[0].astype(F32)
    v = v_ref[0].astype(F32)
    lw = (-math.exp(-0.5)) * jax.nn.sigmoid(zw_ref[0].astype(F32))
    a = jax.nn.sigmoid(za_ref[0].astype(F32))
    kkr = k * kk_ref[...]
    sq = jnp.concatenate(pieces(kkr * kkr), axis=0)
    sq_hi, sq_lo = _split2(sq)
    ssq = _dot(sq_hi, bmask) + _dot(sq_lo, bmask)
    ssq = jnp.concatenate(
        [jnp.concatenate([ssq[(h * ng + g) * CHUNK:(h * ng + g + 1) * CHUNK] for g in range(ng)], axis=1)
         for h in range(PAIR)], axis=0)
    kk = kkr * lax.rsqrt(jnp.maximum(ssq, 1e-24))
    kd = k * (1.0 + (a - 1.0) * ka_ref[...])
    bb = kk * a
    tri = tri_ref[0]

    def cumulative(x):
        x_hi, x_lo = _split2(x)
        return _dot(tri, x_hi) + _dot(tri, x_lo)

    g_cum = halves(cumulative, lw)
    g_end = halves(lambda x: jnp.broadcast_to(jnp.sum(x, axis=0, keepdims=True), x.shape), lw)
    e_neg = jnp.exp(-g_cum)
    e_end = jnp.exp(g_end - g_cum)
    a_t = pieces(-kk * jnp.exp(g_cum - lw))
    r_t = pieces(r * jnp.exp(g_cum))
    b_t = pieces(bb * e_neg)
    k_t = pieces(kd * e_neg)
    b_h = pieces((bb * e_end).astype(BF16))
    k_h = pieces((kd * e_end).astype(BF16))
    v_g = pieces(v)
    decay_end = [x[0:1] for x in pieces(jnp.exp(g_end))]

    ar = [stack(x, y).astype(BF16) for x, y in zip(a_t, r_t)]
    mbk = [_dot_bd_pair(x, y, z, bmask, transposed=True) for x, y, z in zip(ar, b_t, k_t)]
    mb = [x[0] for x in mbk]
    mk = [x[1] for x in mbk]
    l_mat = [x[:CHUNK] * m_strict for x in mb]
    m_rb = [(x[CHUNK:] * m_incl).astype(BF16) for x in mb]
    m_k = [stack(x[:CHUNK] * m_strict, x[CHUNK:] * m_incl).astype(BF16) for x in mk]
    mv = [_dot(x, bd(y)) for x, y in zip(m_k, v_g)]
    t_mat = _unit_triangular_inverse(l_mat, bmask, bmask16_ref[...], eye16_ref[...],
                                     cm_ref[1], cm_ref[2], cm_ref[3])
    t_g = [stack(t, _dot(m, bd(t))).astype(BF16) for t, m in zip(t_mat, m_rb)]
    tau = [_dot_bd_pair(x, y, z[:CHUNK], bmask) for x, y, z in zip(t_g, a_t, mv)]
    ta = [x[0] for x in tau]
    tu = [x[1] for x in tau]
    a_h = [x[:CHUNK].astype(BF16) for x in ta]
    u_0 = [x[:CHUNK] for x in tu]
    rh = [x + y[CHUNK:] for x, y in zip(r_t, ta)]
    yl = [x[CHUNK:] + y[CHUNK:] for x, y in zip(tu, mv)]
    p_full = [_dot_tn(x, y) for x, y in zip(b_h, a_h)]
    q_full = [_dot_tn(stack(x, y), stack(u, w).astype(BF16)) for x, y, u, w in zip(b_h, k_h, u_0, v_g)]
    for i, (h, g) in enumerate(chains):
        cols = slice(g * GROUP, (g + 1) * GROUP)
        p_ref[h, :, cols] = (_fold_heads(p_full[i]) + eye * decay_end[i]).astype(p_ref.dtype)
        rh_ref[h, :, cols] = rh[i].astype(rh_ref.dtype)
        q_ref[h, :, cols] = _fold_heads(q_full[i])
        yl_ref[h, :, cols] = yl[i]


def _scan_consts():
    lane = np.arange(GROUP)
    bmask = (lane[:, None] // HEAD == lane[None, :] // HEAD).astype(np.float32)
    i = np.arange(CHUNK)[:, None]
    j = np.arange(CHUNK)[None, :]
    jl = (lane % HEAD)[None, :]
    tri = np.stack([(j <= i), (j >= i)]).astype(np.float32)
    dir_masks = np.stack([np.stack([(jl < i), (jl <= i)]),
                          np.stack([(jl > i), (jl >= i)])]).astype(np.float32)
    same16 = (jl // SUB == i // SUB)
    same32 = (jl // (2 * SUB) == i // (2 * SUB))
    common = np.stack([(jl == i), same16, same32 & ~same16, ~same32]).astype(np.float32)
    bmask16 = (lane[:, None] // SUB == lane[None, :] // SUB).astype(np.float32)
    eye16 = ((lane % SUB)[None, :] == np.arange(SUB)[:, None]).astype(np.float32)
    return (jnp.asarray(bmask, BF16), jnp.asarray(bmask16, BF16), jnp.asarray(tri, BF16),
            jnp.asarray(dir_masks, F32), jnp.asarray(common, F32), jnp.asarray(eye16, F32))


def _scan(r, k, v, zw, za, k_k, k_a, n_ctx_chunks):
    b, l, d = r.shape
    rows = PAIR * CHUNK
    npair = l // rows
    assert l % rows == 0 and n_ctx_chunks % PAIR == 0
    n_ctx = n_ctx_chunks // PAIR
    bmask, bmask16, tri, dir_masks, common, eye16 = _scan_consts()

    def pair_of(di, s):
        back = jnp.where(s < n_ctx, n_ctx - 1 - s, npair - 1 + n_ctx - s)
        return jnp.where(di == 0, s, back)

    fold_pair = lambda di, s: pair_of(di, jnp.minimum(s, npair - 1))
    apply_pair = lambda di, s: pair_of(di, jnp.maximum(s - 1, 0))
    tok = pl.BlockSpec((1, rows, d), lambda di, bi, s: (bi, fold_pair(di, s), 0))
    tok2 = pl.BlockSpec((1, rows, d), lambda di, bi, s: (bi, fold_pair(di, s), di))
    par = pl.BlockSpec((1, d), lambda di, bi, s: (0, 0))
    return pl.pallas_call(
        _scan_kernel,
        out_shape=jax.ShapeDtypeStruct((2, b, l, d), BF16),
        grid=(2, b, npair + 1),
        in_specs=[tok, tok, tok, tok2, tok2, par, par,
                  pl.BlockSpec((GROUP, GROUP), lambda di, bi, s: (0, 0)),
                  pl.BlockSpec((GROUP, GROUP), lambda di, bi, s: (0, 0)),
                  pl.BlockSpec((1, CHUNK, CHUNK), lambda di, bi, s: (di, 0, 0)),
                  pl.BlockSpec((1, 2, CHUNK, GROUP), lambda di, bi, s: (di, 0, 0, 0)),
                  pl.BlockSpec((4, CHUNK, GROUP), lambda di, bi, s: (0, 0, 0)),
                  pl.BlockSpec((SUB, GROUP), lambda di, bi, s: (0, 0))],
        out_specs=pl.BlockSpec((1, 1, rows, d), lambda di, bi, s: (di, bi, apply_pair(di, s), 0)),
        scratch_shapes=[pltpu.VMEM((CHUNK, d), F32), pltpu.VMEM((PAIR, CHUNK, d), BF16),
                        pltpu.VMEM((PAIR, CHUNK, d), BF16), pltpu.VMEM((PAIR, CHUNK, d), F32),
                        pltpu.VMEM((PAIR, CHUNK, d), F32)],
        compiler_params=_params(("parallel", "parallel", "arbitrary")),
        name="scan",
    )(r, k, v, zw, za, k_k.reshape(1, d), k_a.reshape(1, d), bmask, bmask16, tri, dir_masks, common, eye16)


def _head_sum(x, ones_bd):
    return _dot(x.astype(BF16), ones_bd)


def _rwkv_post_kernel(y_ref, r_ref, k_ref, v_ref, za0_ref, za1_ref, gate_ref,
                      ka_ref, rk_ref, gnw_ref, gnb_ref, ones_ref, o_ref):
    ones_bd = ones_ref[...]
    tm, d = o_ref.shape[1], o_ref.shape[2]
    ng = d // GROUP
    to_rows = lambda x: jnp.concatenate([x[:, g * GROUP:(g + 1) * GROUP] for g in range(ng)], axis=0)
    to_cols = lambda x: jnp.concatenate([x[g * tm:(g + 1) * tm] for g in range(ng)], axis=1)
    head_mean = lambda x: to_cols(_head_sum(to_rows(x), ones_bd)) * (1.0 / HEAD)
    y = y_ref[0, 0].astype(F32) + y_ref[1, 0].astype(F32)
    yc = y - head_mean(y)
    var = head_mean(yc * yc)
    o = yc * lax.rsqrt(var + GN_EPS) * gnw_ref[...] + gnb_ref[...]
    a_sum = jax.nn.sigmoid(za0_ref[0].astype(F32)) + jax.nn.sigmoid(za1_ref[0].astype(F32))
    r = r_ref[0].astype(F32)
    k_sum = k_ref[0].astype(F32) * (2.0 + (a_sum - 2.0) * ka_ref[...])
    bonus = head_mean(r * k_sum * rk_ref[...]) * float(HEAD) * v_ref[0].astype(F32)
    o_ref[0] = ((o + bonus) * gate_ref[0].astype(F32)).astype(o_ref.dtype)


def _rwkv_post(y, r, k, v, za, gate, k_a, r_k, gn_w, gn_b, n_ctx, tm=256):
    _, b, l, d = y.shape
    t = l - n_ctx
    off = n_ctx // tm
    ones_bd = _scan_consts()[0]
    tok = pl.BlockSpec((1, tm, d), lambda bi, s: (bi, s + off, 0))
    par = pl.BlockSpec((1, d), lambda bi, s: (0, 0))
    return pl.pallas_call(
        _rwkv_post_kernel,
        out_shape=jax.ShapeDtypeStruct((b, t, d), BF16),
        grid=(b, t // tm),
        in_specs=[pl.BlockSpec((2, 1, tm, d), lambda bi, s: (0, bi, s + off, 0)),
                  tok, tok, tok,
                  pl.BlockSpec((1, tm, d), lambda bi, s: (bi, s + off, 0)),
                  pl.BlockSpec((1, tm, d), lambda bi, s: (bi, s + off, 1)),
                  tok, par, par, par, par,
                  pl.BlockSpec((GROUP, GROUP), lambda bi, s: (0, 0))],
        out_specs=pl.BlockSpec((1, tm, d), lambda bi, s: (bi, s, 0)),
        compiler_params=_params(("parallel", "parallel")),
        name="rwkv_post",
    )(y, r, k, v, za, za, gate, k_a.reshape(1, d), r_k.reshape(1, d), gn_w.reshape(1, d),
      gn_b.reshape(1, d), ones_bd)


def _proj_norm_kernel(a_ref, w_ref, x_ref, pos_ref, g_ref, lnw_ref, lnb_ref, sc_ref, sh_ref,
                      o_ref, h_ref, acc_ref, *, alpha):
    j = pl.program_id(2)
    nj = acc_ref.shape[0]
    tn = acc_ref.shape[2]
    acc_ref[j] = _dot(a_ref[...], w_ref[...].astype(BF16))

    @pl.when(j == nj - 1)
    def _():
        gate = g_ref[0]
        for jj in range(nj):
            cols = slice(jj * tn, (jj + 1) * tn)
            o_ref[:, cols] = alpha * (x_ref[:, cols] + pos_ref[:, cols]) + gate[:, cols] * acc_ref[jj]
        x_new = _layer_norm(o_ref[...], lnw_ref[...], lnb_ref[...])
        o_ref[...] = x_new
        h_ref[...] = (x_new * (1.0 + sc_ref[0]) + sh_ref[0]).astype(h_ref.dtype)


def _proj_norm(a, w, x, pos, gate, ln_w, ln_b, sc, sh, alpha, tm=512, tn=2048):
    m, k = a.shape
    d = w.shape[1]
    t = pos.shape[0]
    tm = _tile(t, tm)
    tn = _tile(d, tn)
    tpb = t // tm
    nb = m // t
    rows = lambda ti, bi, j: (bi * tpb + ti, 0)
    vec = pl.BlockSpec((1, 1, d), lambda ti, bi, j: (bi, 0, 0))
    return pl.pallas_call(
        functools.partial(_proj_norm_kernel, alpha=alpha),
        out_shape=[jax.ShapeDtypeStruct((m, d), F32), jax.ShapeDtypeStruct((m, d), BF16)],
        grid=(tpb, nb, d // tn),
        in_specs=[pl.BlockSpec((tm, k), rows),
                  pl.BlockSpec((k, tn), lambda ti, bi, j: (0, j)),
                  pl.BlockSpec((tm, d), rows),
                  pl.BlockSpec((tm, d), lambda ti, bi, j: (ti, 0)),
                  vec,
                  pl.BlockSpec((1, d), lambda ti, bi, j: (0, 0)),
                  pl.BlockSpec((1, d), lambda ti, bi, j: (0, 0)),
                  vec, vec],
        out_specs=[pl.BlockSpec((tm, d), rows)] * 2,
        scratch_shapes=[pltpu.VMEM((d // tn, tm, tn), F32)],
        compiler_params=_params(("parallel", "parallel", "arbitrary")),
        name="proj_norm",
    )(a, w, x, pos, gate, ln_w.reshape(1, d), ln_b.reshape(1, d), sc, sh)


def _swiglu_halves(h, w1_ref, w3_ref, w2_ref, lead):
    tf = w1_ref.shape[-1]
    halves = [slice(0, tf // 2), slice(tf // 2, tf)]
    a1 = [_dot(h, w1_ref[lead + (slice(None), c)].astype(BF16)) for c in halves]
    a3 = [_dot(h, w3_ref[lead + (slice(None), c)].astype(BF16)) for c in halves]
    u = [(x * jax.nn.sigmoid(x) * y).astype(BF16) for x, y in zip(a1, a3)]
    y = [_dot(x, w2_ref[lead + (c, slice(None))].astype(BF16)) for x, c in zip(u, halves)]
    return y[0] + y[1]


def _ffn_kernel(h_ref, w1_ref, w3_ref, w2_ref, o_ref, acc_ref):
    j = pl.program_id(1)

    @pl.when(j == 0)
    def _():
        acc_ref[...] = jnp.zeros_like(acc_ref)

    acc_ref[...] += _swiglu_halves(h_ref[...], w1_ref, w3_ref, w2_ref, ())

    @pl.when(j == pl.num_programs(1) - 1)
    def _():
        o_ref[...] = acc_ref[...].astype(o_ref.dtype)


def _ffn(h, w1, w3, w2, tm=1024, tf=512):
    m, d = h.shape
    f = w1.shape[1]
    tm = _tile(m, tm)
    tf = _tile(f, tf)
    return pl.pallas_call(
        _ffn_kernel,
        out_shape=jax.ShapeDtypeStruct((m, d), BF16),
        grid=(m // tm, f // tf),
        in_specs=[pl.BlockSpec((tm, d), lambda i, j: (i, 0)),
                  pl.BlockSpec((d, tf), lambda i, j: (0, j)),
                  pl.BlockSpec((d, tf), lambda i, j: (0, j)),
                  pl.BlockSpec((tf, d), lambda i, j: (j, 0))],
        out_specs=pl.BlockSpec((tm, d), lambda i, j: (i, 0)),
        scratch_shapes=[pltpu.VMEM((tm, d), F32)],
        compiler_params=_params(("parallel", "arbitrary")),
        name="ffn",
    )(h, w1, w3, w2)


def _residual_norm_kernel(x_ref, y_ref, g_ref, lnw_ref, lnb_ref, o_ref, *, alpha):
    z = alpha * x_ref[...] + g_ref[0] * y_ref[...].astype(F32)
    o_ref[...] = _layer_norm(z, lnw_ref[...], lnb_ref[...])


def _residual_norm(x, y, gate, ln_w, ln_b, alpha, rows_per_batch, tm=512):
    m, d = x.shape
    tm = _tile(rows_per_batch, tm)
    tpb = rows_per_batch // tm
    tok = pl.BlockSpec((tm, d), lambda i: (i, 0))
    row = pl.BlockSpec((1, d), lambda i: (0, 0))
    return pl.pallas_call(
        functools.partial(_residual_norm_kernel, alpha=alpha),
        out_shape=jax.ShapeDtypeStruct((m, d), F32),
        grid=(m // tm,),
        in_specs=[tok, tok, pl.BlockSpec((1, 1, d), lambda i: (i // tpb, 0, 0)), row, row],
        out_specs=tok,
        compiler_params=_params(("parallel",)),
        name="residual_norm",
    )(x, y, gate, ln_w.reshape(1, d), ln_b.reshape(1, d))


def _shift_down(x, s, row):
    return jnp.where(row >= s, pltpu.roll(x, s, axis=0), 0.0)


def _shift_up(x, s, row):
    t = x.shape[0]
    return jnp.where(row < t - s, pltpu.roll(x, t - s, axis=0), 0.0)


def _pool_kernel(x_ref, sc_ref, sh_ref, w_ref, scale_ref, o_ref):
    g = pl.program_id(0)
    t = x_ref.shape[1]
    h = x_ref[0] * (1.0 + sc_ref[0]) + sh_ref[0]
    row = lax.broadcasted_iota(jnp.int32, h.shape, 0)
    w = w_ref[0].astype(BF16)
    for gi, win in enumerate(POOL_WINDOWS):
        @pl.when(g == gi)
        def _(win=win):
            half = win // 2
            back = h
            fwd = h
            m = 1
            while m < half:
                back = back + _shift_down(back, m, row)
                fwd = fwd + _shift_up(fwd, m, row)
                m *= 2
            total = _shift_down(back, 1, row) + fwd
            count = (jnp.minimum(row + half, t) - jnp.maximum(row - half, 0)).astype(F32)
            pooled = (total / count - h).astype(BF16)
            o_ref[0] = _dot(pooled, w) * scale_ref[...]


def _pool(x, sc, sh, w_pool, scale):
    b, t, d = x.shape
    ng, p, _ = w_pool.shape
    vec = pl.BlockSpec((1, 1, p), lambda g, bi: (bi, 0, g))
    return pl.pallas_call(
        _pool_kernel,
        out_shape=jax.ShapeDtypeStruct((b, t, d), F32),
        grid=(ng, b),
        in_specs=[pl.BlockSpec((1, t, p), lambda g, bi: (bi, 0, g)), vec, vec,
                  pl.BlockSpec((1, p, p), lambda g, bi: (g, 0, 0)),
                  pl.BlockSpec((1, p), lambda g, bi: (0, g))],
        out_specs=pl.BlockSpec((1, t, p), lambda g, bi: (bi, 0, g)),
        compiler_params=_params(("parallel", "parallel")),
        name="pool",
    )(x, sc, sh, w_pool, scale.reshape(1, d))


def _route_kernel(x_ref, y_ref, g_ref, lnw_ref, lnb_ref, sc_ref, sh_ref, rt_ref,
                  xo_ref, h_ref, route_ref, *, alpha):
    x = _layer_norm(alpha * x_ref[...] + g_ref[0] * y_ref[...], lnw_ref[...], lnb_ref[...])
    xo_ref[...] = x
    h = x * (1.0 + sc_ref[0]) + sh_ref[0]
    h_ref[...] = h
    r1, r2, r3 = _split3(rt_ref[...])
    h1, h2, h3 = _split3(h)
    logits = (_dot_nt(r1, h1) + (_dot_nt(r1, h2) + _dot_nt(r2, h1))
              + (_dot_nt(r1, h3) + _dot_nt(r2, h2) + _dot_nt(r3, h1)))
    mx = jnp.max(logits, axis=0, keepdims=True)
    e = jnp.exp(logits - mx)
    p = e / jnp.sum(e, axis=0, keepdims=True)
    idx = lax.broadcasted_iota(jnp.int32, p.shape, 0)
    p1 = jnp.max(p, axis=0, keepdims=True)
    i1 = jnp.min(jnp.where(p == p1, idx, N_EXPERTS), axis=0, keepdims=True)
    rest = jnp.where(idx == i1, -1.0, p)
    p2 = jnp.max(rest, axis=0, keepdims=True)
    i2 = jnp.min(jnp.where(rest == p2, idx, N_EXPERTS), axis=0, keepdims=True)
    den = p1 + p2
    out = jnp.where(idx == 0, i1.astype(F32), 0.0)
    out = jnp.where(idx == 1, i2.astype(F32), out)
    out = jnp.where(idx == 2, p1 / den, out)
    out = jnp.where(idx == 3, p2 / den, out)
    route_ref[...] = out


def _route(x, y, gate, ln_w, ln_b, sc, sh, router, alpha, rows_per_batch, tm=256):
    m, d = x.shape
    tpb = rows_per_batch // tm
    vec = pl.BlockSpec((1, 1, d), lambda i: (i // tpb, 0, 0))
    row = pl.BlockSpec((1, d), lambda i: (0, 0))
    tok = pl.BlockSpec((tm, d), lambda i: (i, 0))
    return pl.pallas_call(
        functools.partial(_route_kernel, alpha=alpha),
        out_shape=[jax.ShapeDtypeStruct((m, d), F32), jax.ShapeDtypeStruct((m, d), F32),
                   jax.ShapeDtypeStruct((N_EXPERTS, m), F32)],
        grid=(m // tm,),
        in_specs=[tok, tok, vec, row, row, vec, vec,
                  pl.BlockSpec((N_EXPERTS, d), lambda i: (0, 0))],
        out_specs=[tok, tok, pl.BlockSpec((N_EXPERTS, tm), lambda i: (0, i))],
        compiler_params=_params(("parallel",)),
        name="route",
    )(x, y, gate, ln_w.reshape(1, d), ln_b.reshape(1, d), sc, sh, router.T)


def _row_copy(src_hbm, dst_vmem, sem, src_row, dst_row):
    return pltpu.make_async_copy(src_hbm.at[pl.ds(src_row, 1)], dst_vmem.at[pl.ds(dst_row, 1)], sem)


def _moe_kernel(ce_ref, nv_ref, idx_ref, h_ref, w1_ref, w3_ref, w2_ref, o_ref,
                stage_ref, work_ref, gather_sem, *, share):
    c = pl.program_id(0)
    j = pl.program_id(1)
    n_chunks = pl.num_programs(0)
    nj = pl.num_programs(1)
    nv = nv_ref[c]
    full = work_ref.shape[0]
    stage_rows = stage_ref.shape[0]

    def gather(chunk, r):
        return _row_copy(h_ref, stage_ref, gather_sem, idx_ref[chunk * full + r], r)

    def wait_all_gathers():
        def body(r, carry):
            gather(0, r).wait()
            return carry
        lax.fori_loop(0, stage_rows, body, 0, unroll=8)

    @pl.when(j == 0)
    def _():
        @pl.when(c == 0)
        def _():
            def body(r, carry):
                gather(0, r).start()
                return carry
            lax.fori_loop(0, stage_rows, body, 0, unroll=8)

        o_ref[...] = jnp.zeros_like(o_ref)
        wait_all_gathers()
        work_ref[...] = stage_ref[0:full, :].astype(work_ref.dtype)

    def request_next_share():
        for i in range(share):
            gather(c + 1, j * share + i).start(priority=1)

    def swiglu_rows(rows):
        o_ref[rows, :] += _swiglu_halves(work_ref[rows, :], w1_ref, w3_ref, w2_ref, (0,))

    n_sub = (nv + (MOE_SUB - 1)) // MOE_SUB
    for k in range(full // MOE_SUB + 1):
        @pl.when(n_sub == k)
        def _(k=k):
            request_next_share()
            if k:
                swiglu_rows(slice(0, k * MOE_SUB))

    @pl.when(jnp.logical_and(j == nj - 1, c == n_chunks - 1))
    def _():
        wait_all_gathers()


def _moe_ffn(h, src_idx, chunk_expert, chunk_valid, w1, w3, w2, n_chunks, tf=512):
    d = h.shape[1]
    f = w1.shape[2]
    tf = _tile(f, tf)
    nj = f // tf
    share = -(-MOE_CHUNK // nj)
    share = -(-share // 8) * 8
    stage_rows = share * nj
    assert src_idx.shape[0] >= n_chunks * MOE_CHUNK + stage_rows

    def jeff(c, j, nv):
        return jnp.where(nv[c] > 0, j, nj - 1)

    return pl.pallas_call(
        functools.partial(_moe_kernel, share=share),
        out_shape=jax.ShapeDtypeStruct((n_chunks * MOE_CHUNK, d), F32),
        grid_spec=pltpu.PrefetchScalarGridSpec(
            num_scalar_prefetch=3,
            grid=(n_chunks, nj),
            in_specs=[pl.BlockSpec(memory_space=pl.ANY),
                      pl.BlockSpec((1, d, tf), lambda c, j, ce, nv, ix: (ce[c], 0, jeff(c, j, nv))),
                      pl.BlockSpec((1, d, tf), lambda c, j, ce, nv, ix: (ce[c], 0, jeff(c, j, nv))),
                      pl.BlockSpec((1, tf, d), lambda c, j, ce, nv, ix: (ce[c], jeff(c, j, nv), 0))],
            out_specs=pl.BlockSpec((MOE_CHUNK, d), lambda c, j, ce, nv, ix: (c, 0)),
            scratch_shapes=[pltpu.VMEM((stage_rows, d), F32), pltpu.VMEM((MOE_CHUNK, d), BF16),
                            pltpu.SemaphoreType.DMA]),
        compiler_params=_params(("arbitrary", "arbitrary"), MOE_VMEM_LIMIT_BYTES),
        name="moe_ffn",
    )(chunk_expert, chunk_valid, src_idx, h, w1, w3, w2)


def _combine_kernel(p0_ref, p1_ref, ys_ref, x_ref, gates_ref, g_ref, lnw_ref, lnb_ref,
                    o_ref, b0_ref, b1_ref, sem, *, alpha):
    i = pl.program_id(0)
    n = pl.num_programs(0)
    rows = b0_ref.shape[1]
    slot = i % 2

    def copies(tile, to_slot, r):
        return (_row_copy(ys_ref, b0_ref.at[to_slot], sem.at[to_slot], p0_ref[tile * rows + r], r),
                _row_copy(ys_ref, b1_ref.at[to_slot], sem.at[to_slot], p1_ref[tile * rows + r], r))

    def wait_slot(to_slot):
        def body(r, carry):
            for cp in copies(0, to_slot, r):
                cp.wait()
            return carry
        lax.fori_loop(0, rows, body, 0, unroll=4)

    @pl.when(i == 0)
    def _():
        def body(r, carry):
            for cp in copies(0, 0, r):
                cp.start()
            return carry
        lax.fori_loop(0, rows, body, 0, unroll=4)

    wait_slot(slot)
    for r in range(rows):
        for cp in copies(i + 1, 1 - slot, r):
            cp.start()
    gates = gates_ref[...]
    y = gates[:, 0:1] * b0_ref[slot] + gates[:, 1:2] * b1_ref[slot]
    z = alpha * x_ref[...] + g_ref[0] * y
    o_ref[...] = _layer_norm(z, lnw_ref[...], lnb_ref[...])

    @pl.when(i == n - 1)
    def _():
        wait_slot(1 - slot)


def _combine(ys, pos0, pos1, gates, x, gate_vec, ln_w, ln_b, alpha, rows_per_batch, rows=256):
    m, d = x.shape
    tpb = rows_per_batch // rows
    tok = lambda i, a, b: (i, 0)
    return pl.pallas_call(
        functools.partial(_combine_kernel, alpha=alpha),
        out_shape=jax.ShapeDtypeStruct((m, d), F32),
        grid_spec=pltpu.PrefetchScalarGridSpec(
            num_scalar_prefetch=2,
            grid=(m // rows,),
            in_specs=[pl.BlockSpec(memory_space=pl.ANY),
                      pl.BlockSpec((rows, d), tok),
                      pl.BlockSpec((rows, 2), tok),
                      pl.BlockSpec((1, 1, d), lambda i, a, b: (i // tpb, 0, 0)),
                      pl.BlockSpec((1, d), lambda i, a, b: (0, 0)),
                      pl.BlockSpec((1, d), lambda i, a, b: (0, 0))],
            out_specs=pl.BlockSpec((rows, d), tok),
            scratch_shapes=[pltpu.VMEM((2, rows, d), F32), pltpu.VMEM((2, rows, d), F32),
                            pltpu.SemaphoreType.DMA((2,))]),
        compiler_params=_params(("arbitrary",)),
        name="moe_combine",
    )(pos0, pos1, ys, x, gates, gate_vec, ln_w.reshape(1, d), ln_b.reshape(1, d))


def _routing_tables(route, n_chunks, table_len):
    n = route.shape[1]
    experts = jnp.concatenate([route[0], route[1]]).astype(jnp.int32)
    onehot = (experts[:, None] == jnp.arange(N_EXPERTS, dtype=jnp.int32)[None, :]).astype(jnp.int32)
    csum = jnp.cumsum(onehot, axis=0)
    rank = jnp.sum((csum - 1) * onehot, axis=1)
    counts = csum[-1]
    chunks_e = (counts + MOE_CHUNK - 1) // MOE_CHUNK
    chunk_end = jnp.cumsum(chunks_e)
    chunk_start = chunk_end - chunks_e
    dest = (chunk_start * MOE_CHUNK)[experts] + rank
    token = jnp.arange(2 * n, dtype=jnp.int32) % n
    src_idx = jnp.zeros((table_len,), jnp.int32).at[dest].set(token)
    cid = jnp.arange(n_chunks, dtype=jnp.int32)
    used = cid < chunk_end[-1]
    last_used = jnp.maximum(chunk_end[-1] - 1, 0)
    ce = jnp.sum((jnp.minimum(cid, last_used)[:, None] >= chunk_end[None, :]).astype(jnp.int32), axis=1)
    ce = jnp.minimum(ce, N_EXPERTS - 1)
    nvalid = jnp.clip(counts[ce] - (cid - chunk_start[ce]) * MOE_CHUNK, 0, MOE_CHUNK)
    nvalid = jnp.where(used, nvalid, 0).astype(jnp.int32)
    return src_idx, ce, nvalid, dest[:n], dest[n:]


def _position_embedding(rows, width, d):
    quarter = d // 4
    omega = 1.0 / (POS_BASE ** (jnp.arange(quarter, dtype=F32) / quarter))
    ar = jnp.arange(rows, dtype=F32)[:, None] * omega[None, :]
    ac = jnp.arange(width, dtype=F32)[:, None] * omega[None, :]
    row_part = jnp.repeat(jnp.concatenate([jnp.sin(ar), jnp.cos(ar)], axis=-1), width, axis=0)
    col_part = jnp.tile(jnp.concatenate([jnp.sin(ac), jnp.cos(ac)], axis=-1), (rows, 1))
    return jnp.concatenate([row_part, col_part], axis=-1)


def _block_diag2(w):
    z = jnp.zeros_like(w[0])
    return jnp.concatenate([jnp.concatenate([w[0], z], axis=1), jnp.concatenate([z, w[1]], axis=1)], axis=0)


def kernel(x, c, ctx, c_ctx, w_mod, b_mod, ln_w, ln_b, rwkv_mu, rwkv_w_r, rwkv_w_k, rwkv_w_v, rwkv_w_o, rwkv_decay_w0, rwkv_decay_w1, rwkv_decay_w2, rwkv_iclr_a0, rwkv_iclr_a1, rwkv_iclr_a2, rwkv_gate_g1, rwkv_gate_g2, rwkv_k_k, rwkv_k_a, rwkv_r_k, rwkv_gn_w, rwkv_gn_b, pool_w, pool_scale, ffn_w1, ffn_w3, ffn_w2, moe_router, moe_w1, moe_w3, moe_w2):
    b, t, d = x.shape
    n_ctx = ctx.shape[1]
    depth = w_mod.shape[0]
    assert depth == 2 and rwkv_mu.shape[0] == 1 and pool_w.shape[0] == 1
    alpha = (2.0 * depth) ** 0.25
    grid_w = 64
    l = n_ctx + t
    n = b * t

    cond = jnp.zeros((8, d), F32).at[:b].set(c).at[b].set(c_ctx)
    mod = _adaln(cond, w_mod, b_mod).reshape(depth, 8, 6, d)
    lat = lambda layer, which: mod[layer, :b, which].reshape(b, 1, d)
    cvec = lambda layer, which: mod[layer, b, which].reshape(1, 1, d)

    pos = _position_embedding(t // grid_w, grid_w, d)

    xr, xw, xk, xv, xa, xg = _rwkv_mix(ctx, x, pos, lat(0, 1), lat(0, 0), cvec(0, 1), cvec(0, 0), rwkv_mu[0])
    flat = lambda a: a.reshape(b * l, a.shape[-1])
    r = _mm(flat(xr), rwkv_w_r[0], out_dtype=BF16, tn=1024)
    k = _mm(flat(xk), rwkv_w_k[0], out_dtype=BF16, tn=1024)
    v = _mm(flat(xv), rwkv_w_v[0], out_dtype=BF16, tn=1024)
    dw1 = jnp.concatenate([rwkv_decay_w1[0, 0], rwkv_decay_w1[0, 1]], axis=1)
    ia1 = jnp.concatenate([rwkv_iclr_a1[0, 0], rwkv_iclr_a1[0, 1]], axis=1)
    lora_w = _mm(flat(xw), dw1, act="tanh", out_dtype=BF16)
    zw = _mm(lora_w, _block_diag2(rwkv_decay_w2[0]), rwkv_decay_w0[0].reshape(1, 2 * d), out_dtype=BF16, tn=2048)
    lora_a = _mm(flat(xa), ia1, out_dtype=BF16)
    za = _mm(lora_a, _block_diag2(rwkv_iclr_a2[0]), rwkv_iclr_a0[0].reshape(1, 2 * d), out_dtype=BF16, tn=2048)
    lora_g = _mm(flat(xg), rwkv_gate_g1[0], act="sigmoid", out_dtype=BF16)
    gate = _mm(lora_g, rwkv_gate_g2[0], out_dtype=BF16, tn=2048)
    seq = lambda a: a.reshape(b, l, a.shape[-1])
    y_scan = _scan(seq(r), seq(k), seq(v), seq(zw), seq(za), rwkv_k_k[0], rwkv_k_a[0], n_ctx // CHUNK)
    og = _rwkv_post(y_scan, seq(r), seq(k), seq(v), seq(za), seq(gate), rwkv_k_a[0],
                    rwkv_r_k[0].reshape(d), rwkv_gn_w[0], rwkv_gn_b[0], n_ctx)
    x1, h1 = _proj_norm(og.reshape(n, d), rwkv_w_o[0].astype(BF16), x.reshape(n, d), pos, lat(0, 2),
                        ln_w[0, 0], ln_b[0, 0], lat(0, 4), lat(0, 3), alpha)
    y_ffn = _ffn(h1, ffn_w1[0], ffn_w3[0], ffn_w2[0])
    x2 = _residual_norm(x1, y_ffn, lat(0, 5), ln_w[0, 1], ln_b[0, 1], alpha, t)

    y_pool = _pool(x2.reshape(b, t, d), lat(1, 1), lat(1, 0), pool_w[0], pool_scale[0])
    x3, h3, route = _route(x2, y_pool.reshape(n, d), lat(1, 2), ln_w[1, 0], ln_b[1, 0],
                           lat(1, 4), lat(1, 3), moe_router[0], alpha, t)
    n_chunks = (2 * n + N_EXPERTS * (MOE_CHUNK - 1)) // MOE_CHUNK
    src_idx, chunk_expert, chunk_valid, pos0, pos1 = _routing_tables(route, n_chunks, (n_chunks + 2) * MOE_CHUNK)
    ys = _moe_ffn(h3, src_idx, chunk_expert, chunk_valid, moe_w1[0], moe_w3[0], moe_w2[0], n_chunks)
    gates = jnp.stack([route[2], route[3]], axis=1)
    spare = jnp.zeros((256,), jnp.int32)
    out = _combine(ys, jnp.concatenate([pos0, spare]), jnp.concatenate([pos1, spare]), gates, x3,
                   lat(1, 5), ln_w[1, 1], ln_b[1, 1], alpha, t)
    return out.reshape(b, t, d)
```

```python
import functools
import math

import numpy as np
import jax
import jax.numpy as jnp
from jax import lax
from jax.experimental import pallas as pl
from jax.experimental.pallas import tpu as pltpu

F32 = jnp.float32
BF16 = jnp.bfloat16

LANES = 128
HEAD = 64
GROUP = 4 * HEAD
CHUNK = 64
LN_EPS = 1e-5
GN_EPS = 64e-5
POS_BASE = 10000.0
POOL_WINDOWS = (2, 4, 8, 16)
N_EXPERTS = 8
MOE_CHUNK = 1024
MOE_SUB = 256
VMEM_LIMIT_BYTES = 56 * 1024 * 1024
MOE_VMEM_LIMIT_BYTES = 60 * 1024 * 1024


def _params(semantics, vmem_limit_bytes=VMEM_LIMIT_BYTES):
    return pltpu.CompilerParams(dimension_semantics=semantics, vmem_limit_bytes=vmem_limit_bytes)


def _tile(n, preferred):
    t = min(preferred, n)
    while n % t:
        t //= 2
    return t


def _dot(a, b):
    return jnp.dot(a, b, preferred_element_type=F32)


def _dot_nt(a, b):
    return lax.dot_general(a, b, (((1,), (1,)), ((), ())), preferred_element_type=F32)


def _dot_tn(a, b):
    return lax.dot_general(a, b, (((0,), (0,)), ((), ())), preferred_element_type=F32)


def _split2(x):
    hi = x.astype(BF16)
    lo = (x - hi.astype(F32)).astype(BF16)
    return hi, lo


def _split3(x):
    hi = x.astype(BF16)
    r1 = x - hi.astype(F32)
    mid = r1.astype(BF16)
    lo = (r1 - mid.astype(F32)).astype(BF16)
    return hi, mid, lo


def _layer_norm(z, w, b):
    mu = jnp.mean(z, axis=-1, keepdims=True)
    zc = z - mu
    var = jnp.mean(zc * zc, axis=-1, keepdims=True)
    return zc * lax.rsqrt(var + LN_EPS) * w + b


def _adaln_kernel(c_ref, w_ref, b_ref, o_ref):
    c = c_ref[...]
    a = (c * jax.nn.sigmoid(c)).astype(BF16)
    o_ref[0] = _dot(a, w_ref[0].astype(BF16)) + b_ref[0]


def _adaln(cond, w_mod, b_mod, tn=1024):
    depth, d, n = w_mod.shape
    tn = _tile(n, tn)
    rows = cond.shape[0]
    return pl.pallas_call(
        _adaln_kernel,
        out_shape=jax.ShapeDtypeStruct((depth, rows, n), F32),
        grid=(depth, n // tn),
        in_specs=[pl.BlockSpec((rows, d), lambda l, j: (0, 0)),
                  pl.BlockSpec((1, d, tn), lambda l, j: (l, 0, j)),
                  pl.BlockSpec((1, 1, tn), lambda l, j: (l, 0, j))],
        out_specs=pl.BlockSpec((1, rows, tn), lambda l, j: (l, 0, j)),
        compiler_params=_params(("parallel", "parallel")),
        name="adaln",
    )(cond, w_mod, b_mod.reshape(depth, 1, n))


def _mm_kernel(a_ref, w_ref, b_ref, o_ref, *, act):
    acc = _dot(a_ref[...], w_ref[...].astype(BF16)) + b_ref[...]
    if act == "tanh":
        acc = jnp.tanh(acc)
    elif act == "sigmoid":
        acc = jax.nn.sigmoid(acc)
    o_ref[...] = acc.astype(o_ref.dtype)


def _mm(a, w, bias=None, *, act=None, out_dtype=F32, tm=2304, tn=512):
    m, k = a.shape
    n = w.shape[1]
    tm = _tile(m, tm)
    tn = _tile(n, tn)
    if bias is None:
        bias = jnp.zeros((1, n), F32)
    return pl.pallas_call(
        functools.partial(_mm_kernel, act=act),
        out_shape=jax.ShapeDtypeStruct((m, n), out_dtype),
        grid=(m // tm, n // tn),
        in_specs=[pl.BlockSpec((tm, k), lambda i, j: (i, 0)),
                  pl.BlockSpec((k, tn), lambda i, j: (0, j)),
                  pl.BlockSpec((1, tn), lambda i, j: (0, j))],
        out_specs=pl.BlockSpec((tm, tn), lambda i, j: (i, j)),
        compiler_params=_params(("parallel", "arbitrary")),
        name="matmul",
    )(a, w, bias.reshape(1, n))


def _mix_kernel(ctx_ref, x_ref, xp_ref, xn_ref, pos_ref, pp_ref, pn_ref,
                sc_ref, sh_ref, csc_ref, csh_ref, mu_ref,
                o0, o1, o2, o3, o4, o5, *, n_lat_tiles):
    s = pl.program_id(0)
    is_ctx = s == 0
    tm = x_ref.shape[1]
    scale = jnp.where(is_ctx, csc_ref[0], sc_ref[0]) + 1.0
    shift = jnp.where(is_ctx, csh_ref[0], sh_ref[0])
    src = jnp.where(is_ctx, ctx_ref[0], x_ref[0] + pos_ref[...])
    h = src * scale + shift
    has_prev = s > 1
    has_next = jnp.logical_and(s >= 1, s < n_lat_tiles)
    h_prev = jnp.where(has_prev, (xp_ref[0] + pp_ref[...]) * scale + shift, 0.0)[7:8]
    h_next = jnp.where(has_next, (xn_ref[0] + pn_ref[...]) * scale + shift, 0.0)[0:1]
    row = lax.broadcasted_iota(jnp.int32, h.shape, 0)
    h_m1 = jnp.where(row == 0, h_prev, pltpu.roll(h, 1, axis=0))
    h_p1 = jnp.where(row == tm - 1, h_next, pltpu.roll(h, tm - 1, axis=0))
    xx = 0.5 * (h_m1 + h_p1) - h
    for n, o_ref in enumerate((o0, o1, o2, o3, o4, o5)):
        o_ref[0] = (h + xx * mu_ref[n:n + 1]).astype(o_ref.dtype)


def _rwkv_mix(ctx, x, pos, sc, sh, csc, csh, mu):
    b, t, d = x.shape
    tm = ctx.shape[1]
    assert t % tm == 0 and tm % 8 == 0
    n_lat = t // tm
    r8 = tm // 8
    lat = lambda s, bi: (bi, jnp.maximum(s - 1, 0), 0)
    prev8 = lambda s, bi: (bi, jnp.maximum((s - 1) * r8 - 1, 0), 0)
    next8 = lambda s, bi: (bi, jnp.minimum(jnp.maximum(s, 1) * r8, t // 8 - 1), 0)
    vec = pl.BlockSpec((1, 1, d), lambda s, bi: (bi, 0, 0))
    cvec = pl.BlockSpec((1, 1, d), lambda s, bi: (0, 0, 0))
    out_sds = jax.ShapeDtypeStruct((b, tm + t, d), BF16)
    return pl.pallas_call(
        functools.partial(_mix_kernel, n_lat_tiles=n_lat),
        out_shape=[out_sds] * 6,
        grid=(n_lat + 1, b),
        in_specs=[pl.BlockSpec((1, tm, d), lambda s, bi: (bi, 0, 0)),
                  pl.BlockSpec((1, tm, d), lat),
                  pl.BlockSpec((1, 8, d), prev8),
                  pl.BlockSpec((1, 8, d), next8),
                  pl.BlockSpec((tm, d), lambda s, bi: (jnp.maximum(s - 1, 0), 0)),
                  pl.BlockSpec((8, d), lambda s, bi: (jnp.maximum((s - 1) * r8 - 1, 0), 0)),
                  pl.BlockSpec((8, d), lambda s, bi: (jnp.minimum(jnp.maximum(s, 1) * r8, t // 8 - 1), 0)),
                  vec, vec, cvec, cvec,
                  pl.BlockSpec((6, d), lambda s, bi: (0, 0))],
        out_specs=[pl.BlockSpec((1, tm, d), lambda s, bi: (bi, s, 0))] * 6,
        compiler_params=_params(("parallel", "parallel")),
        name="rwkv_mix",
    )(ctx, x, x, x, pos, pos, pos, sc, sh, csc, csh, mu)


def _block_diag(x, bmask):
    xb = x.astype(BF16)
    zero = jnp.zeros((HEAD, GROUP // 2), BF16)
    rows = []
    for h in range(GROUP // HEAD):
        t = h // 2
        blk = xb[:, t * LANES:(t + 1) * LANES] * bmask[h * HEAD:(h + 1) * HEAD, t * LANES:(t + 1) * LANES]
        rows.append(jnp.concatenate([blk, zero] if t == 0 else [zero, blk], axis=1))
    return jnp.concatenate(rows, axis=0)


def _fold_heads(full):
    lane = lax.broadcasted_iota(jnp.int32, (HEAD, LANES), 1)
    tiles = []
    for t in range(GROUP // LANES):
        even = full[(2 * t) * HEAD:(2 * t + 1) * HEAD, t * LANES:(t + 1) * LANES]
        odd = full[(2 * t + 1) * HEAD:(2 * t + 2) * HEAD, t * LANES:(t + 1) * LANES]
        tiles.append(jnp.where(lane < HEAD, even, odd))
    return jnp.concatenate(tiles, axis=1)


def _dot_bd_pair(x, y1, y2, bmask, transposed=False):
    out1, out2 = [], []
    for t in range(GROUP // LANES):
        lanes = slice(t * LANES, (t + 1) * LANES)
        mask = bmask[lanes, lanes]
        b1 = jnp.concatenate([y1[:, lanes].astype(BF16)] * 2, axis=0) * mask
        b2 = jnp.concatenate([y2[:, lanes].astype(BF16)] * 2, axis=0) * mask
        if transposed:
            z = _dot_nt(x[:, lanes], jnp.concatenate([b1, b2], axis=0))
        else:
            z = _dot(x[:, lanes], jnp.concatenate([b1, b2], axis=1))
        out1.append(z[:, :LANES])
        out2.append(z[:, LANES:])
    return jnp.concatenate(out1, axis=1), jnp.concatenate(out2, axis=1)


SUB = 16


def _square_bd16(x, bmask16):
    per_tile = LANES // SUB
    tiles = [slice(t * LANES, (t + 1) * LANES) for t in range(GROUP // LANES)]
    lhs = jnp.concatenate([x[:, lanes] for lanes in tiles], axis=0)
    rhs = jnp.concatenate([jnp.concatenate([x[:, lanes]] * per_tile, axis=0) * bmask16[lanes, lanes]
                           for lanes in tiles], axis=1)
    z = _dot(lhs, rhs)
    return jnp.concatenate([z[t * SUB:(t + 1) * SUB, lanes] for t, lanes in enumerate(tiles)], axis=1)


def _dot3_bd16(a, b, bmask16):
    rows = a.shape[0]
    a_hi, a_lo = _split2(a)
    b_hi, b_lo = _split2(b)
    lhs = jnp.concatenate([a_hi, a_lo], axis=0)
    per_tile = LANES // SUB
    outs = []
    for t in range(GROUP // LANES):
        lanes = slice(t * LANES, (t + 1) * LANES)
        mask = bmask16[lanes, lanes]
        rhs = jnp.concatenate([jnp.concatenate([b_hi[:, lanes]] * per_tile, axis=0) * mask,
                               jnp.concatenate([b_lo[:, lanes]] * per_tile, axis=0) * mask], axis=1)
        z = _dot(lhs[:, lanes], rhs)
        outs.append(z[:rows, :LANES] + z[rows:, :LANES] + z[:rows, LANES:])
    return jnp.concatenate(outs, axis=1)


def _unit_triangular_inverse(l_mats, bmask, bmask16, eye16, diag16, off_a, off_b):
    nq = CHUNK // SUB
    l16 = [sum(l[q * SUB:(q + 1) * SUB] * diag16[q * SUB:(q + 1) * SUB] for q in range(nq)).astype(BF16)
           for l in l_mats]
    t16 = [eye16 + x.astype(F32) for x in l16]
    l_pow = [_square_bd16(x, bmask16) for x in l16]
    for _ in range(2):
        both = [_dot3_bd16(jnp.concatenate([t, lp], axis=0), lp, bmask16) for t, lp in zip(t16, l_pow)]
        t16 = [t + bo[:SUB] for t, bo in zip(t16, both)]
        l_pow = [bo[SUB:] for bo in both]
    t16 = [t + _dot3_bd16(t, lp, bmask16) for t, lp in zip(t16, l_pow)]
    d = [jnp.concatenate([t] * nq, axis=0) * diag16 for t in t16]
    for off in (off_a, off_b):
        x = [_dot(di.astype(BF16), _block_diag(l * off, bmask)) for di, l in zip(d, l_mats)]
        d = [di + _dot(xi.astype(BF16), _block_diag(di, bmask)) for di, xi in zip(d, x)]
    return d


PAIR = 4


def _scan_kernel(r_ref, k_ref, v_ref, zw_ref, za_ref, kk_ref, ka_ref,
                 bmask_ref, bmask16_ref, tri_ref, dm_ref, cm_ref, eye16_ref,
                 y_ref, s_ref, p_ref, rh_ref, q_ref, yl_ref):
    ng = r_ref.shape[2] // GROUP
    bmask = bmask_ref[...]
    di = pl.program_id(0)

    @pl.when(pl.program_id(2) == 0)
    def _():
        s_ref[...] = jnp.zeros_like(s_ref)
        p_ref[...] = jnp.zeros_like(p_ref)
        rh_ref[...] = jnp.zeros_like(rh_ref)
        q_ref[...] = jnp.zeros_like(q_ref)
        yl_ref[...] = jnp.zeros_like(yl_ref)

    for step in range(PAIR):
        h = jnp.where(di == 0, step, PAIR - 1 - step)
        row0 = pl.multiple_of(h * CHUNK, CHUNK)
        for g in range(ng):
            cols = slice(g * GROUP, (g + 1) * GROUP)
            s_bd = _block_diag(s_ref[:, cols], bmask)
            out = _dot(jnp.concatenate([p_ref[h, :, cols], rh_ref[h, :, cols]], axis=0), s_bd)
            s_ref[:, cols] = out[:CHUNK] + q_ref[h, :, cols]
            y_ref[0, 0, pl.ds(row0, CHUNK), cols] = (out[CHUNK:] + yl_ref[h, :, cols]).astype(y_ref.dtype)

    eye = cm_ref[0]
    m_strict = dm_ref[0, 0]
    m_incl = dm_ref[0, 1]
    chains = [(h, g) for h in range(PAIR) for g in range(ng)]
    pieces = lambda x: [x[h * CHUNK:(h + 1) * CHUNK, g * GROUP:(g + 1) * GROUP] for h, g in chains]
    halves = lambda f, x: jnp.concatenate([f(x[h * CHUNK:(h + 1) * CHUNK]) for h in range(PAIR)], axis=0)
    bd = lambda x: _block_diag(x, bmask)
    stack = lambda x, y: jnp.concatenate([x, y], axis=0)

    r = r_ref[0].astype(F32)
    k = k_ref[0].astype(F32)
    v = v_ref[0].astype(F32)
    lw = (-math.exp(-0.5)) * jax.nn.sigmoid(zw_ref[0].astype(F32))
    a = jax.nn.sigmoid(za_ref[0].astype(F32))
    kkr = k * kk_ref[...]
    sq = jnp.concatenate(pieces(kkr * kkr), axis=0)
    sq_hi, sq_lo = _split2(sq)
    ssq = _dot(sq_hi, bmask) + _dot(sq_lo, bmask)
    ssq = jnp.concatenate(
        [jnp.concatenate([ssq[(h * ng + g) * CHUNK:(h * ng + g + 1) * CHUNK] for g in range(ng)], axis=1)
         for h in range(PAIR)], axis=0)
    kk = kkr * lax.rsqrt(jnp.maximum(ssq, 1e-24))
    kd = k * (1.0 + (a - 1.0) * ka_ref[...])
    bb = kk * a
    tri = tri_ref[0]

    def cumulative(x):
        x_hi, x_lo = _split2(x)
        return _dot(tri, x_hi) + _dot(tri, x_lo)

    g_cum = halves(cumulative, lw)
    g_end = halves(lambda x: jnp.broadcast_to(jnp.sum(x, axis=0, keepdims=True), x.shape), lw)
    e_neg = jnp.exp(-g_cum)
    e_end = jnp.exp(g_end - g_cum)
    a_t = pieces(-kk * jnp.exp(g_cum - lw))
    r_t = pieces(r * jnp.exp(g_cum))
    b_t = pieces(bb * e_neg)
    k_t = pieces(kd * e_neg)
    b_h = pieces((bb * e_end).astype(BF16))
    k_h = pieces((kd * e_end).astype(BF16))
    v_g = pieces(v)
    decay_end = [x[0:1] for x in pieces(jnp.exp(g_end))]

    ar = [stack(x, y).astype(BF16) for x, y in zip(a_t, r_t)]
    mbk = [_dot_bd_pair(x, y, z, bmask, transposed=True) for x, y, z in zip(ar, b_t, k_t)]
    mb = [x[0] for x in mbk]
    mk = [x[1] for x in mbk]
    l_mat = [x[:CHUNK] * m_strict for x in mb]
    m_rb = [(x[CHUNK:] * m_incl).astype(BF16) for x in mb]
    m_k = [stack(x[:CHUNK] * m_strict, x[CHUNK:] * m_incl).astype(BF16) for x in mk]
    mv = [_dot(x, bd(y)) for x, y in zip(m_k, v_g)]
    t_mat = _unit_triangular_inverse(l_mat, bmask, bmask16_ref[...], eye16_ref[...],
                                     cm_ref[1], cm_ref[2], cm_ref[3])
    t_g = [stack(t, _dot(m, bd(t))).astype(BF16) for t, m in zip(t_mat, m_rb)]
    tau = [_dot_bd_pair(x, y, z[:CHUNK], bmask) for x, y, z in zip(t_g, a_t, mv)]
    ta = [x[0] for x in tau]
    tu = [x[1] for x in tau]
    a_h = [x[:CHUNK].astype(BF16) for x in ta]
    u_0 = [x[:CHUNK] for x in tu]
    rh = [x + y[CHUNK:] for x, y in zip(r_t, ta)]
    yl = [x[CHUNK:] + y[CHUNK:] for x, y in zip(tu, mv)]
    p_full = [_dot_tn(x, y) for x, y in zip(b_h, a_h)]
    q_full = [_dot_tn(stack(x, y), stack(u, w).astype(BF16)) for x, y, u, w in zip(b_h, k_h, u_0, v_g)]
    for i, (h, g) in enumerate(chains):
        cols = slice(g * GROUP, (g + 1) * GROUP)
        p_ref[h, :, cols] = (_fold_heads(p_full[i]) + eye * decay_end[i]).astype(p_ref.dtype)
        rh_ref[h, :, cols] = rh[i].astype(rh_ref.dtype)
        q_ref[h, :, cols] = _fold_heads(q_full[i])
        yl_ref[h, :, cols] = yl[i]


def _scan_consts():
    lane = np.arange(GROUP)
    bmask = (lane[:, None] // HEAD == lane[None, :] // HEAD).astype(np.float32)
    i = np.arange(CHUNK)[:, None]
    j = np.arange(CHUNK)[None, :]
    jl = (lane % HEAD)[None, :]
    tri = np.stack([(j <= i), (j >= i)]).astype(np.float32)
    dir_masks = np.stack([np.stack([(jl < i), (jl <= i)]),
                          np.stack([(jl > i), (jl >= i)])]).astype(np.float32)
    same16 = (jl // SUB == i // SUB)
    same32 = (jl // (2 * SUB) == i // (2 * SUB))
    common = np.stack([(jl == i), same16, same32 & ~same16, ~same32]).astype(np.float32)
    bmask16 = (lane[:, None] // SUB == lane[None, :] // SUB).astype(np.float32)
    eye16 = ((lane % SUB)[None, :] == np.arange(SUB)[:, None]).astype(np.float32)
    return (jnp.asarray(bmask, BF16), jnp.asarray(bmask16, BF16), jnp.asarray(tri, BF16),
            jnp.asarray(dir_masks, F32), jnp.asarray(common, F32), jnp.asarray(eye16, F32))


def _scan(r, k, v, zw, za, k_k, k_a, n_ctx_chunks):
    b, l, d = r.shape
    rows = PAIR * CHUNK
    npair = l // rows
    assert l % rows == 0 and n_ctx_chunks % PAIR == 0
    n_ctx = n_ctx_chunks // PAIR
    bmask, bmask16, tri, dir_masks, common, eye16 = _scan_consts()

    def pair_of(di, s):
        back = jnp.where(s < n_ctx, n_ctx - 1 - s, npair - 1 + n_ctx - s)
        return jnp.where(di == 0, s, back)

    fold_pair = lambda di, s: pair_of(di, jnp.minimum(s, npair - 1))
    apply_pair = lambda di, s: pair_of(di, jnp.maximum(s - 1, 0))
    tok = pl.BlockSpec((1, rows, d), lambda di, bi, s: (bi, fold_pair(di, s), 0))
    tok2 = pl.BlockSpec((1, rows, d), lambda di, bi, s: (bi, fold_pair(di, s), di))
    par = pl.BlockSpec((1, d), lambda di, bi, s: (0, 0))
    return pl.pallas_call(
        _scan_kernel,
        out_shape=jax.ShapeDtypeStruct((2, b, l, d), BF16),
        grid=(2, b, npair + 1),
        in_specs=[tok, tok, tok, tok2, tok2, par, par,
                  pl.BlockSpec((GROUP, GROUP), lambda di, bi, s: (0, 0)),
                  pl.BlockSpec((GROUP, GROUP), lambda di, bi, s: (0, 0)),
                  pl.BlockSpec((1, CHUNK, CHUNK), lambda di, bi, s: (di, 0, 0)),
                  pl.BlockSpec((1, 2, CHUNK, GROUP), lambda di, bi, s: (di, 0, 0, 0)),
                  pl.BlockSpec((4, CHUNK, GROUP), lambda di, bi, s: (0, 0, 0)),
                  pl.BlockSpec((SUB, GROUP), lambda di, bi, s: (0, 0))],
        out_specs=pl.BlockSpec((1, 1, rows, d), lambda di, bi, s: (di, bi, apply_pair(di, s), 0)),
        scratch_shapes=[pltpu.VMEM((CHUNK, d), F32), pltpu.VMEM((PAIR, CHUNK, d), BF16),
                        pltpu.VMEM((PAIR, CHUNK, d), BF16), pltpu.VMEM((PAIR, CHUNK, d), F32),
                        pltpu.VMEM((PAIR, CHUNK, d), F32)],
        compiler_params=_params(("parallel", "parallel", "arbitrary")),
        name="scan",
    )(r, k, v, zw, za, k_k.reshape(1, d), k_a.reshape(1, d), bmask, bmask16, tri, dir_masks, common, eye16)


def _head_sum(x, ones_bd):
    return _dot(x.astype(BF16), ones_bd)


def _rwkv_post_kernel(y_ref, r_ref, k_ref, v_ref, za0_ref, za1_ref, gate_ref,
                      ka_ref, rk_ref, gnw_ref, gnb_ref, ones_ref, o_ref):
    ones_bd = ones_ref[...]
    tm, d = o_ref.shape[1], o_ref.shape[2]
    ng = d // GROUP
    to_rows = lambda x: jnp.concatenate([x[:, g * GROUP:(g + 1) * GROUP] for g in range(ng)], axis=0)
    to_cols = lambda x: jnp.concatenate([x[g * tm:(g + 1) * tm] for g in range(ng)], axis=1)
    head_mean = lambda x: to_cols(_head_sum(to_rows(x), ones_bd)) * (1.0 / HEAD)
    y = y_ref[0, 0].astype(F32) + y_ref[1, 0].astype(F32)
    yc = y - head_mean(y)
    var = head_mean(yc * yc)
    o = yc * lax.rsqrt(var + GN_EPS) * gnw_ref[...] + gnb_ref[...]
    a_sum = jax.nn.sigmoid(za0_ref[0].astype(F32)) + jax.nn.sigmoid(za1_ref[0].astype(F32))
    r = r_ref[0].astype(F32)
    k_sum = k_ref[0].astype(F32) * (2.0 + (a_sum - 2.0) * ka_ref[...])
    bonus = head_mean(r * k_sum * rk_ref[...]) * float(HEAD) * v_ref[0].astype(F32)
    o_ref[0] = ((o + bonus) * gate_ref[0].astype(F32)).astype(o_ref.dtype)


def _rwkv_post(y, r, k, v, za, gate, k_a, r_k, gn_w, gn_b, n_ctx, tm=256):
    _, b, l, d = y.shape
    t = l - n_ctx
    off = n_ctx // tm
    ones_bd = _scan_consts()[0]
    tok = pl.BlockSpec((1, tm, d), lambda bi, s: (bi, s + off, 0))
    par = pl.BlockSpec((1, d), lambda bi, s: (0, 0))
    return pl.pallas_call(
        _rwkv_post_kernel,
        out_shape=jax.ShapeDtypeStruct((b, t, d), BF16),
        grid=(b, t // tm),
        in_specs=[pl.BlockSpec((2, 1, tm, d), lambda bi, s: (0, bi, s + off, 0)),
                  tok, tok, tok,
                  pl.BlockSpec((1, tm, d), lambda bi, s: (bi, s + off, 0)),
                  pl.BlockSpec((1, tm, d), lambda bi, s: (bi, s + off, 1)),
                  tok, par, par, par, par,
                  pl.BlockSpec((GROUP, GROUP), lambda bi, s: (0, 0))],
        out_specs=pl.BlockSpec((1, tm, d), lambda bi, s: (bi, s, 0)),
        compiler_params=_params(("parallel", "parallel")),
        name="rwkv_post",
    )(y, r, k, v, za, za, gate, k_a.reshape(1, d), r_k.reshape(1, d), gn_w.reshape(1, d),
      gn_b.reshape(1, d), ones_bd)


def _proj_norm_kernel(a_ref, w_ref, x_ref, pos_ref, g_ref, lnw_ref, lnb_ref, sc_ref, sh_ref,
                      o_ref, h_ref, acc_ref, *, alpha):
    j = pl.program_id(2)
    nj = acc_ref.shape[0]
    tn = acc_ref.shape[2]
    acc_ref[j] = _dot(a_ref[...], w_ref[...].astype(BF16))

    @pl.when(j == nj - 1)
    def _():
        gate = g_ref[0]
        for jj in range(nj):
            cols = slice(jj * tn, (jj + 1) * tn)
            o_ref[:, cols] = alpha * (x_ref[:, cols] + pos_ref[:, cols]) + gate[:, cols] * acc_ref[jj]
        x_new = _layer_norm(o_ref[...], lnw_ref[...], lnb_ref[...])
        o_ref[...] = x_new
        h_ref[...] = (x_new * (1.0 + sc_ref[0]) + sh_ref[0]).astype(h_ref.dtype)


def _proj_norm(a, w, x, pos, gate, ln_w, ln_b, sc, sh, alpha, tm=512, tn=2048):
    m, k = a.shape
    d = w.shape[1]
    t = pos.shape[0]
    tm = _tile(t, tm)
    tn = _tile(d, tn)
    tpb = t // tm
    nb = m // t
    rows = lambda ti, bi, j: (bi * tpb + ti, 0)
    vec = pl.BlockSpec((1, 1, d), lambda ti, bi, j: (bi, 0, 0))
    return pl.pallas_call(
        functools.partial(_proj_norm_kernel, alpha=alpha),
        out_shape=[jax.ShapeDtypeStruct((m, d), F32), jax.ShapeDtypeStruct((m, d), BF16)],
        grid=(tpb, nb, d // tn),
        in_specs=[pl.BlockSpec((tm, k), rows),
                  pl.BlockSpec((k, tn), lambda ti, bi, j: (0, j)),
                  pl.BlockSpec((tm, d), rows),
                  pl.BlockSpec((tm, d), lambda ti, bi, j: (ti, 0)),
                  vec,
                  pl.BlockSpec((1, d), lambda ti, bi, j: (0, 0)),
                  pl.BlockSpec((1, d), lambda ti, bi, j: (0, 0)),
                  vec, vec],
        out_specs=[pl.BlockSpec((tm, d), rows)] * 2,
        scratch_shapes=[pltpu.VMEM((d // tn, tm, tn), F32)],
        compiler_params=_params(("parallel", "parallel", "arbitrary")),
        name="proj_norm",
    )(a, w, x, pos, gate, ln_w.reshape(1, d), ln_b.reshape(1, d), sc, sh)


def _swiglu_halves(h, w1_ref, w3_ref, w2_ref, lead):
    tf = w1_ref.shape[-1]
    halves = [slice(0, tf // 2), slice(tf // 2, tf)]
    a1 = [_dot(h, w1_ref[lead + (slice(None), c)].astype(BF16)) for c in halves]
    a3 = [_dot(h, w3_ref[lead + (slice(None), c)].astype(BF16)) for c in halves]
    u = [(x * jax.nn.sigmoid(x) * y).astype(BF16) for x, y in zip(a1, a3)]
    y = [_dot(x, w2_ref[lead + (c, slice(None))].astype(BF16)) for x, c in zip(u, halves)]
    return y[0] + y[1]


def _ffn_kernel(h_ref, w1_ref, w3_ref, w2_ref, o_ref, acc_ref):
    j = pl.program_id(1)

    @pl.when(j == 0)
    def _():
        acc_ref[...] = jnp.zeros_like(acc_ref)

    acc_ref[...] += _swiglu_halves(h_ref[...], w1_ref, w3_ref, w2_ref, ())

    @pl.when(j == pl.num_programs(1) - 1)
    def _():
        o_ref[...] = acc_ref[...].astype(o_ref.dtype)


def _ffn(h, w1, w3, w2, tm=1024, tf=512):
    m, d = h.shape
    f = w1.shape[1]
    tm = _tile(m, tm)
    tf = _tile(f, tf)
    return pl.pallas_call(
        _ffn_kernel,
        out_shape=jax.ShapeDtypeStruct((m, d), BF16),
        grid=(m // tm, f // tf),
        in_specs=[pl.BlockSpec((tm, d), lambda i, j: (i, 0)),
                  pl.BlockSpec((d, tf), lambda i, j: (0, j)),
                  pl.BlockSpec((d, tf), lambda i, j: (0, j)),
                  pl.BlockSpec((tf, d), lambda i, j: (j, 0))],
        out_specs=pl.BlockSpec((tm, d), lambda i, j: (i, 0)),
        scratch_shapes=[pltpu.VMEM((tm, d), F32)],
        compiler_params=_params(("parallel", "arbitrary")),
        name="ffn",
    )(h, w1, w3, w2)


def _residual_norm_kernel(x_ref, y_ref, g_ref, lnw_ref, lnb_ref, o_ref, *, alpha):
    z = alpha * x_ref[...] + g_ref[0] * y_ref[...].astype(F32)
    o_ref[...] = _layer_norm(z, lnw_ref[...], lnb_ref[...])


def _residual_norm(x, y, gate, ln_w, ln_b, alpha, rows_per_batch, tm=512):
    m, d = x.shape
    tm = _tile(rows_per_batch, tm)
    tpb = rows_per_batch // tm
    tok = pl.BlockSpec((tm, d), lambda i: (i, 0))
    row = pl.BlockSpec((1, d), lambda i: (0, 0))
    return pl.pallas_call(
        functools.partial(_residual_norm_kernel, alpha=alpha),
        out_shape=jax.ShapeDtypeStruct((m, d), F32),
        grid=(m // tm,),
        in_specs=[tok, tok, pl.BlockSpec((1, 1, d), lambda i: (i // tpb, 0, 0)), row, row],
        out_specs=tok,
        compiler_params=_params(("parallel",)),
        name="residual_norm",
    )(x, y, gate, ln_w.reshape(1, d), ln_b.reshape(1, d))


def _shift_down(x, s, row):
    return jnp.where(row >= s, pltpu.roll(x, s, axis=0), 0.0)


def _shift_up(x, s, row):
    t = x.shape[0]
    return jnp.where(row < t - s, pltpu.roll(x, t - s, axis=0), 0.0)


def _pool_kernel(x_ref, sc_ref, sh_ref, w_ref, scale_ref, o_ref):
    g = pl.program_id(0)
    t = x_ref.shape[1]
    h = x_ref[0] * (1.0 + sc_ref[0]) + sh_ref[0]
    row = lax.broadcasted_iota(jnp.int32, h.shape, 0)
    w = w_ref[0].astype(BF16)
    for gi, win in enumerate(POOL_WINDOWS):
        @pl.when(g == gi)
        def _(win=win):
            half = win // 2
            back = h
            fwd = h
            m = 1
            while m < half:
                back = back + _shift_down(back, m, row)
                fwd = fwd + _shift_up(fwd, m, row)
                m *= 2
            total = _shift_down(back, 1, row) + fwd
            count = (jnp.minimum(row + half, t) - jnp.maximum(row - half, 0)).astype(F32)
            pooled = (total / count - h).astype(BF16)
            o_ref[0] = _dot(pooled, w) * scale_ref[...]


def _pool(x, sc, sh, w_pool, scale):
    b, t, d = x.shape
    ng, p, _ = w_pool.shape
    vec = pl.BlockSpec((1, 1, p), lambda g, bi: (bi, 0, g))
    return pl.pallas_call(
        _pool_kernel,
        out_shape=jax.ShapeDtypeStruct((b, t, d), F32),
        grid=(ng, b),
        in_specs=[pl.BlockSpec((1, t, p), lambda g, bi: (bi, 0, g)), vec, vec,
                  pl.BlockSpec((1, p, p), lambda g, bi: (g, 0, 0)),
                  pl.BlockSpec((1, p), lambda g, bi: (0, g))],
        out_specs=pl.BlockSpec((1, t, p), lambda g, bi: (bi, 0, g)),
        compiler_params=_params(("parallel", "parallel")),
        name="pool",
    )(x, sc, sh, w_pool, scale.reshape(1, d))


def _route_kernel(x_ref, y_ref, g_ref, lnw_ref, lnb_ref, sc_ref, sh_ref, rt_ref,
                  xo_ref, h_ref, route_ref, *, alpha):
    x = _layer_norm(alpha * x_ref[...] + g_ref[0] * y_ref[...], lnw_ref[...], lnb_ref[...])
    xo_ref[...] = x
    h = x * (1.0 + sc_ref[0]) + sh_ref[0]
    h_ref[...] = h
    r1, r2, r3 = _split3(rt_ref[...])
    h1, h2, h3 = _split3(h)
    logits = (_dot_nt(r1, h1) + (_dot_nt(r1, h2) + _dot_nt(r2, h1))
              + (_dot_nt(r1, h3) + _dot_nt(r2, h2) + _dot_nt(r3, h1)))
    mx = jnp.max(logits, axis=0, keepdims=True)
    e = jnp.exp(logits - mx)
    p = e / jnp.sum(e, axis=0, keepdims=True)
    idx = lax.broadcasted_iota(jnp.int32, p.shape, 0)
    p1 = jnp.max(p, axis=0, keepdims=True)
    i1 = jnp.min(jnp.where(p == p1, idx, N_EXPERTS), axis=0, keepdims=True)
    rest = jnp.where(idx == i1, -1.0, p)
    p2 = jnp.max(rest, axis=0, keepdims=True)
    i2 = jnp.min(jnp.where(rest == p2, idx, N_EXPERTS), axis=0, keepdims=True)
    den = p1 + p2
    out = jnp.where(idx == 0, i1.astype(F32), 0.0)
    out = jnp.where(idx == 1, i2.astype(F32), out)
    out = jnp.where(idx == 2, p1 / den, out)
    out = jnp.where(idx == 3, p2 / den, out)
    route_ref[...] = out


def _route(x, y, gate, ln_w, ln_b, sc, sh, router, alpha, rows_per_batch, tm=256):
    m, d = x.shape
    tpb = rows_per_batch // tm
    vec = pl.BlockSpec((1, 1, d), lambda i: (i // tpb, 0, 0))
    row = pl.BlockSpec((1, d), lambda i: (0, 0))
    tok = pl.BlockSpec((tm, d), lambda i: (i, 0))
    return pl.pallas_call(
        functools.partial(_route_kernel, alpha=alpha),
        out_shape=[jax.ShapeDtypeStruct((m, d), F32), jax.ShapeDtypeStruct((m, d), F32),
                   jax.ShapeDtypeStruct((N_EXPERTS, m), F32)],
        grid=(m // tm,),
        in_specs=[tok, tok, vec, row, row, vec, vec,
                  pl.BlockSpec((N_EXPERTS, d), lambda i: (0, 0))],
        out_specs=[tok, tok, pl.BlockSpec((N_EXPERTS, tm), lambda i: (0, i))],
        compiler_params=_params(("parallel",)),
        name="route",
    )(x, y, gate, ln_w.reshape(1, d), ln_b.reshape(1, d), sc, sh, router.T)


def _row_copy(src_hbm, dst_vmem, sem, src_row, dst_row):
    return pltpu.make_async_copy(src_hbm.at[pl.ds(src_row, 1)], dst_vmem.at[pl.ds(dst_row, 1)], sem)


def _moe_kernel(ce_ref, nv_ref, idx_ref, h_ref, w1_ref, w3_ref, w2_ref, o_ref,
                stage_ref, work_ref, gather_sem, *, share):
    c = pl.program_id(0)
    j = pl.program_id(1)
    n_chunks = pl.num_programs(0)
    nj = pl.num_programs(1)
    nv = nv_ref[c]
    full = work_ref.shape[0]
    stage_rows = stage_ref.shape[0]

    def gather(chunk, r):
        return _row_copy(h_ref, stage_ref, gather_sem, idx_ref[chunk * full + r], r)

    def wait_all_gathers():
        def body(r, carry):
            gather(0, r).wait()
            return carry
        lax.fori_loop(0, stage_rows, body, 0, unroll=8)

    @pl.when(j == 0)
    def _():
        @pl.when(c == 0)
        def _():
            def body(r, carry):
                gather(0, r).start()
                return carry
            lax.fori_loop(0, stage_rows, body, 0, unroll=8)

        o_ref[...] = jnp.zeros_like(o_ref)
        wait_all_gathers()
        work_ref[...] = stage_ref[0:full, :].astype(work_ref.dtype)

    def request_next_share():
        for i in range(share):
            gather(c + 1, j * share + i).start(priority=1)

    def swiglu_rows(rows):
        o_ref[rows, :] += _swiglu_halves(work_ref[rows, :], w1_ref, w3_ref, w2_ref, (0,))

    n_sub = (nv + (MOE_SUB - 1)) // MOE_SUB
    for k in range(full // MOE_SUB + 1):
        @pl.when(n_sub == k)
        def _(k=k):
            request_next_share()
            if k:
                swiglu_rows(slice(0, k * MOE_SUB))

    @pl.when(jnp.logical_and(j == nj - 1, c == n_chunks - 1))
    def _():
        wait_all_gathers()


def _moe_ffn(h, src_idx, chunk_expert, chunk_valid, w1, w3, w2, n_chunks, tf=512):
    d = h.shape[1]
    f = w1.shape[2]
    tf = _tile(f, tf)
    nj = f // tf
    share = -(-MOE_CHUNK // nj)
    share = -(-share // 8) * 8
    stage_rows = share * nj
    assert src_idx.shape[0] >= n_chunks * MOE_CHUNK + stage_rows

    def jeff(c, j, nv):
        return jnp.where(nv[c] > 0, j, nj - 1)

    return pl.pallas_call(
        functools.partial(_moe_kernel, share=share),
        out_shape=jax.ShapeDtypeStruct((n_chunks * MOE_CHUNK, d), F32),
        grid_spec=pltpu.PrefetchScalarGridSpec(
            num_scalar_prefetch=3,
            grid=(n_chunks, nj),
            in_specs=[pl.BlockSpec(memory_space=pl.ANY),
                      pl.BlockSpec((1, d, tf), lambda c, j, ce, nv, ix: (ce[c], 0, jeff(c, j, nv))),
                      pl.BlockSpec((1, d, tf), lambda c, j, ce, nv, ix: (ce[c], 0, jeff(c, j, nv))),
                      pl.BlockSpec((1, tf, d), lambda c, j, ce, nv, ix: (ce[c], jeff(c, j, nv), 0))],
            out_specs=pl.BlockSpec((MOE_CHUNK, d), lambda c, j, ce, nv, ix: (c, 0)),
            scratch_shapes=[pltpu.VMEM((stage_rows, d), F32), pltpu.VMEM((MOE_CHUNK, d), BF16),
                            pltpu.SemaphoreType.DMA]),
        compiler_params=_params(("arbitrary", "arbitrary"), MOE_VMEM_LIMIT_BYTES),
        name="moe_ffn",
    )(chunk_expert, chunk_valid, src_idx, h, w1, w3, w2)


def _combine_kernel(p0_ref, p1_ref, ys_ref, x_ref, gates_ref, g_ref, lnw_ref, lnb_ref,
                    o_ref, b0_ref, b1_ref, sem, *, alpha):
    i = pl.program_id(0)
    n = pl.num_programs(0)
    rows = b0_ref.shape[1]
    slot = i % 2

    def copies(tile, to_slot, r):
        return (_row_copy(ys_ref, b0_ref.at[to_slot], sem.at[to_slot], p0_ref[tile * rows + r], r),
                _row_copy(ys_ref, b1_ref.at[to_slot], sem.at[to_slot], p1_ref[tile * rows + r], r))

    def wait_slot(to_slot):
        def body(r, carry):
            for cp in copies(0, to_slot, r):
                cp.wait()
            return carry
        lax.fori_loop(0, rows, body, 0, unroll=4)

    @pl.when(i == 0)
    def _():
        def body(r, carry):
            for cp in copies(0, 0, r):
                cp.start()
            return carry
        lax.fori_loop(0, rows, body, 0, unroll=4)

    wait_slot(slot)
    for r in range(rows):
        for cp in copies(i + 1, 1 - slot, r):
            cp.start()
    gates = gates_ref[...]
    y = gates[:, 0:1] * b0_ref[slot] + gates[:, 1:2] * b1_ref[slot]
    z = alpha * x_ref[...] + g_ref[0] * y
    o_ref[...] = _layer_norm(z, lnw_ref[...], lnb_ref[...])

    @pl.when(i == n - 1)
    def _():
        wait_slot(1 - slot)


def _combine(ys, pos0, pos1, gates, x, gate_vec, ln_w, ln_b, alpha, rows_per_batch, rows=256):
    m, d = x.shape
    tpb = rows_per_batch // rows
    tok = lambda i, a, b: (i, 0)
    return pl.pallas_call(
        functools.partial(_combine_kernel, alpha=alpha),
        out_shape=jax.ShapeDtypeStruct((m, d), F32),
        grid_spec=pltpu.PrefetchScalarGridSpec(
            num_scalar_prefetch=2,
            grid=(m // rows,),
            in_specs=[pl.BlockSpec(memory_space=pl.ANY),
                      pl.BlockSpec((rows, d), tok),
                      pl.BlockSpec((rows, 2), tok),
                      pl.BlockSpec((1, 1, d), lambda i, a, b: (i // tpb, 0, 0)),
                      pl.BlockSpec((1, d), lambda i, a, b: (0, 0)),
                      pl.BlockSpec((1, d), lambda i, a, b: (0, 0))],
            out_specs=pl.BlockSpec((rows, d), tok),
            scratch_shapes=[pltpu.VMEM((2, rows, d), F32), pltpu.VMEM((2, rows, d), F32),
                            pltpu.SemaphoreType.DMA((2,))]),
        compiler_params=_params(("arbitrary",)),
        name="moe_combine",
    )(pos0, pos1, ys, x, gates, gate_vec, ln_w.reshape(1, d), ln_b.reshape(1, d))


def _routing_tables(route, n_chunks, table_len):
    n = route.shape[1]
    experts = jnp.concatenate([route[0], route[1]]).astype(jnp.int32)
    onehot = (experts[:, None] == jnp.arange(N_EXPERTS, dtype=jnp.int32)[None, :]).astype(jnp.int32)
    csum = jnp.cumsum(onehot, axis=0)
    rank = jnp.sum((csum - 1) * onehot, axis=1)
    counts = csum[-1]
    chunks_e = (counts + MOE_CHUNK - 1) // MOE_CHUNK
    chunk_end = jnp.cumsum(chunks_e)
    chunk_start = chunk_end - chunks_e
    dest = (chunk_start * MOE_CHUNK)[experts] + rank
    token = jnp.arange(2 * n, dtype=jnp.int32) % n
    src_idx = jnp.zeros((table_len,), jnp.int32).at[dest].set(token)
    cid = jnp.arange(n_chunks, dtype=jnp.int32)
    used = cid < chunk_end[-1]
    last_used = jnp.maximum(chunk_end[-1] - 1, 0)
    ce = jnp.sum((jnp.minimum(cid, last_used)[:, None] >= chunk_end[None, :]).astype(jnp.int32), axis=1)
    ce = jnp.minimum(ce, N_EXPERTS - 1)
    nvalid = jnp.clip(counts[ce] - (cid - chunk_start[ce]) * MOE_CHUNK, 0, MOE_CHUNK)
    nvalid = jnp.where(used, nvalid, 0).astype(jnp.int32)
    return src_idx, ce, nvalid, dest[:n], dest[n:]


def _position_embedding(rows, width, d):
    quarter = d // 4
    omega = 1.0 / (POS_BASE ** (jnp.arange(quarter, dtype=F32) / quarter))
    ar = jnp.arange(rows, dtype=F32)[:, None] * omega[None, :]
    ac = jnp.arange(width, dtype=F32)[:, None] * omega[None, :]
    row_part = jnp.repeat(jnp.concatenate([jnp.sin(ar), jnp.cos(ar)], axis=-1), width, axis=0)
    col_part = jnp.tile(jnp.concatenate([jnp.sin(ac), jnp.cos(ac)], axis=-1), (rows, 1))
    return jnp.concatenate([row_part, col_part], axis=-1)


def _block_diag2(w):
    z = jnp.zeros_like(w[0])
    return jnp.concatenate([jnp.concatenate([w[0], z], axis=1), jnp.concatenate([z, w[1]], axis=1)], axis=0)


def kernel(x, c, ctx, c_ctx, w_mod, b_mod, ln_w, ln_b, rwkv_mu, rwkv_w_r, rwkv_w_k, rwkv_w_v, rwkv_w_o, rwkv_decay_w0, rwkv_decay_w1, rwkv_decay_w2, rwkv_iclr_a0, rwkv_iclr_a1, rwkv_iclr_a2, rwkv_gate_g1, rwkv_gate_g2, rwkv_k_k, rwkv_k_a, rwkv_r_k, rwkv_gn_w, rwkv_gn_b, pool_w, pool_scale, ffn_w1, ffn_w3, ffn_w2, moe_router, moe_w1, moe_w3, moe_w2):
    b, t, d = x.shape
    n_ctx = ctx.shape[1]
    depth = w_mod.shape[0]
    assert depth == 2 and rwkv_mu.shape[0] == 1 and pool_w.shape[0] == 1
    alpha = (2.0 * depth) ** 0.25
    grid_w = 64
    l = n_ctx + t
    n = b * t

    cond = jnp.zeros((8, d), F32).at[:b].set(c).at[b].set(c_ctx)
    mod = _adaln(cond, w_mod, b_mod).reshape(depth, 8, 6, d)
    lat = lambda layer, which: mod[layer, :b, which].reshape(b, 1, d)
    cvec = lambda layer, which: mod[layer, b, which].reshape(1, 1, d)

    pos = _position_embedding(t // grid_w, grid_w, d)

    xr, xw, xk, xv, xa, xg = _rwkv_mix(ctx, x, pos, lat(0, 1), lat(0, 0), cvec(0, 1), cvec(0, 0), rwkv_mu[0])
    flat = lambda a: a.reshape(b * l, a.shape[-1])
    r = _mm(flat(xr), rwkv_w_r[0], out_dtype=BF16, tn=1024)
    k = _mm(flat(xk), rwkv_w_k[0], out_dtype=BF16, tn=1024)
    v = _mm(flat(xv), rwkv_w_v[0], out_dtype=BF16, tn=1024)
    dw1 = jnp.concatenate([rwkv_decay_w1[0, 0], rwkv_decay_w1[0, 1]], axis=1)
    ia1 = jnp.concatenate([rwkv_iclr_a1[0, 0], rwkv_iclr_a1[0, 1]], axis=1)
    lora_w = _mm(flat(xw), dw1, act="tanh", out_dtype=BF16)
    zw = _mm(lora_w, _block_diag2(rwkv_decay_w2[0]), rwkv_decay_w0[0].reshape(1, 2 * d), out_dtype=BF16, tn=2048)
    lora_a = _mm(flat(xa), ia1, out_dtype=BF16)
    za = _mm(lora_a, _block_diag2(rwkv_iclr_a2[0]), rwkv_iclr_a0[0].reshape(1, 2 * d), out_dtype=BF16, tn=2048)
    lora_g = _mm(flat(xg), rwkv_gate_g1[0], act="sigmoid", out_dtype=BF16)
    gate = _mm(lora_g, rwkv_gate_g2[0], out_dtype=BF16, tn=2048)
    seq = lambda a: a.reshape(b, l, a.shape[-1])
    y_scan = _scan(seq(r), seq(k), seq(v), seq(zw), seq(za), rwkv_k_k[0], rwkv_k_a[0], n_ctx // CHUNK)
    og = _rwkv_post(y_scan, seq(r), seq(k), seq(v), seq(za), seq(gate), rwkv_k_a[0],
                    rwkv_r_k[0].reshape(d), rwkv_gn_w[0], rwkv_gn_b[0], n_ctx)
    x1, h1 = _proj_norm(og.reshape(n, d), rwkv_w_o[0].astype(BF16), x.reshape(n, d), pos, lat(0, 2),
                        ln_w[0, 0], ln_b[0, 0], lat(0, 4), lat(0, 3), alpha)
    y_ffn = _ffn(h1, ffn_w1[0], ffn_w3[0], ffn_w2[0])
    x2 = _residual_norm(x1, y_ffn, lat(0, 5), ln_w[0, 1], ln_b[0, 1], alpha, t)

    y_pool = _pool(x2.reshape(b, t, d), lat(1, 1), lat(1, 0), pool_w[0], pool_scale[0])
    x3, h3, route = _route(x2, y_pool.reshape(n, d), lat(1, 2), ln_w[1, 0], ln_b[1, 0],
                           lat(1, 4), lat(1, 3), moe_router[0], alpha, t)
    n_chunks = (2 * n + N_EXPERTS * (MOE_CHUNK - 1)) // MOE_CHUNK
    src_idx, chunk_expert, chunk_valid, pos0, pos1 = _routing_tables(route, n_chunks, (n_chunks + 2) * MOE_CHUNK)
    ys = _moe_ffn(h3, src_idx, chunk_expert, chunk_valid, moe_w1[0], moe_w3[0], moe_w2[0], n_chunks)
    gates = jnp.stack([route[2], route[3]], axis=1)
    spare = jnp.zeros((256,), jnp.int32)
    out = _combine(ys, jnp.concatenate([pos0, spare]), jnp.concatenate([pos1, spare]), gates, x3,
                   lat(1, 5), ln_w[1, 1], ln_b[1, 1], alpha, t)
    return out.reshape(b, t, d)
```
